```python
import jax, jax.numpy as jnp
from jax import lax
import numpy as np

D_MODEL = 1024
BATCH = 8
SEQ = 2048
DEPTH = 2
DEC_BATCH = 128
DEC_SEQ = 1
PAST_LEN = 16384
PAGE_SIZE = 128

H_RET = 4
H_M = 4
HEAD_DIM = 128
W_RET = H_RET * HEAD_DIM
W_M = H_M * HEAD_DIM
D_MIX = W_RET + W_M
CONV_W = 4
CHUNK = 128
ROPE_BASE = 10000.0
LN_EPS = 1e-5
GN_EPS = 1e-5
ALPHA = (2 * DEPTH) ** 0.25
BETA = (8 * DEPTH) ** -0.25
N_IN = 4 * W_RET + 5 * W_M + 2 * H_M
SPLITS = [W_RET, 2 * W_RET, 3 * W_RET, 4 * W_RET,
          4 * W_RET + 2 * W_M, 4 * W_RET + 3 * W_M, 4 * W_RET + 4 * W_M,
          4 * W_RET + 5 * W_M, 4 * W_RET + 5 * W_M + H_M]

kernel_name = 'hymba_retention_mlstm_deepnorm_step'

F32 = jnp.float32


def layer_norm(x, g, b):
    xf = x.astype(F32)
    mu = xf.mean(-1, keepdims=True)
    var = jnp.square(xf - mu).mean(-1, keepdims=True)
    return ((xf - mu) * lax.rsqrt(var + LN_EPS)).astype(x.dtype) * g + b


def head_norm(h, g):
    mu = h.mean(-1, keepdims=True)
    var = jnp.square(h - mu).mean(-1, keepdims=True)
    return (h - mu) * lax.rsqrt(var + GN_EPS) * g.reshape(h.shape[-2], h.shape[-1]).astype(F32)


def rotary(x, pos):
    half = HEAD_DIM // 2
    inv = ROPE_BASE ** (-jnp.arange(half, dtype=F32) / half)
    ang = pos.astype(F32)[:, None] * inv[None, :]
    cos = jnp.cos(ang)[None, :, None, :]
    sin = jnp.sin(ang)[None, :, None, :]
    x1 = x[..., :half].astype(F32)
    x2 = x[..., half:].astype(F32)
    return jnp.concatenate([x1 * cos - x2 * sin, x1 * sin + x2 * cos], axis=-1)


def causal_conv(u, buf, w, b):
    T = u.shape[1]
    full = jnp.concatenate([buf.astype(u.dtype), u], axis=1)
    out = b + sum(full[:, j:j + T] * w[j] for j in range(CONV_W))
    return jax.nn.silu(out), full[:, -(CONV_W - 1):]


def chunk_len(T):
    return CHUNK if T % CHUNK == 0 else T


def to_chunks(a, L):
    B, T = a.shape[0], a.shape[1]
    if a.ndim == 4:
        return a.reshape(B, T // L, L, a.shape[2], a.shape[3]).transpose(1, 0, 3, 2, 4).astype(F32)
    return a.reshape(B, T // L, L, a.shape[2]).transpose(1, 0, 3, 2).astype(F32)


def from_chunks(o):
    N, B, H, L, D = o.shape
    return o.transpose(1, 0, 3, 2, 4).reshape(B, N * L, H, D)


def retention(q, k, v, S0):
    T = q.shape[1]
    L = chunk_len(T)
    log_gamma = jnp.log(1.0 - 2.0 ** (-5.0 - jnp.arange(H_RET, dtype=F32)))
    idx = jnp.arange(L, dtype=F32)
    diff = idx[:, None] - idx[None, :]
    decay = jnp.exp(log_gamma[:, None, None] * jnp.maximum(diff, 0.0)) * (diff >= 0)
    q_decay = jnp.exp(log_gamma[:, None] * (idx + 1.0))[None, :, :, None]
    k_decay = jnp.exp(log_gamma[:, None] * (L - 1.0 - idx))[None, :, :, None]
    c_decay = jnp.exp(log_gamma * L)[None, :, None, None]

    def step(S, xs):
        qc, kc, vc = xs
        sc = jnp.einsum('bhld,bhmd->bhlm', qc, kc) * decay
        o = jnp.einsum('bhlm,bhmd->bhld', sc, vc) + jnp.einsum('bhld,bhde->bhle', qc, S) * q_decay
        S = S * c_decay + jnp.einsum('bhld,bhle->bhde', kc * k_decay, vc)
        return S, o

    S, o = lax.scan(step, S0.astype(F32), (to_chunks(q, L), to_chunks(k, L), to_chunks(v, L)))
    return from_chunks(o), S


def mlstm(q, k, v, i_pre, f_pre, C0, n0, m0):
    T = q.shape[1]
    L = chunk_len(T)
    causal = jnp.arange(L)[:, None] >= jnp.arange(L)[None, :]

    def step(carry, xs):
        C, n, m = carry
        qc, kc, vc, ic, fc = xs
        b = jnp.cumsum(jax.nn.log_sigmoid(fc), axis=-1)
        a = b + m[..., None]
        Dm = jnp.where(causal, b[..., :, None] - b[..., None, :] + ic[..., None, :], -jnp.inf)
        mt = jnp.maximum(a, Dm.max(-1))
        w_intra = jnp.exp(Dm - mt[..., None])
        w_inter = jnp.exp(a - mt)
        s = jnp.einsum('bhld,bhmd->bhlm', qc, kc) * w_intra
        num = jnp.einsum('bhlm,bhmd->bhld', s, vc) + jnp.einsum('bhld,bhde->bhle', qc, C) * w_inter[..., None]
        den = s.sum(-1) + jnp.einsum('bhld,bhd->bhl', qc, n) * w_inter
        h = num / jnp.maximum(jnp.abs(den), jnp.exp(-mt))[..., None]
        bL = b[..., -1]
        g = bL[..., None] - b + ic
        m_new = jnp.maximum(bL + m, g.max(-1))
        wk = jnp.exp(g - m_new[..., None])
        wc = jnp.exp(bL + m - m_new)
        C_new = C * wc[..., None, None] + jnp.einsum('bhld,bhle->bhde', kc * wk[..., None], vc)
        n_new = n * wc[..., None] + jnp.einsum('bhld,bhl->bhd', kc, wk)
        return (C_new, n_new, m_new), h

    (C, n, m), h = lax.scan(step, (C0.astype(F32), n0.astype(F32), m0.astype(F32)),
                            (to_chunks(q, L), to_chunks(k, L), to_chunks(v, L),
                             to_chunks(i_pre, L), to_chunks(f_pre, L)))
    return from_chunks(h), C, n, m


def mixer_layer(x, pos, S0, C0, n0, m0, buf0, w_in, conv_w, conv_b, b_i, b_f,
                g_ret, g_m, w_out, ln_g, ln_b):
    B, T, _ = x.shape
    proj = jnp.einsum('btd,dn->btn', x, w_in)
    rq, rk, rv, rz, mqk, mv, mo, mz, mi, mf = jnp.split(proj, SPLITS, axis=-1)
    q_r = rotary(rq.reshape(B, T, H_RET, HEAD_DIM), pos)
    k_r = rotary(rk.reshape(B, T, H_RET, HEAD_DIM), pos) * (HEAD_DIM ** -0.5)
    o_r, S_new = retention(q_r, k_r, rv.reshape(B, T, H_RET, HEAD_DIM), S0)
    o_r = head_norm(o_r, g_ret).astype(x.dtype).reshape(B, T, W_RET) * jax.nn.silu(rz)
    qk, buf_new = causal_conv(mqk, buf0, conv_w, conv_b)
    q_m = qk[..., :W_M].reshape(B, T, H_M, HEAD_DIM)
    k_m = qk[..., W_M:].reshape(B, T, H_M, HEAD_DIM) * (HEAD_DIM ** -0.5)
    i_pre = mi.astype(F32) + b_i.astype(F32)
    f_pre = mf.astype(F32) + b_f.astype(F32)
    h_m, C_new, n_new, m_new = mlstm(q_m, k_m, mv.reshape(B, T, H_M, HEAD_DIM), i_pre, f_pre, C0, n0, m0)
    h_m = head_norm(h_m, g_m).astype(x.dtype).reshape(B, T, W_M) * jax.nn.sigmoid(mo) * jax.nn.silu(mz)
    mix = jnp.einsum('btm,md->btd', jnp.concatenate([o_r, h_m], axis=-1), w_out)
    y = layer_norm(ALPHA * x + mix, ln_g, ln_b)
    return y, S_new, C_new, n_new, m_new, buf_new


def setup_inputs(seed: int = 0) -> dict:
    key = jax.random.key(seed)
    ks = jax.random.split(key, 20)
    nrm = jax.random.normal
    return {
        'x_prompt': nrm(ks[0], (BATCH, SEQ, D_MODEL), F32),
        'x_sample': nrm(ks[1], (DEC_BATCH, DEC_SEQ, D_MODEL), F32),
        'state_ret': 0.5 * nrm(ks[2], (DEPTH, DEC_BATCH, H_RET, HEAD_DIM, HEAD_DIM), F32),
        'state_mlstm_C': 0.5 * nrm(ks[3], (DEPTH, DEC_BATCH, H_M, HEAD_DIM, HEAD_DIM), F32),
        'state_mlstm_n': 0.5 * nrm(ks[4], (DEPTH, DEC_BATCH, H_M, HEAD_DIM), F32),
        'state_mlstm_m': nrm(ks[5], (DEPTH, DEC_BATCH, H_M), F32),
        'state_conv': nrm(ks[6], (DEPTH, DEC_BATCH, CONV_W - 1, 2 * W_M), F32),
        'w_in': nrm(ks[7], (DEPTH, D_MODEL, N_IN), F32) * D_MODEL ** -0.5,
        'conv_w': nrm(ks[8], (DEPTH, CONV_W, 2 * W_M), F32) * CONV_W ** -0.5,
        'conv_b': 0.02 * nrm(ks[9], (DEPTH, 2 * W_M), F32),
        'b_i': 0.1 * nrm(ks[10], (DEPTH, H_M), F32),
        'b_f': jnp.linspace(3.0, 6.0, H_M, dtype=F32)[None, :] + 0.1 * nrm(ks[11], (DEPTH, H_M), F32),
        'g_ret': 1.0 + 0.02 * nrm(ks[12], (DEPTH, W_RET), F32),
        'g_m': 1.0 + 0.02 * nrm(ks[13], (DEPTH, W_M), F32),
        'w_out': nrm(ks[14], (DEPTH, D_MIX, D_MODEL), F32) * (D_MIX ** -0.5) * BETA,
        'ln_g': 1.0 + 0.02 * nrm(ks[15], (DEPTH, D_MODEL), F32),
        'ln_b': 0.02 * nrm(ks[16], (DEPTH, D_MODEL), F32),
    }


def reference(x_prompt, x_sample, state_ret, state_mlstm_C, state_mlstm_n, state_mlstm_m,
              state_conv, w_in, conv_w, conv_b, b_i, b_f, g_ret, g_m, w_out, ln_g, ln_b):
    B, T = x_prompt.shape[0], x_prompt.shape[1]
    Bs, Ts = x_sample.shape[0], x_sample.shape[1]
    pos_p = jnp.arange(T, dtype=jnp.int32)
    pos_s = PAST_LEN + jnp.arange(Ts, dtype=jnp.int32)
    S0p = jnp.zeros((B, H_RET, HEAD_DIM, HEAD_DIM), F32)
    C0p = jnp.zeros((B, H_M, HEAD_DIM, HEAD_DIM), F32)
    n0p = jnp.zeros((B, H_M, HEAD_DIM), F32)
    m0p = jnp.zeros((B, H_M), F32)
    buf0p = jnp.zeros((B, CONV_W - 1, 2 * W_M), x_prompt.dtype)
    xp, xs = x_prompt, x_sample
    rp, cp, np_, mp, vp = [], [], [], [], []
    rs, cs, ns, ms, vs = [], [], [], [], []
    for l in range(DEPTH):
        w = (w_in[l], conv_w[l], conv_b[l], b_i[l], b_f[l], g_ret[l], g_m[l], w_out[l], ln_g[l], ln_b[l])
        xp, S, C, n, m, buf = mixer_layer(xp, pos_p, S0p, C0p, n0p, m0p, buf0p, *w)
        rp.append(S); cp.append(C); np_.append(n); mp.append(m); vp.append(buf)
        xs, S, C, n, m, buf = mixer_layer(xs, pos_s, state_ret[l], state_mlstm_C[l], state_mlstm_n[l],
                                          state_mlstm_m[l], state_conv[l], *w)
        rs.append(S); cs.append(C); ns.append(n); ms.append(m); vs.append(buf)
    return (xp, xs,
            jnp.stack(rp), jnp.stack(cp), jnp.stack(np_), jnp.stack(mp), jnp.stack(vp),
            jnp.stack(rs), jnp.stack(cs), jnp.stack(ns), jnp.stack(ms), jnp.stack(vs))
```

```python
import functools

import jax
import jax.numpy as jnp
from jax import lax
from jax.experimental import pallas as pl
from jax.experimental.pallas import tpu as pltpu

F32 = jnp.float32
BF16 = jnp.bfloat16

D_MODEL = 1024
DEPTH = 2
PAST_LEN = 16384
N_HEADS = 4
HEAD_DIM = 128
GROUP_W = N_HEADS * HEAD_DIM
CONV_W = 4
CHUNK = 128
ROPE_BASE = 10000.0
LN_EPS = 1e-5
GN_EPS = 1e-5
ALPHA = (2 * DEPTH) ** 0.25
QK_SCALE = HEAD_DIM ** -0.5

LANES = 128
SUBLANES = 8

COL_RQ = 0
COL_RK = COL_RQ + GROUP_W
COL_RV = COL_RK + GROUP_W
COL_RZ = COL_RV + GROUP_W
COL_MQK = COL_RZ + GROUP_W
COL_MV = COL_MQK + 2 * GROUP_W
COL_MO = COL_MV + GROUP_W
COL_MZ = COL_MO + GROUP_W
COL_GATE = COL_MZ + GROUP_W
N_IN = COL_GATE + 2 * N_HEADS
N_PAD = COL_GATE + LANES

PROMPT_BLOCK_T = 256
SAMPLE_BLOCK_B = 8
VMEM_LIMIT_BYTES = 56 * 1024 * 1024


def _sigmoid(x):
    return 1.0 / (1.0 + jnp.exp(-x))


def _silu(x):
    return x * _sigmoid(x)


def _log_sigmoid(x):
    return jnp.minimum(x, 0.0) - jnp.log1p(jnp.exp(-jnp.abs(x)))


def _dot(a, b):
    return jnp.dot(a, b, preferred_element_type=F32)


def _dot_nt(a, b):
    return lax.dot_general(a, b, (((1,), (1,)), ((), ())), preferred_element_type=F32)


def _dot_tn(a, b):
    return lax.dot_general(a, b, (((0,), (0,)), ((), ())), preferred_element_type=F32)


def _rotary(x, cos_t, sin_t):
    return x * cos_t + pltpu.roll(x, HEAD_DIM // 2, axis=1) * sin_t


def _head_norm(h, g):
    mu = jnp.mean(h, axis=-1, keepdims=True)
    d = h - mu
    var = jnp.mean(d * d, axis=-1, keepdims=True)
    return d * lax.rsqrt(var + GN_EPS) * g


def _layer_norm(x, g, b):
    mu = jnp.mean(x, axis=-1, keepdims=True)
    d = x - mu
    var = jnp.mean(d * d, axis=-1, keepdims=True)
    return d * lax.rsqrt(var + LN_EPS) * g + b


def _cumsum_rows(tril_bf, x):
    hi = x.astype(BF16)
    r1 = x - hi.astype(F32)
    mid = r1.astype(BF16)
    lo = (r1 - mid.astype(F32)).astype(BF16)
    return _dot(tril_bf, hi) + _dot(tril_bf, mid) + _dot(tril_bf, lo)


def _prompt_kernel(x_ref, w_in_ref, conv_w_ref, conv_b_ref, gbias_ref, g_ret_ref, g_m_ref,
                   w_out_ref, ln_g_ref, ln_b_ref, cos_ref, sin_ref, kcos_ref, ksin_ref,
                   decay_ref, qd_ref, kd_ref, cdec_ref, tril_ref, mask_ref, e0_ref,
                   y_ref, s_out_ref, c_out_ref, n_out_ref, m_out_ref, conv_out_ref,
                   proj_scr, u_scr, qk_scr, mix_scr, s_scr, caug_scr, m_scr):
    t = pl.program_id(1)
    block_t = x_ref.shape[1]
    n_chunks = block_t // CHUNK
    carry_rows = CONV_W - 1

    @pl.when(t == 0)
    def _init_state():
        s_scr[...] = jnp.zeros_like(s_scr)
        caug_scr[...] = jnp.zeros_like(caug_scr)
        m_scr[...] = jnp.zeros_like(m_scr)
        u_scr[0:SUBLANES, :] = jnp.zeros((SUBLANES, 2 * GROUP_W), F32)

    xb = x_ref[0].astype(BF16)
    for lo in range(0, N_PAD, 512):
        hi = min(lo + 512, N_PAD)
        proj_scr[:, lo:hi] = _dot(xb, w_in_ref[:, lo:hi])

    for cs in range(0, 2 * GROUP_W, LANES):
        cols = slice(cs, cs + LANES)
        u_scr[SUBLANES:SUBLANES + block_t, cols] = proj_scr[:, COL_MQK + cs:COL_MQK + cs + LANES]
        acc = conv_b_ref[:, cols]
        for j in range(CONV_W):
            r0 = SUBLANES - carry_rows + j
            acc = acc + u_scr[r0:r0 + block_t, cols] * conv_w_ref[j:j + 1, cols]
        act = _silu(acc)
        if cs >= GROUP_W:
            act = act * QK_SCALE
        qk_scr[:, cols] = act

    @pl.when(t == pl.num_programs(1) - 1)
    def _write_conv_state():
        conv_out_ref[0] = u_scr[SUBLANES + block_t - carry_rows:SUBLANES + block_t, :]

    u_scr[0:SUBLANES, :] = u_scr[block_t:block_t + SUBLANES, :]

    tril_bf = tril_ref[...]
    causal_add = mask_ref[...]
    ones_col = e0_ref[...]
    lane_id = lax.broadcasted_iota(jnp.int32, (CHUNK, LANES), 1)

    def chunk_body(c, carry):
        r0 = pl.multiple_of(c * CHUNK, CHUNK)
        rows = pl.ds(r0, CHUNK)
        cos_t = cos_ref[rows, :]
        sin_t = sin_ref[rows, :]
        kcos_t = kcos_ref[rows, :]
        ksin_t = ksin_ref[rows, :]

        for h in range(N_HEADS):
            hc = h * HEAD_DIM
            q = _rotary(proj_scr[rows, COL_RQ + hc:COL_RQ + hc + HEAD_DIM], cos_t, sin_t)
            k = _rotary(proj_scr[rows, COL_RK + hc:COL_RK + hc + HEAD_DIM], kcos_t, ksin_t)
            v_bf = proj_scr[rows, COL_RV + hc:COL_RV + hc + HEAD_DIM].astype(BF16)
            state = s_scr[h]
            sc = _dot_nt(q.astype(BF16), k.astype(BF16)) * decay_ref[h]
            lhs = jnp.concatenate([sc.astype(BF16), (q * qd_ref[h]).astype(BF16)], axis=1)
            rhs = jnp.concatenate([v_bf, state.astype(BF16)], axis=0)
            o = _dot(lhs, rhs)
            s_scr[h] = state * cdec_ref[h] + _dot_tn((k * kd_ref[h]).astype(BF16), v_bf)
            o = _head_norm(o, g_ret_ref[:, hc:hc + HEAD_DIM])
            z = proj_scr[rows, COL_RZ + hc:COL_RZ + hc + HEAD_DIM]
            mix_scr[rows, hc:hc + HEAD_DIM] = (o * _silu(z)).astype(BF16)

        gates = proj_scr[rows, COL_GATE:COL_GATE + LANES] + gbias_ref[...]
        bcum = _cumsum_rows(tril_bf, _log_sigmoid(gates))
        rows_t = jnp.where(lane_id < N_HEADS, gates, bcum).T
        for h in range(N_HEADS):
            hc = h * HEAD_DIM
            q = qk_scr[rows, hc:hc + HEAD_DIM]
            k = qk_scr[rows, GROUP_W + hc:GROUP_W + hc + HEAD_DIM]
            v_bf = proj_scr[rows, COL_MV + hc:COL_MV + hc + HEAD_DIM].astype(BF16)
            vaug_bf = jnp.concatenate([v_bf, ones_col], axis=1)
            caug = caug_scr[h]
            m_old = m_scr[h, 0:1, 0:1]
            i_row = rows_t[h:h + 1, :]
            b_row = rows_t[N_HEADS + h:N_HEADS + h + 1, :]
            i_col = gates[:, h:h + 1]
            b_col = bcum[:, N_HEADS + h:N_HEADS + h + 1]
            a_col = b_col + m_old
            dm = (b_col + (i_row - b_row)) + causal_add
            mt = jnp.maximum(a_col, jnp.max(dm, axis=-1, keepdims=True))
            w_intra = jnp.exp(dm - mt)
            w_inter = jnp.exp(a_col - mt)
            s = _dot_nt(q.astype(BF16), k.astype(BF16)) * w_intra
            lhs = jnp.concatenate([s.astype(BF16), (q * w_inter).astype(BF16)], axis=1)
            rhs = jnp.concatenate([vaug_bf, caug.astype(BF16)], axis=0)
            res = _dot(lhs, rhs)
            num = res[:, :HEAD_DIM]
            den = res[:, HEAD_DIM:HEAD_DIM + 1]
            hout = num / jnp.maximum(jnp.abs(den), jnp.exp(-mt))
            b_last = b_col[CHUNK - 1:CHUNK, :]
            g_col = b_last - b_col + i_col
            m_new = jnp.maximum(b_last + m_old, jnp.max(g_col, axis=0, keepdims=True))
            wk = jnp.exp(g_col - m_new)
            wc = jnp.exp(b_last + m_old - m_new)
            caug_scr[h] = caug * wc + _dot_tn((k * wk).astype(BF16), vaug_bf)
            m_scr[h] = jnp.broadcast_to(m_new, (SUBLANES, LANES))
            hn = _head_norm(hout, g_m_ref[:, hc:hc + HEAD_DIM])
            og = _sigmoid(proj_scr[rows, COL_MO + hc:COL_MO + hc + HEAD_DIM])
            zg = _silu(proj_scr[rows, COL_MZ + hc:COL_MZ + hc + HEAD_DIM])
            mix_scr[rows, GROUP_W + hc:GROUP_W + hc + HEAD_DIM] = (hn * og * zg).astype(BF16)
        return carry

    lax.fori_loop(0, n_chunks, chunk_body, 0)

    mix = _dot(mix_scr[...], w_out_ref[...])
    y_ref[0] = _layer_norm(ALPHA * x_ref[0] + mix, ln_g_ref[...], ln_b_ref[...])

    @pl.when(t == pl.num_programs(1) - 1)
    def _write_state():
        s_out_ref[0] = s_scr[...]
        m_out_ref[0] = jnp.zeros((SUBLANES, LANES), F32)
        for h in range(N_HEADS):
            caug = caug_scr[h]
            c_out_ref[0, h] = caug[:, :HEAD_DIM]
            n_out_ref[0, h:h + 1, :] = caug[:, HEAD_DIM:].T[0:1, :]
            m_out_ref[0, h:h + 1, :] = m_scr[h, 0:1, :]


def _const_spec(shape):
    nd = len(shape)
    return pl.BlockSpec(shape, lambda b, t: (0,) * nd)


def _prompt_layer(x, w_in_bf, conv_w, conv_b, gbias, g_ret, g_m, w_out_bf, ln_g, ln_b, tabs):
    B, T, _ = x.shape
    bt = PROMPT_BLOCK_T
    assert T % bt == 0 and bt % CHUNK == 0
    cos_t, sin_t, kcos_t, ksin_t, decay, qd, kd, cdec, tril, mask, e0 = tabs
    row_spec = pl.BlockSpec((bt, LANES), lambda b, t: (t, 0))
    in_specs = [
        pl.BlockSpec((1, bt, D_MODEL), lambda b, t: (b, t, 0)),
        _const_spec(w_in_bf.shape), _const_spec(conv_w.shape), _const_spec(conv_b.shape),
        _const_spec(gbias.shape), _const_spec(g_ret.shape), _const_spec(g_m.shape),
        _const_spec(w_out_bf.shape), _const_spec(ln_g.shape), _const_spec(ln_b.shape),
        row_spec, row_spec, row_spec, row_spec,
        _const_spec(decay.shape), _const_spec(qd.shape), _const_spec(kd.shape),
        pl.BlockSpec(memory_space=pltpu.SMEM),
        _const_spec(tril.shape), _const_spec(mask.shape), _const_spec(e0.shape),
    ]
    out_shape = (
        jax.ShapeDtypeStruct((B, T, D_MODEL), F32),
        jax.ShapeDtypeStruct((B, N_HEADS, HEAD_DIM, HEAD_DIM), F32),
        jax.ShapeDtypeStruct((B, N_HEADS, HEAD_DIM, HEAD_DIM), F32),
        jax.ShapeDtypeStruct((B, N_HEADS, HEAD_DIM), F32),
        jax.ShapeDtypeStruct((B, SUBLANES, LANES), F32),
        jax.ShapeDtypeStruct((B, CONV_W - 1, 2 * GROUP_W), F32),
    )
    out_specs = (
        pl.BlockSpec((1, bt, D_MODEL), lambda b, t: (b, t, 0)),
        pl.BlockSpec((1, N_HEADS, HEAD_DIM, HEAD_DIM), lambda b, t: (b, 0, 0, 0)),
        pl.BlockSpec((1, N_HEADS, HEAD_DIM, HEAD_DIM), lambda b, t: (b, 0, 0, 0)),
        pl.BlockSpec((1, N_HEADS, HEAD_DIM), lambda b, t: (b, 0, 0)),
        pl.BlockSpec((1, SUBLANES, LANES), lambda b, t: (b, 0, 0)),
        pl.BlockSpec((1, CONV_W - 1, 2 * GROUP_W), lambda b, t: (b, 0, 0)),
    )
    scratch = [
        pltpu.VMEM((bt, N_PAD), F32),
        pltpu.VMEM((SUBLANES + bt, 2 * GROUP_W), F32),
        pltpu.VMEM((bt, 2 * GROUP_W), F32),
        pltpu.VMEM((bt, 2 * GROUP_W), BF16),
        pltpu.VMEM((N_HEADS, HEAD_DIM, HEAD_DIM), F32),
        pltpu.VMEM((N_HEADS, HEAD_DIM, 2 * HEAD_DIM), F32),
        pltpu.VMEM((N_HEADS, SUBLANES, LANES), F32),
    ]
    y, s, c, n, m, cv = pl.pallas_call(
        _prompt_kernel,
        grid=(B, T // bt),
        in_specs=in_specs,
        out_specs=out_specs,
        out_shape=out_shape,
        scratch_shapes=scratch,
        compiler_params=pltpu.CompilerParams(
            dimension_semantics=("parallel", "arbitrary"),
            vmem_limit_bytes=VMEM_LIMIT_BYTES),
        name="prompt_layer",
    )(x, w_in_bf, conv_w, conv_b, gbias, g_ret, g_m, w_out_bf, ln_g, ln_b,
      cos_t, sin_t, kcos_t, ksin_t, decay, qd, kd, cdec, tril, mask, e0)
    return y, s, c, n, m[:, :N_HEADS, 0], cv


def _sample_kernel(x_ref, w_in_ref, conv_w_ref, conv_b_ref, gbias_ref, g_ret_ref, g_m_ref,
                   w_out_ref, ln_g_ref, ln_b_ref, rot_ref, gam_ref, esel_ref,
                   s_ref, c_ref, n_ref, m_ref, cv_ref,
                   y_ref, s_out_ref, c_out_ref, n_out_ref, m_out_ref, cv_out_ref,
                   proj_scr, xcur_scr, mix_scr):
    layer = pl.program_id(0)
    j = pl.program_id(1)
    bb = s_ref.shape[1]

    @pl.when(jnp.logical_and(layer == 0, j == 0))
    def _load_x():
        xcur_scr[...] = x_ref[...]

    @pl.when(j == 0)
    def _project():
        xb = xcur_scr[...].astype(BF16)
        for lo in range(0, N_PAD, 512):
            hi = min(lo + 512, N_PAD)
            proj_scr[:, lo:hi] = _dot(xb, w_in_ref[0, :, lo:hi])

    r0 = pl.multiple_of(j * bb, bb)
    rows = pl.ds(r0, bb)
    cos_t = rot_ref[0:1, :]
    sin_t = rot_ref[1:2, :]
    kcos_t = rot_ref[2:3, :]
    ksin_t = rot_ref[3:4, :]
    esel = esel_ref[...]
    row_id = lax.broadcasted_iota(jnp.int32, (bb, bb * HEAD_DIM), 0)
    blk_id = jnp.right_shift(lax.broadcasted_iota(jnp.int32, (bb, bb * HEAD_DIM), 1), 7)
    pad_rows = jnp.zeros((LANES - bb, HEAD_DIM), F32)
    pad_wide = jnp.zeros((LANES - bb, bb * HEAD_DIM), BF16)

    def col_form(x8):
        return jnp.concatenate([x8, pad_rows], axis=0).T.astype(BF16)

    def outer_all(k8, v8):
        vt = jnp.concatenate([v8] * bb, axis=1)
        vsel = jnp.where(row_id == blk_id, vt, 0.0).astype(BF16)
        return _dot(col_form(k8), jnp.concatenate([vsel, pad_wide], axis=0))

    def col_bcast_all(q8):
        return _dot(col_form(q8), esel)

    for h in range(N_HEADS):
        hc = h * HEAD_DIM
        q8 = _rotary(proj_scr[rows, COL_RQ + hc:COL_RQ + hc + HEAD_DIM], cos_t, sin_t)
        k8 = _rotary(proj_scr[rows, COL_RK + hc:COL_RK + hc + HEAD_DIM], kcos_t, ksin_t)
        v8 = proj_scr[rows, COL_RV + hc:COL_RV + hc + HEAD_DIM]
        kv_all = outer_all(k8, v8)
        qc_all = col_bcast_all(q8)
        gamma = gam_ref[h]
        o_rows = []
        for r in range(bb):
            blk = slice(r * HEAD_DIM, (r + 1) * HEAD_DIM)
            s_new = s_ref[0, r, h] * gamma + kv_all[:, blk]
            s_out_ref[0, r, h] = s_new
            o_rows.append(jnp.sum(qc_all[:, blk] * s_new, axis=0, keepdims=True))
        o8 = _head_norm(jnp.concatenate(o_rows, axis=0), g_ret_ref[0, :, hc:hc + HEAD_DIM])
        z8 = proj_scr[rows, COL_RZ + hc:COL_RZ + hc + HEAD_DIM]
        mix_scr[rows, hc:hc + HEAD_DIM] = o8 * _silu(z8)

    u8 =proj_scr[rows, COL_MQK:COL_MQK + 2 * GROUP_W]
    acc = conv_b_ref[0] + u8 * conv_w_ref[0, CONV_W - 1:CONV_W, :]
    for jj in range(CONV_W - 1):
        acc = acc + cv_ref[0, :, jj, :] * conv_w_ref[0, jj:jj + 1, :]
    for jj in range(1, CONV_W - 1):
        cv_out_ref[0, :, jj - 1, :] = cv_ref[0, :, jj, :]
    cv_out_ref[0, :, CONV_W - 2, :] = u8
    qk8 = _silu(acc)

    gates = proj_scr[rows, COL_GATE:COL_GATE + LANES] + gbias_ref[0]
    i_al = pltpu.roll(gates, N_HEADS, axis=1)
    bm = _log_sigmoid(gates) + m_ref[0]
    m_new = jnp.maximum(bm, i_al)
    wk = jnp.exp(i_al - m_new)
    wc = jnp.exp(bm - m_new)
    einv = jnp.exp(-m_new)
    m_out_ref[0] = m_new
    for h in range(N_HEADS):
        hc = h * HEAD_DIM
        gl = N_HEADS + h
        q8 = qk8[:, hc:hc + HEAD_DIM]
        k8 = qk8[:, GROUP_W + hc:GROUP_W + hc + HEAD_DIM] * QK_SCALE
        v8 = proj_scr[rows, COL_MV + hc:COL_MV + hc + HEAD_DIM]
        wk_h = wk[:, gl:gl + 1]
        wc_h = jnp.broadcast_to(wc[:, gl:gl + 1], (bb, HEAD_DIM))
        kw8 = k8 * wk_h
        kv_all = outer_all(kw8, v8)
        qc_all = col_bcast_all(q8)
        n_new = n_ref[0, :, h, :] * wc_h + kw8
        n_out_ref[0, :, h, :] = n_new
        num_rows = []
        for r in range(bb):
            blk = slice(r * HEAD_DIM, (r + 1) * HEAD_DIM)
            c_new = c_ref[0, r, h] * wc_h[r:r + 1, :] + kv_all[:, blk]
            c_out_ref[0, r, h] = c_new
            num_rows.append(jnp.sum(qc_all[:, blk] * c_new, axis=0, keepdims=True))
        num = jnp.concatenate(num_rows, axis=0)
        q_bf = q8.astype(BF16).astype(F32)
        den = jnp.sum(q_bf * n_new, axis=-1, keepdims=True)
        hout = num / jnp.maximum(jnp.abs(den), einv[:, gl:gl + 1])
        hn = _head_norm(hout, g_m_ref[0, :, hc:hc + HEAD_DIM])
        og = _sigmoid(proj_scr[rows, COL_MO + hc:COL_MO + hc + HEAD_DIM])
        zg = _silu(proj_scr[rows, COL_MZ + hc:COL_MZ + hc + HEAD_DIM])
        mix_scr[rows, GROUP_W + hc:GROUP_W + hc + HEAD_DIM] = hn * og * zg

    @pl.when(j == pl.num_programs(1) - 1)
    def _finish_layer():
        mix = _dot(mix_scr[...].astype(BF16), w_out_ref[0])
        y = _layer_norm(ALPHA * xcur_scr[...] + mix, ln_g_ref[0], ln_b_ref[0])
        xcur_scr[...] = y
        y_ref[...] = y


def _sample_layers(x, w_in_bf, conv_w, conv_b, gbias, g_ret, g_m, w_out_bf, ln_g, ln_b,
                   rot, gam, esel, state_ret, state_c, state_n, m_pad, state_conv):
    bs = x.shape[0]
    bb = SAMPLE_BLOCK_B
    assert bs % bb == 0
    nb = bs // bb

    def lspec(shape):
        nd = len(shape)
        return pl.BlockSpec((1,) + tuple(shape[1:]), lambda l, j: (l,) + (0,) * (nd - 1))

    def cspec(shape):
        nd = len(shape)
        return pl.BlockSpec(tuple(shape), lambda l, j: (0,) * nd)

    mat_spec = pl.BlockSpec((1, bb, N_HEADS, HEAD_DIM, HEAD_DIM), lambda l, j: (l, j, 0, 0, 0))
    n_spec = pl.BlockSpec((1, bb, N_HEADS, HEAD_DIM), lambda l, j: (l, j, 0, 0))
    m_spec = pl.BlockSpec((1, bb, LANES), lambda l, j: (l, j, 0))
    cv_spec = pl.BlockSpec((1, bb, CONV_W - 1, 2 * GROUP_W), lambda l, j: (l, j, 0, 0))
    in_specs = [
        cspec(x.shape), lspec(w_in_bf.shape), lspec(conv_w.shape), lspec(conv_b.shape),
        lspec(gbias.shape), lspec(g_ret.shape), lspec(g_m.shape), lspec(w_out_bf.shape),
        lspec(ln_g.shape), lspec(ln_b.shape), cspec(rot.shape),
        pl.BlockSpec(memory_space=pltpu.SMEM), cspec(esel.shape),
        mat_spec, mat_spec, n_spec, m_spec, cv_spec,
    ]
    out_shape = (
        jax.ShapeDtypeStruct(x.shape, F32),
        jax.ShapeDtypeStruct(state_ret.shape, F32),
        jax.ShapeDtypeStruct(state_c.shape, F32),
        jax.ShapeDtypeStruct(state_n.shape, F32),
        jax.ShapeDtypeStruct(m_pad.shape, F32),
        jax.ShapeDtypeStruct(state_conv.shape, F32),
    )
    out_specs = (cspec(x.shape), mat_spec, mat_spec, n_spec, m_spec, cv_spec)
    scratch = [
        pltpu.VMEM((bs, N_PAD), F32),
        pltpu.VMEM((bs, D_MODEL), F32),
        pltpu.VMEM((bs, 2 * GROUP_W), F32),
    ]
    return pl.pallas_call(
        _sample_kernel,
        grid=(DEPTH, nb),
        in_specs=in_specs,
        out_specs=out_specs,
        out_shape=out_shape,
        scratch_shapes=scratch,
        compiler_params=pltpu.CompilerParams(
            dimension_semantics=("arbitrary", "arbitrary"),
            vmem_limit_bytes=VMEM_LIMIT_BYTES),
        name="sample_layers",
    )(x, w_in_bf, conv_w, conv_b, gbias, g_ret, g_m, w_out_bf, ln_g, ln_b,
      rot, gam, esel, state_ret, state_c, state_n, m_pad, state_conv)


def _rotary_tables(pos):
    half = HEAD_DIM // 2
    inv = ROPE_BASE ** (-jnp.arange(half, dtype=F32) / half)
    ang = pos.astype(F32)[:, None] * inv[None, :]
    cos = jnp.cos(ang)
    sin = jnp.sin(ang)
    cos_t = jnp.concatenate([cos, cos], axis=-1)
    sin_t = jnp.concatenate([-sin, sin], axis=-1)
    return cos_t, sin_t, cos_t * QK_SCALE, sin_t * QK_SCALE


def _retention_tables():
    L = CHUNK
    log_gamma = jnp.log(1.0 - 2.0 ** (-5.0 - jnp.arange(N_HEADS, dtype=F32)))
    idx = jnp.arange(L, dtype=F32)
    diff = idx[:, None] - idx[None, :]
    decay = jnp.exp(log_gamma[:, None, None] * jnp.maximum(diff, 0.0)) * (diff >= 0)
    q_decay = jnp.exp(log_gamma[:, None] * (idx + 1.0))
    k_decay = jnp.exp(log_gamma[:, None] * (L - 1.0 - idx))
    c_decay = jnp.exp(log_gamma * L)
    qd = jnp.broadcast_to(q_decay[:, :, None], (N_HEADS, L, HEAD_DIM))
    kd = jnp.broadcast_to(k_decay[:, :, None], (N_HEADS, L, HEAD_DIM))
    gamma1 = jnp.exp(log_gamma * 1.0)
    return decay.astype(F32), qd, kd, c_decay, gamma1


def kernel(x_prompt, x_sample, state_ret, state_mlstm_C, state_mlstm_n, state_mlstm_m, state_conv,
           w_in, conv_w, conv_b, b_i, b_f, g_ret, g_m, w_out, ln_g, ln_b):
    B, T, _ = x_prompt.shape
    Bs, Ts, _ = x_sample.shape
    assert Ts == 1

    w_in_bf = jnp.pad(w_in, ((0, 0), (0, 0), (0, N_PAD - N_IN))).astype(BF16)
    w_out_bf = w_out.astype(BF16)
    gbias = jnp.pad(jnp.concatenate([b_i, b_f], axis=-1), ((0, 0), (0, LANES - 2 * N_HEADS)))
    gbias = gbias.reshape(DEPTH, 1, LANES)
    conv_b3 = conv_b.reshape(DEPTH, 1, 2 * GROUP_W)
    g_ret3 = g_ret.reshape(DEPTH, 1, GROUP_W)
    g_m3 = g_m.reshape(DEPTH, 1, GROUP_W)
    ln_g3 = ln_g.reshape(DEPTH, 1, D_MODEL)
    ln_b3 = ln_b.reshape(DEPTH, 1, D_MODEL)

    decay, qd, kd, c_decay, gamma1 = _retention_tables()
    idx = jnp.arange(CHUNK)
    causal = idx[:, None] >= idx[None, :]
    tril = causal.astype(BF16)
    mask_add = jnp.where(causal, 0.0, -jnp.inf).astype(F32)
    e0 = (jnp.arange(LANES)[None, :] == 0).astype(BF16) * jnp.ones((CHUNK, 1), BF16)
    tabs_p = _rotary_tables(jnp.arange(T, dtype=jnp.int32)) + (decay, qd, kd, c_decay, tril, mask_add, e0)

    xp = x_prompt
    rp, cp, np_, mp, vp = [], [], [], [], []
    for l in range(DEPTH):
        xp, s, c, n, m, cv = _prompt_layer(
            xp, w_in_bf[l], conv_w[l], conv_b3[l], gbias[l], g_ret3[l], g_m3[l],
            w_out_bf[l], ln_g3[l], ln_b3[l], tabs_p)
        rp.append(s); cp.append(c); np_.append(n); mp.append(m); vp.append(cv)

    rot = jnp.concatenate(_rotary_tables(PAST_LEN + jnp.arange(Ts, dtype=jnp.int32)), axis=0)
    bb = SAMPLE_BLOCK_B
    esel = (jnp.arange(LANES)[:, None] == (jnp.arange(bb * HEAD_DIM)[None, :] // HEAD_DIM)).astype(BF16)
    m_pad = jnp.pad(state_mlstm_m, ((0, 0), (0, 0), (N_HEADS, LANES - 2 * N_HEADS)))
    ys, rs, cs, ns, ms_pad, vs = _sample_layers(
        x_sample.reshape(Bs, D_MODEL), w_in_bf, conv_w, conv_b3, gbias, g_ret3, g_m3, w_out_bf,
        ln_g3, ln_b3, rot, gamma1, esel, state_ret, state_mlstm_C, state_mlstm_n, m_pad, state_conv)
    ms = ms_pad[:, :, N_HEADS:2 * N_HEADS]

    return (xp, ys.reshape(Bs, Ts, D_MODEL),
            jnp.stack(rp), jnp.stack(cp), jnp.stack(np_), jnp.stack(mp), jnp.stack(vp),
            rs, cs, ns, ms, vs)
```

```python
import functools

import jax
import jax.numpy as jnp
from jax import lax
from jax.experimental import pallas as pl
from jax.experimental.pallas import tpu as pltpu

F32 = jnp.float32
BF16 = jnp.bfloat16

D_MODEL = 1024
DEPTH = 2
PAST_LEN = 16384
N_HEADS = 4
HEAD_DIM = 128
GROUP_W = N_HEADS * HEAD_DIM
CONV_W = 4
CHUNK = 128
ROPE_BASE = 10000.0
LN_EPS = 1e-5
GN_EPS = 1e-5
ALPHA = (2 * DEPTH) ** 0.25
QK_SCALE = HEAD_DIM ** -0.5

LANES = 128
SUBLANES = 8

COL_RQ = 0
COL_RK = COL_RQ + GROUP_W
COL_RV = COL_RK + GROUP_W
COL_RZ = COL_RV + GROUP_W
COL_MQK = COL_RZ + GROUP_W
COL_MV = COL_MQK + 2 * GROUP_W
COL_MO = COL_MV + GROUP_W
COL_MZ = COL_MO + GROUP_W
COL_GATE = COL_MZ + GROUP_W
N_IN = COL_GATE + 2 * N_HEADS
N_PAD = COL_GATE + LANES

PROMPT_BLOCK_T = 256
PROJ_PIECE_COLS = 256
SAMPLE_BLOCK_B = 8
VMEM_LIMIT_BYTES = 56 * 1024 * 1024


def _sigmoid(x):
    return 1.0 / (1.0 + jnp.exp(-x))


def _silu(x):
    return x * _sigmoid(x)


def _log_sigmoid(x):
    return jnp.minimum(x, 0.0) - jnp.log1p(jnp.exp(-jnp.abs(x)))


def _dot(a, b):
    return jnp.dot(a, b, preferred_element_type=F32)


def _dot_nt(a, b):
    return lax.dot_general(a, b, (((1,), (1,)), ((), ())), preferred_element_type=F32)


def _dot_tn(a, b):
    return lax.dot_general(a, b, (((0,), (0,)), ((), ())), preferred_element_type=F32)


def _rotary(x, cos_t, sin_t):
    return x * cos_t + pltpu.roll(x, HEAD_DIM // 2, axis=1) * sin_t


def _head_norm(h, g):
    mu = jnp.mean(h, axis=-1, keepdims=True)
    d = h - mu
    var = jnp.mean(d * d, axis=-1, keepdims=True)
    return d * lax.rsqrt(var + GN_EPS) * g


def _layer_norm(x, g, b):
    mu = jnp.mean(x, axis=-1, keepdims=True)
    d = x - mu
    var = jnp.mean(d * d, axis=-1, keepdims=True)
    return d * lax.rsqrt(var + LN_EPS) * g + b


def _cumsum_rows(tril_bf, x):
    hi = x.astype(BF16)
    r1 = x - hi.astype(F32)
    mid = r1.astype(BF16)
    lo = (r1 - mid.astype(F32)).astype(BF16)
    return _dot(tril_bf, hi) + _dot(tril_bf, mid) + _dot(tril_bf, lo)


def _prompt_kernel(nt, xn_ref, x_ref, w_in_ref, conv_w_ref, conv_b_ref, gbias_ref, g_ret_ref, g_m_ref,
                   w_out_ref, ln_g_ref, ln_b_ref, cos_ref, sin_ref, kcos_ref, ksin_ref,
                   decay_ref, qd_ref, kd_ref, cdec_ref, tril_ref, mask_ref, e0_ref,
                   y_ref, s_out_ref, c_out_ref, n_out_ref, m_out_ref, conv_out_ref,
                   proj_a_scr, proj_b_scr, u_scr, qk_scr, mix_scr, s_scr, caug_scr, m_scr):
    g = pl.program_id(0)
    t = lax.rem(jnp.maximum(g - 1, 0), nt)
    block_t = x_ref.shape[1]
    n_chunks = block_t // CHUNK
    carry_rows = CONV_W - 1
    heads = range(N_HEADS)

    @pl.when(g == 0)
    def _init_pipeline():
        proj_b_scr[...] = jnp.zeros_like(proj_b_scr)

    @pl.when(t == 0)
    def _init_state():
        s_scr[...] = jnp.zeros_like(s_scr)
        caug_scr[...] = jnp.zeros_like(caug_scr)
        m_scr[...] = jnp.zeros_like(m_scr)
        u_scr[0:SUBLANES, :] = jnp.zeros((SUBLANES, 2 * GROUP_W), F32)

    def hcols(base, h):
        return slice(base + h * HEAD_DIM, base + (h + 1) * HEAD_DIM)

    def step_body(proj_in, proj_cur):
        xb = xn_ref[0].astype(BF16)
        piece_starts = list(range(0, N_PAD, PROJ_PIECE_COLS))
        n_pieces = len(piece_starts)
        n_slots = 10 * n_chunks
        slots_done = [0]

        def emit_proj_pieces():
            slots_done[0] += 1
            target = min(n_pieces, (slots_done[0] * n_pieces) // n_slots)
            while n_pieces - len(piece_starts) < target:
                lo = piece_starts.pop(0)
                hi = min(lo + PROJ_PIECE_COLS, N_PAD)
                proj_in[:, lo:hi] = _dot(xb, w_in_ref[:, lo:hi])

        tril_bf = tril_ref[...]
        causal_add = mask_ref[...]
        ones_col = e0_ref[...]
        lane_id = lax.broadcasted_iota(jnp.int32, (CHUNK, LANES), 1)

        for c in range(n_chunks):
            rows = slice(c * CHUNK, (c + 1) * CHUNK)
            urows = slice(SUBLANES + c * CHUNK, SUBLANES + (c + 1) * CHUNK)
            cos_t = cos_ref[rows, :]
            sin_t = sin_ref[rows, :]
            kcos_t = kcos_ref[rows, :]
            ksin_t = ksin_ref[rows, :]

            emit_proj_pieces()
            q = [_rotary(proj_cur[rows, hcols(COL_RQ, h)], cos_t, sin_t) for h in heads]
            k = [_rotary(proj_cur[rows, hcols(COL_RK, h)], kcos_t, ksin_t) for h in heads]
            v_bf = [proj_cur[rows, hcols(COL_RV, h)].astype(BF16) for h in heads]
            sc = [_dot_nt(q[h].astype(BF16), k[h].astype(BF16)) for h in heads]
            emit_proj_pieces()
            state = [s_scr[h] for h in heads]
            upd = [_dot_tn((k[h] * kd_ref[h]).astype(BF16), v_bf[h]) for h in heads]
            for h in heads:
                s_scr[h] = state[h] * cdec_ref[h] + upd[h]
            emit_proj_pieces()
            lhs = [jnp.concatenate([(sc[h] * decay_ref[h]).astype(BF16),
                                    (q[h] * qd_ref[h]).astype(BF16)], axis=1) for h in heads]
            rhs = [jnp.concatenate([v_bf[h], state[h].astype(BF16)], axis=0) for h in heads]
            o = [_dot(lhs[h], rhs[h]) for h in heads]
            emit_proj_pieces()
            mu = [jnp.mean(o[h], axis=-1, keepdims=True) for h in heads]
            d = [o[h] - mu[h] for h in heads]
            var = [jnp.mean(d[h] * d[h], axis=-1, keepdims=True) for h in heads]
            zg = [_silu(proj_cur[rows, hcols(COL_RZ, h)]) for h in heads]
            for h in heads:
                hn = d[h] * lax.rsqrt(var[h] + GN_EPS) * g_ret_ref[:, hcols(0, h)]
                mix_scr[rows, hcols(0, h)] = (hn * zg[h]).astype(BF16)

            emit_proj_pieces()
            for cs in range(0, 2 * GROUP_W, LANES):
                cols = slice(cs, cs + LANES)
                u_scr[urows, cols] = proj_cur[rows, COL_MQK + cs:COL_MQK + cs + LANES]
            for cs in range(0, 2 * GROUP_W, LANES):
                cols = slice(cs, cs + LANES)
                acc = conv_b_ref[:, cols]
                for j in range(CONV_W):
                    r0 = SUBLANES + c * CHUNK - carry_rows + j
                    acc = acc + u_scr[r0:r0 + CHUNK, cols] * conv_w_ref[j:j + 1, cols]
                act = _silu(acc)
                if cs >= GROUP_W:
                    act = act * QK_SCALE
                qk_scr[rows, cols] = act

            emit_proj_pieces()
            gates = proj_cur[rows, COL_GATE:COL_GATE + LANES] + gbias_ref[...]
            bcum = _cumsum_rows(tril_bf, _log_sigmoid(gates))
            rows_t = jnp.where(lane_id < N_HEADS, gates, bcum).T
            q = [qk_scr[rows, hcols(0, h)] for h in heads]
            k = [qk_scr[rows, hcols(GROUP_W, h)] for h in heads]
            vaug_bf = [jnp.concatenate([proj_cur[rows, hcols(COL_MV, h)].astype(BF16), ones_col], axis=1)
                       for h in heads]
            qk = [_dot_nt(q[h].astype(BF16), k[h].astype(BF16)) for h in heads]
            caug = [caug_scr[h] for h in heads]
            m_old = [m_scr[h, 0:1, 0:1] for h in heads]
            i_row = [rows_t[h:h + 1, :] for h in heads]
            b_row = [rows_t[N_HEADS + h:N_HEADS + h + 1, :] for h in heads]
            i_col = [gates[:, h:h + 1] for h in heads]
            b_col = [bcum[:, N_HEADS + h:N_HEADS + h + 1] for h in heads]
            a_col = [b_col[h] + m_old[h] for h in heads]
            dm = [(b_col[h] + (i_row[h] - b_row[h])) + causal_add for h in heads]
            mt = [jnp.maximum(a_col[h], jnp.max(dm[h], axis=-1, keepdims=True)) for h in heads]
            emit_proj_pieces()
            w_intra = [jnp.exp(dm[h] - mt[h]) for h in heads]
            w_inter = [jnp.exp(a_col[h] - mt[h]) for h in heads]
            lhs = [jnp.concatenate([(qk[h] * w_intra[h]).astype(BF16),
                                    (q[h] * w_inter[h]).astype(BF16)], axis=1) for h in heads]
            rhs = [jnp.concatenate([vaug_bf[h], caug[h].astype(BF16)], axis=0) for h in heads]
            res = [_dot(lhs[h], rhs[h]) for h in heads]
            emit_proj_pieces()
            b_last = [b_col[h][CHUNK - 1:CHUNK, :] for h in heads]
            g_col = [b_last[h] - b_col[h] + i_col[h] for h in heads]
            m_new = [jnp.maximum(b_last[h] + m_old[h], jnp.max(g_col[h], axis=0, keepdims=True))
                     for h in heads]
            wk = [jnp.exp(g_col[h] - m_new[h]) for h in heads]
            wc = [jnp.exp(b_last[h] + m_old[h] - m_new[h]) for h in heads]
            upd = [_dot_tn((k[h] * wk[h]).astype(BF16), vaug_bf[h]) for h in heads]
            for h in heads:
                caug_scr[h] = caug[h] * wc[h] + upd[h]
                m_scr[h] = jnp.broadcast_to(m_new[h], (SUBLANES, LANES))
            emit_proj_pieces()
            hout = [res[h][:, :HEAD_DIM]
                    / jnp.maximum(jnp.abs(res[h][:, HEAD_DIM:HEAD_DIM + 1]), jnp.exp(-mt[h]))
                    for h in heads]
            mu = [jnp.mean(hout[h], axis=-1, keepdims=True) for h in heads]
            d = [hout[h] - mu[h] for h in heads]
            var = [jnp.mean(d[h] * d[h], axis=-1, keepdims=True) for h in heads]
            og = [_sigmoid(proj_cur[rows, hcols(COL_MO, h)]) for h in heads]
            zg = [_silu(proj_cur[rows, hcols(COL_MZ, h)]) for h in heads]
            for h in heads:
                hn = d[h] * lax.rsqrt(var[h] + GN_EPS) * g_m_ref[:, hcols(0, h)]
                mix_scr[rows, hcols(GROUP_W, h)] = (hn * og[h] * zg[h]).astype(BF16)

            emit_proj_pieces()
            mix = _dot(mix_scr[rows, :], w_out_ref[...])
            y_ref[0, rows, :] = _layer_norm(ALPHA * x_ref[0, rows, :] + mix, ln_g_ref[...], ln_b_ref[...])

        assert not piece_starts

    parity = lax.rem(g, 2)

    @pl.when(parity == 0)
    def _even_step():
        step_body(proj_a_scr, proj_b_scr)

    @pl.when(parity == 1)
    def _odd_step():
        step_body(proj_b_scr, proj_a_scr)

    @pl.when(jnp.logical_and(t == nt - 1, g > 0))
    def _write_state():
        conv_out_ref[0] = u_scr[SUBLANES + block_t - carry_rows:SUBLANES + block_t, :]
        s_out_ref[0] = s_scr[...]
        m_out_ref[0] = jnp.zeros((SUBLANES, LANES), F32)
        for h in heads:
            caug = caug_scr[h]
            c_out_ref[0, h] = caug[:, :HEAD_DIM]
            n_out_ref[0, h:h + 1, :] = caug[:, HEAD_DIM:].T[0:1, :]
            m_out_ref[0, h:h + 1, :] = m_scr[h, 0:1, :]

    u_scr[0:SUBLANES, :] = u_scr[block_t:block_t + SUBLANES, :]


def _const_spec(shape):
    nd = len(shape)
    return pl.BlockSpec(shape, lambda g: (0,) * nd)


def _prompt_layer(x, w_in_bf, conv_w, conv_b, gbias, g_ret, g_m, w_out_bf, ln_g, ln_b, tabs):
    B, T, _ = x.shape
    bt = PROMPT_BLOCK_T
    assert T % bt == 0 and bt % CHUNK == 0
    nt = T // bt
    n_blocks = B * nt
    cos_t, sin_t, kcos_t, ksin_t, decay, qd, kd, cdec, tril, mask, e0 = tabs

    def nxt(g):
        return jnp.minimum(g, n_blocks - 1)

    def cur(g):
        return jnp.maximum(g - 1, 0)

    row_spec = pl.BlockSpec((bt, LANES), lambda g: (cur(g) % nt, 0))
    in_specs = [
        pl.BlockSpec((1, bt, D_MODEL), lambda g: (nxt(g) // nt, nxt(g) % nt, 0)),
        pl.BlockSpec((1, bt, D_MODEL), lambda g: (cur(g) // nt, cur(g) % nt, 0)),
        _const_spec(w_in_bf.shape), _const_spec(conv_w.shape), _const_spec(conv_b.shape),
        _const_spec(gbias.shape), _const_spec(g_ret.shape), _const_spec(g_m.shape),
        _const_spec(w_out_bf.shape), _const_spec(ln_g.shape), _const_spec(ln_b.shape),
        row_spec, row_spec, row_spec, row_spec,
        _const_spec(decay.shape), _const_spec(qd.shape), _const_spec(kd.shape),
        pl.BlockSpec(memory_space=pltpu.SMEM),
        _const_spec(tril.shape), _const_spec(mask.shape), _const_spec(e0.shape),
    ]
    out_shape = (
        jax.ShapeDtypeStruct((B, T, D_MODEL), F32),
        jax.ShapeDtypeStruct((B, N_HEADS, HEAD_DIM, HEAD_DIM), F32),
        jax.ShapeDtypeStruct((B, N_HEADS, HEAD_DIM, HEAD_DIM), F32),
        jax.ShapeDtypeStruct((B, N_HEADS, HEAD_DIM), F32),
        jax.ShapeDtypeStruct((B, SUBLANES, LANES), F32),
        jax.ShapeDtypeStruct((B, CONV_W - 1, 2 * GROUP_W), F32),
    )
    out_specs = (
        pl.BlockSpec((1, bt, D_MODEL), lambda g: (cur(g) // nt, cur(g) % nt, 0)),
        pl.BlockSpec((1, N_HEADS, HEAD_DIM, HEAD_DIM), lambda g: (cur(g) // nt, 0, 0, 0)),
        pl.BlockSpec((1, N_HEADS, HEAD_DIM, HEAD_DIM), lambda g: (cur(g) // nt, 0, 0, 0)),
        pl.BlockSpec((1, N_HEADS, HEAD_DIM), lambda g: (cur(g) // nt, 0, 0)),
        pl.BlockSpec((1, SUBLANES, LANES), lambda g: (cur(g) // nt, 0, 0)),
        pl.BlockSpec((1, CONV_W - 1, 2 * GROUP_W), lambda g: (cur(g) // nt, 0, 0)),
    )
    scratch = [
        pltpu.VMEM((bt, N_PAD), F32),
        pltpu.VMEM((bt, N_PAD), F32),
        pltpu.VMEM((SUBLANES + bt, 2 * GROUP_W), F32),
        pltpu.VMEM((bt, 2 * GROUP_W), F32),
        pltpu.VMEM((bt, 2 * GROUP_W), BF16),
        pltpu.VMEM((N_HEADS, HEAD_DIM, HEAD_DIM), F32),
        pltpu.VMEM((N_HEADS, HEAD_DIM, 2 * HEAD_DIM), F32),
        pltpu.VMEM((N_HEADS, SUBLANES, LANES), F32),
    ]
    y, s, c, n, m, cv = pl.pallas_call(
        functools.partial(_prompt_kernel, nt),
        grid=(n_blocks + 1,),
        in_specs=in_specs,
        out_specs=out_specs,
        out_shape=out_shape,
        scratch_shapes=scratch,
        compiler_params=pltpu.CompilerParams(
            dimension_semantics=("arbitrary",),
            vmem_limit_bytes=VMEM_LIMIT_BYTES),
        name="prompt_layer",
    )(x, x, w_in_bf, conv_w, conv_b, gbias, g_ret, g_m, w_out_bf, ln_g, ln_b,
      cos_t, sin_t, kcos_t, ksin_t, decay, qd, kd, cdec, tril, mask, e0)
    return y, s, c, n, m[:, :N_HEADS, 0], cv


def _sample_kernel(x_ref, w_in_ref, conv_w_ref, conv_b_ref, gbias_ref, g_ret_ref, g_m_ref,
                   w_out_ref, ln_g_ref, ln_b_ref, rot_ref, gam_ref, esel_ref,
                   s_ref, c_ref, n_ref, m_ref, cv_ref,
                   y_ref, s_out_ref, c_out_ref, n_out_ref, m_out_ref, cv_out_ref,
                   proj_scr, xcur_scr, mix_scr):
    layer = pl.program_id(0)
    j = pl.program_id(1)
    bb = s_ref.shape[1]

    @pl.when(jnp.logical_and(layer == 0, j == 0))
    def _load_x():
        xcur_scr[...] = x_ref[...]

    @pl.when(j == 0)
    def _project():
        xb = xcur_scr[...].astype(BF16)
        for lo in range(0, N_PAD, 512):
            hi = min(lo + 512, N_PAD)
            proj_scr[:, lo:hi] = _dot(xb, w_in_ref[0, :, lo:hi])

    r0 = pl.multiple_of(j * bb, bb)
    rows = pl.ds(r0, bb)
    cos_t = rot_ref[0:1, :]
    sin_t = rot_ref[1:2, :]
    kcos_t = rot_ref[2:3, :]
    ksin_t = rot_ref[3:4, :]
    esel = esel_ref[...]
    row_id = lax.broadcasted_iota(jnp.int32, (bb, bb * HEAD_DIM), 0)
    blk_id = jnp.right_shift(lax.broadcasted_iota(jnp.int32, (bb, bb * HEAD_DIM), 1), 7)
    pad_rows = jnp.zeros((LANES - bb, HEAD_DIM), F32)
    pad_wide = jnp.zeros((LANES - bb, bb * HEAD_DIM), BF16)

    def col_form(x8):
        return jnp.concatenate([x8, pad_rows], axis=0).T.astype(BF16)

    def outer_all(k8, v8):
        vt = jnp.concatenate([v8] * bb, axis=1)
        vsel = jnp.where(row_id == blk_id, vt, 0.0).astype(BF16)
        return _dot(col_form(k8), jnp.concatenate([vsel, pad_wide], axis=0))

    def col_bcast_all(q8):
        return _dot(col_form(q8), esel)

    for h in range(N_HEADS):
        hc = h * HEAD_DIM
        q8 = _rotary(proj_scr[rows, COL_RQ + hc:COL_RQ + hc + HEAD_DIM], cos_t, sin_t)
        k8 = _rotary(proj_scr[rows, COL_RK + hc:COL_RK + hc + HEAD_DIM], kcos_t, ksin_t)
        v8 = proj_scr[rows, COL_RV + hc:COL_RV + hc + HEAD_DIM]
        kv_all = outer_all(k8, v8)
        qc_all = col_bcast_all(q8)
        gamma = gam_ref[h]
        o_rows = []
        for r in range(bb):
            blk = slice(r * HEAD_DIM, (r + 1) * HEAD_DIM)
            s_new = s_ref[0, r, h] * gamma + kv_all[:, blk]
            s_out_ref[0, r, h] = s_new
            o_rows.append(jnp.sum(qc_all[:, blk] * s_new, axis=0, keepdims=True))
        o8 = _head_norm(jnp.concatenate(o_rows, axis=0), g_ret_ref[0, :, hc:hc + HEAD_DIM])
        z8 = proj_scr[rows, COL_RZ + hc:COL_RZ + hc + HEAD_DIM]
        mix_scr[rows, hc:hc + HEAD_DIM] = o8 * _silu(z8)

    u8 = proj_scr[rows, COL_MQK:COL_MQK + 2 * GROUP_W]
    acc = conv_b_ref[0] + u8 * conv_w_ref[0, CONV_W - 1:CONV_W, :]
    for jj in range(CONV_W - 1):
        acc = acc + cv_ref[0, :, jj, :] * conv_w_ref[0, jj:jj + 1, :]
    for jj in range(1, CONV_W - 1):
        cv_out_ref[0, :, jj - 1, :] = cv_ref[0, :, jj, :]
    cv_out_ref[0, :, CONV_W - 2, :] = u8
    qk8 = _silu(acc)

    gates = proj_scr[rows, COL_GATE:COL_GATE + LANES] + gbias_ref[0]
    i_al = pltpu.roll(gates, N_HEADS, axis=1)
    bm = _log_sigmoid(gates) + m_ref[0]
    m_new = jnp.maximum(bm, i_al)
    wk = jnp.exp(i_al - m_new)
    wc = jnp.exp(bm - m_new)
    einv = jnp.exp(-m_new)
    m_out_ref[0] = m_new
    for h in range(N_HEADS):
        hc = h * HEAD_DIM
        gl = N_HEADS + h
        q8 = qk8[:, hc:hc + HEAD_DIM]
        k8 = qk8[:, GROUP_W + hc:GROUP_W + hc + HEAD_DIM] * QK_SCALE
        v8 = proj_scr[rows, COL_MV + hc:COL_MV + hc + HEAD_DIM]
        wk_h = wk[:, gl:gl + 1]
        wc_h = jnp.broadcast_to(wc[:, gl:gl + 1], (bb, HEAD_DIM))
        kw8 = k8 * wk_h
        kv_all = outer_all(kw8, v8)
        qc_all = col_bcast_all(q8)
        n_new = n_ref[0, :, h, :] * wc_h + kw8
        n_out_ref[0, :, h, :] = n_new
        num_rows = []
        for r in range(bb):
            blk = slice(r * HEAD_DIM, (r + 1) * HEAD_DIM)
            c_new = c_ref[0, r, h] * wc_h[r:r + 1, :] + kv_all[:, blk]
            c_out_ref[0, r, h] = c_new
            num_rows.append(jnp.sum(qc_all[:, blk] * c_new, axis=0, keepdims=True))
        num = jnp.concatenate(num_rows, axis=0)
        q_bf = q8.astype(BF16).astype(F32)
        den = jnp.sum(q_bf * n_new, axis=-1, keepdims=True)
        hout = num / jnp.maximum(jnp.abs(den), einv[:, gl:gl + 1])
        hn = _head_norm(hout, g_m_ref[0, :, hc:hc + HEAD_DIM])
        og = _sigmoid(proj_scr[rows, COL_MO + hc:COL_MO + hc + HEAD_DIM])
        zg = _silu(proj_scr[rows, COL_MZ + hc:COL_MZ + hc + HEAD_DIM])
        mix_scr[rows, GROUP_W + hc:GROUP_W + hc + HEAD_DIM] = hn * og * zg

    @pl.when(j == pl.num_programs(1) - 1)
    def _finish_layer():
        mix = _dot(mix_scr[...].astype(BF16), w_out_ref[0])
        y = _layer_norm(ALPHA * xcur_scr[...] + mix, ln_g_ref[0], ln_b_ref[0])
        xcur_scr[...] = y
        y_ref[...] = y


def _sample_layers(x, w_in_bf, conv_w, conv_b, gbias, g_ret, g_m, w_out_bf, ln_g, ln_b,
                   rot, gam, esel, state_ret, state_c, state_n, m_pad, state_conv):
    bs = x.shape[0]
    bb = SAMPLE_BLOCK_B
    assert bs % bb == 0
    nb = bs // bb

    def lspec(shape):
        nd = len(shape)
        return pl.BlockSpec((1,) + tuple(shape[1:]), lambda l, j: (l,) + (0,) * (nd - 1))

    def cspec(shape):
        nd = len(shape)
        return pl.BlockSpec(tuple(shape), lambda l, j: (0,) * nd)

    mat_spec = pl.BlockSpec((1, bb, N_HEADS, HEAD_DIM, HEAD_DIM), lambda l, j: (l, j, 0, 0, 0))
    n_spec = pl.BlockSpec((1, bb, N_HEADS, HEAD_DIM), lambda l, j: (l, j, 0, 0))
    m_spec = pl.BlockSpec((1, bb, LANES), lambda l, j: (l, j, 0))
    cv_spec = pl.BlockSpec((1, bb, CONV_W - 1, 2 * GROUP_W), lambda l, j: (l, j, 0, 0))
    in_specs = [
        cspec(x.shape), lspec(w_in_bf.shape), lspec(conv_w.shape), lspec(conv_b.shape),
        lspec(gbias.shape), lspec(g_ret.shape), lspec(g_m.shape), lspec(w_out_bf.shape),
        lspec(ln_g.shape), lspec(ln_b.shape), cspec(rot.shape),
        pl.BlockSpec(memory_space=pltpu.SMEM), cspec(esel.shape),
        mat_spec, mat_spec, n_spec, m_spec, cv_spec,
    ]
    out_shape = (
        jax.ShapeDtypeStruct(x.shape, F32),
        jax.ShapeDtypeStruct(state_ret.shape, F32),
        jax.ShapeDtypeStruct(state_c.shape, F32),
        jax.ShapeDtypeStruct(state_n.shape, F32),
        jax.ShapeDtypeStruct(m_pad.shape, F32),
        jax.ShapeDtypeStruct(state_conv.shape, F32),
    )
    out_specs = (cspec(x.shape), mat_spec, mat_spec, n_spec, m_spec, cv_spec)
    scratch = [
        pltpu.VMEM((bs, N_PAD), F32),
        pltpu.VMEM((bs, D_MODEL), F32),
        pltpu.VMEM((bs, 2 * GROUP_W), F32),
    ]
    return pl.pallas_call(
        _sample_kernel,
        grid=(DEPTH, nb),
        in_specs=in_specs,
        out_specs=out_specs,
        out_shape=out_shape,
        scratch_shapes=scratch,
        compiler_params=pltpu.CompilerParams(
            dimension_semantics=("arbitrary", "arbitrary"),
            vmem_limit_bytes=VMEM_LIMIT_BYTES),
        name="sample_layers",
    )(x, w_in_bf, conv_w, conv_b, gbias, g_ret, g_m, w_out_bf, ln_g, ln_b,
      rot, gam, esel, state_ret, state_c, state_n, m_pad, state_conv)


def _rotary_tables(pos):
    half = HEAD_DIM // 2
    inv = ROPE_BASE ** (-jnp.arange(half, dtype=F32) / half)
    ang = pos.astype(F32)[:, None] * inv[None, :]
    cos = jnp.cos(ang)
    sin = jnp.sin(ang)
    cos_t = jnp.concatenate([cos, cos], axis=-1)
    sin_t = jnp.concatenate([-sin, sin], axis=-1)
    return cos_t, sin_t, cos_t * QK_SCALE, sin_t * QK_SCALE


def _retention_tables():
    L = CHUNK
    log_gamma = jnp.log(1.0 - 2.0 ** (-5.0 - jnp.arange(N_HEADS, dtype=F32)))
    idx = jnp.arange(L, dtype=F32)
    diff = idx[:, None] - idx[None, :]
    decay = jnp.exp(log_gamma[:, None, None] * jnp.maximum(diff, 0.0)) * (diff >= 0)
    q_decay = jnp.exp(log_gamma[:, None] * (idx + 1.0))
    k_decay = jnp.exp(log_gamma[:, None] * (L - 1.0 - idx))
    c_decay = jnp.exp(log_gamma * L)
    qd = jnp.broadcast_to(q_decay[:, :, None], (N_HEADS, L, HEAD_DIM))
    kd = jnp.broadcast_to(k_decay[:, :, None], (N_HEADS, L, HEAD_DIM))
    gamma1 = jnp.exp(log_gamma * 1.0)
    return decay.astype(F32), qd, kd, c_decay, gamma1


def kernel(x_prompt, x_sample, state_ret, state_mlstm_C, state_mlstm_n, state_mlstm_m, state_conv,
           w_in, conv_w, conv_b, b_i, b_f, g_ret, g_m, w_out, ln_g, ln_b):
    B, T, _ = x_prompt.shape
    Bs, Ts, _ = x_sample.shape
    assert Ts == 1

    w_in_bf = jnp.pad(w_in, ((0, 0), (0, 0), (0, N_PAD - N_IN))).astype(BF16)
    w_out_bf = w_out.astype(BF16)
    gbias = jnp.pad(jnp.concatenate([b_i, b_f], axis=-1), ((0, 0), (0, LANES - 2 * N_HEADS)))
    gbias = gbias.reshape(DEPTH, 1, LANES)
    conv_b3 = conv_b.reshape(DEPTH, 1, 2 * GROUP_W)
    g_ret3 = g_ret.reshape(DEPTH, 1, GROUP_W)
    g_m3 = g_m.reshape(DEPTH, 1, GROUP_W)
    ln_g3 = ln_g.reshape(DEPTH, 1, D_MODEL)
    ln_b3 = ln_b.reshape(DEPTH, 1, D_MODEL)

    decay, qd, kd, c_decay, gamma1 = _retention_tables()
    idx = jnp.arange(CHUNK)
    causal = idx[:, None] >= idx[None, :]
    tril = causal.astype(BF16)
    mask_add = jnp.where(causal, 0.0, -jnp.inf).astype(F32)
    e0 = (jnp.arange(LANES)[None, :] == 0).astype(BF16) * jnp.ones((CHUNK, 1), BF16)
    tabs_p = _rotary_tables(jnp.arange(T, dtype=jnp.int32)) + (decay, qd, kd, c_decay, tril, mask_add, e0)

    xp = x_prompt
    rp, cp, np_, mp, vp = [], [], [], [], []
    for l in range(DEPTH):
        xp, s, c, n, m, cv = _prompt_layer(
            xp, w_in_bf[l], conv_w[l], conv_b3[l], gbias[l], g_ret3[l], g_m3[l],
            w_out_bf[l], ln_g3[l], ln_b3[l], tabs_p)
        rp.append(s); cp.append(c); np_.append(n); mp.append(m); vp.append(cv)

    rot = jnp.concatenate(_rotary_tables(PAST_LEN + jnp.arange(Ts, dtype=jnp.int32)), axis=0)
    bb = SAMPLE_BLOCK_B
    esel = (jnp.arange(LANES)[:, None] == (jnp.arange(bb * HEAD_DIM)[None, :] // HEAD_DIM)).astype(BF16)
    m_pad = jnp.pad(state_mlstm_m, ((0, 0), (0, 0), (N_HEADS, LANES - 2 * N_HEADS)))
    ys, rs, cs, ns, ms_pad, vs = _sample_layers(
        x_sample.reshape(Bs, D_MODEL), w_in_bf, conv_w, conv_b3, gbias, g_ret3, g_m3, w_out_bf,
        ln_g3, ln_b3, rot, gamma1, esel, state_ret, state_mlstm_C, state_mlstm_n, m_pad, state_conv)
    ms = ms_pad[:, :, N_HEADS:2 * N_HEADS]

    return (xp, ys.reshape(Bs, Ts, D_MODEL),
            jnp.stack(rp), jnp.stack(cp), jnp.stack(np_), jnp.stack(mp), jnp.stack(vp),
            rs, cs, ns, ms, vs)
```

```python
import functools

import jax
import jax.numpy as jnp
from jax import lax
from jax.experimental import pallas as pl
from jax.experimental.pallas import tpu as pltpu

F32 = jnp.float32
BF16 = jnp.bfloat16

D_MODEL = 1024
DEPTH = 2
PAST_LEN = 16384
N_HEADS = 4
HEAD_DIM = 128
GROUP_W = N_HEADS * HEAD_DIM
CONV_W = 4
CHUNK = 128
ROPE_BASE = 10000.0
LN_EPS = 1e-5
GN_EPS = 1e-5
ALPHA = (2 * DEPTH) ** 0.25
QK_SCALE = HEAD_DIM ** -0.5

LANES = 128
SUBLANES = 8

COL_RQ = 0
COL_RK = COL_RQ + GROUP_W
COL_RV = COL_RK + GROUP_W
COL_RZ = COL_RV + GROUP_W
COL_MQK = COL_RZ + GROUP_W
COL_MV = COL_MQK + 2 * GROUP_W
COL_MO = COL_MV + GROUP_W
COL_MZ = COL_MO + GROUP_W
COL_GATE = COL_MZ + GROUP_W
N_IN = COL_GATE + 2 * N_HEADS
N_PAD = COL_GATE + LANES

PROMPT_BLOCK_T = 256
PROJ_PIECE_COLS = 256
SAMPLE_BLOCK_B = 8
VMEM_LIMIT_BYTES = 56 * 1024 * 1024


def _sigmoid(x):
    return 1.0 / (1.0 + jnp.exp(-x))


def _silu(x):
    return x * _sigmoid(x)


def _log_sigmoid(x):
    return jnp.minimum(x, 0.0) - jnp.log1p(jnp.exp(-jnp.abs(x)))


def _dot(a, b):
    return jnp.dot(a, b, preferred_element_type=F32)


def _dot_nt(a, b):
    return lax.dot_general(a, b, (((1,), (1,)), ((), ())), preferred_element_type=F32)


def _dot_tn(a, b):
    return lax.dot_general(a, b, (((0,), (0,)), ((), ())), preferred_element_type=F32)


def _rotary(x, cos_t, sin_t):
    return x * cos_t + pltpu.roll(x, HEAD_DIM // 2, axis=1) * sin_t


def _head_norm(h, g):
    mu = jnp.mean(h, axis=-1, keepdims=True)
    d = h - mu
    var = jnp.mean(d * d, axis=-1, keepdims=True)
    return d * lax.rsqrt(var + GN_EPS) * g


def _layer_norm(x, g, b):
    mu = jnp.mean(x, axis=-1, keepdims=True)
    d = x - mu
    var = jnp.mean(d * d, axis=-1, keepdims=True)
    return d * lax.rsqrt(var + LN_EPS) * g + b


def _cumsum_rows(tril_bf, x):
    hi = x.astype(BF16)
    r1 = x - hi.astype(F32)
    mid = r1.astype(BF16)
    lo = (r1 - mid.astype(F32)).astype(BF16)
    return _dot(tril_bf, hi) + _dot(tril_bf, mid) + _dot(tril_bf, lo)


RQK_Q, RQK_QDEC, RQK_K, RQK_KDEC = 0, GROUP_W, 2 * GROUP_W, 3 * GROUP_W
VV_RET, VV_M = 0, GROUP_W
GZ_RET, GZ_M = 0, GROUP_W
GT_GATES, GT_BCUM, GT_ROWS, GT_PER_CHUNK = 0, 1, 2, 3
EMITS_PER_CHUNK = 9 * N_HEADS + 1
TAIL_PIECES = 0


def _prompt_kernel(nt, n_blocks, xn_ref, x_ref, w_in_ref, conv_w_ref, conv_b_ref, gbias_ref,
                   g_ret_ref, g_m_ref, w_out_ref, ln_g_ref, ln_b_ref,
                   cos_ref, sin_ref, kcos_ref, ksin_ref,
                   decay_ref, qd_ref, kd_ref, cdec_ref, tril_ref, mask_ref, e0_ref,
                   y_ref, s_out_ref, c_out_ref, n_out_ref, m_out_ref, conv_out_ref,
                   rqk_a, vv_a, gz_a, qkm_a, gt_a, rqk_b, vv_b, gz_b, qkm_b, gt_b,
                   xb_scr, u_scr, mix_scr, s_scr, caug_scr, m_scr):
    g = pl.program_id(0)
    t = lax.rem(jnp.maximum(g - 1, 0), nt)
    tn = lax.rem(jnp.minimum(g, n_blocks - 1), nt)
    block_t = x_ref.shape[1]
    n_chunks = block_t // CHUNK
    carry_rows = CONV_W - 1
    heads = range(N_HEADS)

    @pl.when(g == 0)
    def _init_pipeline():
        for ref in (rqk_b, vv_b, gz_b, qkm_b, gt_b):
            ref[...] = jnp.zeros_like(ref)

    @pl.when(t == 0)
    def _init_state():
        s_scr[...] = jnp.zeros_like(s_scr)
        caug_scr[...] = jnp.zeros_like(caug_scr)
        m_scr[...] = jnp.zeros_like(m_scr)

    @pl.when(tn == 0)
    def _init_conv_carry():
        u_scr[0:SUBLANES, :] = jnp.zeros((SUBLANES, 2 * GROUP_W), F32)

    def hcols(base, h):
        return slice(base + h * HEAD_DIM, base + (h + 1) * HEAD_DIM)

    def step_body(set_in, set_cur):
        rqk_i, vv_i, gz_i, qkm_i, gt_i = set_in
        rqk_c, vv_c, gz_c, qkm_c, gt_c = set_cur
        tril_bf = tril_ref[...]
        causal_add = mask_ref[...]
        ones_col = e0_ref[...]
        lane_id = lax.broadcasted_iota(jnp.int32, (CHUNK, LANES), 1)
        xb_scr[...] = xn_ref[0].astype(BF16)
        pw = PROJ_PIECE_COLS
        heads_per_piece = pw // HEAD_DIM

        def chunk_rows(c):
            return slice(c * CHUNK, (c + 1) * CHUNK)

        def proj(c, col, width=pw):
            return _dot(xb_scr[chunk_rows(c), :], w_in_ref[:, col:col + width])

        def piece_rot(c, i, col_base, cos_r, sin_r, dec_ref, dst, dst_dec):
            rows = chunk_rows(c)
            res = proj(c, col_base + i * pw)
            for hh in range(heads_per_piece):
                h = i * heads_per_piece + hh
                r = _rotary(res[:, hh * HEAD_DIM:(hh + 1) * HEAD_DIM], cos_r[rows, :], sin_r[rows, :])
                rqk_i[rows, hcols(dst, h)] = r.astype(BF16)
                rqk_i[rows, hcols(dst_dec, h)] = (r * dec_ref[h]).astype(BF16)

        def piece_cast(c, i, col_base, dst):
            vv_i[chunk_rows(c), dst + i * pw:dst + (i + 1) * pw] = proj(c, col_base + i * pw).astype(BF16)

        def piece_rz(c, i):
            gz_i[chunk_rows(c), GZ_RET + i * pw:GZ_RET + (i + 1) * pw] = _silu(proj(c, COL_RZ + i * pw))

        def piece_moz(c, i):
            og = _sigmoid(proj(c, COL_MO + i * pw))
            gz_i[chunk_rows(c), GZ_M + i * pw:GZ_M + (i + 1) * pw] = og * _silu(proj(c, COL_MZ + i * pw))

        def piece_conv(c, i):
            base = SUBLANES + c * CHUNK
            u_scr[base:base + CHUNK, i * pw:(i + 1) * pw] = proj(c, COL_MQK + i * pw)
            for cs in range(i * pw, (i + 1) * pw, LANES):
                cols = slice(cs, cs + LANES)
                acc = conv_b_ref[:, cols]
                for j in range(CONV_W):
                    r0 = base - carry_rows + j
                    acc = acc + u_scr[r0:r0 + CHUNK, cols] * conv_w_ref[j:j + 1, cols]
                act = _silu(acc)
                if cs >= GROUP_W:
                    act = act * QK_SCALE
                qkm_i[chunk_rows(c), cols] = act

        def piece_gate(c):
            gates = proj(c, COL_GATE, LANES) + gbias_ref[...]
            bcum = _cumsum_rows(tril_bf, _log_sigmoid(gates))
            gt_i[c * GT_PER_CHUNK + GT_GATES] = gates
            gt_i[c * GT_PER_CHUNK + GT_BCUM] = bcum
            gt_i[c * GT_PER_CHUNK + GT_ROWS] = jnp.where(lane_id < N_HEADS, gates, bcum).T

        n_col_pieces = GROUP_W // pw
        pieces = []
        for c in range(n_chunks):
            pieces.append(functools.partial(piece_gate, c))
            for i in range(n_col_pieces):
                pieces.append(functools.partial(piece_conv, c, 2 * i))
                pieces.append(functools.partial(piece_cast, c, i, COL_RV, VV_RET))
                pieces.append(functools.partial(piece_rot, c, i, COL_RQ, cos_ref, sin_ref, qd_ref,
                                                RQK_Q, RQK_QDEC))
                pieces.append(functools.partial(piece_rz, c, i))
                pieces.append(functools.partial(piece_conv, c, 2 * i + 1))
                pieces.append(functools.partial(piece_cast, c, i, COL_MV, VV_M))
                pieces.append(functools.partial(piece_rot, c, i, COL_RK, kcos_ref, ksin_ref, kd_ref,
                                                RQK_K, RQK_KDEC))
                pieces.append(functools.partial(piece_moz, c, i))
        n_pieces = len(pieces)
        n_slots = EMITS_PER_CHUNK * n_chunks
        slots_done = [0]

        def emit_proj_pieces():
            slots_done[0] += 1
            target = -(-(slots_done[0] * (n_pieces - TAIL_PIECES)) // n_slots)
            while n_pieces - len(pieces) < target:
                pieces.pop(0)()

        def per_head(fn):
            out = []
            for h in heads:
                emit_proj_pieces()
                out.append(fn(h))
            return out

        for c in range(n_chunks):
            rows = slice(c * CHUNK, (c + 1) * CHUNK)

            q_bf = [rqk_c[rows, hcols(RQK_Q, h)] for h in heads]
            k_bf = [rqk_c[rows, hcols(RQK_K, h)] for h in heads]
            v_bf = [vv_c[rows, hcols(VV_RET, h)] for h in heads]
            sc = per_head(lambda h: _dot_nt(q_bf[h], k_bf[h]))
            state = [s_scr[h] for h in heads]
            upd = per_head(lambda h: _dot_tn(rqk_c[rows, hcols(RQK_KDEC, h)], v_bf[h]))
            for h in heads:
                s_scr[h] = state[h] * cdec_ref[h] + upd[h]
            o = per_head(lambda h: _dot(
                jnp.concatenate([(sc[h] * decay_ref[h]).astype(BF16),
                                 rqk_c[rows, hcols(RQK_QDEC, h)]], axis=1),
                jnp.concatenate([v_bf[h], state[h].astype(BF16)], axis=0)))

            def ret_out(h):
                hn = _head_norm(o[h], g_ret_ref[:, hcols(0, h)])
                mix_scr[rows, hcols(0, h)] = (hn * gz_c[rows, hcols(GZ_RET, h)]).astype(BF16)

            per_head(ret_out)

            gates = gt_c[c * GT_PER_CHUNK + GT_GATES]
            bcum = gt_c[c * GT_PER_CHUNK + GT_BCUM]
            rows_t = gt_c[c * GT_PER_CHUNK + GT_ROWS]
            q = [qkm_c[rows, hcols(0, h)] for h in heads]
            k = [qkm_c[rows, hcols(GROUP_W, h)] for h in heads]
            vaug_bf = [jnp.concatenate([vv_c[rows, hcols(VV_M, h)], ones_col], axis=1)
                       for h in heads]
            qk = per_head(lambda h: _dot_nt(q[h].astype(BF16), k[h].astype(BF16)))
            caug = [caug_scr[h] for h in heads]
            m_old = [m_scr[h, 0:1, 0:1] for h in heads]
            i_row = [rows_t[h:h + 1, :] for h in heads]
            b_row = [rows_t[N_HEADS + h:N_HEADS + h + 1, :] for h in heads]
            i_col = [gates[:, h:h + 1] for h in heads]
            b_col = [bcum[:, N_HEADS + h:N_HEADS + h + 1] for h in heads]
            a_col = [b_col[h] + m_old[h] for h in heads]
            dm = per_head(lambda h: (b_col[h] + (i_row[h] - b_row[h])) + causal_add)
            mt = [jnp.maximum(a_col[h], jnp.max(dm[h], axis=-1, keepdims=True)) for h in heads]
            w_inter = [jnp.exp(a_col[h] - mt[h]) for h in heads]
            res = per_head(lambda h: _dot(
                jnp.concatenate([(qk[h] * jnp.exp(dm[h] - mt[h])).astype(BF16),
                                 (q[h] * w_inter[h]).astype(BF16)], axis=1),
                jnp.concatenate([vaug_bf[h], caug[h].astype(BF16)], axis=0)))
            b_last = [b_col[h][CHUNK - 1:CHUNK, :] for h in heads]
            g_col = [b_last[h] - b_col[h] + i_col[h] for h in heads]
            m_new = [jnp.maximum(b_last[h] + m_old[h], jnp.max(g_col[h], axis=0, keepdims=True))
                     for h in heads]
            wk = [jnp.exp(g_col[h] - m_new[h]) for h in heads]
            wc = [jnp.exp(b_last[h] + m_old[h] - m_new[h]) for h in heads]

            def mlstm_update(h):
                caug_scr[h] = caug[h] * wc[h] + _dot_tn((k[h] * wk[h]).astype(BF16), vaug_bf[h])
                m_scr[h] = jnp.broadcast_to(m_new[h], (SUBLANES, LANES))

            per_head(mlstm_update)

            def mlstm_out(h):
                den = jnp.maximum(jnp.abs(res[h][:, HEAD_DIM:HEAD_DIM + 1]), jnp.exp(-mt[h]))
                hn = _head_norm(res[h][:, :HEAD_DIM] / den, g_m_ref[:, hcols(0, h)])
                mix_scr[rows, hcols(GROUP_W, h)] = (hn * gz_c[rows, hcols(GZ_M, h)]).astype(BF16)

            per_head(mlstm_out)

            emit_proj_pieces()
            mix = _dot(mix_scr[rows, :], w_out_ref[...])
            y_ref[0, rows, :] = _layer_norm(ALPHA * x_ref[0, rows, :] + mix, ln_g_ref[...], ln_b_ref[...])

        assert slots_done[0] == n_slots and len(pieces) == TAIL_PIECES
        while pieces:
            pieces.pop(0)()

    set_a = (rqk_a, vv_a, gz_a, qkm_a, gt_a)
    set_b = (rqk_b, vv_b, gz_b, qkm_b, gt_b)
    parity = lax.rem(g, 2)

    @pl.when(parity == 0)
    def _even_step():
        step_body(set_a, set_b)

    @pl.when(parity == 1)
    def _odd_step():
        step_body(set_b, set_a)

    @pl.when(jnp.logical_and(tn == nt - 1, g < n_blocks))
    def _write_conv_state():
        conv_out_ref[0] = u_scr[SUBLANES + block_t - carry_rows:SUBLANES + block_t, :]

    @pl.when(jnp.logical_and(t == nt - 1, g > 0))
    def _write_state():
        s_out_ref[0] = s_scr[...]
        m_out_ref[0] = jnp.zeros((SUBLANES, LANES), F32)
        for h in heads:
            caug = caug_scr[h]
            c_out_ref[0, h] = caug[:, :HEAD_DIM]
            n_out_ref[0, h:h + 1, :] = caug[:, HEAD_DIM:].T[0:1, :]
            m_out_ref[0, h:h + 1, :] = m_scr[h, 0:1, :]

    u_scr[0:SUBLANES, :] = u_scr[block_t:block_t + SUBLANES, :]


def _const_spec(shape):
    nd = len(shape)
    return pl.BlockSpec(shape, lambda g: (0,) * nd)


def _prompt_layer(x, w_in_bf, conv_w, conv_b, gbias, g_ret, g_m, w_out_bf, ln_g, ln_b, tabs):
    B, T, _ = x.shape
    bt = PROMPT_BLOCK_T
    assert T % bt == 0 and bt % CHUNK == 0 and GROUP_W % PROJ_PIECE_COLS == 0
    nt = T // bt
    n_blocks = B * nt
    n_chunks = bt // CHUNK
    cos_t, sin_t, kcos_t, ksin_t, decay, qd, kd, cdec, tril, mask, e0 = tabs

    def nxt(g):
        return jnp.minimum(g, n_blocks - 1)

    def cur(g):
        return jnp.maximum(g - 1, 0)

    row_spec = pl.BlockSpec((bt, LANES), lambda g: (nxt(g) % nt, 0))
    in_specs = [
        pl.BlockSpec((1, bt, D_MODEL), lambda g: (nxt(g) // nt, nxt(g) % nt, 0)),
        pl.BlockSpec((1, bt, D_MODEL), lambda g: (cur(g) // nt, cur(g) % nt, 0)),
        _const_spec(w_in_bf.shape), _const_spec(conv_w.shape), _const_spec(conv_b.shape),
        _const_spec(gbias.shape), _const_spec(g_ret.shape), _const_spec(g_m.shape),
        _const_spec(w_out_bf.shape), _const_spec(ln_g.shape), _const_spec(ln_b.shape),
        row_spec, row_spec, row_spec, row_spec,
        _const_spec(decay.shape), _const_spec(qd.shape), _const_spec(kd.shape),
        pl.BlockSpec(memory_space=pltpu.SMEM),
        _const_spec(tril.shape), _const_spec(mask.shape), _const_spec(e0.shape),
    ]
    out_shape = (
        jax.ShapeDtypeStruct((B, T, D_MODEL), F32),
        jax.ShapeDtypeStruct((B, N_HEADS, HEAD_DIM, HEAD_DIM), F32),
        jax.ShapeDtypeStruct((B, N_HEADS, HEAD_DIM, HEAD_DIM), F32),
        jax.ShapeDtypeStruct((B, N_HEADS, HEAD_DIM), F32),
        jax.ShapeDtypeStruct((B, SUBLANES, LANES), F32),
        jax.ShapeDtypeStruct((B, CONV_W - 1, 2 * GROUP_W), F32),
    )
    out_specs = (
        pl.BlockSpec((1, bt, D_MODEL), lambda g: (cur(g) // nt, cur(g) % nt, 0)),
        pl.BlockSpec((1, N_HEADS, HEAD_DIM, HEAD_DIM), lambda g: (cur(g) // nt, 0, 0, 0)),
        pl.BlockSpec((1, N_HEADS, HEAD_DIM, HEAD_DIM), lambda g: (cur(g) // nt, 0, 0, 0)),
        pl.BlockSpec((1, N_HEADS, HEAD_DIM), lambda g: (cur(g) // nt, 0, 0)),
        pl.BlockSpec((1, SUBLANES, LANES), lambda g: (cur(g) // nt, 0, 0)),
        pl.BlockSpec((1, CONV_W - 1, 2 * GROUP_W), lambda g: (nxt(g) // nt, 0, 0)),
    )
    operand_set = [
        pltpu.VMEM((bt, 4 * GROUP_W), BF16),
        pltpu.VMEM((bt, 2 * GROUP_W), BF16),
        pltpu.VMEM((bt, 2 * GROUP_W), F32),
        pltpu.VMEM((bt, 2 * GROUP_W), F32),
        pltpu.VMEM((n_chunks * GT_PER_CHUNK, CHUNK, LANES), F32),
    ]
    scratch = operand_set + operand_set + [
        pltpu.VMEM((bt, D_MODEL), BF16),
        pltpu.VMEM((SUBLANES + bt, 2 * GROUP_W), F32),
        pltpu.VMEM((bt, 2 * GROUP_W), BF16),
        pltpu.VMEM((N_HEADS, HEAD_DIM, HEAD_DIM), F32),
        pltpu.VMEM((N_HEADS, HEAD_DIM, 2 * HEAD_DIM), F32),
        pltpu.VMEM((N_HEADS, SUBLANES, LANES), F32),
    ]
    y, s, c, n, m, cv = pl.pallas_call(
        functools.partial(_prompt_kernel, nt, n_blocks),
        grid=(n_blocks + 1,),
        in_specs=in_specs,
        out_specs=out_specs,
        out_shape=out_shape,
        scratch_shapes=scratch,
        compiler_params=pltpu.CompilerParams(
            dimension_semantics=("arbitrary",),
            vmem_limit_bytes=VMEM_LIMIT_BYTES),
        name="prompt_layer",
    )(x, x, w_in_bf, conv_w, conv_b, gbias, g_ret, g_m, w_out_bf, ln_g, ln_b,
      cos_t, sin_t, kcos_t, ksin_t, decay, qd, kd, cdec, tril, mask, e0)
    return y, s, c, n, m[:, :N_HEADS, 0], cv


def _sample_kernel(x_ref, w_in_ref, conv_w_ref, conv_b_ref, gbias_ref, g_ret_ref, g_m_ref,
                   w_out_ref, ln_g_ref, ln_b_ref, rot_ref, gam_ref, esel_ref,
                   s_ref, c_ref, n_ref, m_ref, cv_ref,
                   y_ref, s_out_ref, c_out_ref, n_out_ref, m_out_ref, cv_out_ref,
                   proj_scr, xcur_scr, mix_scr):
    layer = pl.program_id(0)
    j = pl.program_id(1)
    bb = s_ref.shape[1]

    @pl.when(jnp.logical_and(layer == 0, j == 0))
    def _load_x():
        xcur_scr[...] = x_ref[...]

    @pl.when(j == 0)
    def _project():
        xb = xcur_scr[...].astype(BF16)
        for lo in range(0, N_PAD, 512):
            hi = min(lo + 512, N_PAD)
            proj_scr[:, lo:hi] = _dot(xb, w_in_ref[0, :, lo:hi])

    r0 = pl.multiple_of(j * bb, bb)
    rows = pl.ds(r0, bb)
    cos_t = rot_ref[0:1, :]
    sin_t = rot_ref[1:2, :]
    kcos_t = rot_ref[2:3, :]
    ksin_t = rot_ref[3:4, :]
    esel = esel_ref[...]
    row_id = lax.broadcasted_iota(jnp.int32, (bb, bb * HEAD_DIM), 0)
    blk_id = jnp.right_shift(lax.broadcasted_iota(jnp.int32, (bb, bb * HEAD_DIM), 1), 7)
    pad_rows = jnp.zeros((LANES - bb, HEAD_DIM), F32)
    pad_wide = jnp.zeros((LANES - bb, bb * HEAD_DIM), BF16)

    def col_form(x8):
        return jnp.concatenate([x8, pad_rows], axis=0).T.astype(BF16)

    def outer_all(k8, v8):
        vt = jnp.concatenate([v8] * bb, axis=1)
        vsel = jnp.where(row_id == blk_id, vt, 0.0).astype(BF16)
        return _dot(col_form(k8), jnp.concatenate([vsel, pad_wide], axis=0))

    def col_bcast_all(q8):
        return _dot(col_form(q8), esel)

    for h in range(N_HEADS):
        hc = h * HEAD_DIM
        q8 = _rotary(proj_scr[rows, COL_RQ + hc:COL_RQ + hc + HEAD_DIM], cos_t, sin_t)
        k8 = _rotary(proj_scr[rows, COL_RK + hc:COL_RK + hc + HEAD_DIM], kcos_t, ksin_t)
        v8 = proj_scr[rows, COL_RV + hc:COL_RV + hc + HEAD_DIM]
        kv_all = outer_all(k8, v8)
        qc_all = col_bcast_all(q8)
        gamma = gam_ref[h]
        o_rows = []
        for r in range(bb):
            blk = slice(r * HEAD_DIM, (r + 1) * HEAD_DIM)
            s_new = s_ref[0, r, h] * gamma + kv_all[:, blk]
            s_out_ref[0, r, h] = s_new
            o_rows.append(jnp.sum(qc_all[:, blk] * s_new, axis=0, keepdims=True))
        o8 = _head_norm(jnp.concatenate(o_rows, axis=0), g_ret_ref[0, :, hc:hc + HEAD_DIM])
        z8 = proj_scr[rows, COL_RZ + hc:COL_RZ + hc + HEAD_DIM]
        mix_scr[rows, hc:hc + HEAD_DIM] = o8 * _silu(z8)

    u8 = proj_scr[rows, COL_MQK:COL_MQK + 2 * GROUP_W]
    acc = conv_b_ref[0] + u8 * conv_w_ref[0, CONV_W - 1:CONV_W, :]
    for jj in range(CONV_W - 1):
        acc = acc + cv_ref[0, :, jj, :] * conv_w_ref[0, jj:jj + 1, :]
    for jj in range(1, CONV_W - 1):
        cv_out_ref[0, :, jj - 1, :] = cv_ref[0, :, jj, :]
    cv_out_ref[0, :, CONV_W - 2, :] = u8
    qk8 = _silu(acc)

    gates = proj_scr[rows, COL_GATE:COL_GATE + LANES] + gbias_ref[0]
    i_al = pltpu.roll(gates, N_HEADS, axis=1)
    bm = _log_sigmoid(gates) + m_ref[0]
    m_new = jnp.maximum(bm, i_al)
    wk = jnp.exp(i_al - m_new)
    wc = jnp.exp(bm - m_new)
    einv = jnp.exp(-m_new)
    m_out_ref[0] = m_new
    for h in range(N_HEADS):
        hc = h * HEAD_DIM
        gl = N_HEADS + h
        q8 = qk8[:, hc:hc + HEAD_DIM]
        k8 = qk8[:, GROUP_W + hc:GROUP_W + hc + HEAD_DIM] * QK_SCALE
        v8 = proj_scr[rows, COL_MV + hc:COL_MV + hc + HEAD_DIM]
        wk_h = wk[:, gl:gl + 1]
        wc_h = jnp.broadcast_to(wc[:, gl:gl + 1], (bb, HEAD_DIM))
        kw8 = k8 * wk_h
        kv_all = outer_all(kw8, v8)
        qc_all = col_bcast_all(q8)
        n_new = n_ref[0, :, h, :] * wc_h + kw8
        n_out_ref[0, :, h, :] = n_new
        num_rows = []
        for r in range(bb):
            blk = slice(r * HEAD_DIM, (r + 1) * HEAD_DIM)
            c_new = c_ref[0, r, h] * wc_h[r:r + 1, :] + kv_all[:, blk]
            c_out_ref[0, r, h] = c_new
            num_rows.append(jnp.sum(qc_all[:, blk] * c_new, axis=0, keepdims=True))
        num = jnp.concatenate(num_rows, axis=0)
        q_bf = q8.astype(BF16).astype(F32)
        den = jnp.sum(q_bf * n_new, axis=-1, keepdims=True)
        hout = num / jnp.maximum(jnp.abs(den), einv[:, gl:gl + 1])
        hn = _head_norm(hout, g_m_ref[0, :, hc:hc + HEAD_DIM])
        og = _sigmoid(proj_scr[rows, COL_MO + hc:COL_MO + hc + HEAD_DIM])
        zg = _silu(proj_scr[rows, COL_MZ + hc:COL_MZ + hc + HEAD_DIM])
        mix_scr[rows, GROUP_W + hc:GROUP_W + hc + HEAD_DIM] = hn * og * zg

    @pl.when(j == pl.num_programs(1) - 1)
    def _finish_layer():
        mix = _dot(mix_scr[...].astype(BF16), w_out_ref[0])
        y = _layer_norm(ALPHA * xcur_scr[...] + mix, ln_g_ref[0], ln_b_ref[0])
        xcur_scr[...] = y
        y_ref[...] = y


def _sample_layers(x, w_in_bf, conv_w, conv_b, gbias, g_ret, g_m, w_out_bf, ln_g, ln_b,
                   rot, gam, esel, state_ret, state_c, state_n, m_pad, state_conv):
    bs = x.shape[0]
    bb = SAMPLE_BLOCK_B
    assert bs % bb == 0
    nb = bs // bb

    def lspec(shape):
        nd = len(shape)
        return pl.BlockSpec((1,) + tuple(shape[1:]), lambda l, j: (l,) + (0,) * (nd - 1))

    def cspec(shape):
        nd = len(shape)
        return pl.BlockSpec(tuple(shape), lambda l, j: (0,) * nd)

    mat_spec = pl.BlockSpec((1, bb, N_HEADS, HEAD_DIM, HEAD_DIM), lambda l, j: (l, j, 0, 0, 0))
    n_spec = pl.BlockSpec((1, bb, N_HEADS, HEAD_DIM), lambda l, j: (l, j, 0, 0))
    m_spec = pl.BlockSpec((1, bb, LANES), lambda l, j: (l, j, 0))
    cv_spec = pl.BlockSpec((1, bb, CONV_W - 1, 2 * GROUP_W), lambda l, j: (l, j, 0, 0))
    in_specs = [
        cspec(x.shape), lspec(w_in_bf.shape), lspec(conv_w.shape), lspec(conv_b.shape),
        lspec(gbias.shape), lspec(g_ret.shape), lspec(g_m.shape), lspec(w_out_bf.shape),
        lspec(ln_g.shape), lspec(ln_b.shape), cspec(rot.shape),
        pl.BlockSpec(memory_space=pltpu.SMEM), cspec(esel.shape),
        mat_spec, mat_spec, n_spec, m_spec, cv_spec,
    ]
    out_shape = (
        jax.ShapeDtypeStruct(x.shape, F32),
        jax.ShapeDtypeStruct(state_ret.shape, F32),
        jax.ShapeDtypeStruct(state_c.shape, F32),
        jax.ShapeDtypeStruct(state_n.shape, F32),
        jax.ShapeDtypeStruct(m_pad.shape, F32),
        jax.ShapeDtypeStruct(state_conv.shape, F32),
    )
    out_specs = (cspec(x.shape), mat_spec, mat_spec, n_spec, m_spec, cv_spec)
    scratch = [
        pltpu.VMEM((bs, N_PAD), F32),
        pltpu.VMEM((bs, D_MODEL), F32),
        pltpu.VMEM((bs, 2 * GROUP_W), F32),
    ]
    return pl.pallas_call(
        _sample_kernel,
        grid=(DEPTH, nb),
        in_specs=in_specs,
        out_specs=out_specs,
        out_shape=out_shape,
        scratch_shapes=scratch,
        compiler_params=pltpu.CompilerParams(
            dimension_semantics=("arbitrary", "arbitrary"),
            vmem_limit_bytes=VMEM_LIMIT_BYTES),
        name="sample_layers",
    )(x, w_in_bf, conv_w, conv_b, gbias, g_ret, g_m, w_out_bf, ln_g, ln_b,
      rot, gam, esel, state_ret, state_c, state_n, m_pad, state_conv)


def _rotary_tables(pos):
    half = HEAD_DIM // 2
    inv = ROPE_BASE ** (-jnp.arange(half, dtype=F32) / half)
    ang = pos.astype(F32)[:, None] * inv[None, :]
    cos = jnp.cos(ang)
    sin = jnp.sin(ang)
    cos_t = jnp.concatenate([cos, cos], axis=-1)
    sin_t = jnp.concatenate([-sin, sin], axis=-1)
    return cos_t, sin_t, cos_t * QK_SCALE, sin_t * QK_SCALE


def _retention_tables():
    L = CHUNK
    log_gamma = jnp.log(1.0 - 2.0 ** (-5.0 - jnp.arange(N_HEADS, dtype=F32)))
    idx = jnp.arange(L, dtype=F32)
    diff = idx[:, None] - idx[None, :]
    decay = jnp.exp(log_gamma[:, None, None] * jnp.maximum(diff, 0.0)) * (diff >= 0)
    q_decay = jnp.exp(log_gamma[:, None] * (idx + 1.0))
    k_decay = jnp.exp(log_gamma[:, None] * (L - 1.0 - idx))
    c_decay = jnp.exp(log_gamma * L)
    qd = jnp.broadcast_to(q_decay[:, :, None], (N_HEADS, L, HEAD_DIM))
    kd = jnp.broadcast_to(k_decay[:, :, None], (N_HEADS, L, HEAD_DIM))
    gamma1 = jnp.exp(log_gamma * 1.0)
    return decay.astype(F32), qd, kd, c_decay, gamma1


def kernel(x_prompt, x_sample, state_ret, state_mlstm_C, state_mlstm_n, state_mlstm_m, state_conv,
           w_in, conv_w, conv_b, b_i, b_f, g_ret, g_m, w_out, ln_g, ln_b):
    B, T, _ = x_prompt.shape
    Bs, Ts, _ = x_sample.shape
    assert Ts == 1

    w_in_bf = jnp.pad(w_in, ((0, 0), (0, 0), (0, N_PAD - N_IN))).astype(BF16)
    w_out_bf = w_out.astype(BF16)
    gbias = jnp.pad(jnp.concatenate([b_i, b_f], axis=-1), ((0, 0), (0, LANES - 2 * N_HEADS)))
    gbias = gbias.reshape(DEPTH, 1, LANES)
    conv_b3 = conv_b.reshape(DEPTH, 1, 2 * GROUP_W)
    g_ret3 = g_ret.reshape(DEPTH, 1, GROUP_W)
    g_m3 = g_m.reshape(DEPTH, 1, GROUP_W)
    ln_g3 = ln_g.reshape(DEPTH, 1, D_MODEL)
    ln_b3 = ln_b.reshape(DEPTH, 1, D_MODEL)

    decay, qd, kd, c_decay, gamma1 = _retention_tables()
    idx = jnp.arange(CHUNK)
    causal = idx[:, None] >= idx[None, :]
    tril = causal.astype(BF16)
    mask_add = jnp.where(causal, 0.0, -jnp.inf).astype(F32)
    e0 = (jnp.arange(LANES)[None, :] == 0).astype(BF16) * jnp.ones((CHUNK, 1), BF16)
    tabs_p = _rotary_tables(jnp.arange(T, dtype=jnp.int32)) + (decay, qd, kd, c_decay, tril, mask_add, e0)

    xp = x_prompt
    rp, cp, np_, mp, vp = [], [], [], [], []
    for l in range(DEPTH):
        xp, s, c, n, m, cv = _prompt_layer(
            xp, w_in_bf[l], conv_w[l], conv_b3[l], gbias[l], g_ret3[l], g_m3[l],
            w_out_bf[l], ln_g3[l], ln_b3[l], tabs_p)
        rp.append(s); cp.append(c); np_.append(n); mp.append(m); vp.append(cv)

    rot = jnp.concatenate(_rotary_tables(PAST_LEN + jnp.arange(Ts, dtype=jnp.int32)), axis=0)
    bb = SAMPLE_BLOCK_B
    esel = (jnp.arange(LANES)[:, None] == (jnp.arange(bb * HEAD_DIM)[None, :] // HEAD_DIM)).astype(BF16)
    m_pad = jnp.pad(state_mlstm_m, ((0, 0), (0, 0), (N_HEADS, LANES - 2 * N_HEADS)))
    ys, rs, cs, ns, ms_pad, vs = _sample_layers(
        x_sample.reshape(Bs, D_MODEL), w_in_bf, conv_w, conv_b3, gbias, g_ret3, g_m3, w_out_bf,
        ln_g3, ln_b3, rot, gamma1, esel, state_ret, state_mlstm_C, state_mlstm_n, m_pad, state_conv)
    ms = ms_pad[:, :, N_HEADS:2 * N_HEADS]

    return (xp, ys.reshape(Bs, Ts, D_MODEL),
            jnp.stack(rp), jnp.stack(cp), jnp.stack(np_), jnp.stack(mp), jnp.stack(vp),
            rs, cs, ns, ms, vs)
```

```python
import functools

import jax
import jax.numpy as jnp
from jax import lax
from jax.experimental import pallas as pl
from jax.experimental.pallas import tpu as pltpu

F32 = jnp.float32
BF16 = jnp.bfloat16

D_MODEL = 1024
DEPTH = 2
PAST_LEN = 16384
N_HEADS = 4
HEAD_DIM = 128
GROUP_W = N_HEADS * HEAD_DIM
CONV_W = 4
CHUNK = 128
ROPE_BASE = 10000.0
LN_EPS = 1e-5
GN_EPS = 1e-5
ALPHA = (2 * DEPTH) ** 0.25
QK_SCALE = HEAD_DIM ** -0.5

LANES = 128
SUBLANES = 8

COL_RQ = 0
COL_RK = COL_RQ + GROUP_W
COL_RV = COL_RK + GROUP_W
COL_RZ = COL_RV + GROUP_W
COL_MQK = COL_RZ + GROUP_W
COL_MV = COL_MQK + 2 * GROUP_W
COL_MO = COL_MV + GROUP_W
COL_MZ = COL_MO + GROUP_W
COL_GATE = COL_MZ + GROUP_W
N_IN = COL_GATE + 2 * N_HEADS
N_PAD = COL_GATE + LANES

PROMPT_BLOCK_T = 256
PROJ_PIECE_COLS = 256
SAMPLE_BLOCK_B = 8
WEIGHT_CAST_ROWS = 256
VMEM_LIMIT_BYTES = 56 * 1024 * 1024


def _sigmoid(x):
    return 1.0 / (1.0 + jnp.exp(-x))


def _silu(x):
    return x * _sigmoid(x)


def _log_sigmoid(x):
    return jnp.minimum(x, 0.0) - jnp.log1p(jnp.exp(-jnp.abs(x)))


def _dot(a, b):
    return jnp.dot(a, b, preferred_element_type=F32)


def _dot_nt(a, b):
    return lax.dot_general(a, b, (((1,), (1,)), ((), ())), preferred_element_type=F32)


def _dot_tn(a, b):
    return lax.dot_general(a, b, (((0,), (0,)), ((), ())), preferred_element_type=F32)


def _rotary(x, cos_t, sin_t):
    return x * cos_t + pltpu.roll(x, HEAD_DIM // 2, axis=1) * sin_t


def _head_norm(h, g):
    mu = jnp.mean(h, axis=-1, keepdims=True)
    d = h - mu
    var = jnp.mean(d * d, axis=-1, keepdims=True)
    return d * lax.rsqrt(var + GN_EPS) * g


def _layer_norm(x, g, b):
    mu = jnp.mean(x, axis=-1, keepdims=True)
    d = x - mu
    var = jnp.mean(d * d, axis=-1, keepdims=True)
    return d * lax.rsqrt(var + LN_EPS) * g + b


def _cumsum_rows(tril_bf, x):
    hi = x.astype(BF16)
    r1 = x - hi.astype(F32)
    mid = r1.astype(BF16)
    lo = (r1 - mid.astype(F32)).astype(BF16)
    return _dot(tril_bf, hi) + _dot(tril_bf, mid) + _dot(tril_bf, lo)


RQK_Q, RQK_QDEC, RQK_K, RQK_KDEC = 0, GROUP_W, 2 * GROUP_W, 3 * GROUP_W
VV_RET, VV_M = 0, GROUP_W
GZ_RET, GZ_M = 0, GROUP_W
GT_GATES, GT_BCUM, GT_ROWS, GT_PER_CHUNK = 0, 1, 2, 3
EMITS_PER_CHUNK = 9 * N_HEADS + 1
TAIL_PIECES = 0


def _prompt_kernel(nt, n_blocks, xn_ref, x_ref, w_in_ref, conv_w_ref, conv_b_ref, gbias_ref,
                   g_ret_ref, g_m_ref, w_out_ref, ln_g_ref, ln_b_ref,
                   cos_ref, sin_ref, kcos_ref, ksin_ref,
                   decay_ref, qd_ref, kd_ref, cdec_ref, tril_ref, mask_ref, e0_ref,
                   y_ref, s_out_ref, c_out_ref, n_out_ref, m_out_ref, conv_out_ref,
                   rqk_a, vv_a, gz_a, qkm_a, gt_a, rqk_b, vv_b, gz_b, qkm_b, gt_b,
                   xb_scr, u_scr, mix_scr, s_scr, caug_scr, m_scr):
    g = pl.program_id(0)
    t = lax.rem(jnp.maximum(g - 1, 0), nt)
    tn = lax.rem(jnp.minimum(g, n_blocks - 1), nt)
    block_t = x_ref.shape[1]
    n_chunks = block_t // CHUNK
    carry_rows = CONV_W - 1
    heads = range(N_HEADS)

    @pl.when(g == 0)
    def _init_pipeline():
        for ref in (rqk_b, vv_b, gz_b, qkm_b, gt_b):
            ref[...] = jnp.zeros_like(ref)

    @pl.when(t == 0)
    def _init_state():
        s_scr[...] = jnp.zeros_like(s_scr)
        caug_scr[...] = jnp.zeros_like(caug_scr)
        m_scr[...] = jnp.zeros_like(m_scr)

    @pl.when(tn == 0)
    def _init_conv_carry():
        u_scr[0:SUBLANES, :] = jnp.zeros((SUBLANES, 2 * GROUP_W), F32)

    def hcols(base, h):
        return slice(base + h * HEAD_DIM, base + (h + 1) * HEAD_DIM)

    def step_body(set_in, set_cur):
        rqk_i, vv_i, gz_i, qkm_i, gt_i = set_in
        rqk_c, vv_c, gz_c, qkm_c, gt_c = set_cur
        tril_bf = tril_ref[...]
        causal_add = mask_ref[...]
        ones_col = e0_ref[...]
        lane_id = lax.broadcasted_iota(jnp.int32, (CHUNK, LANES), 1)
        xb_scr[...] = xn_ref[0].astype(BF16)
        pw = PROJ_PIECE_COLS
        heads_per_piece = pw // HEAD_DIM

        def chunk_rows(c):
            return slice(c * CHUNK, (c + 1) * CHUNK)

        def proj(c, col, width=pw):
            return _dot(xb_scr[chunk_rows(c), :], w_in_ref[:, col:col + width])

        def piece_rot(c, i, col_base, cos_r, sin_r, dec_ref, dst, dst_dec):
            rows = chunk_rows(c)
            res = proj(c, col_base + i * pw)
            for hh in range(heads_per_piece):
                h = i * heads_per_piece + hh
                r = _rotary(res[:, hh * HEAD_DIM:(hh + 1) * HEAD_DIM], cos_r[rows, :], sin_r[rows, :])
                rqk_i[rows, hcols(dst, h)] = r.astype(BF16)
                rqk_i[rows, hcols(dst_dec, h)] = (r * dec_ref[h]).astype(BF16)

        def piece_cast(c, i, col_base, dst):
            vv_i[chunk_rows(c), dst + i * pw:dst + (i + 1) * pw] = proj(c, col_base + i * pw).astype(BF16)

        def piece_rz(c, i):
            gz_i[chunk_rows(c), GZ_RET + i * pw:GZ_RET + (i + 1) * pw] = _silu(proj(c, COL_RZ + i * pw))

        def piece_moz(c, i):
            og = _sigmoid(proj(c, COL_MO + i * pw))
            gz_i[chunk_rows(c), GZ_M + i * pw:GZ_M + (i + 1) * pw] = og * _silu(proj(c, COL_MZ + i * pw))

        def piece_conv(c, i):
            base = SUBLANES + c * CHUNK
            u_scr[base:base + CHUNK, i * pw:(i + 1) * pw] = proj(c, COL_MQK + i * pw)
            for cs in range(i * pw, (i + 1) * pw, LANES):
                cols = slice(cs, cs + LANES)
                acc = conv_b_ref[:, cols]
                for j in range(CONV_W):
                    r0 = base - carry_rows + j
                    acc = acc + u_scr[r0:r0 + CHUNK, cols] * conv_w_ref[j:j + 1, cols]
                act = _silu(acc)
                if cs >= GROUP_W:
                    act = act * QK_SCALE
                qkm_i[chunk_rows(c), cols] = act

        def piece_gate(c):
            gates = proj(c, COL_GATE, LANES) + gbias_ref[...]
            bcum = _cumsum_rows(tril_bf, _log_sigmoid(gates))
            gt_i[c * GT_PER_CHUNK + GT_GATES] = gates
            gt_i[c * GT_PER_CHUNK + GT_BCUM] = bcum
            gt_i[c * GT_PER_CHUNK + GT_ROWS] = jnp.where(lane_id < N_HEADS, gates, bcum).T

        n_col_pieces = GROUP_W // pw
        pieces = []
        for c in range(n_chunks):
            pieces.append(functools.partial(piece_gate, c))
            for i in range(n_col_pieces):
                pieces.append(functools.partial(piece_conv, c, 2 * i))
                pieces.append(functools.partial(piece_cast, c, i, COL_RV, VV_RET))
                pieces.append(functools.partial(piece_rot, c, i, COL_RQ, cos_ref, sin_ref, qd_ref,
                                                RQK_Q, RQK_QDEC))
                pieces.append(functools.partial(piece_rz, c, i))
                pieces.append(functools.partial(piece_conv, c, 2 * i + 1))
                pieces.append(functools.partial(piece_cast, c, i, COL_MV, VV_M))
                pieces.append(functools.partial(piece_rot, c, i, COL_RK, kcos_ref, ksin_ref, kd_ref,
                                                RQK_K, RQK_KDEC))
                pieces.append(functools.partial(piece_moz, c, i))
        n_pieces = len(pieces)
        n_slots = EMITS_PER_CHUNK * n_chunks
        slots_done = [0]

        def emit_proj_pieces():
            slots_done[0] += 1
            target = -(-(slots_done[0] * (n_pieces - TAIL_PIECES)) // n_slots)
            while n_pieces - len(pieces) < target:
                pieces.pop(0)()

        def per_head(fn):
            out = []
            for h in heads:
                emit_proj_pieces()
                out.append(fn(h))
            return out

        for c in range(n_chunks):
            rows = slice(c * CHUNK, (c + 1) * CHUNK)

            q_bf = [rqk_c[rows, hcols(RQK_Q, h)] for h in heads]
            k_bf = [rqk_c[rows, hcols(RQK_K, h)] for h in heads]
            v_bf = [vv_c[rows, hcols(VV_RET, h)] for h in heads]
            sc = per_head(lambda h: _dot_nt(q_bf[h], k_bf[h]))
            state = [s_scr[h] for h in heads]
            upd = per_head(lambda h: _dot_tn(rqk_c[rows, hcols(RQK_KDEC, h)], v_bf[h]))
            for h in heads:
                s_scr[h] = state[h] * cdec_ref[h] + upd[h]
            o = per_head(lambda h: _dot(
                jnp.concatenate([(sc[h] * decay_ref[h]).astype(BF16),
                                 rqk_c[rows, hcols(RQK_QDEC, h)]], axis=1),
                jnp.concatenate([v_bf[h], state[h].astype(BF16)], axis=0)))

            def ret_out(h):
                hn = _head_norm(o[h], g_ret_ref[:, hcols(0, h)])
                mix_scr[rows, hcols(0, h)] = (hn * gz_c[rows, hcols(GZ_RET, h)]).astype(BF16)

            per_head(ret_out)

            gates = gt_c[c * GT_PER_CHUNK + GT_GATES]
            bcum = gt_c[c * GT_PER_CHUNK + GT_BCUM]
            rows_t = gt_c[c * GT_PER_CHUNK + GT_ROWS]
            q = [qkm_c[rows, hcols(0, h)] for h in heads]
            k = [qkm_c[rows, hcols(GROUP_W, h)] for h in heads]
            vaug_bf = [jnp.concatenate([vv_c[rows, hcols(VV_M, h)], ones_col], axis=1)
                       for h in heads]
            qk = per_head(lambda h: _dot_nt(q[h].astype(BF16), k[h].astype(BF16)))
            caug = [caug_scr[h] for h in heads]
            m_old = [m_scr[h, 0:1, 0:1] for h in heads]
            i_row = [rows_t[h:h + 1, :] for h in heads]
            b_row = [rows_t[N_HEADS + h:N_HEADS + h + 1, :] for h in heads]
            i_col = [gates[:, h:h + 1] for h in heads]
            b_col = [bcum[:, N_HEADS + h:N_HEADS + h + 1] for h in heads]
            a_col = [b_col[h] + m_old[h] for h in heads]
            dm = per_head(lambda h: (b_col[h] + (i_row[h] - b_row[h])) + causal_add)
            mt = [jnp.maximum(a_col[h], jnp.max(dm[h], axis=-1, keepdims=True)) for h in heads]
            w_inter = [jnp.exp(a_col[h] - mt[h]) for h in heads]
            res = per_head(lambda h: _dot(
                jnp.concatenate([(qk[h] * jnp.exp(dm[h] - mt[h])).astype(BF16),
                                 (q[h] * w_inter[h]).astype(BF16)], axis=1),
                jnp.concatenate([vaug_bf[h], caug[h].astype(BF16)], axis=0)))
            b_last = [b_col[h][CHUNK - 1:CHUNK, :] for h in heads]
            g_col = [b_last[h] - b_col[h] + i_col[h] for h in heads]
            m_new = [jnp.maximum(b_last[h] + m_old[h], jnp.max(g_col[h], axis=0, keepdims=True))
                     for h in heads]
            wk = [jnp.exp(g_col[h] - m_new[h]) for h in heads]
            wc = [jnp.exp(b_last[h] + m_old[h] - m_new[h]) for h in heads]

            def mlstm_update(h):
                caug_scr[h] = caug[h] * wc[h] + _dot_tn((k[h] * wk[h]).astype(BF16), vaug_bf[h])
                m_scr[h] = jnp.broadcast_to(m_new[h], (SUBLANES, LANES))

            per_head(mlstm_update)

            def mlstm_out(h):
                den = jnp.maximum(jnp.abs(res[h][:, HEAD_DIM:HEAD_DIM + 1]), jnp.exp(-mt[h]))
                hn = _head_norm(res[h][:, :HEAD_DIM] / den, g_m_ref[:, hcols(0, h)])
                mix_scr[rows, hcols(GROUP_W, h)] = (hn * gz_c[rows, hcols(GZ_M, h)]).astype(BF16)

            per_head(mlstm_out)

            emit_proj_pieces()
            mix = _dot(mix_scr[rows, :], w_out_ref[...])
            y_ref[0, rows, :] = _layer_norm(ALPHA * x_ref[0, rows, :] + mix, ln_g_ref[...], ln_b_ref[...])

        assert slots_done[0] == n_slots and len(pieces) == TAIL_PIECES
        while pieces:
            pieces.pop(0)()

    set_a = (rqk_a, vv_a, gz_a, qkm_a, gt_a)
    set_b = (rqk_b, vv_b, gz_b, qkm_b, gt_b)
    parity = lax.rem(g, 2)

    @pl.when(parity == 0)
    def _even_step():
        step_body(set_a, set_b)

    @pl.when(parity == 1)
    def _odd_step():
        step_body(set_b, set_a)

    @pl.when(jnp.logical_and(tn == nt - 1, g < n_blocks))
    def _write_conv_state():
        conv_out_ref[0] = u_scr[SUBLANES + block_t - carry_rows:SUBLANES + block_t, :]

    @pl.when(jnp.logical_and(t == nt - 1, g > 0))
    def _write_state():
        s_out_ref[0] = s_scr[...]
        m_out_ref[0] = jnp.zeros((SUBLANES, LANES), F32)
        for h in heads:
            caug = caug_scr[h]
            c_out_ref[0, h] = caug[:, :HEAD_DIM]
            n_out_ref[0, h:h + 1, :] = caug[:, HEAD_DIM:].T[0:1, :]
            m_out_ref[0, h:h + 1, :] = m_scr[h, 0:1, :]

    u_scr[0:SUBLANES, :] = u_scr[block_t:block_t + SUBLANES, :]


def _const_spec(shape):
    nd = len(shape)
    return pl.BlockSpec(shape, lambda g: (0,) * nd)


def _layer_spec(shape, layer):
    return pl.BlockSpec((None,) + tuple(shape[1:]), lambda g: (layer,) + (0,) * (len(shape) - 1))


def _prompt_layer(layer, x, w_in_bf, conv_w, conv_b, gbias, g_ret, g_m, w_out_bf, ln_g, ln_b, tabs):
    B, T, _ = x.shape
    bt = PROMPT_BLOCK_T
    assert T % bt == 0 and bt % CHUNK == 0 and GROUP_W % PROJ_PIECE_COLS == 0
    nt = T // bt
    n_blocks = B * nt
    n_chunks = bt // CHUNK
    cos_t, sin_t, kcos_t, ksin_t, decay, qd, kd, cdec, tril, mask, e0 = tabs

    def nxt(g):
        return jnp.minimum(g, n_blocks - 1)

    def cur(g):
        return jnp.maximum(g - 1, 0)

    row_spec = pl.BlockSpec((bt, LANES), lambda g: (nxt(g) % nt, 0))
    in_specs = [
        pl.BlockSpec((1, bt, D_MODEL), lambda g: (nxt(g) // nt, nxt(g) % nt, 0)),
        pl.BlockSpec((1, bt, D_MODEL), lambda g: (cur(g) // nt, cur(g) % nt, 0)),
        _layer_spec(w_in_bf.shape, layer), _const_spec(conv_w.shape), _const_spec(conv_b.shape),
        _const_spec(gbias.shape), _const_spec(g_ret.shape), _const_spec(g_m.shape),
        _layer_spec(w_out_bf.shape, layer), _const_spec(ln_g.shape), _const_spec(ln_b.shape),
        row_spec, row_spec, row_spec, row_spec,
        _const_spec(decay.shape), _const_spec(qd.shape), _const_spec(kd.shape),
        pl.BlockSpec(memory_space=pltpu.SMEM),
        _const_spec(tril.shape), _const_spec(mask.shape), _const_spec(e0.shape),
    ]
    out_shape = (
        jax.ShapeDtypeStruct((B, T, D_MODEL), F32),
        jax.ShapeDtypeStruct((B, N_HEADS, HEAD_DIM, HEAD_DIM), F32),
        jax.ShapeDtypeStruct((B, N_HEADS, HEAD_DIM, HEAD_DIM), F32),
        jax.ShapeDtypeStruct((B, N_HEADS, HEAD_DIM), F32),
        jax.ShapeDtypeStruct((B, SUBLANES, LANES), F32),
        jax.ShapeDtypeStruct((B, CONV_W - 1, 2 * GROUP_W), F32),
    )
    out_specs = (
        pl.BlockSpec((1, bt, D_MODEL), lambda g: (cur(g) // nt, cur(g) % nt, 0)),
        pl.BlockSpec((1, N_HEADS, HEAD_DIM, HEAD_DIM), lambda g: (cur(g) // nt, 0, 0, 0)),
        pl.BlockSpec((1, N_HEADS, HEAD_DIM, HEAD_DIM), lambda g: (cur(g) // nt, 0, 0, 0)),
        pl.BlockSpec((1, N_HEADS, HEAD_DIM), lambda g: (cur(g) // nt, 0, 0)),
        pl.BlockSpec((1, SUBLANES, LANES), lambda g: (cur(g) // nt, 0, 0)),
        pl.BlockSpec((1, CONV_W - 1, 2 * GROUP_W), lambda g: (nxt(g) // nt, 0, 0)),
    )
    operand_set = [
        pltpu.VMEM((bt, 4 * GROUP_W), BF16),
        pltpu.VMEM((bt, 2 * GROUP_W), BF16),
        pltpu.VMEM((bt, 2 * GROUP_W), F32),
        pltpu.VMEM((bt, 2 * GROUP_W), F32),
        pltpu.VMEM((n_chunks * GT_PER_CHUNK, CHUNK, LANES), F32),
    ]
    scratch = operand_set + operand_set + [
        pltpu.VMEM((bt, D_MODEL), BF16),
        pltpu.VMEM((SUBLANES + bt, 2 * GROUP_W), F32),
        pltpu.VMEM((bt, 2 * GROUP_W), BF16),
        pltpu.VMEM((N_HEADS, HEAD_DIM, HEAD_DIM), F32),
        pltpu.VMEM((N_HEADS, HEAD_DIM, 2 * HEAD_DIM), F32),
        pltpu.VMEM((N_HEADS, SUBLANES, LANES), F32),
    ]
    y, s, c, n, m, cv = pl.pallas_call(
        functools.partial(_prompt_kernel, nt, n_blocks),
        grid=(n_blocks + 1,),
        in_specs=in_specs,
        out_specs=out_specs,
        out_shape=out_shape,
        scratch_shapes=scratch,
        compiler_params=pltpu.CompilerParams(
            dimension_semantics=("arbitrary",),
            vmem_limit_bytes=VMEM_LIMIT_BYTES),
        name="prompt_layer",
    )(x, x, w_in_bf, conv_w, conv_b, gbias, g_ret, g_m, w_out_bf, ln_g, ln_b,
      cos_t, sin_t, kcos_t, ksin_t, decay, qd, kd, cdec, tril, mask, e0)
    return y, s, c, n, m[:, :N_HEADS, 0], cv


def _sample_kernel(x_ref, w_in_ref, conv_w_ref, conv_b_ref, gbias_ref, g_ret_ref, g_m_ref,
                   w_out_ref, ln_g_ref, ln_b_ref, rot_ref, gam_ref, esel_ref,
                   s_ref, c_ref, n_ref, m_ref, cv_ref,
                   y_ref, s_out_ref, c_out_ref, n_out_ref, m_out_ref, cv_out_ref,
                   proj_scr, xcur_scr, mix_scr):
    layer = pl.program_id(0)
    j = pl.program_id(1)
    bb = s_ref.shape[1]

    @pl.when(jnp.logical_and(layer == 0, j == 0))
    def _load_x():
        xcur_scr[...] = x_ref[...]

    @pl.when(j == 0)
    def _project():
        xb = xcur_scr[...].astype(BF16)
        for lo in range(0, N_PAD, 512):
            hi = min(lo + 512, N_PAD)
            proj_scr[:, lo:hi] = _dot(xb, w_in_ref[0, :, lo:hi])

    r0 = pl.multiple_of(j * bb, bb)
    rows = pl.ds(r0, bb)
    cos_t = rot_ref[0:1, :]
    sin_t = rot_ref[1:2, :]
    kcos_t = rot_ref[2:3, :]
    ksin_t = rot_ref[3:4, :]
    esel = esel_ref[...]
    row_id = lax.broadcasted_iota(jnp.int32, (bb, bb * HEAD_DIM), 0)
    blk_id = jnp.right_shift(lax.broadcasted_iota(jnp.int32, (bb, bb * HEAD_DIM), 1), 7)
    pad_rows = jnp.zeros((LANES - bb, HEAD_DIM), F32)
    pad_wide = jnp.zeros((LANES - bb, bb * HEAD_DIM), BF16)

    def col_form(x8):
        return jnp.concatenate([x8, pad_rows], axis=0).T.astype(BF16)

    def outer_all(k8, v8):
        vt = jnp.concatenate([v8] * bb, axis=1)
        vsel = jnp.where(row_id == blk_id, vt, 0.0).astype(BF16)
        return _dot(col_form(k8), jnp.concatenate([vsel, pad_wide], axis=0))

    def col_bcast_all(q8):
        return _dot(col_form(q8), esel)

    for h in range(N_HEADS):
        hc = h * HEAD_DIM
        q8 = _rotary(proj_scr[rows, COL_RQ + hc:COL_RQ + hc + HEAD_DIM], cos_t, sin_t)
        k8 = _rotary(proj_scr[rows, COL_RK + hc:COL_RK + hc + HEAD_DIM], kcos_t, ksin_t)
        v8 = proj_scr[rows, COL_RV + hc:COL_RV + hc + HEAD_DIM]
        kv_all = outer_all(k8, v8)
        qc_all = col_bcast_all(q8)
        gamma = gam_ref[h]
        o_rows = []
        for r in range(bb):
            blk = slice(r * HEAD_DIM, (r + 1) * HEAD_DIM)
            s_new = s_ref[0, r, h] * gamma + kv_all[:, blk]
            s_out_ref[0, r, h] = s_new
            o_rows.append(jnp.sum(qc_all[:, blk] * s_new, axis=0, keepdims=True))
        o8 = _head_norm(jnp.concatenate(o_rows, axis=0), g_ret_ref[0, :, hc:hc + HEAD_DIM])
        z8 = proj_scr[rows, COL_RZ + hc:COL_RZ + hc + HEAD_DIM]
        mix_scr[rows, hc:hc + HEAD_DIM] = o8 * _silu(z8)

    u8 = proj_scr[rows, COL_MQK:COL_MQK + 2 * GROUP_W]
    acc = conv_b_ref[0] + u8 * conv_w_ref[0, CONV_W - 1:CONV_W, :]
    for jj in range(CONV_W - 1):
        acc = acc + cv_ref[0, :, jj, :] * conv_w_ref[0, jj:jj + 1, :]
    for jj in range(1, CONV_W - 1):
        cv_out_ref[0, :, jj - 1, :] = cv_ref[0, :, jj, :]
    cv_out_ref[0, :, CONV_W - 2, :] = u8
    qk8 = _silu(acc)

    gates = proj_scr[rows, COL_GATE:COL_GATE + LANES] + gbias_ref[0]
    i_al = pltpu.roll(gates, N_HEADS, axis=1)
    bm = _log_sigmoid(gates) + m_ref[0]
    m_new = jnp.maximum(bm, i_al)
    wk = jnp.exp(i_al - m_new)
    wc = jnp.exp(bm - m_new)
    einv = jnp.exp(-m_new)
    m_out_ref[0] = m_new
    for h in range(N_HEADS):
        hc = h * HEAD_DIM
        gl = N_HEADS + h
        q8 = qk8[:, hc:hc + HEAD_DIM]
        k8 = qk8[:, GROUP_W + hc:GROUP_W + hc + HEAD_DIM] * QK_SCALE
        v8 = proj_scr[rows, COL_MV + hc:COL_MV + hc + HEAD_DIM]
        wk_h = wk[:, gl:gl + 1]
        wc_h = jnp.broadcast_to(wc[:, gl:gl + 1], (bb, HEAD_DIM))
        kw8 = k8 * wk_h
        kv_all = outer_all(kw8, v8)
        qc_all = col_bcast_all(q8)
        n_new = n_ref[0, :, h, :] * wc_h + kw8
        n_out_ref[0, :, h, :] = n_new
        num_rows = []
        for r in range(bb):
            blk = slice(r * HEAD_DIM, (r + 1) * HEAD_DIM)
            c_new = c_ref[0, r, h] * wc_h[r:r + 1, :] + kv_all[:, blk]
            c_out_ref[0, r, h] = c_new
            num_rows.append(jnp.sum(qc_all[:, blk] * c_new, axis=0, keepdims=True))
        num = jnp.concatenate(num_rows, axis=0)
        q_bf = q8.astype(BF16).astype(F32)
        den = jnp.sum(q_bf * n_new, axis=-1, keepdims=True)
        hout = num / jnp.maximum(jnp.abs(den), einv[:, gl:gl + 1])
        hn = _head_norm(hout, g_m_ref[0, :, hc:hc + HEAD_DIM])
        og = _sigmoid(proj_scr[rows, COL_MO + hc:COL_MO + hc + HEAD_DIM])
        zg = _silu(proj_scr[rows, COL_MZ + hc:COL_MZ + hc + HEAD_DIM])
        mix_scr[rows, GROUP_W + hc:GROUP_W + hc + HEAD_DIM] = hn * og * zg

    @pl.when(j == pl.num_programs(1) - 1)
    def _finish_layer():
        mix = _dot(mix_scr[...].astype(BF16), w_out_ref[0])
        y = _layer_norm(ALPHA * xcur_scr[...] + mix, ln_g_ref[0], ln_b_ref[0])
        xcur_scr[...] = y
        y_ref[...] = y


def _sample_layers(x, w_in_bf, conv_w, conv_b, gbias, g_ret, g_m, w_out_bf, ln_g, ln_b,
                   rot, gam, esel, state_ret, state_c, state_n, m_pad, state_conv):
    bs = x.shape[0]
    bb = SAMPLE_BLOCK_B
    assert bs % bb == 0
    nb = bs // bb

    def lspec(shape):
        nd = len(shape)
        return pl.BlockSpec((1,) + tuple(shape[1:]), lambda l, j: (l,) + (0,) * (nd - 1))

    def cspec(shape):
        nd = len(shape)
        return pl.BlockSpec(tuple(shape), lambda l, j: (0,) * nd)

    mat_spec = pl.BlockSpec((1, bb, N_HEADS, HEAD_DIM, HEAD_DIM), lambda l, j: (l, j, 0, 0, 0))
    n_spec = pl.BlockSpec((1, bb, N_HEADS, HEAD_DIM), lambda l, j: (l, j, 0, 0))
    m_spec = pl.BlockSpec((1, bb, LANES), lambda l, j: (l, j, 0))
    cv_spec = pl.BlockSpec((1, bb, CONV_W - 1, 2 * GROUP_W), lambda l, j: (l, j, 0, 0))
    in_specs = [
        cspec(x.shape), lspec(w_in_bf.shape), lspec(conv_w.shape), lspec(conv_b.shape),
        lspec(gbias.shape), lspec(g_ret.shape), lspec(g_m.shape), lspec(w_out_bf.shape),
        lspec(ln_g.shape), lspec(ln_b.shape), cspec(rot.shape),
        pl.BlockSpec(memory_space=pltpu.SMEM), cspec(esel.shape),
        mat_spec, mat_spec, n_spec, m_spec, cv_spec,
    ]
    out_shape = (
        jax.ShapeDtypeStruct(x.shape, F32),
        jax.ShapeDtypeStruct(state_ret.shape, F32),
        jax.ShapeDtypeStruct(state_c.shape, F32),
        jax.ShapeDtypeStruct(state_n.shape, F32),
        jax.ShapeDtypeStruct(m_pad.shape, F32),
        jax.ShapeDtypeStruct(state_conv.shape, F32),
    )
    out_specs = (cspec(x.shape), mat_spec, mat_spec, n_spec, m_spec, cv_spec)
    scratch = [
        pltpu.VMEM((bs, N_PAD), F32),
        pltpu.VMEM((bs, D_MODEL), F32),
        pltpu.VMEM((bs, 2 * GROUP_W), F32),
    ]
    return pl.pallas_call(
        _sample_kernel,
        grid=(DEPTH, nb),
        in_specs=in_specs,
        out_specs=out_specs,
        out_shape=out_shape,
        scratch_shapes=scratch,
        compiler_params=pltpu.CompilerParams(
            dimension_semantics=("arbitrary", "arbitrary"),
            vmem_limit_bytes=VMEM_LIMIT_BYTES),
        name="sample_layers",
    )(x, w_in_bf, conv_w, conv_b, gbias, g_ret, g_m, w_out_bf, ln_g, ln_b,
      rot, gam, esel, state_ret, state_c, state_n, m_pad, state_conv)


def _cast_weight_kernel(n_in, w_ref, o_ref, tail_scr):
    n_out = o_ref.shape[-1]
    n_full = (n_in // LANES) * LANES
    o_ref[0, :, 0:n_full] = w_ref[0, :, 0:n_full].astype(BF16)
    if n_full < n_out:
        tail_scr[...] = jnp.zeros_like(tail_scr)
        tail_scr[:, 0:n_in - n_full] = w_ref[0, :, n_full:n_in]
        o_ref[0, :, n_full:n_out] = tail_scr[...].astype(BF16)


def _cast_weight(w, n_out):
    depth, k, n_in = w.shape
    rows = WEIGHT_CAST_ROWS
    assert k % rows == 0 and n_out % LANES == 0 and 0 <= n_out - n_in < LANES
    return pl.pallas_call(
        functools.partial(_cast_weight_kernel, n_in),
        grid=(depth, k // rows),
        in_specs=[pl.BlockSpec((1, rows, n_in), lambda l, r: (l, r, 0))],
        out_specs=pl.BlockSpec((1, rows, n_out), lambda l, r: (l, r, 0)),
        out_shape=jax.ShapeDtypeStruct((depth, k, n_out), BF16),
        scratch_shapes=[pltpu.VMEM((rows, LANES), F32)],
        compiler_params=pltpu.CompilerParams(dimension_semantics=("parallel", "parallel")),
        name="cast_weight",
    )(w)


def _rotary_tables(pos):
    half = HEAD_DIM // 2
    inv = ROPE_BASE ** (-jnp.arange(half, dtype=F32) / half)
    ang = pos.astype(F32)[:, None] * inv[None, :]
    cos = jnp.cos(ang)
    sin = jnp.sin(ang)
    cos_t = jnp.concatenate([cos, cos], axis=-1)
    sin_t = jnp.concatenate([-sin, sin], axis=-1)
    return cos_t, sin_t, cos_t * QK_SCALE, sin_t * QK_SCALE


def _retention_tables():
    L = CHUNK
    log_gamma = jnp.log(1.0 - 2.0 ** (-5.0 - jnp.arange(N_HEADS, dtype=F32)))
    idx = jnp.arange(L, dtype=F32)
    diff = idx[:, None] - idx[None, :]
    decay = jnp.exp(log_gamma[:, None, None] * jnp.maximum(diff, 0.0)) * (diff >= 0)
    q_decay = jnp.exp(log_gamma[:, None] * (idx + 1.0))
    k_decay = jnp.exp(log_gamma[:, None] * (L - 1.0 - idx))
    c_decay = jnp.exp(log_gamma * L)
    qd = jnp.broadcast_to(q_decay[:, :, None], (N_HEADS, L, HEAD_DIM))
    kd = jnp.broadcast_to(k_decay[:, :, None], (N_HEADS, L, HEAD_DIM))
    gamma1 = jnp.exp(log_gamma * 1.0)
    return decay.astype(F32), qd, kd, c_decay, gamma1


def kernel(x_prompt, x_sample, state_ret, state_mlstm_C, state_mlstm_n, state_mlstm_m, state_conv,
           w_in, conv_w, conv_b, b_i, b_f, g_ret, g_m, w_out, ln_g, ln_b):
    B, T, _ = x_prompt.shape
    Bs, Ts, _ = x_sample.shape
    assert Ts == 1

    w_in_bf = _cast_weight(w_in, N_PAD)
    w_out_bf = _cast_weight(w_out, D_MODEL)
    gbias = jnp.pad(jnp.concatenate([b_i, b_f], axis=-1), ((0, 0), (0, LANES - 2 * N_HEADS)))
    gbias = gbias.reshape(DEPTH, 1, LANES)
    conv_b3 = conv_b.reshape(DEPTH, 1, 2 * GROUP_W)
    g_ret3 = g_ret.reshape(DEPTH, 1, GROUP_W)
    g_m3 = g_m.reshape(DEPTH, 1, GROUP_W)
    ln_g3 = ln_g.reshape(DEPTH, 1, D_MODEL)
    ln_b3 = ln_b.reshape(DEPTH, 1, D_MODEL)

    decay, qd, kd, c_decay, gamma1 = _retention_tables()
    idx = jnp.arange(CHUNK)
    causal = idx[:, None] >= idx[None, :]
    tril = causal.astype(BF16)
    mask_add = jnp.where(causal, 0.0, -jnp.inf).astype(F32)
    e0 = (jnp.arange(LANES)[None, :] == 0).astype(BF16) * jnp.ones((CHUNK, 1), BF16)
    tabs_p = _rotary_tables(jnp.arange(T, dtype=jnp.int32)) + (decay, qd, kd, c_decay, tril, mask_add, e0)

    xp = x_prompt
    rp, cp, np_, mp, vp = [], [], [], [], []
    for l in range(DEPTH):
        xp, s, c, n, m, cv = _prompt_layer(
            l, xp, w_in_bf, conv_w[l], conv_b3[l], gbias[l], g_ret3[l], g_m3[l],
            w_out_bf, ln_g3[l], ln_b3[l], tabs_p)
        rp.append(s); cp.append(c); np_.append(n); mp.append(m); vp.append(cv)

    rot = jnp.concatenate(_rotary_tables(PAST_LEN + jnp.arange(Ts, dtype=jnp.int32)), axis=0)
    bb = SAMPLE_BLOCK_B
    esel = (jnp.arange(LANES)[:, None] == (jnp.arange(bb * HEAD_DIM)[None, :] // HEAD_DIM)).astype(BF16)
    m_pad = jnp.pad(state_mlstm_m, ((0, 0), (0, 0), (N_HEADS, LANES - 2 * N_HEADS)))
    ys, rs, cs, ns, ms_pad, vs = _sample_layers(
        x_sample.reshape(Bs, D_MODEL), w_in_bf, conv_w, conv_b3, gbias, g_ret3, g_m3, w_out_bf,
        ln_g3, ln_b3, rot, gamma1, esel, state_ret, state_mlstm_C, state_mlstm_n, m_pad, state_conv)
    ms = ms_pad[:, :, N_HEADS:2 * N_HEADS]

    return (xp, ys.reshape(Bs, Ts, D_MODEL),
            jnp.stack(rp), jnp.stack(cp), jnp.stack(np_), jnp.stack(mp), jnp.stack(vp),
            rs, cs, ns, ms, vs)
```

```python
import functools

import jax
import jax.numpy as jnp
from jax import lax
from jax.experimental import pallas as pl
from jax.experimental.pallas import tpu as pltpu

F32 = jnp.float32
BF16 = jnp.bfloat16

D_MODEL = 1024
DEPTH = 2
PAST_LEN = 16384
N_HEADS = 4
HEAD_DIM = 128
GROUP_W = N_HEADS * HEAD_DIM
CONV_W = 4
CHUNK = 128
ROPE_BASE = 10000.0
LN_EPS = 1e-5
GN_EPS = 1e-5
ALPHA = (2 * DEPTH) ** 0.25
QK_SCALE = HEAD_DIM ** -0.5

LANES = 128
SUBLANES = 8

COL_RQ = 0
COL_RK = COL_RQ + GROUP_W
COL_RV = COL_RK + GROUP_W
COL_RZ = COL_RV + GROUP_W
COL_MQK = COL_RZ + GROUP_W
COL_MV = COL_MQK + 2 * GROUP_W
COL_MO = COL_MV + GROUP_W
COL_MZ = COL_MO + GROUP_W
COL_GATE = COL_MZ + GROUP_W
N_IN = COL_GATE + 2 * N_HEADS
N_PAD = COL_GATE + LANES

PROMPT_BLOCK_T = 256
PROJ_PIECE_COLS = 256
SAMPLE_BLOCK_B = 8
WEIGHT_CAST_ROWS = 512
VMEM_LIMIT_BYTES = 56 * 1024 * 1024


def _sigmoid(x):
    return 1.0 / (1.0 + jnp.exp(-x))


def _silu(x):
    return x * _sigmoid(x)


def _log_sigmoid(x):
    return jnp.minimum(x, 0.0) - jnp.log1p(jnp.exp(-jnp.abs(x)))


def _dot(a, b):
    return jnp.dot(a, b, preferred_element_type=F32)


def _dot_nt(a, b):
    return lax.dot_general(a, b, (((1,), (1,)), ((), ())), preferred_element_type=F32)


def _dot_tn(a, b):
    return lax.dot_general(a, b, (((0,), (0,)), ((), ())), preferred_element_type=F32)


def _rotary(x, cos_t, sin_t):
    return x * cos_t + pltpu.roll(x, HEAD_DIM // 2, axis=1) * sin_t


def _head_norm(h, g):
    mu = jnp.mean(h, axis=-1, keepdims=True)
    d = h - mu
    var = jnp.mean(d * d, axis=-1, keepdims=True)
    return d * lax.rsqrt(var + GN_EPS) * g


def _layer_norm(x, g, b):
    mu = jnp.mean(x, axis=-1, keepdims=True)
    d = x - mu
    var = jnp.mean(d * d, axis=-1, keepdims=True)
    return d * lax.rsqrt(var + LN_EPS) * g + b


def _cumsum_rows(tril_bf, x):
    hi = x.astype(BF16)
    r1 = x - hi.astype(F32)
    mid = r1.astype(BF16)
    lo = (r1 - mid.astype(F32)).astype(BF16)
    return _dot(tril_bf, hi) + _dot(tril_bf, mid) + _dot(tril_bf, lo)


RQK_Q, RQK_QDEC, RQK_K, RQK_KDEC = 0, GROUP_W, 2 * GROUP_W, 3 * GROUP_W
VV_RET, VV_M = 0, GROUP_W
GZ_RET, GZ_M = 0, GROUP_W
GT_GATES, GT_BCUM, GT_ROWS, GT_PER_CHUNK = 0, 1, 2, 3
EMITS_PER_CHUNK = 9 * N_HEADS + 1
TAIL_PIECES = 0


def _prompt_kernel(nt, n_blocks, xn_ref, x_ref, w_in_ref, conv_w_ref, conv_b_ref, gbias_ref,
                   g_ret_ref, g_m_ref, w_out_ref, ln_g_ref, ln_b_ref,
                   cos_ref, sin_ref, kcos_ref, ksin_ref,
                   decay_ref, qd_ref, kd_ref, cdec_ref, tril_ref, mask_ref, e0_ref,
                   y_ref, s_out_ref, c_out_ref, n_out_ref, m_out_ref, conv_out_ref,
                   rqk_a, vv_a, gz_a, qkm_a, gt_a, rqk_b, vv_b, gz_b, qkm_b, gt_b,
                   xb_scr, u_scr, mix_scr, s_scr, caug_scr, m_scr):
    g = pl.program_id(0)
    t = lax.rem(jnp.maximum(g - 1, 0), nt)
    tn = lax.rem(jnp.minimum(g, n_blocks - 1), nt)
    block_t = x_ref.shape[1]
    n_chunks = block_t // CHUNK
    carry_rows = CONV_W - 1
    heads = range(N_HEADS)

    @pl.when(g == 0)
    def _init_pipeline():
        for ref in (rqk_b, vv_b, gz_b, qkm_b, gt_b):
            ref[...] = jnp.zeros_like(ref)

    @pl.when(t == 0)
    def _init_state():
        s_scr[...] = jnp.zeros_like(s_scr)
        caug_scr[...] = jnp.zeros_like(caug_scr)
        m_scr[...] = jnp.zeros_like(m_scr)

    @pl.when(tn == 0)
    def _init_conv_carry():
        u_scr[0:SUBLANES, :] = jnp.zeros((SUBLANES, 2 * GROUP_W), F32)

    def hcols(base, h):
        return slice(base + h * HEAD_DIM, base + (h + 1) * HEAD_DIM)

    def step_body(set_in, set_cur):
        rqk_i, vv_i, gz_i, qkm_i, gt_i = set_in
        rqk_c, vv_c, gz_c, qkm_c, gt_c = set_cur
        tril_bf = tril_ref[...]
        causal_add = mask_ref[...]
        ones_col = e0_ref[...]
        lane_id = lax.broadcasted_iota(jnp.int32, (CHUNK, LANES), 1)
        xb_scr[...] = xn_ref[0].astype(BF16)
        pw = PROJ_PIECE_COLS
        heads_per_piece = pw // HEAD_DIM

        def chunk_rows(c):
            return slice(c * CHUNK, (c + 1) * CHUNK)

        def proj(c, col, width=pw):
            return _dot(xb_scr[chunk_rows(c), :], w_in_ref[:, col:col + width])

        def piece_rot(c, i, col_base, cos_r, sin_r, dec_ref, dst, dst_dec):
            rows = chunk_rows(c)
            res = proj(c, col_base + i * pw)
            for hh in range(heads_per_piece):
                h = i * heads_per_piece + hh
                r = _rotary(res[:, hh * HEAD_DIM:(hh + 1) * HEAD_DIM], cos_r[rows, :], sin_r[rows, :])
                rqk_i[rows, hcols(dst, h)] = r.astype(BF16)
                rqk_i[rows, hcols(dst_dec, h)] = (r * dec_ref[h]).astype(BF16)

        def piece_cast(c, i, col_base, dst):
            vv_i[chunk_rows(c), dst + i * pw:dst + (i + 1) * pw] = proj(c, col_base + i * pw).astype(BF16)

        def piece_rz(c, i):
            gz_i[chunk_rows(c), GZ_RET + i * pw:GZ_RET + (i + 1) * pw] = _silu(proj(c, COL_RZ + i * pw))

        def piece_moz(c, i):
            og = _sigmoid(proj(c, COL_MO + i * pw))
            gz_i[chunk_rows(c), GZ_M + i * pw:GZ_M + (i + 1) * pw] = og * _silu(proj(c, COL_MZ + i * pw))

        def piece_conv(c, i):
            base = SUBLANES + c * CHUNK
            u_scr[base:base + CHUNK, i * pw:(i + 1) * pw] = proj(c, COL_MQK + i * pw)
            for cs in range(i * pw, (i + 1) * pw, LANES):
                cols = slice(cs, cs + LANES)
                acc = conv_b_ref[:, cols]
                for j in range(CONV_W):
                    r0 = base - carry_rows + j
                    acc = acc + u_scr[r0:r0 + CHUNK, cols] * conv_w_ref[j:j + 1, cols]
                act = _silu(acc)
                if cs >= GROUP_W:
                    act = act * QK_SCALE
                qkm_i[chunk_rows(c), cols] = act

        def piece_gate(c):
            gates = proj(c, COL_GATE, LANES) + gbias_ref[...]
            bcum = _cumsum_rows(tril_bf, _log_sigmoid(gates))
            gt_i[c * GT_PER_CHUNK + GT_GATES] = gates
            gt_i[c * GT_PER_CHUNK + GT_BCUM] = bcum
            gt_i[c * GT_PER_CHUNK + GT_ROWS] = jnp.where(lane_id < N_HEADS, gates, bcum).T

        n_col_pieces = GROUP_W // pw
        pieces = []
        for c in range(n_chunks):
            pieces.append(functools.partial(piece_gate, c))
            for i in range(n_col_pieces):
                pieces.append(functools.partial(piece_conv, c, 2 * i))
                pieces.append(functools.partial(piece_cast, c, i, COL_RV, VV_RET))
                pieces.append(functools.partial(piece_rot, c, i, COL_RQ, cos_ref, sin_ref, qd_ref,
                                                RQK_Q, RQK_QDEC))
                pieces.append(functools.partial(piece_rz, c, i))
                pieces.append(functools.partial(piece_conv, c, 2 * i + 1))
                pieces.append(functools.partial(piece_cast, c, i, COL_MV, VV_M))
                pieces.append(functools.partial(piece_rot, c, i, COL_RK, kcos_ref, ksin_ref, kd_ref,
                                                RQK_K, RQK_KDEC))
                pieces.append(functools.partial(piece_moz, c, i))
        n_pieces = len(pieces)
        n_slots = EMITS_PER_CHUNK * n_chunks
        slots_done = [0]

        def emit_proj_pieces():
            slots_done[0] += 1
            target = -(-(slots_done[0] * (n_pieces - TAIL_PIECES)) // n_slots)
            while n_pieces - len(pieces) < target:
                pieces.pop(0)()

        def per_head(fn):
            out = []
            for h in heads:
                emit_proj_pieces()
                out.append(fn(h))
            return out

        for c in range(n_chunks):
            rows = slice(c * CHUNK, (c + 1) * CHUNK)

            q_bf = [rqk_c[rows, hcols(RQK_Q, h)] for h in heads]
            k_bf = [rqk_c[rows, hcols(RQK_K, h)] for h in heads]
            v_bf = [vv_c[rows, hcols(VV_RET, h)] for h in heads]
            sc = per_head(lambda h: _dot_nt(q_bf[h], k_bf[h]))
            state = [s_scr[h] for h in heads]
            upd = per_head(lambda h: _dot_tn(rqk_c[rows, hcols(RQK_KDEC, h)], v_bf[h]))
            for h in heads:
                s_scr[h] = state[h] * cdec_ref[h] + upd[h]
            o = per_head(lambda h: _dot(
                jnp.concatenate([(sc[h] * decay_ref[h]).astype(BF16),
                                 rqk_c[rows, hcols(RQK_QDEC, h)]], axis=1),
                jnp.concatenate([v_bf[h], state[h].astype(BF16)], axis=0)))

            def ret_out(h):
                hn = _head_norm(o[h], g_ret_ref[:, hcols(0, h)])
                mix_scr[rows, hcols(0, h)] = (hn * gz_c[rows, hcols(GZ_RET, h)]).astype(BF16)

            per_head(ret_out)

            gates = gt_c[c * GT_PER_CHUNK + GT_GATES]
            bcum = gt_c[c * GT_PER_CHUNK + GT_BCUM]
            rows_t = gt_c[c * GT_PER_CHUNK + GT_ROWS]
            q = [qkm_c[rows, hcols(0, h)] for h in heads]
            k = [qkm_c[rows, hcols(GROUP_W, h)] for h in heads]
            vaug_bf = [jnp.concatenate([vv_c[rows, hcols(VV_M, h)], ones_col], axis=1)
                       for h in heads]
            qk = per_head(lambda h: _dot_nt(q[h].astype(BF16), k[h].astype(BF16)))
            caug = [caug_scr[h] for h in heads]
            m_old = [m_scr[h, 0:1, 0:1] for h in heads]
            i_row = [rows_t[h:h + 1, :] for h in heads]
            b_row = [rows_t[N_HEADS + h:N_HEADS + h + 1, :] for h in heads]
            i_col = [gates[:, h:h + 1] for h in heads]
            b_col = [bcum[:, N_HEADS + h:N_HEADS + h + 1] for h in heads]
            a_col = [b_col[h] + m_old[h] for h in heads]
            dm = per_head(lambda h: (b_col[h] + (i_row[h] - b_row[h])) + causal_add)
            mt = [jnp.maximum(a_col[h], jnp.max(dm[h], axis=-1, keepdims=True)) for h in heads]
            w_inter = [jnp.exp(a_col[h] - mt[h]) for h in heads]
            res = per_head(lambda h: _dot(
                jnp.concatenate([(qk[h] * jnp.exp(dm[h] - mt[h])).astype(BF16),
                                 (q[h] * w_inter[h]).astype(BF16)], axis=1),
                jnp.concatenate([vaug_bf[h], caug[h].astype(BF16)], axis=0)))
            b_last = [b_col[h][CHUNK - 1:CHUNK, :] for h in heads]
            g_col = [b_last[h] - b_col[h] + i_col[h] for h in heads]
            m_new = [jnp.maximum(b_last[h] + m_old[h], jnp.max(g_col[h], axis=0, keepdims=True))
                     for h in heads]
            wk = [jnp.exp(g_col[h] - m_new[h]) for h in heads]
            wc = [jnp.exp(b_last[h] + m_old[h] - m_new[h]) for h in heads]

            def mlstm_update(h):
                caug_scr[h] = caug[h] * wc[h] + _dot_tn((k[h] * wk[h]).astype(BF16), vaug_bf[h])
                m_scr[h] = jnp.broadcast_to(m_new[h], (SUBLANES, LANES))

            per_head(mlstm_update)

            def mlstm_out(h):
                den = jnp.maximum(jnp.abs(res[h][:, HEAD_DIM:HEAD_DIM + 1]), jnp.exp(-mt[h]))
                hn = _head_norm(res[h][:, :HEAD_DIM] / den, g_m_ref[:, hcols(0, h)])
                mix_scr[rows, hcols(GROUP_W, h)] = (hn * gz_c[rows, hcols(GZ_M, h)]).astype(BF16)

            per_head(mlstm_out)

            emit_proj_pieces()
            mix = _dot(mix_scr[rows, :], w_out_ref[...])
            y_ref[0, rows, :] = _layer_norm(ALPHA * x_ref[0, rows, :] + mix, ln_g_ref[...], ln_b_ref[...])

        assert slots_done[0] == n_slots and len(pieces) == TAIL_PIECES
        while pieces:
            pieces.pop(0)()

    set_a = (rqk_a, vv_a, gz_a, qkm_a, gt_a)
    set_b = (rqk_b, vv_b, gz_b, qkm_b, gt_b)
    parity = lax.rem(g, 2)

    @pl.when(parity == 0)
    def _even_step():
        step_body(set_a, set_b)

    @pl.when(parity == 1)
    def _odd_step():
        step_body(set_b, set_a)

    @pl.when(jnp.logical_and(tn == nt - 1, g < n_blocks))
    def _write_conv_state():
        conv_out_ref[0] = u_scr[SUBLANES + block_t - carry_rows:SUBLANES + block_t, :]

    @pl.when(jnp.logical_and(t == nt - 1, g > 0))
    def _write_state():
        s_out_ref[0] = s_scr[...]
        m_out_ref[0] = jnp.zeros((SUBLANES, LANES), F32)
        for h in heads:
            caug = caug_scr[h]
            c_out_ref[0, h] = caug[:, :HEAD_DIM]
            n_out_ref[0, h:h + 1, :] = caug[:, HEAD_DIM:].T[0:1, :]
            m_out_ref[0, h:h + 1, :] = m_scr[h, 0:1, :]

    u_scr[0:SUBLANES, :] = u_scr[block_t:block_t + SUBLANES, :]


def _const_spec(shape):
    nd = len(shape)
    return pl.BlockSpec(shape, lambda g: (0,) * nd)


def _layer_spec(shape, layer):
    return pl.BlockSpec((None,) + tuple(shape[1:]), lambda g: (layer,) + (0,) * (len(shape) - 1))


def _prompt_layer(layer, x, w_in_bf, conv_w, conv_b, gbias, g_ret, g_m, w_out_bf, ln_g, ln_b, tabs):
    B, T, _ = x.shape
    bt = PROMPT_BLOCK_T
    assert T % bt == 0 and bt % CHUNK == 0 and GROUP_W % PROJ_PIECE_COLS == 0
    nt = T // bt
    n_blocks = B * nt
    n_chunks = bt // CHUNK
    cos_t, sin_t, kcos_t, ksin_t, decay, qd, kd, cdec, tril, mask, e0 = tabs

    def nxt(g):
        return jnp.minimum(g, n_blocks - 1)

    def cur(g):
        return jnp.maximum(g - 1, 0)

    row_spec = pl.BlockSpec((bt, LANES), lambda g: (nxt(g) % nt, 0))
    in_specs = [
        pl.BlockSpec((1, bt, D_MODEL), lambda g: (nxt(g) // nt, nxt(g) % nt, 0)),
        pl.BlockSpec((1, bt, D_MODEL), lambda g: (cur(g) // nt, cur(g) % nt, 0)),
        _layer_spec(w_in_bf.shape, layer), _const_spec(conv_w.shape), _const_spec(conv_b.shape),
        _const_spec(gbias.shape), _const_spec(g_ret.shape), _const_spec(g_m.shape),
        _layer_spec(w_out_bf.shape, layer), _const_spec(ln_g.shape), _const_spec(ln_b.shape),
        row_spec, row_spec, row_spec, row_spec,
        _const_spec(decay.shape), _const_spec(qd.shape), _const_spec(kd.shape),
        pl.BlockSpec(memory_space=pltpu.SMEM),
        _const_spec(tril.shape), _const_spec(mask.shape), _const_spec(e0.shape),
    ]
    out_shape = (
        jax.ShapeDtypeStruct((B, T, D_MODEL), F32),
        jax.ShapeDtypeStruct((B, N_HEADS, HEAD_DIM, HEAD_DIM), F32),
        jax.ShapeDtypeStruct((B, N_HEADS, HEAD_DIM, HEAD_DIM), F32),
        jax.ShapeDtypeStruct((B, N_HEADS, HEAD_DIM), F32),
        jax.ShapeDtypeStruct((B, SUBLANES, LANES), F32),
        jax.ShapeDtypeStruct((B, CONV_W - 1, 2 * GROUP_W), F32),
    )
    out_specs = (
        pl.BlockSpec((1, bt, D_MODEL), lambda g: (cur(g) // nt, cur(g) % nt, 0)),
        pl.BlockSpec((1, N_HEADS, HEAD_DIM, HEAD_DIM), lambda g: (cur(g) // nt, 0, 0, 0)),
        pl.BlockSpec((1, N_HEADS, HEAD_DIM, HEAD_DIM), lambda g: (cur(g) // nt, 0, 0, 0)),
        pl.BlockSpec((1, N_HEADS, HEAD_DIM), lambda g: (cur(g) // nt, 0, 0)),
        pl.BlockSpec((1, SUBLANES, LANES), lambda g: (cur(g) // nt, 0, 0)),
        pl.BlockSpec((1, CONV_W - 1, 2 * GROUP_W), lambda g: (nxt(g) // nt, 0, 0)),
    )
    operand_set = [
        pltpu.VMEM((bt, 4 * GROUP_W), BF16),
        pltpu.VMEM((bt, 2 * GROUP_W), BF16),
        pltpu.VMEM((bt, 2 * GROUP_W), F32),
        pltpu.VMEM((bt, 2 * GROUP_W), F32),
        pltpu.VMEM((n_chunks * GT_PER_CHUNK, CHUNK, LANES), F32),
    ]
    scratch = operand_set + operand_set + [
        pltpu.VMEM((bt, D_MODEL), BF16),
        pltpu.VMEM((SUBLANES + bt, 2 * GROUP_W), F32),
        pltpu.VMEM((bt, 2 * GROUP_W), BF16),
        pltpu.VMEM((N_HEADS, HEAD_DIM, HEAD_DIM), F32),
        pltpu.VMEM((N_HEADS, HEAD_DIM, 2 * HEAD_DIM), F32),
        pltpu.VMEM((N_HEADS, SUBLANES, LANES), F32),
    ]
    y, s, c, n, m, cv = pl.pallas_call(
        functools.partial(_prompt_kernel, nt, n_blocks),
        grid=(n_blocks + 1,),
        in_specs=in_specs,
        out_specs=out_specs,
        out_shape=out_shape,
        scratch_shapes=scratch,
        compiler_params=pltpu.CompilerParams(
            dimension_semantics=("arbitrary",),
            vmem_limit_bytes=VMEM_LIMIT_BYTES),
        name="prompt_layer",
    )(x, x, w_in_bf, conv_w, conv_b, gbias, g_ret, g_m, w_out_bf, ln_g, ln_b,
      cos_t, sin_t, kcos_t, ksin_t, decay, qd, kd, cdec, tril, mask, e0)
    return y, s, c, n, m[:, :N_HEADS, 0], cv


def _sample_kernel(x_ref, w_in_ref, conv_w_ref, conv_b_ref, gbias_ref, g_ret_ref, g_m_ref,
                   w_out_ref, ln_g_ref, ln_b_ref, rot_ref, gam_ref, esel_ref,
                   s_ref, c_ref, n_ref, m_ref, cv_ref,
                   y_ref, s_out_ref, c_out_ref, n_out_ref, m_out_ref, cv_out_ref,
                   proj_scr, xcur_scr, mix_scr):
    layer = pl.program_id(0)
    j = pl.program_id(1)
    bb = s_ref.shape[1]

    @pl.when(jnp.logical_and(layer == 0, j == 0))
    def _load_x():
        xcur_scr[...] = x_ref[...]

    @pl.when(j == 0)
    def _project():
        xb = xcur_scr[...].astype(BF16)
        for lo in range(0, N_PAD, 512):
            hi = min(lo + 512, N_PAD)
            proj_scr[:, lo:hi] = _dot(xb, w_in_ref[0, :, lo:hi])

    r0 = pl.multiple_of(j * bb, bb)
    rows = pl.ds(r0, bb)
    cos_t = rot_ref[0:1, :]
    sin_t = rot_ref[1:2, :]
    kcos_t = rot_ref[2:3, :]
    ksin_t = rot_ref[3:4, :]
    esel = esel_ref[...]
    row_id = lax.broadcasted_iota(jnp.int32, (bb, bb * HEAD_DIM), 0)
    blk_id = jnp.right_shift(lax.broadcasted_iota(jnp.int32, (bb, bb * HEAD_DIM), 1), 7)
    pad_rows = jnp.zeros((LANES - bb, HEAD_DIM), F32)
    pad_wide = jnp.zeros((LANES - bb, bb * HEAD_DIM), BF16)

    def col_form(x8):
        return jnp.concatenate([x8, pad_rows], axis=0).T.astype(BF16)

    def outer_all(k8, v8):
        vt = jnp.concatenate([v8] * bb, axis=1)
        vsel = jnp.where(row_id == blk_id, vt, 0.0).astype(BF16)
        return _dot(col_form(k8), jnp.concatenate([vsel, pad_wide], axis=0))

    def col_bcast_all(q8):
        return _dot(col_form(q8), esel)

    for h in range(N_HEADS):
        hc = h * HEAD_DIM
        q8 = _rotary(proj_scr[rows, COL_RQ + hc:COL_RQ + hc + HEAD_DIM], cos_t, sin_t)
        k8 = _rotary(proj_scr[rows, COL_RK + hc:COL_RK + hc + HEAD_DIM], kcos_t, ksin_t)
        v8 = proj_scr[rows, COL_RV + hc:COL_RV + hc + HEAD_DIM]
        kv_all = outer_all(k8, v8)
        qc_all = col_bcast_all(q8)
        gamma = gam_ref[h]
        o_rows = []
        for r in range(bb):
            blk = slice(r * HEAD_DIM, (r + 1) * HEAD_DIM)
            s_new = s_ref[0, r, h] * gamma + kv_all[:, blk]
            s_out_ref[0, r, h] = s_new
            o_rows.append(jnp.sum(qc_all[:, blk] * s_new, axis=0, keepdims=True))
        o8 = _head_norm(jnp.concatenate(o_rows, axis=0), g_ret_ref[0, :, hc:hc + HEAD_DIM])
        z8 = proj_scr[rows, COL_RZ + hc:COL_RZ + hc + HEAD_DIM]
        mix_scr[rows, hc:hc + HEAD_DIM] = o8 * _silu(z8)

    u8 = proj_scr[rows, COL_MQK:COL_MQK + 2 * GROUP_W]
    acc = conv_b_ref[0] + u8 * conv_w_ref[0, CONV_W - 1:CONV_W, :]
    for jj in range(CONV_W - 1):
        acc = acc + cv_ref[0, jj] * conv_w_ref[0, jj:jj + 1, :]
    for jj in range(1, CONV_W - 1):
        cv_out_ref[0, jj - 1] = cv_ref[0, jj]
    cv_out_ref[0, CONV_W - 2] = u8
    qk8 = _silu(acc)

    gates = proj_scr[rows, COL_GATE:COL_GATE + LANES] + gbias_ref[0]
    i_al = pltpu.roll(gates, N_HEADS, axis=1)
    bm = _log_sigmoid(gates) + m_ref[0]
    m_new = jnp.maximum(bm, i_al)
    wk = jnp.exp(i_al - m_new)
    wc = jnp.exp(bm - m_new)
    einv = jnp.exp(-m_new)
    m_out_ref[0] = m_new
    for h in range(N_HEADS):
        hc = h * HEAD_DIM
        gl = N_HEADS + h
        q8 = qk8[:, hc:hc + HEAD_DIM]
        k8 = qk8[:, GROUP_W + hc:GROUP_W + hc + HEAD_DIM] * QK_SCALE
        v8 = proj_scr[rows, COL_MV + hc:COL_MV + hc + HEAD_DIM]
        wk_h = wk[:, gl:gl + 1]
        wc_h = jnp.broadcast_to(wc[:, gl:gl + 1], (bb, HEAD_DIM))
        kw8 = k8 * wk_h
        kv_all = outer_all(kw8, v8)
        qc_all = col_bcast_all(q8)
        n_new = n_ref[0, :, h, :] * wc_h + kw8
        n_out_ref[0, :, h, :] = n_new
        num_rows = []
        for r in range(bb):
            blk = slice(r * HEAD_DIM, (r + 1) * HEAD_DIM)
            c_new = c_ref[0, r, h] * wc_h[r:r + 1, :] + kv_all[:, blk]
            c_out_ref[0, r, h] = c_new
            num_rows.append(jnp.sum(qc_all[:, blk] * c_new, axis=0, keepdims=True))
        num = jnp.concatenate(num_rows, axis=0)
        q_bf = q8.astype(BF16).astype(F32)
        den = jnp.sum(q_bf * n_new, axis=-1, keepdims=True)
        hout = num / jnp.maximum(jnp.abs(den), einv[:, gl:gl + 1])
        hn = _head_norm(hout, g_m_ref[0, :, hc:hc + HEAD_DIM])
        og = _sigmoid(proj_scr[rows, COL_MO + hc:COL_MO + hc + HEAD_DIM])
        zg = _silu(proj_scr[rows, COL_MZ + hc:COL_MZ + hc + HEAD_DIM])
        mix_scr[rows, GROUP_W + hc:GROUP_W + hc + HEAD_DIM] = hn * og * zg

    @pl.when(j == pl.num_programs(1) - 1)
    def _finish_layer():
        mix = _dot(mix_scr[...].astype(BF16), w_out_ref[0])
        y = _layer_norm(ALPHA * xcur_scr[...] + mix, ln_g_ref[0], ln_b_ref[0])
        xcur_scr[...] = y
        y_ref[...] = y


def _sample_layers(x, w_in_bf, conv_w, conv_b, gbias, g_ret, g_m, w_out_bf, ln_g, ln_b,
                   rot, gam, esel, state_ret, state_c, state_n, m_pad, state_conv):
    bs = x.shape[0]
    bb = SAMPLE_BLOCK_B
    assert bs % bb == 0
    nb = bs // bb

    def lspec(shape):
        nd = len(shape)
        return pl.BlockSpec((1,) + tuple(shape[1:]), lambda l, j: (l,) + (0,) * (nd - 1))

    def cspec(shape):
        nd = len(shape)
        return pl.BlockSpec(tuple(shape), lambda l, j: (0,) * nd)

    mat_spec = pl.BlockSpec((1, bb, N_HEADS, HEAD_DIM, HEAD_DIM), lambda l, j: (l, j, 0, 0, 0))
    n_spec = pl.BlockSpec((1, bb, N_HEADS, HEAD_DIM), lambda l, j: (l, j, 0, 0))
    m_spec = pl.BlockSpec((1, bb, LANES), lambda l, j: (l, j, 0))
    cv_spec = pl.BlockSpec((1, CONV_W - 1, bb, 2 * GROUP_W), lambda l, j: (l, 0, j, 0))
    in_specs = [
        cspec(x.shape), lspec(w_in_bf.shape), lspec(conv_w.shape), lspec(conv_b.shape),
        lspec(gbias.shape), lspec(g_ret.shape), lspec(g_m.shape), lspec(w_out_bf.shape),
        lspec(ln_g.shape), lspec(ln_b.shape), cspec(rot.shape),
        pl.BlockSpec(memory_space=pltpu.SMEM), cspec(esel.shape),
        mat_spec, mat_spec, n_spec, m_spec, cv_spec,
    ]
    out_shape = (
        jax.ShapeDtypeStruct(x.shape, F32),
        jax.ShapeDtypeStruct(state_ret.shape, F32),
        jax.ShapeDtypeStruct(state_c.shape, F32),
        jax.ShapeDtypeStruct(state_n.shape, F32),
        jax.ShapeDtypeStruct(m_pad.shape, F32),
        jax.ShapeDtypeStruct(state_conv.shape, F32),
    )
    out_specs = (cspec(x.shape), mat_spec, mat_spec, n_spec, m_spec, cv_spec)
    scratch = [
        pltpu.VMEM((bs, N_PAD), F32),
        pltpu.VMEM((bs, D_MODEL), F32),
        pltpu.VMEM((bs, 2 * GROUP_W), F32),
    ]
    return pl.pallas_call(
        _sample_kernel,
        grid=(DEPTH, nb),
        in_specs=in_specs,
        out_specs=out_specs,
        out_shape=out_shape,
        scratch_shapes=scratch,
        compiler_params=pltpu.CompilerParams(
            dimension_semantics=("arbitrary", "arbitrary"),
            vmem_limit_bytes=VMEM_LIMIT_BYTES),
        name="sample_layers",
    )(x, w_in_bf, conv_w, conv_b, gbias, g_ret, g_m, w_out_bf, ln_g, ln_b,
      rot, gam, esel, state_ret, state_c, state_n, m_pad, state_conv)


def _cast_weight_kernel(w_ref, o_ref):
    o_ref[0] = w_ref[0].astype(BF16)


def _cast_weight(w):
    depth, k, n = w.shape
    rows = WEIGHT_CAST_ROWS
    assert k % rows == 0
    return pl.pallas_call(
        _cast_weight_kernel,
        grid=(depth, k // rows),
        in_specs=[pl.BlockSpec((1, rows, n), lambda l, r: (l, r, 0))],
        out_specs=pl.BlockSpec((1, rows, n), lambda l, r: (l, r, 0)),
        out_shape=jax.ShapeDtypeStruct((depth, k, n), BF16),
        compiler_params=pltpu.CompilerParams(dimension_semantics=("parallel", "parallel")),
        name="cast_weight",
    )(w)


def _cast_transposed_weight_kernel(n_valid, wt_ref, o_ref):
    rows = wt_ref.shape[1]
    row_id = pl.program_id(1) * rows + lax.broadcasted_iota(jnp.int32, wt_ref.shape[1:], 0)
    wt = jnp.where(row_id < n_valid, wt_ref[0], 0.0)
    o_ref[0] = wt.T.astype(BF16)


def _cast_transposed_weight(wt, n_out):
    depth, n, k = wt.shape
    rows = WEIGHT_CAST_ROWS
    assert n_out % LANES == 0 and 0 <= n_out - n < LANES
    return pl.pallas_call(
        functools.partial(_cast_transposed_weight_kernel, n),
        grid=(depth, pl.cdiv(n_out, rows)),
        in_specs=[pl.BlockSpec((1, rows, k), lambda l, r: (l, r, 0))],
        out_specs=pl.BlockSpec((1, k, rows), lambda l, r: (l, 0, r)),
        out_shape=jax.ShapeDtypeStruct((depth, k, n_out), BF16),
        compiler_params=pltpu.CompilerParams(dimension_semantics=("parallel", "parallel")),
        name="cast_weight_t",
    )(wt)


def _rotary_tables(pos):
    half = HEAD_DIM // 2
    inv = ROPE_BASE ** (-jnp.arange(half, dtype=F32) / half)
    ang = pos.astype(F32)[:, None] * inv[None, :]
    cos = jnp.cos(ang)
    sin = jnp.sin(ang)
    cos_t = jnp.concatenate([cos, cos], axis=-1)
    sin_t = jnp.concatenate([-sin, sin], axis=-1)
    return cos_t, sin_t, cos_t * QK_SCALE, sin_t * QK_SCALE


def _retention_tables():
    L = CHUNK
    log_gamma = jnp.log(1.0 - 2.0 ** (-5.0 - jnp.arange(N_HEADS, dtype=F32)))
    idx = jnp.arange(L, dtype=F32)
    diff = idx[:, None] - idx[None, :]
    decay = jnp.exp(log_gamma[:, None, None] * jnp.maximum(diff, 0.0)) * (diff >= 0)
    q_decay = jnp.exp(log_gamma[:, None] * (idx + 1.0))
    k_decay = jnp.exp(log_gamma[:, None] * (L - 1.0 - idx))
    c_decay = jnp.exp(log_gamma * L)
    qd = jnp.broadcast_to(q_decay[:, :, None], (N_HEADS, L, HEAD_DIM))
    kd = jnp.broadcast_to(k_decay[:, :, None], (N_HEADS, L, HEAD_DIM))
    gamma1 = jnp.exp(log_gamma * 1.0)
    return decay.astype(F32), qd, kd, c_decay, gamma1


def kernel(x_prompt, x_sample, state_ret, state_mlstm_C, state_mlstm_n, state_mlstm_m, state_conv,
           w_in, conv_w, conv_b, b_i, b_f, g_ret, g_m, w_out, ln_g, ln_b):
    B, T, _ = x_prompt.shape
    Bs, Ts, _ = x_sample.shape
    assert Ts == 1

    w_in_bf = _cast_transposed_weight(jnp.swapaxes(w_in, 1, 2), N_PAD)
    w_out_bf = _cast_weight(w_out)
    gbias = jnp.pad(jnp.concatenate([b_i, b_f], axis=-1), ((0, 0), (0, LANES - 2 * N_HEADS)))
    gbias = gbias.reshape(DEPTH, 1, LANES)
    conv_b3 = conv_b.reshape(DEPTH, 1, 2 * GROUP_W)
    g_ret3 = g_ret.reshape(DEPTH, 1, GROUP_W)
    g_m3 = g_m.reshape(DEPTH, 1, GROUP_W)
    ln_g3 = ln_g.reshape(DEPTH, 1, D_MODEL)
    ln_b3 = ln_b.reshape(DEPTH, 1, D_MODEL)

    decay, qd, kd, c_decay, gamma1 = _retention_tables()
    idx = jnp.arange(CHUNK)
    causal = idx[:, None] >= idx[None, :]
    tril = causal.astype(BF16)
    mask_add = jnp.where(causal, 0.0, -jnp.inf).astype(F32)
    e0 = (jnp.arange(LANES)[None, :] == 0).astype(BF16) * jnp.ones((CHUNK, 1), BF16)
    tabs_p = _rotary_tables(jnp.arange(T, dtype=jnp.int32)) + (decay, qd, kd, c_decay, tril, mask_add, e0)

    xp = x_prompt
    rp, cp, np_, mp, vp = [], [], [], [], []
    for l in range(DEPTH):
        xp, s, c, n, m, cv = _prompt_layer(
            l, xp, w_in_bf, conv_w[l], conv_b3[l], gbias[l], g_ret3[l], g_m3[l],
            w_out_bf, ln_g3[l], ln_b3[l], tabs_p)
        rp.append(s); cp.append(c); np_.append(n); mp.append(m); vp.append(cv)

    rot = jnp.concatenate(_rotary_tables(PAST_LEN + jnp.arange(Ts, dtype=jnp.int32)), axis=0)
    bb = SAMPLE_BLOCK_B
    esel = (jnp.arange(LANES)[:, None] == (jnp.arange(bb * HEAD_DIM)[None, :] // HEAD_DIM)).astype(BF16)
    m_pad = jnp.pad(state_mlstm_m, ((0, 0), (0, 0), (N_HEADS, LANES - 2 * N_HEADS)))
    ys, rs, cs, ns, ms_pad, vs = _sample_layers(
        x_sample.reshape(Bs, D_MODEL), w_in_bf, conv_w, conv_b3, gbias, g_ret3, g_m3, w_out_bf,
        ln_g3, ln_b3, rot, gamma1, esel, state_ret, state_mlstm_C, state_mlstm_n, m_pad,
        jnp.swapaxes(state_conv, 1, 2))
    vs = jnp.swapaxes(vs, 1, 2)
    ms = ms_pad[:, :, N_HEADS:2 * N_HEADS]

    return (xp, ys.reshape(Bs, Ts, D_MODEL),
            jnp.stack(rp), jnp.stack(cp), jnp.stack(np_), jnp.stack(mp), jnp.stack(vp),
            rs, cs, ns, ms, vs)
```

```python
import functools

import jax
import jax.numpy as jnp
import numpy as np
from jax import lax
from jax.experimental import pallas as pl
from jax.experimental.pallas import tpu as pltpu

F32 = jnp.float32
BF16 = jnp.bfloat16

D_MODEL = 1024
DEPTH = 2
PAST_LEN = 16384
N_HEADS = 4
HEAD_DIM = 128
GROUP_W = N_HEADS * HEAD_DIM
CONV_W = 4
CHUNK = 128
ROPE_BASE = 10000.0
LN_EPS = 1e-5
GN_EPS = 1e-5
ALPHA = (2 * DEPTH) ** 0.25
QK_SCALE = HEAD_DIM ** -0.5

LANES = 128
SUBLANES = 8

COL_RQ = 0
COL_RK = COL_RQ + GROUP_W
COL_RV = COL_RK + GROUP_W
COL_RZ = COL_RV + GROUP_W
COL_MQK = COL_RZ + GROUP_W
COL_MV = COL_MQK + 2 * GROUP_W
COL_MO = COL_MV + GROUP_W
COL_MZ = COL_MO + GROUP_W
COL_GATE = COL_MZ + GROUP_W
N_IN = COL_GATE + 2 * N_HEADS
N_PAD = COL_GATE + LANES

PROMPT_BLOCK_T = 256
PROJ_PIECE_COLS = 256
SAMPLE_BLOCK_B = 8
WEIGHT_CAST_ROWS = 512
VMEM_LIMIT_BYTES = 56 * 1024 * 1024


def _sigmoid(x):
    return 1.0 / (1.0 + jnp.exp(-x))


def _silu(x):
    return x * _sigmoid(x)


def _log_sigmoid(x):
    return jnp.minimum(x, 0.0) - jnp.log1p(jnp.exp(-jnp.abs(x)))


def _dot(a, b):
    return jnp.dot(a, b, preferred_element_type=F32)


def _dot_nt(a, b):
    return lax.dot_general(a, b, (((1,), (1,)), ((), ())), preferred_element_type=F32)


def _dot_tn(a, b):
    return lax.dot_general(a, b, (((0,), (0,)), ((), ())), preferred_element_type=F32)


def _rotary(x, cos_t, sin_t):
    return x * cos_t + pltpu.roll(x, HEAD_DIM // 2, axis=1) * sin_t


def _head_norm(h, g):
    mu = jnp.mean(h, axis=-1, keepdims=True)
    d = h - mu
    var = jnp.mean(d * d, axis=-1, keepdims=True)
    return d * lax.rsqrt(var + GN_EPS) * g


def _layer_norm(x, g, b):
    mu = jnp.mean(x, axis=-1, keepdims=True)
    d = x - mu
    var = jnp.mean(d * d, axis=-1, keepdims=True)
    return d * lax.rsqrt(var + LN_EPS) * g + b


def _cumsum_rows(tril_bf, x):
    hi = x.astype(BF16)
    r1 = x - hi.astype(F32)
    mid = r1.astype(BF16)
    lo = (r1 - mid.astype(F32)).astype(BF16)
    return _dot(tril_bf, hi) + _dot(tril_bf, mid) + _dot(tril_bf, lo)


RQK_Q, RQK_QDEC, RQK_K, RQK_KDEC = 0, GROUP_W, 2 * GROUP_W, 3 * GROUP_W
VV_RET, VV_M = 0, GROUP_W
GZ_RET, GZ_M = 0, GROUP_W
GT_GATES, GT_BCUM, GT_ROWS, GT_PER_CHUNK = 0, 1, 2, 3
EMITS_PER_CHUNK = 9 * N_HEADS + 1
TAIL_PIECES = 0


def _prompt_kernel(nt, n_blocks, xn_ref, x_ref, w_in_ref, conv_w_ref, conv_b_ref, gbias_ref,
                   g_ret_ref, g_m_ref, w_out_ref, ln_g_ref, ln_b_ref,
                   cos_ref, sin_ref, kcos_ref, ksin_ref,
                   decay_ref, qd_ref, kd_ref, cdec_ref, tril_ref, mask_ref, e0_ref,
                   y_ref, s_out_ref, c_out_ref, n_out_ref, m_out_ref, conv_out_ref,
                   rqk_a, vv_a, gz_a, qkm_a, gt_a, rqk_b, vv_b, gz_b, qkm_b, gt_b,
                   xb_scr, u_scr, mix_scr, s_scr, caug_scr, m_scr):
    g = pl.program_id(0)
    t = lax.rem(jnp.maximum(g - 1, 0), nt)
    tn = lax.rem(jnp.minimum(g, n_blocks - 1), nt)
    block_t = x_ref.shape[1]
    n_chunks = block_t // CHUNK
    carry_rows = CONV_W - 1
    heads = range(N_HEADS)

    @pl.when(g == 0)
    def _init_pipeline():
        for ref in (rqk_b, vv_b, gz_b, qkm_b, gt_b):
            ref[...] = jnp.zeros_like(ref)

    @pl.when(t == 0)
    def _init_state():
        s_scr[...] = jnp.zeros_like(s_scr)
        caug_scr[...] = jnp.zeros_like(caug_scr)
        m_scr[...] = jnp.zeros_like(m_scr)

    @pl.when(tn == 0)
    def _init_conv_carry():
        u_scr[0:SUBLANES, :] = jnp.zeros((SUBLANES, 2 * GROUP_W), F32)

    def hcols(base, h):
        return slice(base + h * HEAD_DIM, base + (h + 1) * HEAD_DIM)

    def step_body(set_in, set_cur):
        rqk_i, vv_i, gz_i, qkm_i, gt_i = set_in
        rqk_c, vv_c, gz_c, qkm_c, gt_c = set_cur
        tril_bf = tril_ref[...]
        causal_add = mask_ref[...]
        ones_col = e0_ref[...]
        lane_id = lax.broadcasted_iota(jnp.int32, (CHUNK, LANES), 1)
        xb_scr[...] = xn_ref[0].astype(BF16)
        pw = PROJ_PIECE_COLS
        heads_per_piece = pw // HEAD_DIM

        def chunk_rows(c):
            return slice(c * CHUNK, (c + 1) * CHUNK)

        def proj(c, col, width=pw):
            return _dot(xb_scr[chunk_rows(c), :], w_in_ref[:, col:col + width])

        def piece_rot(c, i, col_base, cos_r, sin_r, dec_ref, dst, dst_dec):
            rows = chunk_rows(c)
            res = proj(c, col_base + i * pw)
            for hh in range(heads_per_piece):
                h = i * heads_per_piece + hh
                r = _rotary(res[:, hh * HEAD_DIM:(hh + 1) * HEAD_DIM], cos_r[rows, :], sin_r[rows, :])
                rqk_i[rows, hcols(dst, h)] = r.astype(BF16)
                rqk_i[rows, hcols(dst_dec, h)] = (r * dec_ref[h]).astype(BF16)

        def piece_cast(c, i, col_base, dst):
            vv_i[chunk_rows(c), dst + i * pw:dst + (i + 1) * pw] = proj(c, col_base + i * pw).astype(BF16)

        def piece_rz(c, i):
            gz_i[chunk_rows(c), GZ_RET + i * pw:GZ_RET + (i + 1) * pw] = _silu(proj(c, COL_RZ + i * pw))

        def piece_moz(c, i):
            og = _sigmoid(proj(c, COL_MO + i * pw))
            gz_i[chunk_rows(c), GZ_M + i * pw:GZ_M + (i + 1) * pw] = og * _silu(proj(c, COL_MZ + i * pw))

        def piece_conv(c, i):
            base = SUBLANES + c * CHUNK
            u_scr[base:base + CHUNK, i * pw:(i + 1) * pw] = proj(c, COL_MQK + i * pw)
            for cs in range(i * pw, (i + 1) * pw, LANES):
                cols = slice(cs, cs + LANES)
                acc = conv_b_ref[:, cols]
                for j in range(CONV_W):
                    r0 = base - carry_rows + j
                    acc = acc + u_scr[r0:r0 + CHUNK, cols] * conv_w_ref[j:j + 1, cols]
                act = _silu(acc)
                if cs >= GROUP_W:
                    act = act * QK_SCALE
                qkm_i[chunk_rows(c), cols] = act

        def piece_gate(c):
            gates = proj(c, COL_GATE, LANES) + gbias_ref[...]
            bcum = _cumsum_rows(tril_bf, _log_sigmoid(gates))
            gt_i[c * GT_PER_CHUNK + GT_GATES] = gates
            gt_i[c * GT_PER_CHUNK + GT_BCUM] = bcum
            gt_i[c * GT_PER_CHUNK + GT_ROWS] = jnp.where(lane_id < N_HEADS, gates, bcum).T

        n_col_pieces = GROUP_W // pw
        pieces = []
        for c in range(n_chunks):
            pieces.append(functools.partial(piece_gate, c))
            for i in range(n_col_pieces):
                pieces.append(functools.partial(piece_conv, c, 2 * i))
                pieces.append(functools.partial(piece_cast, c, i, COL_RV, VV_RET))
                pieces.append(functools.partial(piece_rot, c, i, COL_RQ, cos_ref, sin_ref, qd_ref,
                                                RQK_Q, RQK_QDEC))
                pieces.append(functools.partial(piece_rz, c, i))
                pieces.append(functools.partial(piece_conv, c, 2 * i + 1))
                pieces.append(functools.partial(piece_cast, c, i, COL_MV, VV_M))
                pieces.append(functools.partial(piece_rot, c, i, COL_RK, kcos_ref, ksin_ref, kd_ref,
                                                RQK_K, RQK_KDEC))
                pieces.append(functools.partial(piece_moz, c, i))
        n_pieces = len(pieces)
        n_slots = EMITS_PER_CHUNK * n_chunks
        slots_done = [0]

        def emit_proj_pieces():
            slots_done[0] += 1
            target = -(-(slots_done[0] * (n_pieces - TAIL_PIECES)) // n_slots)
            while n_pieces - len(pieces) < target:
                pieces.pop(0)()

        def per_head(fn):
            out = []
            for h in heads:
                emit_proj_pieces()
                out.append(fn(h))
            return out

        for c in range(n_chunks):
            rows = slice(c * CHUNK, (c + 1) * CHUNK)

            q_bf = [rqk_c[rows, hcols(RQK_Q, h)] for h in heads]
            k_bf = [rqk_c[rows, hcols(RQK_K, h)] for h in heads]
            v_bf = [vv_c[rows, hcols(VV_RET, h)] for h in heads]
            sc = per_head(lambda h: _dot_nt(q_bf[h], k_bf[h]))
            state = [s_scr[h] for h in heads]
            upd = per_head(lambda h: _dot_tn(rqk_c[rows, hcols(RQK_KDEC, h)], v_bf[h]))
            for h in heads:
                s_scr[h] = state[h] * cdec_ref[h] + upd[h]
            o = per_head(lambda h: _dot(
                jnp.concatenate([(sc[h] * decay_ref[h]).astype(BF16),
                                 rqk_c[rows, hcols(RQK_QDEC, h)]], axis=1),
                jnp.concatenate([v_bf[h], state[h].astype(BF16)], axis=0)))

            def ret_out(h):
                hn = _head_norm(o[h], g_ret_ref[:, hcols(0, h)])
                mix_scr[rows, hcols(0, h)] = (hn * gz_c[rows, hcols(GZ_RET, h)]).astype(BF16)

            per_head(ret_out)

            gates = gt_c[c * GT_PER_CHUNK + GT_GATES]
            bcum = gt_c[c * GT_PER_CHUNK + GT_BCUM]
            rows_t = gt_c[c * GT_PER_CHUNK + GT_ROWS]
            q = [qkm_c[rows, hcols(0, h)] for h in heads]
            k = [qkm_c[rows, hcols(GROUP_W, h)] for h in heads]
            vaug_bf = [jnp.concatenate([vv_c[rows, hcols(VV_M, h)], ones_col], axis=1)
                       for h in heads]
            qk = per_head(lambda h: _dot_nt(q[h].astype(BF16), k[h].astype(BF16)))
            caug = [caug_scr[h] for h in heads]
            m_old = [m_scr[h, 0:1, 0:1] for h in heads]
            i_row = [rows_t[h:h + 1, :] for h in heads]
            b_row = [rows_t[N_HEADS + h:N_HEADS + h + 1, :] for h in heads]
            i_col = [gates[:, h:h + 1] for h in heads]
            b_col = [bcum[:, N_HEADS + h:N_HEADS + h + 1] for h in heads]
            a_col = [b_col[h] + m_old[h] for h in heads]
            dm = per_head(lambda h: (b_col[h] + (i_row[h] - b_row[h])) + causal_add)
            mt = [jnp.maximum(a_col[h], jnp.max(dm[h], axis=-1, keepdims=True)) for h in heads]
            w_inter = [jnp.exp(a_col[h] - mt[h]) for h in heads]
            res = per_head(lambda h: _dot(
                jnp.concatenate([(qk[h] * jnp.exp(dm[h] - mt[h])).astype(BF16),
                                 (q[h] * w_inter[h]).astype(BF16)], axis=1),
                jnp.concatenate([vaug_bf[h], caug[h].astype(BF16)], axis=0)))
            b_last = [b_col[h][CHUNK - 1:CHUNK, :] for h in heads]
            g_col = [b_last[h] - b_col[h] + i_col[h] for h in heads]
            m_new = [jnp.maximum(b_last[h] + m_old[h], jnp.max(g_col[h], axis=0, keepdims=True))
                     for h in heads]
            wk = [jnp.exp(g_col[h] - m_new[h]) for h in heads]
            wc = [jnp.exp(b_last[h] + m_old[h] - m_new[h]) for h in heads]

            def mlstm_update(h):
                caug_scr[h] = caug[h] * wc[h] + _dot_tn((k[h] * wk[h]).astype(BF16), vaug_bf[h])
                m_scr[h] = jnp.broadcast_to(m_new[h], (SUBLANES, LANES))

            per_head(mlstm_update)

            def mlstm_out(h):
                den = jnp.maximum(jnp.abs(res[h][:, HEAD_DIM:HEAD_DIM + 1]), jnp.exp(-mt[h]))
                hn = _head_norm(res[h][:, :HEAD_DIM] / den, g_m_ref[:, hcols(0, h)])
                mix_scr[rows, hcols(GROUP_W, h)] = (hn * gz_c[rows, hcols(GZ_M, h)]).astype(BF16)

            per_head(mlstm_out)

            emit_proj_pieces()
            mix = _dot(mix_scr[rows, :], w_out_ref[...])
            y_ref[0, rows, :] = _layer_norm(ALPHA * x_ref[0, rows, :] + mix, ln_g_ref[...], ln_b_ref[...])

        assert slots_done[0] == n_slots and len(pieces) == TAIL_PIECES
        while pieces:
            pieces.pop(0)()

    set_a = (rqk_a, vv_a, gz_a, qkm_a, gt_a)
    set_b = (rqk_b, vv_b, gz_b, qkm_b, gt_b)
    parity = lax.rem(g, 2)

    @pl.when(parity == 0)
    def _even_step():
        step_body(set_a, set_b)

    @pl.when(parity == 1)
    def _odd_step():
        step_body(set_b, set_a)

    @pl.when(jnp.logical_and(tn == nt - 1, g < n_blocks))
    def _write_conv_state():
        conv_out_ref[0] = u_scr[SUBLANES + block_t - carry_rows:SUBLANES + block_t, :]

    @pl.when(jnp.logical_and(t == nt - 1, g > 0))
    def _write_state():
        s_out_ref[0] = s_scr[...]
        m_out_ref[0] = jnp.zeros((SUBLANES, LANES), F32)
        for h in heads:
            caug = caug_scr[h]
            c_out_ref[0, h] = caug[:, :HEAD_DIM]
            n_out_ref[0, h:h + 1, :] = caug[:, HEAD_DIM:].T[0:1, :]
            m_out_ref[0, h:h + 1, :] = m_scr[h, 0:1, :]

    u_scr[0:SUBLANES, :] = u_scr[block_t:block_t + SUBLANES, :]


def _const_spec(shape):
    nd = len(shape)
    return pl.BlockSpec(shape, lambda g: (0,) * nd)


def _layer_spec(shape, layer):
    return pl.BlockSpec((None,) + tuple(shape[1:]), lambda g: (layer,) + (0,) * (len(shape) - 1))


def _prompt_layer(layer, x, w_in_bf, conv_w, conv_b, gbias, g_ret, g_m, w_out_bf, ln_g, ln_b, tabs):
    B, T, _ = x.shape
    bt = PROMPT_BLOCK_T
    assert T % bt == 0 and bt % CHUNK == 0 and GROUP_W % PROJ_PIECE_COLS == 0
    nt = T // bt
    n_blocks = B * nt
    n_chunks = bt // CHUNK
    cos_t, sin_t, kcos_t, ksin_t, decay, qd, kd, cdec, tril, mask, e0 = tabs

    def nxt(g):
        return jnp.minimum(g, n_blocks - 1)

    def cur(g):
        return jnp.maximum(g - 1, 0)

    row_spec = pl.BlockSpec((bt, LANES), lambda g: (nxt(g) % nt, 0))
    in_specs = [
        pl.BlockSpec((1, bt, D_MODEL), lambda g: (nxt(g) // nt, nxt(g) % nt, 0)),
        pl.BlockSpec((1, bt, D_MODEL), lambda g: (cur(g) // nt, cur(g) % nt, 0)),
        _layer_spec(w_in_bf.shape, layer), _const_spec(conv_w.shape), _const_spec(conv_b.shape),
        _const_spec(gbias.shape), _const_spec(g_ret.shape), _const_spec(g_m.shape),
        _layer_spec(w_out_bf.shape, layer), _const_spec(ln_g.shape), _const_spec(ln_b.shape),
        row_spec, row_spec, row_spec, row_spec,
        _const_spec(decay.shape), _const_spec(qd.shape), _const_spec(kd.shape),
        pl.BlockSpec(memory_space=pltpu.SMEM),
        _const_spec(tril.shape), _const_spec(mask.shape), _const_spec(e0.shape),
    ]
    out_shape = (
        jax.ShapeDtypeStruct((B, T, D_MODEL), F32),
        jax.ShapeDtypeStruct((B, N_HEADS, HEAD_DIM, HEAD_DIM), F32),
        jax.ShapeDtypeStruct((B, N_HEADS, HEAD_DIM, HEAD_DIM), F32),
        jax.ShapeDtypeStruct((B, N_HEADS, HEAD_DIM), F32),
        jax.ShapeDtypeStruct((B, SUBLANES, LANES), F32),
        jax.ShapeDtypeStruct((B, CONV_W - 1, 2 * GROUP_W), F32),
    )
    out_specs = (
        pl.BlockSpec((1, bt, D_MODEL), lambda g: (cur(g) // nt, cur(g) % nt, 0)),
        pl.BlockSpec((1, N_HEADS, HEAD_DIM, HEAD_DIM), lambda g: (cur(g) // nt, 0, 0, 0)),
        pl.BlockSpec((1, N_HEADS, HEAD_DIM, HEAD_DIM), lambda g: (cur(g) // nt, 0, 0, 0)),
        pl.BlockSpec((1, N_HEADS, HEAD_DIM), lambda g: (cur(g) // nt, 0, 0)),
        pl.BlockSpec((1, SUBLANES, LANES), lambda g: (cur(g) // nt, 0, 0)),
        pl.BlockSpec((1, CONV_W - 1, 2 * GROUP_W), lambda g: (nxt(g) // nt, 0, 0)),
    )
    operand_set = [
        pltpu.VMEM((bt, 4 * GROUP_W), BF16),
        pltpu.VMEM((bt, 2 * GROUP_W), BF16),
        pltpu.VMEM((bt, 2 * GROUP_W), F32),
        pltpu.VMEM((bt, 2 * GROUP_W), F32),
        pltpu.VMEM((n_chunks * GT_PER_CHUNK, CHUNK, LANES), F32),
    ]
    scratch = operand_set + operand_set + [
        pltpu.VMEM((bt, D_MODEL), BF16),
        pltpu.VMEM((SUBLANES + bt, 2 * GROUP_W), F32),
        pltpu.VMEM((bt, 2 * GROUP_W), BF16),
        pltpu.VMEM((N_HEADS, HEAD_DIM, HEAD_DIM), F32),
        pltpu.VMEM((N_HEADS, HEAD_DIM, 2 * HEAD_DIM), F32),
        pltpu.VMEM((N_HEADS, SUBLANES, LANES), F32),
    ]
    y, s, c, n, m, cv = pl.pallas_call(
        functools.partial(_prompt_kernel, nt, n_blocks),
        grid=(n_blocks + 1,),
        in_specs=in_specs,
        out_specs=out_specs,
        out_shape=out_shape,
        scratch_shapes=scratch,
        compiler_params=pltpu.CompilerParams(
            dimension_semantics=("arbitrary",),
            vmem_limit_bytes=VMEM_LIMIT_BYTES),
        name="prompt_layer",
    )(x, x, w_in_bf, conv_w, conv_b, gbias, g_ret, g_m, w_out_bf, ln_g, ln_b,
      cos_t, sin_t, kcos_t, ksin_t, decay, qd, kd, cdec, tril, mask, e0)
    return y, s, c, n, m[:, :N_HEADS, 0], cv


def _sample_kernel(x_ref, w_in_ref, conv_w_ref, conv_b_ref, gbias_ref, g_ret_ref, g_m_ref,
                   w_out_ref, ln_g_ref, ln_b_ref, rot_ref, gam_ref, esel_ref,
                   s_ref, c_ref, n_ref, m_ref, cv_ref,
                   y_ref, s_out_ref, c_out_ref, n_out_ref, m_out_ref, cv_out_ref,
                   proj_scr, xcur_scr, mix_scr):
    layer = pl.program_id(0)
    j = pl.program_id(1)
    bb = s_ref.shape[1]

    @pl.when(jnp.logical_and(layer == 0, j == 0))
    def _load_x():
        xcur_scr[...] = x_ref[...]

    @pl.when(j == 0)
    def _project():
        xb = xcur_scr[...].astype(BF16)
        for lo in range(0, N_PAD, 512):
            hi = min(lo + 512, N_PAD)
            proj_scr[:, lo:hi] = _dot(xb, w_in_ref[0, :, lo:hi])

    r0 = pl.multiple_of(j * bb, bb)
    rows = pl.ds(r0, bb)
    cos_t = rot_ref[0:1, :]
    sin_t = rot_ref[1:2, :]
    kcos_t = rot_ref[2:3, :]
    ksin_t = rot_ref[3:4, :]
    esel = esel_ref[...]
    row_id = lax.broadcasted_iota(jnp.int32, (bb, bb * HEAD_DIM), 0)
    blk_id = jnp.right_shift(lax.broadcasted_iota(jnp.int32, (bb, bb * HEAD_DIM), 1), 7)
    pad_rows = jnp.zeros((LANES - bb, HEAD_DIM), F32)
    pad_wide = jnp.zeros((LANES - bb, bb * HEAD_DIM), BF16)

    def col_form(x8):
        return jnp.concatenate([x8, pad_rows], axis=0).T.astype(BF16)

    def outer_all(k8, v8):
        vt = jnp.concatenate([v8] * bb, axis=1)
        vsel = jnp.where(row_id == blk_id, vt, 0.0).astype(BF16)
        return _dot(col_form(k8), jnp.concatenate([vsel, pad_wide], axis=0))

    def col_bcast_all(q8):
        return _dot(col_form(q8), esel)

    for h in range(N_HEADS):
        hc = h * HEAD_DIM
        q8 = _rotary(proj_scr[rows, COL_RQ + hc:COL_RQ + hc + HEAD_DIM], cos_t, sin_t)
        k8 = _rotary(proj_scr[rows, COL_RK + hc:COL_RK + hc + HEAD_DIM], kcos_t, ksin_t)
        v8 = proj_scr[rows, COL_RV + hc:COL_RV + hc + HEAD_DIM]
        kv_all = outer_all(k8, v8)
        qc_all = col_bcast_all(q8)
        gamma = gam_ref[h]
        o_rows = []
        for r in range(bb):
            blk = slice(r * HEAD_DIM, (r + 1) * HEAD_DIM)
            s_new = s_ref[0, r, h] * gamma + kv_all[:, blk]
            s_out_ref[0, r, h] = s_new
            o_rows.append(jnp.sum(qc_all[:, blk] * s_new, axis=0, keepdims=True))
        o8 = _head_norm(jnp.concatenate(o_rows, axis=0), g_ret_ref[0, :, hc:hc + HEAD_DIM])
        z8 = proj_scr[rows, COL_RZ + hc:COL_RZ + hc + HEAD_DIM]
        mix_scr[rows, hc:hc + HEAD_DIM] = o8 * _silu(z8)

    u8 = proj_scr[rows, COL_MQK:COL_MQK + 2 * GROUP_W]
    acc = conv_b_ref[0] + u8 * conv_w_ref[0, CONV_W - 1:CONV_W, :]
    for jj in range(CONV_W - 1):
        acc = acc + cv_ref[0, jj] * conv_w_ref[0, jj:jj + 1, :]
    for jj in range(1, CONV_W - 1):
        cv_out_ref[0, jj - 1] = cv_ref[0, jj]
    cv_out_ref[0, CONV_W - 2] = u8
    qk8 = _silu(acc)

    gates = proj_scr[rows, COL_GATE:COL_GATE + LANES] + gbias_ref[0]
    i_al = pltpu.roll(gates, N_HEADS, axis=1)
    bm = _log_sigmoid(gates) + m_ref[0]
    m_new = jnp.maximum(bm, i_al)
    wk = jnp.exp(i_al - m_new)
    wc = jnp.exp(bm - m_new)
    einv = jnp.exp(-m_new)
    m_out_ref[0] = m_new
    for h in range(N_HEADS):
        hc = h * HEAD_DIM
        gl = N_HEADS + h
        q8 = qk8[:, hc:hc + HEAD_DIM]
        k8 = qk8[:, GROUP_W + hc:GROUP_W + hc + HEAD_DIM] * QK_SCALE
        v8 = proj_scr[rows, COL_MV + hc:COL_MV + hc + HEAD_DIM]
        wk_h = wk[:, gl:gl + 1]
        wc_h = jnp.broadcast_to(wc[:, gl:gl + 1], (bb, HEAD_DIM))
        kw8 = k8 * wk_h
        kv_all = outer_all(kw8, v8)
        qc_all = col_bcast_all(q8)
        n_new = n_ref[0, :, h, :] * wc_h + kw8
        n_out_ref[0, :, h, :] = n_new
        num_rows = []
        for r in range(bb):
            blk = slice(r * HEAD_DIM, (r + 1) * HEAD_DIM)
            c_new = c_ref[0, r, h] * wc_h[r:r + 1, :] + kv_all[:, blk]
            c_out_ref[0, r, h] = c_new
            num_rows.append(jnp.sum(qc_all[:, blk] * c_new, axis=0, keepdims=True))
        num = jnp.concatenate(num_rows, axis=0)
        q_bf = q8.astype(BF16).astype(F32)
        den = jnp.sum(q_bf * n_new, axis=-1, keepdims=True)
        hout = num / jnp.maximum(jnp.abs(den), einv[:, gl:gl + 1])
        hn = _head_norm(hout, g_m_ref[0, :, hc:hc + HEAD_DIM])
        og = _sigmoid(proj_scr[rows, COL_MO + hc:COL_MO + hc + HEAD_DIM])
        zg = _silu(proj_scr[rows, COL_MZ + hc:COL_MZ + hc + HEAD_DIM])
        mix_scr[rows, GROUP_W + hc:GROUP_W + hc + HEAD_DIM] = hn * og * zg

    @pl.when(j == pl.num_programs(1) - 1)
    def _finish_layer():
        mix = _dot(mix_scr[...].astype(BF16), w_out_ref[0])
        y = _layer_norm(ALPHA * xcur_scr[...] + mix, ln_g_ref[0], ln_b_ref[0])
        xcur_scr[...] = y
        y_ref[...] = y


def _sample_layers(x, w_in_bf, conv_w, conv_b, gbias, g_ret, g_m, w_out_bf, ln_g, ln_b,
                   rot, gam, esel, state_ret, state_c, state_n, m_pad, state_conv):
    bs = x.shape[0]
    bb = SAMPLE_BLOCK_B
    assert bs % bb == 0
    nb = bs // bb

    def lspec(shape):
        nd = len(shape)
        return pl.BlockSpec((1,) + tuple(shape[1:]), lambda l, j: (l,) + (0,) * (nd - 1))

    def cspec(shape):
        nd = len(shape)
        return pl.BlockSpec(tuple(shape), lambda l, j: (0,) * nd)

    mat_spec = pl.BlockSpec((1, bb, N_HEADS, HEAD_DIM, HEAD_DIM), lambda l, j: (l, j, 0, 0, 0))
    n_spec = pl.BlockSpec((1, bb, N_HEADS, HEAD_DIM), lambda l, j: (l, j, 0, 0))
    m_spec = pl.BlockSpec((1, bb, LANES), lambda l, j: (l, j, 0))
    cv_spec = pl.BlockSpec((1, CONV_W - 1, bb, 2 * GROUP_W), lambda l, j: (l, 0, j, 0))
    in_specs = [
        cspec(x.shape), lspec(w_in_bf.shape), lspec(conv_w.shape), lspec(conv_b.shape),
        lspec(gbias.shape), lspec(g_ret.shape), lspec(g_m.shape), lspec(w_out_bf.shape),
        lspec(ln_g.shape), lspec(ln_b.shape), cspec(rot.shape),
        pl.BlockSpec(memory_space=pltpu.SMEM), cspec(esel.shape),
        mat_spec, mat_spec, n_spec, m_spec, cv_spec,
    ]
    out_shape = (
        jax.ShapeDtypeStruct(x.shape, F32),
        jax.ShapeDtypeStruct(state_ret.shape, F32),
        jax.ShapeDtypeStruct(state_c.shape, F32),
        jax.ShapeDtypeStruct(state_n.shape, F32),
        jax.ShapeDtypeStruct(m_pad.shape, F32),
        jax.ShapeDtypeStruct(state_conv.shape, F32),
    )
    out_specs = (cspec(x.shape), mat_spec, mat_spec, n_spec, m_spec, cv_spec)
    scratch = [
        pltpu.VMEM((bs, N_PAD), F32),
        pltpu.VMEM((bs, D_MODEL), F32),
        pltpu.VMEM((bs, 2 * GROUP_W), F32),
    ]
    return pl.pallas_call(
        _sample_kernel,
        grid=(DEPTH, nb),
        in_specs=in_specs,
        out_specs=out_specs,
        out_shape=out_shape,
        scratch_shapes=scratch,
        compiler_params=pltpu.CompilerParams(
            dimension_semantics=("arbitrary", "arbitrary"),
            vmem_limit_bytes=VMEM_LIMIT_BYTES),
        name="sample_layers",
    )(x, w_in_bf, conv_w, conv_b, gbias, g_ret, g_m, w_out_bf, ln_g, ln_b,
      rot, gam, esel, state_ret, state_c, state_n, m_pad, state_conv)


def _cast_weight_kernel(w_ref, o_ref):
    o_ref[0] = w_ref[0].astype(BF16)


def _cast_weight(w):
    depth, k, n = w.shape
    rows = WEIGHT_CAST_ROWS
    assert k % rows == 0
    return pl.pallas_call(
        _cast_weight_kernel,
        grid=(depth, k // rows),
        in_specs=[pl.BlockSpec((1, rows, n), lambda l, r: (l, r, 0))],
        out_specs=pl.BlockSpec((1, rows, n), lambda l, r: (l, r, 0)),
        out_shape=jax.ShapeDtypeStruct((depth, k, n), BF16),
        compiler_params=pltpu.CompilerParams(dimension_semantics=("parallel", "parallel")),
        name="cast_weight",
    )(w)


def _cast_transposed_weight_kernel(n_valid, wt_ref, o_ref):
    rows = wt_ref.shape[1]
    row_id = pl.program_id(1) * rows + lax.broadcasted_iota(jnp.int32, wt_ref.shape[1:], 0)
    wt = jnp.where(row_id < n_valid, wt_ref[0], 0.0)
    o_ref[0] = wt.T.astype(BF16)


def _cast_transposed_weight(wt, n_out):
    depth, n, k = wt.shape
    rows = WEIGHT_CAST_ROWS
    assert n_out % LANES == 0 and 0 <= n_out - n < LANES
    return pl.pallas_call(
        functools.partial(_cast_transposed_weight_kernel, n),
        grid=(depth, pl.cdiv(n_out, rows)),
        in_specs=[pl.BlockSpec((1, rows, k), lambda l, r: (l, r, 0))],
        out_specs=pl.BlockSpec((1, k, rows), lambda l, r: (l, 0, r)),
        out_shape=jax.ShapeDtypeStruct((depth, k, n_out), BF16),
        compiler_params=pltpu.CompilerParams(dimension_semantics=("parallel", "parallel")),
        name="cast_weight_t",
    )(wt)


def _rotary_tables(pos):
    half = HEAD_DIM // 2
    inv = np.float32(ROPE_BASE) ** (-np.arange(half, dtype=np.float32) / np.float32(half))
    ang = pos.astype(np.float32)[:, None] * inv[None, :].astype(np.float32)
    cos = np.cos(ang.astype(np.float64)).astype(np.float32)
    sin = np.sin(ang.astype(np.float64)).astype(np.float32)
    cos_t = np.concatenate([cos, cos], axis=-1)
    sin_t = np.concatenate([-sin, sin], axis=-1)
    scale = np.float32(QK_SCALE)
    return cos_t, sin_t, cos_t * scale, sin_t * scale


def _retention_tables():
    L = CHUNK
    f32 = np.float32
    log_gamma = np.log(f32(1.0) - f32(2.0) ** (f32(-5.0) - np.arange(N_HEADS, dtype=f32))).astype(f32)
    idx = np.arange(L, dtype=f32)
    diff = idx[:, None] - idx[None, :]
    decay = (np.exp(log_gamma[:, None, None] * np.maximum(diff, f32(0.0))) * (diff >= 0)).astype(f32)
    q_decay = np.exp(log_gamma[:, None] * (idx + f32(1.0))).astype(f32)
    k_decay = np.exp(log_gamma[:, None] * (f32(L - 1.0) - idx)).astype(f32)
    c_decay = np.exp(log_gamma * f32(L)).astype(f32)
    qd = np.ascontiguousarray(np.broadcast_to(q_decay[:, :, None], (N_HEADS, L, HEAD_DIM)))
    kd = np.ascontiguousarray(np.broadcast_to(k_decay[:, :, None], (N_HEADS, L, HEAD_DIM)))
    gamma1 = np.exp(log_gamma).astype(f32)
    return decay, qd, kd, c_decay, gamma1


def kernel(x_prompt, x_sample, state_ret, state_mlstm_C, state_mlstm_n, state_mlstm_m, state_conv,
           w_in, conv_w, conv_b, b_i, b_f, g_ret, g_m, w_out, ln_g, ln_b):
    B, T, _ = x_prompt.shape
    Bs, Ts, _ = x_sample.shape
    assert Ts == 1

    w_in_bf = _cast_transposed_weight(jnp.swapaxes(w_in, 1, 2), N_PAD)
    w_out_bf = _cast_weight(w_out)
    gbias = jnp.pad(jnp.concatenate([b_i, b_f], axis=-1), ((0, 0), (0, LANES - 2 * N_HEADS)))
    gbias = gbias.reshape(DEPTH, 1, LANES)
    conv_b3 = conv_b.reshape(DEPTH, 1, 2 * GROUP_W)
    g_ret3 = g_ret.reshape(DEPTH, 1, GROUP_W)
    g_m3 = g_m.reshape(DEPTH, 1, GROUP_W)
    ln_g3 = ln_g.reshape(DEPTH, 1, D_MODEL)
    ln_b3 = ln_b.reshape(DEPTH, 1, D_MODEL)

    decay, qd, kd, c_decay, gamma1 = _retention_tables()
    idx = np.arange(CHUNK)
    causal = idx[:, None] >= idx[None, :]
    tril = jnp.asarray(causal, BF16)
    mask_add = np.where(causal, 0.0, -np.inf).astype(np.float32)
    e0 = jnp.asarray(np.broadcast_to(np.arange(LANES)[None, :] == 0, (CHUNK, LANES)), BF16)
    tabs_p = _rotary_tables(np.arange(T)) + (decay, qd, kd, c_decay, tril, mask_add, e0)
    tabs_p = tuple(jnp.asarray(a) for a in tabs_p)

    xp = x_prompt
    rp, cp, np_, mp, vp = [], [], [], [], []
    for l in range(DEPTH):
        xp, s, c, n, m, cv = _prompt_layer(
            l, xp, w_in_bf, conv_w[l], conv_b3[l], gbias[l], g_ret3[l], g_m3[l],
            w_out_bf, ln_g3[l], ln_b3[l], tabs_p)
        rp.append(s); cp.append(c); np_.append(n); mp.append(m); vp.append(cv)

    rot = jnp.asarray(np.concatenate(_rotary_tables(PAST_LEN + np.arange(Ts)), axis=0))
    bb = SAMPLE_BLOCK_B
    esel = jnp.asarray(np.arange(LANES)[:, None] == (np.arange(bb * HEAD_DIM)[None, :] // HEAD_DIM), BF16)
    gamma1 = jnp.asarray(gamma1)
    m_pad = jnp.pad(state_mlstm_m, ((0, 0), (0, 0), (N_HEADS, LANES - 2 * N_HEADS)))
    ys, rs, cs, ns, ms_pad, vs = _sample_layers(
        x_sample.reshape(Bs, D_MODEL), w_in_bf, conv_w, conv_b3, gbias, g_ret3, g_m3, w_out_bf,
        ln_g3, ln_b3, rot, gamma1, esel, state_ret, state_mlstm_C, state_mlstm_n, m_pad,
        jnp.swapaxes(state_conv, 1, 2))
    vs = jnp.swapaxes(vs, 1, 2)
    ms = ms_pad[:, :, N_HEADS:2 * N_HEADS]

    return (xp, ys.reshape(Bs, Ts, D_MODEL),
            jnp.stack(rp), jnp.stack(cp), jnp.stack(np_), jnp.stack(mp), jnp.stack(vp),
            rs, cs, ns, ms, vs)
```

```python
import functools

import jax
import jax.numpy as jnp
import numpy as np
from jax import lax
from jax.experimental import pallas as pl
from jax.experimental.pallas import tpu as pltpu

F32 = jnp.float32
BF16 = jnp.bfloat16

D_MODEL = 1024
DEPTH = 2
PAST_LEN = 16384
N_HEADS = 4
HEAD_DIM = 128
GROUP_W = N_HEADS * HEAD_DIM
CONV_W = 4
CHUNK = 128
ROPE_BASE = 10000.0
LN_EPS = 1e-5
GN_EPS = 1e-5
ALPHA = (2 * DEPTH) ** 0.25
QK_SCALE = HEAD_DIM ** -0.5

LANES = 128
SUBLANES = 8

COL_RQ = 0
COL_RK = COL_RQ + GROUP_W
COL_RV = COL_RK + GROUP_W
COL_RZ = COL_RV + GROUP_W
COL_MQK = COL_RZ + GROUP_W
COL_MV = COL_MQK + 2 * GROUP_W
COL_MO = COL_MV + GROUP_W
COL_MZ = COL_MO + GROUP_W
COL_GATE = COL_MZ + GROUP_W
N_IN = COL_GATE + 2 * N_HEADS
N_PAD = COL_GATE + LANES

PROMPT_BLOCK_T = 256
PROJ_PIECE_COLS = 256
PROJ_PIECE_ROWS = 256
SAMPLE_BLOCK_B = 8
WEIGHT_CAST_ROWS = 512
VMEM_LIMIT_BYTES = 56 * 1024 * 1024


def _sigmoid(x):
    return 1.0 / (1.0 + jnp.exp(-x))


def _silu(x):
    return x * _sigmoid(x)


def _log_sigmoid(x):
    return jnp.minimum(x, 0.0) - jnp.log1p(jnp.exp(-jnp.abs(x)))


def _dot(a, b):
    return jnp.dot(a, b, preferred_element_type=F32)


def _dot_nt(a, b):
    return lax.dot_general(a, b, (((1,), (1,)), ((), ())), preferred_element_type=F32)


def _dot_tn(a, b):
    return lax.dot_general(a, b, (((0,), (0,)), ((), ())), preferred_element_type=F32)


def _rotary(x, cos_t, sin_t):
    return x * cos_t + pltpu.roll(x, HEAD_DIM // 2, axis=1) * sin_t


def _head_norm(h, g):
    mu = jnp.mean(h, axis=-1, keepdims=True)
    d = h - mu
    var = jnp.mean(d * d, axis=-1, keepdims=True)
    return d * lax.rsqrt(var + GN_EPS) * g


def _layer_norm(x, g, b):
    mu = jnp.mean(x, axis=-1, keepdims=True)
    d = x - mu
    var = jnp.mean(d * d, axis=-1, keepdims=True)
    return d * lax.rsqrt(var + LN_EPS) * g + b


def _cumsum_rows(tril_bf, x):
    hi = x.astype(BF16)
    r1 = x - hi.astype(F32)
    mid = r1.astype(BF16)
    lo = (r1 - mid.astype(F32)).astype(BF16)
    return _dot(tril_bf, hi) + _dot(tril_bf, mid) + _dot(tril_bf, lo)


RQK_Q, RQK_QDEC, RQK_K, RQK_KDEC = 0, GROUP_W, 2 * GROUP_W, 3 * GROUP_W
VV_RET, VV_M = 0, GROUP_W
GZ_RET, GZ_M = 0, GROUP_W
GT_GATES, GT_BCUM, GT_ROWS, GT_PER_CHUNK = 0, 1, 2, 3
EMITS_PER_CHUNK = 9 * N_HEADS + 1
TAIL_PIECES = 0


def _prompt_kernel(nt, n_blocks, xn_ref, x_ref, w_in_ref, conv_w_ref, conv_b_ref, gbias_ref,
                   g_ret_ref, g_m_ref, w_out_ref, ln_g_ref, ln_b_ref,
                   cos_ref, sin_ref, kcos_ref, ksin_ref,
                   decay_ref, qd_ref, kd_ref, cdec_ref, tril_ref, mask_ref, e0_ref,
                   y_ref, s_out_ref, c_out_ref, n_out_ref, m_out_ref, conv_out_ref,
                   rqk_a, vv_a, gz_a, qkm_a, gt_a, rqk_b, vv_b, gz_b, qkm_b, gt_b,
                   xb_scr, u_scr, mix_scr, s_scr, caug_scr, m_scr):
    g = pl.program_id(0)
    t = lax.rem(jnp.maximum(g - 1, 0), nt)
    tn = lax.rem(jnp.minimum(g, n_blocks - 1), nt)
    block_t = x_ref.shape[1]
    n_chunks = block_t // CHUNK
    carry_rows = CONV_W - 1
    heads = range(N_HEADS)

    @pl.when(g == 0)
    def _init_pipeline():
        for ref in (rqk_b, vv_b, gz_b, qkm_b, gt_b):
            ref[...] = jnp.zeros_like(ref)

    @pl.when(t == 0)
    def _init_state():
        s_scr[...] = jnp.zeros_like(s_scr)
        caug_scr[...] = jnp.zeros_like(caug_scr)
        m_scr[...] = jnp.zeros_like(m_scr)

    @pl.when(tn == 0)
    def _init_conv_carry():
        u_scr[0:SUBLANES, :] = jnp.zeros((SUBLANES, 2 * GROUP_W), F32)

    def hcols(base, h):
        return slice(base + h * HEAD_DIM, base + (h + 1) * HEAD_DIM)

    def step_body(set_in, set_cur):
        rqk_i, vv_i, gz_i, qkm_i, gt_i = set_in
        rqk_c, vv_c, gz_c, qkm_c, gt_c = set_cur
        tril_bf = tril_ref[...]
        causal_add = mask_ref[...]
        ones_col = e0_ref[...]
        lane_id = lax.broadcasted_iota(jnp.int32, (CHUNK, LANES), 1)
        xb_scr[...] = xn_ref[0].astype(BF16)
        pw = PROJ_PIECE_COLS
        heads_per_piece = pw // HEAD_DIM

        pr = PROJ_PIECE_ROWS
        chunks_per_piece = pr // CHUNK

        def group_rows(p):
            return slice(p * pr, (p + 1) * pr)

        def proj(p, col, width=pw):
            return _dot(xb_scr[group_rows(p), :], w_in_ref[:, col:col + width])

        def piece_rot(p, i, col_base, cos_r, sin_r, dec_ref, dst, dst_dec):
            res = proj(p, col_base + i * pw)
            for hh in range(heads_per_piece):
                h = i * heads_per_piece + hh
                for cc in range(chunks_per_piece):
                    rows = slice(p * pr + cc * CHUNK, p * pr + (cc + 1) * CHUNK)
                    r = _rotary(res[cc * CHUNK:(cc + 1) * CHUNK, hh * HEAD_DIM:(hh + 1) * HEAD_DIM],
                                cos_r[rows, :], sin_r[rows, :])
                    rqk_i[rows, hcols(dst, h)] = r.astype(BF16)
                    rqk_i[rows, hcols(dst_dec, h)] = (r * dec_ref[h]).astype(BF16)

        def piece_cast(p, i, col_base, dst):
            vv_i[group_rows(p), dst + i * pw:dst + (i + 1) * pw] = proj(p, col_base + i * pw).astype(BF16)

        def piece_rz(p, i):
            gz_i[group_rows(p), GZ_RET + i * pw:GZ_RET + (i + 1) * pw] = _silu(proj(p, COL_RZ + i * pw))

        def piece_moz(p, i):
            og = _sigmoid(proj(p, COL_MO + i * pw))
            gz_i[group_rows(p), GZ_M + i * pw:GZ_M + (i + 1) * pw] = og * _silu(proj(p, COL_MZ + i * pw))

        def piece_conv(p, i):
            base = SUBLANES + p * pr
            u_scr[base:base + pr, i * pw:(i + 1) * pw] = proj(p, COL_MQK + i * pw)
            for cs in range(i * pw, (i + 1) * pw, LANES):
                cols = slice(cs, cs + LANES)
                acc = conv_b_ref[:, cols]
                for j in range(CONV_W):
                    r0 = base - carry_rows + j
                    acc = acc + u_scr[r0:r0 + pr, cols] * conv_w_ref[j:j + 1, cols]
                act = _silu(acc)
                if cs >= GROUP_W:
                    act = act * QK_SCALE
                qkm_i[group_rows(p), cols] = act

        def piece_gate(p):
            res = proj(p, COL_GATE, LANES)
            for cc in range(chunks_per_piece):
                c = p * chunks_per_piece + cc
                gates = res[cc * CHUNK:(cc + 1) * CHUNK, :] + gbias_ref[...]
                bcum = _cumsum_rows(tril_bf, _log_sigmoid(gates))
                gt_i[c * GT_PER_CHUNK + GT_GATES] = gates
                gt_i[c * GT_PER_CHUNK + GT_BCUM] = bcum
                gt_i[c * GT_PER_CHUNK + GT_ROWS] = jnp.where(lane_id < N_HEADS, gates, bcum).T

        n_col_pieces = GROUP_W // pw
        pieces = []
        for p in range(block_t // pr):
            pieces.append(functools.partial(piece_gate, p))
            for i in range(n_col_pieces):
                pieces.append(functools.partial(piece_conv, p, 2 * i))
                pieces.append(functools.partial(piece_cast, p, i, COL_RV, VV_RET))
                pieces.append(functools.partial(piece_rot, p, i, COL_RQ, cos_ref, sin_ref, qd_ref,
                                                RQK_Q, RQK_QDEC))
                pieces.append(functools.partial(piece_rz, p, i))
                pieces.append(functools.partial(piece_conv, p, 2 * i + 1))
                pieces.append(functools.partial(piece_cast, p, i, COL_MV, VV_M))
                pieces.append(functools.partial(piece_rot, p, i, COL_RK, kcos_ref, ksin_ref, kd_ref,
                                                RQK_K, RQK_KDEC))
                pieces.append(functools.partial(piece_moz, p, i))
        n_pieces = len(pieces)
        n_slots = EMITS_PER_CHUNK * n_chunks
        slots_done = [0]

        def emit_proj_pieces():
            slots_done[0] += 1
            target = -(-(slots_done[0] * (n_pieces - TAIL_PIECES)) // n_slots)
            while n_pieces - len(pieces) < target:
                pieces.pop(0)()

        def per_head(fn):
            out = []
            for h in heads:
                emit_proj_pieces()
                out.append(fn(h))
            return out

        for c in range(n_chunks):
            rows = slice(c * CHUNK, (c + 1) * CHUNK)

            q_bf = [rqk_c[rows, hcols(RQK_Q, h)] for h in heads]
            k_bf = [rqk_c[rows, hcols(RQK_K, h)] for h in heads]
            v_bf = [vv_c[rows, hcols(VV_RET, h)] for h in heads]
            sc = per_head(lambda h: _dot_nt(q_bf[h], k_bf[h]))
            state = [s_scr[h] for h in heads]
            upd = per_head(lambda h: _dot_tn(rqk_c[rows, hcols(RQK_KDEC, h)], v_bf[h]))
            for h in heads:
                s_scr[h] = state[h] * cdec_ref[h] + upd[h]
            o = per_head(lambda h: _dot(
                jnp.concatenate([(sc[h] * decay_ref[h]).astype(BF16),
                                 rqk_c[rows, hcols(RQK_QDEC, h)]], axis=1),
                jnp.concatenate([v_bf[h], state[h].astype(BF16)], axis=0)))

            def ret_out(h):
                hn = _head_norm(o[h], g_ret_ref[:, hcols(0, h)])
                mix_scr[rows, hcols(0, h)] = (hn * gz_c[rows, hcols(GZ_RET, h)]).astype(BF16)

            per_head(ret_out)

            gates = gt_c[c * GT_PER_CHUNK + GT_GATES]
            bcum = gt_c[c * GT_PER_CHUNK + GT_BCUM]
            rows_t = gt_c[c * GT_PER_CHUNK + GT_ROWS]
            q = [qkm_c[rows, hcols(0, h)] for h in heads]
            k = [qkm_c[rows, hcols(GROUP_W, h)] for h in heads]
            vaug_bf = [jnp.concatenate([vv_c[rows, hcols(VV_M, h)], ones_col], axis=1)
                       for h in heads]
            qk = per_head(lambda h: _dot_nt(q[h].astype(BF16), k[h].astype(BF16)))
            caug = [caug_scr[h] for h in heads]
            m_old = [m_scr[h, 0:1, 0:1] for h in heads]
            i_row = [rows_t[h:h + 1, :] for h in heads]
            b_row = [rows_t[N_HEADS + h:N_HEADS + h + 1, :] for h in heads]
            i_col = [gates[:, h:h + 1] for h in heads]
            b_col = [bcum[:, N_HEADS + h:N_HEADS + h + 1] for h in heads]
            a_col = [b_col[h] + m_old[h] for h in heads]
            dm = per_head(lambda h: (b_col[h] + (i_row[h] - b_row[h])) + causal_add)
            mt = [jnp.maximum(a_col[h], jnp.max(dm[h], axis=-1, keepdims=True)) for h in heads]
            w_inter = [jnp.exp(a_col[h] - mt[h]) for h in heads]
            res = per_head(lambda h: _dot(
                jnp.concatenate([(qk[h] * jnp.exp(dm[h] - mt[h])).astype(BF16),
                                 (q[h] * w_inter[h]).astype(BF16)], axis=1),
                jnp.concatenate([vaug_bf[h], caug[h].astype(BF16)], axis=0)))
            b_last = [b_col[h][CHUNK - 1:CHUNK, :] for h in heads]
            g_col = [b_last[h] - b_col[h] + i_col[h] for h in heads]
            m_new = [jnp.maximum(b_last[h] + m_old[h], jnp.max(g_col[h], axis=0, keepdims=True))
                     for h in heads]
            wk = [jnp.exp(g_col[h] - m_new[h]) for h in heads]
            wc = [jnp.exp(b_last[h] + m_old[h] - m_new[h]) for h in heads]

            def mlstm_update(h):
                caug_scr[h] = caug[h] * wc[h] + _dot_tn((k[h] * wk[h]).astype(BF16), vaug_bf[h])
                m_scr[h] = jnp.broadcast_to(m_new[h], (SUBLANES, LANES))

            per_head(mlstm_update)

            def mlstm_out(h):
                den = jnp.maximum(jnp.abs(res[h][:, HEAD_DIM:HEAD_DIM + 1]), jnp.exp(-mt[h]))
                hn = _head_norm(res[h][:, :HEAD_DIM] / den, g_m_ref[:, hcols(0, h)])
                mix_scr[rows, hcols(GROUP_W, h)] = (hn * gz_c[rows, hcols(GZ_M, h)]).astype(BF16)

            per_head(mlstm_out)

            emit_proj_pieces()
            mix = _dot(mix_scr[rows, :], w_out_ref[...])
            y_ref[0, rows, :] = _layer_norm(ALPHA * x_ref[0, rows, :] + mix, ln_g_ref[...], ln_b_ref[...])

        assert slots_done[0] == n_slots and len(pieces) == TAIL_PIECES
        while pieces:
            pieces.pop(0)()

    set_a = (rqk_a, vv_a, gz_a, qkm_a, gt_a)
    set_b = (rqk_b, vv_b, gz_b, qkm_b, gt_b)
    parity = lax.rem(g, 2)

    @pl.when(parity == 0)
    def _even_step():
        step_body(set_a, set_b)

    @pl.when(parity == 1)
    def _odd_step():
        step_body(set_b, set_a)

    @pl.when(jnp.logical_and(tn == nt - 1, g < n_blocks))
    def _write_conv_state():
        conv_out_ref[0] = u_scr[SUBLANES + block_t - carry_rows:SUBLANES + block_t, :]

    @pl.when(jnp.logical_and(t == nt - 1, g > 0))
    def _write_state():
        s_out_ref[0] = s_scr[...]
        m_out_ref[0] = jnp.zeros((SUBLANES, LANES), F32)
        for h in heads:
            caug = caug_scr[h]
            c_out_ref[0, h] = caug[:, :HEAD_DIM]
            n_out_ref[0, h:h + 1, :] = caug[:, HEAD_DIM:].T[0:1, :]
            m_out_ref[0, h:h + 1, :] = m_scr[h, 0:1, :]

    u_scr[0:SUBLANES, :] = u_scr[block_t:block_t + SUBLANES, :]


def _const_spec(shape):
    nd = len(shape)
    return pl.BlockSpec(shape, lambda g: (0,) * nd)


def _layer_spec(shape, layer):
    return pl.BlockSpec((None,) + tuple(shape[1:]), lambda g: (layer,) + (0,) * (len(shape) - 1))


def _prompt_layer(layer, x, w_in_bf, conv_w, conv_b, gbias, g_ret, g_m, w_out_bf, ln_g, ln_b, tabs):
    B, T, _ = x.shape
    bt = PROMPT_BLOCK_T
    assert T % bt == 0 and bt % CHUNK == 0 and GROUP_W % PROJ_PIECE_COLS == 0
    nt = T // bt
    n_blocks = B * nt
    n_chunks = bt // CHUNK
    cos_t, sin_t, kcos_t, ksin_t, decay, qd, kd, cdec, tril, mask, e0 = tabs

    def nxt(g):
        return jnp.minimum(g, n_blocks - 1)

    def cur(g):
        return jnp.maximum(g - 1, 0)

    row_spec = pl.BlockSpec((bt, LANES), lambda g: (nxt(g) % nt, 0))
    in_specs = [
        pl.BlockSpec((1, bt, D_MODEL), lambda g: (nxt(g) // nt, nxt(g) % nt, 0)),
        pl.BlockSpec((1, bt, D_MODEL), lambda g: (cur(g) // nt, cur(g) % nt, 0)),
        _layer_spec(w_in_bf.shape, layer), _const_spec(conv_w.shape), _const_spec(conv_b.shape),
        _const_spec(gbias.shape), _const_spec(g_ret.shape), _const_spec(g_m.shape),
        _layer_spec(w_out_bf.shape, layer), _const_spec(ln_g.shape), _const_spec(ln_b.shape),
        row_spec, row_spec, row_spec, row_spec,
        _const_spec(decay.shape), _const_spec(qd.shape), _const_spec(kd.shape),
        pl.BlockSpec(memory_space=pltpu.SMEM),
        _const_spec(tril.shape), _const_spec(mask.shape), _const_spec(e0.shape),
    ]
    out_shape = (
        jax.ShapeDtypeStruct((B, T, D_MODEL), F32),
        jax.ShapeDtypeStruct((B, N_HEADS, HEAD_DIM, HEAD_DIM), F32),
        jax.ShapeDtypeStruct((B, N_HEADS, HEAD_DIM, HEAD_DIM), F32),
        jax.ShapeDtypeStruct((B, N_HEADS, HEAD_DIM), F32),
        jax.ShapeDtypeStruct((B, SUBLANES, LANES), F32),
        jax.ShapeDtypeStruct((B, CONV_W - 1, 2 * GROUP_W), F32),
    )
    out_specs = (
        pl.BlockSpec((1, bt, D_MODEL), lambda g: (cur(g) // nt, cur(g) % nt, 0)),
        pl.BlockSpec((1, N_HEADS, HEAD_DIM, HEAD_DIM), lambda g: (cur(g) // nt, 0, 0, 0)),
        pl.BlockSpec((1, N_HEADS, HEAD_DIM, HEAD_DIM), lambda g: (cur(g) // nt, 0, 0, 0)),
        pl.BlockSpec((1, N_HEADS, HEAD_DIM), lambda g: (cur(g) // nt, 0, 0)),
        pl.BlockSpec((1, SUBLANES, LANES), lambda g: (cur(g) // nt, 0, 0)),
        pl.BlockSpec((1, CONV_W - 1, 2 * GROUP_W), lambda g: (nxt(g) // nt, 0, 0)),
    )
    operand_set = [
        pltpu.VMEM((bt, 4 * GROUP_W), BF16),
        pltpu.VMEM((bt, 2 * GROUP_W), BF16),
        pltpu.VMEM((bt, 2 * GROUP_W), F32),
        pltpu.VMEM((bt, 2 * GROUP_W), F32),
        pltpu.VMEM((n_chunks * GT_PER_CHUNK, CHUNK, LANES), F32),
    ]
    scratch = operand_set + operand_set + [
        pltpu.VMEM((bt, D_MODEL), BF16),
        pltpu.VMEM((SUBLANES + bt, 2 * GROUP_W), F32),
        pltpu.VMEM((bt, 2 * GROUP_W), BF16),
        pltpu.VMEM((N_HEADS, HEAD_DIM, HEAD_DIM), F32),
        pltpu.VMEM((N_HEADS, HEAD_DIM, 2 * HEAD_DIM), F32),
        pltpu.VMEM((N_HEADS, SUBLANES, LANES), F32),
    ]
    y, s, c, n, m, cv = pl.pallas_call(
        functools.partial(_prompt_kernel, nt, n_blocks),
        grid=(n_blocks + 1,),
        in_specs=in_specs,
        out_specs=out_specs,
        out_shape=out_shape,
        scratch_shapes=scratch,
        compiler_params=pltpu.CompilerParams(
            dimension_semantics=("arbitrary",),
            vmem_limit_bytes=VMEM_LIMIT_BYTES),
        name="prompt_layer",
    )(x, x, w_in_bf, conv_w, conv_b, gbias, g_ret, g_m, w_out_bf, ln_g, ln_b,
      cos_t, sin_t, kcos_t, ksin_t, decay, qd, kd, cdec, tril, mask, e0)
    return y, s, c, n, m[:, :N_HEADS, 0], cv


def _sample_kernel(x_ref, w_in_ref, conv_w_ref, conv_b_ref, gbias_ref, g_ret_ref, g_m_ref,
                   w_out_ref, ln_g_ref, ln_b_ref, rot_ref, gam_ref, esel_ref,
                   s_ref, c_ref, n_ref, m_ref, cv_ref,
                   y_ref, s_out_ref, c_out_ref, n_out_ref, m_out_ref, cv_out_ref,
                   proj_scr, xcur_scr, mix_scr):
    layer = pl.program_id(0)
    j = pl.program_id(1)
    bb = s_ref.shape[1]

    @pl.when(jnp.logical_and(layer == 0, j == 0))
    def _load_x():
        xcur_scr[...] = x_ref[...]

    @pl.when(j == 0)
    def _project():
        xb = xcur_scr[...].astype(BF16)
        for lo in range(0, N_PAD, 512):
            hi = min(lo + 512, N_PAD)
            proj_scr[:, lo:hi] = _dot(xb, w_in_ref[0, :, lo:hi])

    r0 = pl.multiple_of(j * bb, bb)
    rows = pl.ds(r0, bb)
    cos_t = rot_ref[0:1, :]
    sin_t = rot_ref[1:2, :]
    kcos_t = rot_ref[2:3, :]
    ksin_t = rot_ref[3:4, :]
    esel = esel_ref[...]
    row_id = lax.broadcasted_iota(jnp.int32, (bb, bb * HEAD_DIM), 0)
    blk_id = jnp.right_shift(lax.broadcasted_iota(jnp.int32, (bb, bb * HEAD_DIM), 1), 7)
    pad_rows = jnp.zeros((LANES - bb, HEAD_DIM), F32)
    pad_wide = jnp.zeros((LANES - bb, bb * HEAD_DIM), BF16)

    def col_form(x8):
        return jnp.concatenate([x8, pad_rows], axis=0).T.astype(BF16)

    def outer_all(k8, v8):
        vt = jnp.concatenate([v8] * bb, axis=1)
        vsel = jnp.where(row_id == blk_id, vt, 0.0).astype(BF16)
        return _dot(col_form(k8), jnp.concatenate([vsel, pad_wide], axis=0))

    def col_bcast_all(q8):
        return _dot(col_form(q8), esel)

    for h in range(N_HEADS):
        hc = h * HEAD_DIM
        q8 = _rotary(proj_scr[rows, COL_RQ + hc:COL_RQ + hc + HEAD_DIM], cos_t, sin_t)
        k8 = _rotary(proj_scr[rows, COL_RK + hc:COL_RK + hc + HEAD_DIM], kcos_t, ksin_t)
        v8 = proj_scr[rows, COL_RV + hc:COL_RV + hc + HEAD_DIM]
        kv_all = outer_all(k8, v8)
        qc_all = col_bcast_all(q8)
        gamma = gam_ref[h]
        o_rows = []
        for r in range(bb):
            blk = slice(r * HEAD_DIM, (r + 1) * HEAD_DIM)
            s_new = s_ref[0, r, h] * gamma + kv_all[:, blk]
            s_out_ref[0, r, h] = s_new
            o_rows.append(jnp.sum(qc_all[:, blk] * s_new, axis=0, keepdims=True))
        o8 = _head_norm(jnp.concatenate(o_rows, axis=0), g_ret_ref[0, :, hc:hc + HEAD_DIM])
        z8 = proj_scr[rows, COL_RZ + hc:COL_RZ + hc + HEAD_DIM]
        mix_scr[rows, hc:hc + HEAD_DIM] = o8 * _silu(z8)

    u8 = proj_scr[rows, COL_MQK:COL_MQK + 2 * GROUP_W]
    acc = conv_b_ref[0] + u8 * conv_w_ref[0, CONV_W - 1:CONV_W, :]
    for jj in range(CONV_W - 1):
        acc = acc + cv_ref[0, jj] * conv_w_ref[0, jj:jj + 1, :]
    for jj in range(1, CONV_W - 1):
        cv_out_ref[0, jj - 1] = cv_ref[0, jj]
    cv_out_ref[0, CONV_W - 2] = u8
    qk8 = _silu(acc)

    gates = proj_scr[rows, COL_GATE:COL_GATE + LANES] + gbias_ref[0]
    i_al = pltpu.roll(gates, N_HEADS, axis=1)
    bm = _log_sigmoid(gates) + m_ref[0]
    m_new = jnp.maximum(bm, i_al)
    wk = jnp.exp(i_al - m_new)
    wc = jnp.exp(bm - m_new)
    einv = jnp.exp(-m_new)
    m_out_ref[0] = m_new
    for h in range(N_HEADS):
        hc = h * HEAD_DIM
        gl = N_HEADS + h
        q8 = qk8[:, hc:hc + HEAD_DIM]
        k8 = qk8[:, GROUP_W + hc:GROUP_W + hc + HEAD_DIM] * QK_SCALE
        v8 = proj_scr[rows, COL_MV + hc:COL_MV + hc + HEAD_DIM]
        wk_h = wk[:, gl:gl + 1]
        wc_h = jnp.broadcast_to(wc[:, gl:gl + 1], (bb, HEAD_DIM))
        kw8 = k8 * wk_h
        kv_all = outer_all(kw8, v8)
        qc_all = col_bcast_all(q8)
        n_new = n_ref[0, :, h, :] * wc_h + kw8
        n_out_ref[0, :, h, :] = n_new
        num_rows = []
        for r in range(bb):
            blk = slice(r * HEAD_DIM, (r + 1) * HEAD_DIM)
            c_new = c_ref[0, r, h] * wc_h[r:r + 1, :] + kv_all[:, blk]
            c_out_ref[0, r, h] = c_new
            num_rows.append(jnp.sum(qc_all[:, blk] * c_new, axis=0, keepdims=True))
        num = jnp.concatenate(num_rows, axis=0)
        q_bf = q8.astype(BF16).astype(F32)
        den = jnp.sum(q_bf * n_new, axis=-1, keepdims=True)
        hout = num / jnp.maximum(jnp.abs(den), einv[:, gl:gl + 1])
        hn = _head_norm(hout, g_m_ref[0, :, hc:hc + HEAD_DIM])
        og = _sigmoid(proj_scr[rows, COL_MO + hc:COL_MO + hc + HEAD_DIM])
        zg = _silu(proj_scr[rows, COL_MZ + hc:COL_MZ + hc + HEAD_DIM])
        mix_scr[rows, GROUP_W + hc:GROUP_W + hc + HEAD_DIM] = hn * og * zg

    @pl.when(j == pl.num_programs(1) - 1)
    def _finish_layer():
        mix = _dot(mix_scr[...].astype(BF16), w_out_ref[0])
        y = _layer_norm(ALPHA * xcur_scr[...] + mix, ln_g_ref[0], ln_b_ref[0])
        xcur_scr[...] = y
        y_ref[...] = y


def _sample_layers(x, w_in_bf, conv_w, conv_b, gbias, g_ret, g_m, w_out_bf, ln_g, ln_b,
                   rot, gam, esel, state_ret, state_c, state_n, m_pad, state_conv):
    bs = x.shape[0]
    bb = SAMPLE_BLOCK_B
    assert bs % bb == 0
    nb = bs // bb

    def lspec(shape):
        nd = len(shape)
        return pl.BlockSpec((1,) + tuple(shape[1:]), lambda l, j: (l,) + (0,) * (nd - 1))

    def cspec(shape):
        nd = len(shape)
        return pl.BlockSpec(tuple(shape), lambda l, j: (0,) * nd)

    mat_spec = pl.BlockSpec((1, bb, N_HEADS, HEAD_DIM, HEAD_DIM), lambda l, j: (l, j, 0, 0, 0))
    n_spec = pl.BlockSpec((1, bb, N_HEADS, HEAD_DIM), lambda l, j: (l, j, 0, 0))
    m_spec = pl.BlockSpec((1, bb, LANES), lambda l, j: (l, j, 0))
    cv_spec = pl.BlockSpec((1, CONV_W - 1, bb, 2 * GROUP_W), lambda l, j: (l, 0, j, 0))
    in_specs = [
        cspec(x.shape), lspec(w_in_bf.shape), lspec(conv_w.shape), lspec(conv_b.shape),
        lspec(gbias.shape), lspec(g_ret.shape), lspec(g_m.shape), lspec(w_out_bf.shape),
        lspec(ln_g.shape), lspec(ln_b.shape), cspec(rot.shape),
        pl.BlockSpec(memory_space=pltpu.SMEM), cspec(esel.shape),
        mat_spec, mat_spec, n_spec, m_spec, cv_spec,
    ]
    out_shape = (
        jax.ShapeDtypeStruct(x.shape, F32),
        jax.ShapeDtypeStruct(state_ret.shape, F32),
        jax.ShapeDtypeStruct(state_c.shape, F32),
        jax.ShapeDtypeStruct(state_n.shape, F32),
        jax.ShapeDtypeStruct(m_pad.shape, F32),
        jax.ShapeDtypeStruct(state_conv.shape, F32),
    )
    out_specs = (cspec(x.shape), mat_spec, mat_spec, n_spec, m_spec, cv_spec)
    scratch = [
        pltpu.VMEM((bs, N_PAD), F32),
        pltpu.VMEM((bs, D_MODEL), F32),
        pltpu.VMEM((bs, 2 * GROUP_W), F32),
    ]
    return pl.pallas_call(
        _sample_kernel,
        grid=(DEPTH, nb),
        in_specs=in_specs,
        out_specs=out_specs,
        out_shape=out_shape,
        scratch_shapes=scratch,
        compiler_params=pltpu.CompilerParams(
            dimension_semantics=("arbitrary", "arbitrary"),
            vmem_limit_bytes=VMEM_LIMIT_BYTES),
        name="sample_layers",
    )(x, w_in_bf, conv_w, conv_b, gbias, g_ret, g_m, w_out_bf, ln_g, ln_b,
      rot, gam, esel, state_ret, state_c, state_n, m_pad, state_conv)


def _cast_weight_kernel(w_ref, o_ref):
    o_ref[0] = w_ref[0].astype(BF16)


def _cast_weight(w):
    depth, k, n = w.shape
    rows = WEIGHT_CAST_ROWS
    assert k % rows == 0
    return pl.pallas_call(
        _cast_weight_kernel,
        grid=(depth, k // rows),
        in_specs=[pl.BlockSpec((1, rows, n), lambda l, r: (l, r, 0))],
        out_specs=pl.BlockSpec((1, rows, n), lambda l, r: (l, r, 0)),
        out_shape=jax.ShapeDtypeStruct((depth, k, n), BF16),
        compiler_params=pltpu.CompilerParams(dimension_semantics=("parallel", "parallel")),
        name="cast_weight",
    )(w)


def _cast_transposed_weight_kernel(n_valid, wt_ref, o_ref):
    rows = wt_ref.shape[1]
    row_id = pl.program_id(1) * rows + lax.broadcasted_iota(jnp.int32, wt_ref.shape[1:], 0)
    wt = jnp.where(row_id < n_valid, wt_ref[0], 0.0)
    o_ref[0] = wt.T.astype(BF16)


def _cast_transposed_weight(wt, n_out):
    depth, n, k = wt.shape
    rows = WEIGHT_CAST_ROWS
    assert n_out % LANES == 0 and 0 <= n_out - n < LANES
    return pl.pallas_call(
        functools.partial(_cast_transposed_weight_kernel, n),
        grid=(depth, pl.cdiv(n_out, rows)),
        in_specs=[pl.BlockSpec((1, rows, k), lambda l, r: (l, r, 0))],
        out_specs=pl.BlockSpec((1, k, rows), lambda l, r: (l, 0, r)),
        out_shape=jax.ShapeDtypeStruct((depth, k, n_out), BF16),
        compiler_params=pltpu.CompilerParams(dimension_semantics=("parallel", "parallel")),
        name="cast_weight_t",
    )(wt)


def _rotary_tables(pos):
    half = HEAD_DIM // 2
    inv = np.float32(ROPE_BASE) ** (-np.arange(half, dtype=np.float32) / np.float32(half))
    ang = pos.astype(np.float32)[:, None] * inv[None, :].astype(np.float32)
    cos = np.cos(ang.astype(np.float64)).astype(np.float32)
    sin = np.sin(ang.astype(np.float64)).astype(np.float32)
    cos_t = np.concatenate([cos, cos], axis=-1)
    sin_t = np.concatenate([-sin, sin], axis=-1)
    scale = np.float32(QK_SCALE)
    return cos_t, sin_t, cos_t * scale, sin_t * scale


def _retention_tables():
    L = CHUNK
    f32 = np.float32
    log_gamma = np.log(f32(1.0) - f32(2.0) ** (f32(-5.0) - np.arange(N_HEADS, dtype=f32))).astype(f32)
    idx = np.arange(L, dtype=f32)
    diff = idx[:, None] - idx[None, :]
    decay = (np.exp(log_gamma[:, None, None] * np.maximum(diff, f32(0.0))) * (diff >= 0)).astype(f32)
    q_decay = np.exp(log_gamma[:, None] * (idx + f32(1.0))).astype(f32)
    k_decay = np.exp(log_gamma[:, None] * (f32(L - 1.0) - idx)).astype(f32)
    c_decay = np.exp(log_gamma * f32(L)).astype(f32)
    qd = np.ascontiguousarray(np.broadcast_to(q_decay[:, :, None], (N_HEADS, L, HEAD_DIM)))
    kd = np.ascontiguousarray(np.broadcast_to(k_decay[:, :, None], (N_HEADS, L, HEAD_DIM)))
    gamma1 = np.exp(log_gamma).astype(f32)
    return decay, qd, kd, c_decay, gamma1


def kernel(x_prompt, x_sample, state_ret, state_mlstm_C, state_mlstm_n, state_mlstm_m, state_conv,
           w_in, conv_w, conv_b, b_i, b_f, g_ret, g_m, w_out, ln_g, ln_b):
    B, T, _ = x_prompt.shape
    Bs, Ts, _ = x_sample.shape
    assert Ts == 1

    w_in_bf = _cast_transposed_weight(jnp.swapaxes(w_in, 1, 2), N_PAD)
    w_out_bf = _cast_weight(w_out)
    gbias = jnp.pad(jnp.concatenate([b_i, b_f], axis=-1), ((0, 0), (0, LANES - 2 * N_HEADS)))
    gbias = gbias.reshape(DEPTH, 1, LANES)
    conv_b3 = conv_b.reshape(DEPTH, 1, 2 * GROUP_W)
    g_ret3 = g_ret.reshape(DEPTH, 1, GROUP_W)
    g_m3 = g_m.reshape(DEPTH, 1, GROUP_W)
    ln_g3 = ln_g.reshape(DEPTH, 1, D_MODEL)
    ln_b3 = ln_b.reshape(DEPTH, 1, D_MODEL)

    decay, qd, kd, c_decay, gamma1 = _retention_tables()
    idx = np.arange(CHUNK)
    causal = idx[:, None] >= idx[None, :]
    tril = jnp.asarray(causal, BF16)
    mask_add = np.where(causal, 0.0, -np.inf).astype(np.float32)
    e0 = jnp.asarray(np.broadcast_to(np.arange(LANES)[None, :] == 0, (CHUNK, LANES)), BF16)
    tabs_p = _rotary_tables(np.arange(T)) + (decay, qd, kd, c_decay, tril, mask_add, e0)
    tabs_p = tuple(jnp.asarray(a) for a in tabs_p)

    xp = x_prompt
    rp, cp, np_, mp, vp = [], [], [], [], []
    for l in range(DEPTH):
        xp, s, c, n, m, cv = _prompt_layer(
            l, xp, w_in_bf, conv_w[l], conv_b3[l], gbias[l], g_ret3[l], g_m3[l],
            w_out_bf, ln_g3[l], ln_b3[l], tabs_p)
        rp.append(s); cp.append(c); np_.append(n); mp.append(m); vp.append(cv)

    rot = jnp.asarray(np.concatenate(_rotary_tables(PAST_LEN + np.arange(Ts)), axis=0))
    bb = SAMPLE_BLOCK_B
    esel = jnp.asarray(np.arange(LANES)[:, None] == (np.arange(bb * HEAD_DIM)[None, :] // HEAD_DIM), BF16)
    gamma1 = jnp.asarray(gamma1)
    m_pad = jnp.pad(state_mlstm_m, ((0, 0), (0, 0), (N_HEADS, LANES - 2 * N_HEADS)))
    ys, rs, cs, ns, ms_pad, vs = _sample_layers(
        x_sample.reshape(Bs, D_MODEL), w_in_bf, conv_w, conv_b3, gbias, g_ret3, g_m3, w_out_bf,
        ln_g3, ln_b3, rot, gamma1, esel, state_ret, state_mlstm_C, state_mlstm_n, m_pad,
        jnp.swapaxes(state_conv, 1, 2))
    vs = jnp.swapaxes(vs, 1, 2)
    ms = ms_pad[:, :, N_HEADS:2 * N_HEADS]

    return (xp, ys.reshape(Bs, Ts, D_MODEL),
            jnp.stack(rp), jnp.stack(cp), jnp.stack(np_), jnp.stack(mp), jnp.stack(vp),
            rs, cs, ns, ms, vs)
```

```python
import functools

import jax
import jax.numpy as jnp
import numpy as np
from jax import lax
from jax.experimental import pallas as pl
from jax.experimental.pallas import tpu as pltpu

F32 = jnp.float32
BF16 = jnp.bfloat16

D_MODEL = 1024
DEPTH = 2
PAST_LEN = 16384
N_HEADS = 4
HEAD_DIM = 128
GROUP_W = N_HEADS * HEAD_DIM
CONV_W = 4
CHUNK = 128
ROPE_BASE = 10000.0
LN_EPS = 1e-5
GN_EPS = 1e-5
ALPHA = (2 * DEPTH) ** 0.25
QK_SCALE = HEAD_DIM ** -0.5

LANES = 128
SUBLANES = 8

COL_RQ = 0
COL_RK = COL_RQ + GROUP_W
COL_RV = COL_RK + GROUP_W
COL_RZ = COL_RV + GROUP_W
COL_MQK = COL_RZ + GROUP_W
COL_MV = COL_MQK + 2 * GROUP_W
COL_MO = COL_MV + GROUP_W
COL_MZ = COL_MO + GROUP_W
COL_GATE = COL_MZ + GROUP_W
N_IN = COL_GATE + 2 * N_HEADS
N_PAD = COL_GATE + LANES

PROMPT_BLOCK_T = 256
PROJ_PIECE_COLS = 256
PROJ_PIECE_ROWS = 256
SAMPLE_BLOCK_B = 8
WEIGHT_CAST_ROWS = 512
VMEM_LIMIT_BYTES = 56 * 1024 * 1024


def _sigmoid(x):
    return 1.0 / (1.0 + jnp.exp(-x))


def _silu(x):
    return x * _sigmoid(x)


def _log_sigmoid(x):
    return jnp.minimum(x, 0.0) - jnp.log1p(jnp.exp(-jnp.abs(x)))


def _dot(a, b):
    return jnp.dot(a, b, preferred_element_type=F32)


def _dot_nt(a, b):
    return lax.dot_general(a, b, (((1,), (1,)), ((), ())), preferred_element_type=F32)


def _dot_tn(a, b):
    return lax.dot_general(a, b, (((0,), (0,)), ((), ())), preferred_element_type=F32)


def _rotary(x, cos_t, sin_t):
    return x * cos_t + pltpu.roll(x, HEAD_DIM // 2, axis=1) * sin_t


def _head_norm(h, g):
    mu = jnp.mean(h, axis=-1, keepdims=True)
    d = h - mu
    var = jnp.mean(d * d, axis=-1, keepdims=True)
    return d * lax.rsqrt(var + GN_EPS) * g


def _layer_norm(x, g, b):
    mu = jnp.mean(x, axis=-1, keepdims=True)
    d = x - mu
    var = jnp.mean(d * d, axis=-1, keepdims=True)
    return d * lax.rsqrt(var + LN_EPS) * g + b


def _cumsum_lanes(triu_bf, x):
    hi = x.astype(BF16)
    r1 = x - hi.astype(F32)
    mid = r1.astype(BF16)
    lo = (r1 - mid.astype(F32)).astype(BF16)
    return _dot(hi, triu_bf) + _dot(mid, triu_bf) + _dot(lo, triu_bf)


RQK_Q, RQK_QDEC, RQK_K, RQK_KDEC = 0, GROUP_W, 2 * GROUP_W, 3 * GROUP_W
VV_RET, VV_M = 0, GROUP_W
GZ_RET, GZ_M = 0, GROUP_W
GT_COLS, GT_ROWS, GT_PER_CHUNK = 0, 1, 2
GT_ROWS_USED = SUBLANES
EMITS_PER_CHUNK = 9 * N_HEADS + 1
TAIL_PIECES = 0


def _prompt_kernel(nt, n_blocks, xn_ref, x_ref, w_in_ref, conv_w_ref, conv_b_ref, gbias_ref,
                   g_ret_ref, g_m_ref, w_out_ref, ln_g_ref, ln_b_ref,
                   cos_ref, sin_ref, kcos_ref, ksin_ref,
                   decay_ref, qd_ref, kd_ref, cdec_ref, triu_ref, mask_ref, e0_ref,
                   y_ref, s_out_ref, c_out_ref, n_out_ref, m_out_ref, conv_out_ref,
                   rqk_a, vv_a, gz_a, qkm_a, gt_a, rqk_b, vv_b, gz_b, qkm_b, gt_b,
                   xb_scr, u_scr, mix_scr, s_scr, caug_scr, m_scr):
    g = pl.program_id(0)
    t = lax.rem(jnp.maximum(g - 1, 0), nt)
    tn = lax.rem(jnp.minimum(g, n_blocks - 1), nt)
    block_t = x_ref.shape[1]
    n_chunks = block_t // CHUNK
    carry_rows = CONV_W - 1
    heads = range(N_HEADS)

    @pl.when(g == 0)
    def _init_pipeline():
        for ref in (rqk_b, vv_b, gz_b, qkm_b, gt_b):
            ref[...] = jnp.zeros_like(ref)

    @pl.when(t == 0)
    def _init_state():
        s_scr[...] = jnp.zeros_like(s_scr)
        caug_scr[...] = jnp.zeros_like(caug_scr)
        m_scr[...] = jnp.zeros_like(m_scr)

    @pl.when(tn == 0)
    def _init_conv_carry():
        u_scr[0:SUBLANES, :] = jnp.zeros((SUBLANES, 2 * GROUP_W), F32)

    def hcols(base, h):
        return slice(base + h * HEAD_DIM, base + (h + 1) * HEAD_DIM)

    def step_body(set_in, set_cur):
        rqk_i, vv_i, gz_i, qkm_i, gt_i = set_in
        rqk_c, vv_c, gz_c, qkm_c, gt_c = set_cur
        triu_bf = triu_ref[...]
        causal_add = mask_ref[...]
        ones_col = e0_ref[...]
        sub_id = lax.broadcasted_iota(jnp.int32, (SUBLANES, LANES), 0)
        pad_rows = jnp.zeros((CHUNK - GT_ROWS_USED, LANES), F32)
        xb_scr[...] = xn_ref[0].astype(BF16)
        pw = PROJ_PIECE_COLS
        heads_per_piece = pw // HEAD_DIM

        pr = PROJ_PIECE_ROWS
        chunks_per_piece = pr // CHUNK

        def group_rows(p):
            return slice(p * pr, (p + 1) * pr)

        def proj(p, col, width=pw):
            return _dot(xb_scr[group_rows(p), :], w_in_ref[:, col:col + width])

        def piece_rot(p, i, col_base, cos_r, sin_r, dec_ref, dst, dst_dec):
            res = proj(p, col_base + i * pw)
            for hh in range(heads_per_piece):
                h = i * heads_per_piece + hh
                for cc in range(chunks_per_piece):
                    rows = slice(p * pr + cc * CHUNK, p * pr + (cc + 1) * CHUNK)
                    r = _rotary(res[cc * CHUNK:(cc + 1) * CHUNK, hh * HEAD_DIM:(hh + 1) * HEAD_DIM],
                                cos_r[rows, :], sin_r[rows, :])
                    rqk_i[rows, hcols(dst, h)] = r.astype(BF16)
                    rqk_i[rows, hcols(dst_dec, h)] = (r * dec_ref[h]).astype(BF16)

        def piece_cast(p, i, col_base, dst):
            vv_i[group_rows(p), dst + i * pw:dst + (i + 1) * pw] = proj(p, col_base + i * pw).astype(BF16)

        def piece_rz(p, i):
            gz_i[group_rows(p), GZ_RET + i * pw:GZ_RET + (i + 1) * pw] = _silu(proj(p, COL_RZ + i * pw))

        def piece_moz(p, i):
            og = _sigmoid(proj(p, COL_MO + i * pw))
            gz_i[group_rows(p), GZ_M + i * pw:GZ_M + (i + 1) * pw] = og * _silu(proj(p, COL_MZ + i * pw))

        def piece_conv(p, i):
            base = SUBLANES + p * pr
            u_scr[base:base + pr, i * pw:(i + 1) * pw] = proj(p, COL_MQK + i * pw)
            for cs in range(i * pw, (i + 1) * pw, LANES):
                cols = slice(cs, cs + LANES)
                acc = conv_b_ref[:, cols]
                for j in range(CONV_W):
                    r0 = base - carry_rows + j
                    acc = acc + u_scr[r0:r0 + pr, cols] * conv_w_ref[j:j + 1, cols]
                act = _silu(acc)
                if cs >= GROUP_W:
                    act = act * QK_SCALE
                qkm_i[group_rows(p), cols] = act

        def piece_gate(p):
            res = proj(p, COL_GATE, LANES)
            for cc in range(chunks_per_piece):
                c = p * chunks_per_piece + cc
                gates = res[cc * CHUNK:(cc + 1) * CHUNK, :] + gbias_ref[...]
                g8 = gates.T[0:SUBLANES, :]
                rows8 = jnp.where(sub_id < N_HEADS, g8, _cumsum_lanes(triu_bf, _log_sigmoid(g8)))
                gt_i[c * GT_PER_CHUNK + GT_ROWS, 0:GT_ROWS_USED, :] = rows8
                gt_i[c * GT_PER_CHUNK + GT_COLS] = jnp.concatenate([rows8, pad_rows], axis=0).T

        n_col_pieces = GROUP_W // pw
        pieces = []
        for p in range(block_t // pr):
            pieces.append(functools.partial(piece_gate, p))
            for i in range(n_col_pieces):
                pieces.append(functools.partial(piece_conv, p, 2 * i))
                pieces.append(functools.partial(piece_cast, p, i, COL_RV, VV_RET))
                pieces.append(functools.partial(piece_rot, p, i, COL_RQ, cos_ref, sin_ref, qd_ref,
                                                RQK_Q, RQK_QDEC))
                pieces.append(functools.partial(piece_rz, p, i))
                pieces.append(functools.partial(piece_conv, p, 2 * i + 1))
                pieces.append(functools.partial(piece_cast, p, i, COL_MV, VV_M))
                pieces.append(functools.partial(piece_rot, p, i, COL_RK, kcos_ref, ksin_ref, kd_ref,
                                                RQK_K, RQK_KDEC))
                pieces.append(functools.partial(piece_moz, p, i))
        n_pieces = len(pieces)
        n_slots = EMITS_PER_CHUNK * n_chunks
        slots_done = [0]

        def emit_proj_pieces():
            slots_done[0] += 1
            target = -(-(slots_done[0] * (n_pieces - TAIL_PIECES)) // n_slots)
            while n_pieces - len(pieces) < target:
                pieces.pop(0)()

        def per_head(fn):
            out = []
            for h in heads:
                emit_proj_pieces()
                out.append(fn(h))
            return out

        for c in range(n_chunks):
            rows = slice(c * CHUNK, (c + 1) * CHUNK)

            q_bf = [rqk_c[rows, hcols(RQK_Q, h)] for h in heads]
            k_bf = [rqk_c[rows, hcols(RQK_K, h)] for h in heads]
            v_bf = [vv_c[rows, hcols(VV_RET, h)] for h in heads]
            sc = per_head(lambda h: _dot_nt(q_bf[h], k_bf[h]))
            state = [s_scr[h] for h in heads]
            upd = per_head(lambda h: _dot_tn(rqk_c[rows, hcols(RQK_KDEC, h)], v_bf[h]))
            for h in heads:
                s_scr[h] = state[h] * cdec_ref[h] + upd[h]
            o = per_head(lambda h: _dot(
                jnp.concatenate([(sc[h] * decay_ref[h]).astype(BF16),
                                 rqk_c[rows, hcols(RQK_QDEC, h)]], axis=1),
                jnp.concatenate([v_bf[h], state[h].astype(BF16)], axis=0)))

            def ret_out(h):
                hn = _head_norm(o[h], g_ret_ref[:, hcols(0, h)])
                mix_scr[rows, hcols(0, h)] = (hn * gz_c[rows, hcols(GZ_RET, h)]).astype(BF16)

            per_head(ret_out)

            cols_t = gt_c[c * GT_PER_CHUNK + GT_COLS]
            rows_t = gt_c[c * GT_PER_CHUNK + GT_ROWS, 0:GT_ROWS_USED, :]
            q = [qkm_c[rows, hcols(0, h)] for h in heads]
            k = [qkm_c[rows, hcols(GROUP_W, h)] for h in heads]
            vaug_bf = [jnp.concatenate([vv_c[rows, hcols(VV_M, h)], ones_col], axis=1)
                       for h in heads]
            qk = per_head(lambda h: _dot_nt(q[h].astype(BF16), k[h].astype(BF16)))
            caug = [caug_scr[h] for h in heads]
            m_old = [m_scr[h, 0:1, 0:1] for h in heads]
            i_row = [rows_t[h:h + 1, :] for h in heads]
            b_row = [rows_t[N_HEADS + h:N_HEADS + h + 1, :] for h in heads]
            i_col = [cols_t[:, h:h + 1] for h in heads]
            b_col = [cols_t[:, N_HEADS + h:N_HEADS + h + 1] for h in heads]
            a_col = [b_col[h] + m_old[h] for h in heads]
            dm = per_head(lambda h: (b_col[h] + (i_row[h] - b_row[h])) + causal_add)
            mt = [jnp.maximum(a_col[h], jnp.max(dm[h], axis=-1, keepdims=True)) for h in heads]
            w_inter = [jnp.exp(a_col[h] - mt[h]) for h in heads]
            res = per_head(lambda h: _dot(
                jnp.concatenate([(qk[h] * jnp.exp(dm[h] - mt[h])).astype(BF16),
                                 (q[h] * w_inter[h]).astype(BF16)], axis=1),
                jnp.concatenate([vaug_bf[h], caug[h].astype(BF16)], axis=0)))
            b_last = [b_col[h][CHUNK - 1:CHUNK, :] for h in heads]
            g_col = [b_last[h] - b_col[h] + i_col[h] for h in heads]
            m_new = [jnp.maximum(b_last[h] + m_old[h], jnp.max(g_col[h], axis=0, keepdims=True))
                     for h in heads]
            wk = [jnp.exp(g_col[h] - m_new[h]) for h in heads]
            wc = [jnp.exp(b_last[h] + m_old[h] - m_new[h]) for h in heads]

            def mlstm_update(h):
                caug_scr[h] = caug[h] * wc[h] + _dot_tn((k[h] * wk[h]).astype(BF16), vaug_bf[h])
                m_scr[h] = jnp.broadcast_to(m_new[h], (SUBLANES, LANES))

            per_head(mlstm_update)

            def mlstm_out(h):
                den = jnp.maximum(jnp.abs(res[h][:, HEAD_DIM:HEAD_DIM + 1]), jnp.exp(-mt[h]))
                hn = _head_norm(res[h][:, :HEAD_DIM] / den, g_m_ref[:, hcols(0, h)])
                mix_scr[rows, hcols(GROUP_W, h)] = (hn * gz_c[rows, hcols(GZ_M, h)]).astype(BF16)

            per_head(mlstm_out)

            emit_proj_pieces()
            mix = _dot(mix_scr[rows, :], w_out_ref[...])
            y_ref[0, rows, :] = _layer_norm(ALPHA * x_ref[0, rows, :] + mix, ln_g_ref[...], ln_b_ref[...])

        assert slots_done[0] == n_slots and len(pieces) == TAIL_PIECES
        while pieces:
            pieces.pop(0)()

    set_a = (rqk_a, vv_a, gz_a, qkm_a, gt_a)
    set_b = (rqk_b, vv_b, gz_b, qkm_b, gt_b)
    parity = lax.rem(g, 2)

    @pl.when(parity == 0)
    def _even_step():
        step_body(set_a, set_b)

    @pl.when(parity == 1)
    def _odd_step():
        step_body(set_b, set_a)

    @pl.when(jnp.logical_and(tn == nt - 1, g < n_blocks))
    def _write_conv_state():
        conv_out_ref[0] = u_scr[SUBLANES + block_t - carry_rows:SUBLANES + block_t, :]

    @pl.when(jnp.logical_and(t == nt - 1, g > 0))
    def _write_state():
        s_out_ref[0] = s_scr[...]
        m_out_ref[0] = jnp.zeros((SUBLANES, LANES), F32)
        for h in heads:
            caug = caug_scr[h]
            c_out_ref[0, h] = caug[:, :HEAD_DIM]
            n_out_ref[0, h:h + 1, :] = caug[:, HEAD_DIM:].T[0:1, :]
            m_out_ref[0, h:h + 1, :] = m_scr[h, 0:1, :]

    u_scr[0:SUBLANES, :] = u_scr[block_t:block_t + SUBLANES, :]


def _const_spec(shape):
    nd = len(shape)
    return pl.BlockSpec(shape, lambda g: (0,) * nd)


def _layer_spec(shape, layer):
    return pl.BlockSpec((None,) + tuple(shape[1:]), lambda g: (layer,) + (0,) * (len(shape) - 1))


def _prompt_layer(layer, x, w_in_bf, conv_w, conv_b, gbias, g_ret, g_m, w_out_bf, ln_g, ln_b, tabs):
    B, T, _ = x.shape
    bt = PROMPT_BLOCK_T
    assert T % bt == 0 and bt % CHUNK == 0 and GROUP_W % PROJ_PIECE_COLS == 0
    nt = T // bt
    n_blocks = B * nt
    n_chunks = bt // CHUNK
    cos_t, sin_t, kcos_t, ksin_t, decay, qd, kd, cdec, triu, mask, e0 = tabs

    def nxt(g):
        return jnp.minimum(g, n_blocks - 1)

    def cur(g):
        return jnp.maximum(g - 1, 0)

    row_spec = pl.BlockSpec((bt, LANES), lambda g: (nxt(g) % nt, 0))
    in_specs = [
        pl.BlockSpec((1, bt, D_MODEL), lambda g: (nxt(g) // nt, nxt(g) % nt, 0)),
        pl.BlockSpec((1, bt, D_MODEL), lambda g: (cur(g) // nt, cur(g) % nt, 0)),
        _layer_spec(w_in_bf.shape, layer), _const_spec(conv_w.shape), _const_spec(conv_b.shape),
        _const_spec(gbias.shape), _const_spec(g_ret.shape), _const_spec(g_m.shape),
        _layer_spec(w_out_bf.shape, layer), _const_spec(ln_g.shape), _const_spec(ln_b.shape),
        row_spec, row_spec, row_spec, row_spec,
        _const_spec(decay.shape), _const_spec(qd.shape), _const_spec(kd.shape),
        pl.BlockSpec(memory_space=pltpu.SMEM),
        _const_spec(triu.shape), _const_spec(mask.shape), _const_spec(e0.shape),
    ]
    out_shape = (
        jax.ShapeDtypeStruct((B, T, D_MODEL), F32),
        jax.ShapeDtypeStruct((B, N_HEADS, HEAD_DIM, HEAD_DIM), F32),
        jax.ShapeDtypeStruct((B, N_HEADS, HEAD_DIM, HEAD_DIM), F32),
        jax.ShapeDtypeStruct((B, N_HEADS, HEAD_DIM), F32),
        jax.ShapeDtypeStruct((B, SUBLANES, LANES), F32),
        jax.ShapeDtypeStruct((B, CONV_W - 1, 2 * GROUP_W), F32),
    )
    out_specs = (
        pl.BlockSpec((1, bt, D_MODEL), lambda g: (cur(g) // nt, cur(g) % nt, 0)),
        pl.BlockSpec((1, N_HEADS, HEAD_DIM, HEAD_DIM), lambda g: (cur(g) // nt, 0, 0, 0)),
        pl.BlockSpec((1, N_HEADS, HEAD_DIM, HEAD_DIM), lambda g: (cur(g) // nt, 0, 0, 0)),
        pl.BlockSpec((1, N_HEADS, HEAD_DIM), lambda g: (cur(g) // nt, 0, 0)),
        pl.BlockSpec((1, SUBLANES, LANES), lambda g: (cur(g) // nt, 0, 0)),
        pl.BlockSpec((1, CONV_W - 1, 2 * GROUP_W), lambda g: (nxt(g) // nt, 0, 0)),
    )
    operand_set = [
        pltpu.VMEM((bt, 4 * GROUP_W), BF16),
        pltpu.VMEM((bt, 2 * GROUP_W), BF16),
        pltpu.VMEM((bt, 2 * GROUP_W), F32),
        pltpu.VMEM((bt, 2 * GROUP_W), F32),
        pltpu.VMEM((n_chunks * GT_PER_CHUNK, CHUNK, LANES), F32),
    ]
    scratch = operand_set + operand_set + [
        pltpu.VMEM((bt, D_MODEL), BF16),
        pltpu.VMEM((SUBLANES + bt, 2 * GROUP_W), F32),
        pltpu.VMEM((bt, 2 * GROUP_W), BF16),
        pltpu.VMEM((N_HEADS, HEAD_DIM, HEAD_DIM), F32),
        pltpu.VMEM((N_HEADS, HEAD_DIM, 2 * HEAD_DIM), F32),
        pltpu.VMEM((N_HEADS, SUBLANES, LANES), F32),
    ]
    y, s, c, n, m, cv = pl.pallas_call(
        functools.partial(_prompt_kernel, nt, n_blocks),
        grid=(n_blocks + 1,),
        in_specs=in_specs,
        out_specs=out_specs,
        out_shape=out_shape,
        scratch_shapes=scratch,
        compiler_params=pltpu.CompilerParams(
            dimension_semantics=("arbitrary",),
            vmem_limit_bytes=VMEM_LIMIT_BYTES),
        name="prompt_layer",
    )(x, x, w_in_bf, conv_w, conv_b, gbias, g_ret, g_m, w_out_bf, ln_g, ln_b,
      cos_t, sin_t, kcos_t, ksin_t, decay, qd, kd, cdec, triu, mask, e0)
    return y, s, c, n, m[:, :N_HEADS, 0], cv


def _sample_kernel(x_ref, w_in_ref, conv_w_ref, conv_b_ref, gbias_ref, g_ret_ref, g_m_ref,
                   w_out_ref, ln_g_ref, ln_b_ref, rot_ref, gam_ref, esel_ref,
                   s_ref, c_ref, n_ref, m_ref, cv_ref,
                   y_ref, s_out_ref, c_out_ref, n_out_ref, m_out_ref, cv_out_ref,
                   proj_scr, xcur_scr, mix_scr):
    layer = pl.program_id(0)
    j = pl.program_id(1)
    bb = s_ref.shape[1]

    @pl.when(jnp.logical_and(layer == 0, j == 0))
    def _load_x():
        xcur_scr[...] = x_ref[...]

    @pl.when(j == 0)
    def _project():
        xb = xcur_scr[...].astype(BF16)
        for lo in range(0, N_PAD, 512):
            hi = min(lo + 512, N_PAD)
            proj_scr[:, lo:hi] = _dot(xb, w_in_ref[0, :, lo:hi])

    r0 = pl.multiple_of(j * bb, bb)
    rows = pl.ds(r0, bb)
    cos_t = rot_ref[0:1, :]
    sin_t = rot_ref[1:2, :]
    kcos_t = rot_ref[2:3, :]
    ksin_t = rot_ref[3:4, :]
    esel = esel_ref[...]
    row_id = lax.broadcasted_iota(jnp.int32, (bb, bb * HEAD_DIM), 0)
    blk_id = jnp.right_shift(lax.broadcasted_iota(jnp.int32, (bb, bb * HEAD_DIM), 1), 7)
    pad_rows = jnp.zeros((LANES - bb, HEAD_DIM), F32)
    pad_wide = jnp.zeros((LANES - bb, bb * HEAD_DIM), BF16)

    def col_form(x8):
        return jnp.concatenate([x8, pad_rows], axis=0).T.astype(BF16)

    def outer_all(k8, v8):
        vt = jnp.concatenate([v8] * bb, axis=1)
        vsel = jnp.where(row_id == blk_id, vt, 0.0).astype(BF16)
        return _dot(col_form(k8), jnp.concatenate([vsel, pad_wide], axis=0))

    def col_bcast_all(q8):
        return _dot(col_form(q8), esel)

    for h in range(N_HEADS):
        hc = h * HEAD_DIM
        q8 = _rotary(proj_scr[rows, COL_RQ + hc:COL_RQ + hc + HEAD_DIM], cos_t, sin_t)
        k8 = _rotary(proj_scr[rows, COL_RK + hc:COL_RK + hc + HEAD_DIM], kcos_t, ksin_t)
        v8 = proj_scr[rows, COL_RV + hc:COL_RV + hc + HEAD_DIM]
        kv_all = outer_all(k8, v8)
        qc_all = col_bcast_all(q8)
        gamma = gam_ref[h]
        o_rows = []
        for r in range(bb):
            blk = slice(r * HEAD_DIM, (r + 1) * HEAD_DIM)
            s_new = s_ref[0, r, h] * gamma + kv_all[:, blk]
            s_out_ref[0, r, h] = s_new
            o_rows.append(jnp.sum(qc_all[:, blk] * s_new, axis=0, keepdims=True))
        o8 = _head_norm(jnp.concatenate(o_rows, axis=0), g_ret_ref[0, :, hc:hc + HEAD_DIM])
        z8 = proj_scr[rows, COL_RZ + hc:COL_RZ + hc + HEAD_DIM]
        mix_scr[rows, hc:hc + HEAD_DIM] = o8 * _silu(z8)

    u8 = proj_scr[rows, COL_MQK:COL_MQK + 2 * GROUP_W]
    acc = conv_b_ref[0] + u8 * conv_w_ref[0, CONV_W - 1:CONV_W, :]
    for jj in range(CONV_W - 1):
        acc = acc + cv_ref[0, jj] * conv_w_ref[0, jj:jj + 1, :]
    for jj in range(1, CONV_W - 1):
        cv_out_ref[0, jj - 1] = cv_ref[0, jj]
    cv_out_ref[0, CONV_W - 2] = u8
    qk8 = _silu(acc)

    gates = proj_scr[rows, COL_GATE:COL_GATE + LANES] + gbias_ref[0]
    i_al = pltpu.roll(gates, N_HEADS, axis=1)
    bm = _log_sigmoid(gates) + m_ref[0]
    m_new = jnp.maximum(bm, i_al)
    wk = jnp.exp(i_al - m_new)
    wc = jnp.exp(bm - m_new)
    einv = jnp.exp(-m_new)
    m_out_ref[0] = m_new
    for h in range(N_HEADS):
        hc = h * HEAD_DIM
        gl = N_HEADS + h
        q8 = qk8[:, hc:hc + HEAD_DIM]
        k8 = qk8[:, GROUP_W + hc:GROUP_W + hc + HEAD_DIM] * QK_SCALE
        v8 = proj_scr[rows, COL_MV + hc:COL_MV + hc + HEAD_DIM]
        wk_h = wk[:, gl:gl + 1]
        wc_h = jnp.broadcast_to(wc[:, gl:gl + 1], (bb, HEAD_DIM))
        kw8 = k8 * wk_h
        kv_all = outer_all(kw8, v8)
        qc_all = col_bcast_all(q8)
        n_new = n_ref[0, :, h, :] * wc_h + kw8
        n_out_ref[0, :, h, :] = n_new
        num_rows = []
        for r in range(bb):
            blk = slice(r * HEAD_DIM, (r + 1) * HEAD_DIM)
            c_new = c_ref[0, r, h] * wc_h[r:r + 1, :] + kv_all[:, blk]
            c_out_ref[0, r, h] = c_new
            num_rows.append(jnp.sum(qc_all[:, blk] * c_new, axis=0, keepdims=True))
        num = jnp.concatenate(num_rows, axis=0)
        q_bf = q8.astype(BF16).astype(F32)
        den = jnp.sum(q_bf * n_new, axis=-1, keepdims=True)
        hout = num / jnp.maximum(jnp.abs(den), einv[:, gl:gl + 1])
        hn = _head_norm(hout, g_m_ref[0, :, hc:hc + HEAD_DIM])
        og = _sigmoid(proj_scr[rows, COL_MO + hc:COL_MO + hc + HEAD_DIM])
        zg = _silu(proj_scr[rows, COL_MZ + hc:COL_MZ + hc + HEAD_DIM])
        mix_scr[rows, GROUP_W + hc:GROUP_W + hc + HEAD_DIM] = hn * og * zg

    @pl.when(j == pl.num_programs(1) - 1)
    def _finish_layer():
        mix = _dot(mix_scr[...].astype(BF16), w_out_ref[0])
        y = _layer_norm(ALPHA * xcur_scr[...] + mix, ln_g_ref[0], ln_b_ref[0])
        xcur_scr[...] = y
        y_ref[...] = y


def _sample_layers(x, w_in_bf, conv_w, conv_b, gbias, g_ret, g_m, w_out_bf, ln_g, ln_b,
                   rot, gam, esel, state_ret, state_c, state_n, m_pad, state_conv):
    bs = x.shape[0]
    bb = SAMPLE_BLOCK_B
    assert bs % bb == 0
    nb = bs // bb

    def lspec(shape):
        nd = len(shape)
        return pl.BlockSpec((1,) + tuple(shape[1:]), lambda l, j: (l,) + (0,) * (nd - 1))

    def cspec(shape):
        nd = len(shape)
        return pl.BlockSpec(tuple(shape), lambda l, j: (0,) * nd)

    mat_spec = pl.BlockSpec((1, bb, N_HEADS, HEAD_DIM, HEAD_DIM), lambda l, j: (l, j, 0, 0, 0))
    n_spec = pl.BlockSpec((1, bb, N_HEADS, HEAD_DIM), lambda l, j: (l, j, 0, 0))
    m_spec = pl.BlockSpec((1, bb, LANES), lambda l, j: (l, j, 0))
    cv_spec = pl.BlockSpec((1, CONV_W - 1, bb, 2 * GROUP_W), lambda l, j: (l, 0, j, 0))
    in_specs = [
        cspec(x.shape), lspec(w_in_bf.shape), lspec(conv_w.shape), lspec(conv_b.shape),
        lspec(gbias.shape), lspec(g_ret.shape), lspec(g_m.shape), lspec(w_out_bf.shape),
        lspec(ln_g.shape), lspec(ln_b.shape), cspec(rot.shape),
        pl.BlockSpec(memory_space=pltpu.SMEM), cspec(esel.shape),
        mat_spec, mat_spec, n_spec, m_spec, cv_spec,
    ]
    out_shape = (
        jax.ShapeDtypeStruct(x.shape, F32),
        jax.ShapeDtypeStruct(state_ret.shape, F32),
        jax.ShapeDtypeStruct(state_c.shape, F32),
        jax.ShapeDtypeStruct(state_n.shape, F32),
        jax.ShapeDtypeStruct(m_pad.shape, F32),
        jax.ShapeDtypeStruct(state_conv.shape, F32),
    )
    out_specs = (cspec(x.shape), mat_spec, mat_spec, n_spec, m_spec, cv_spec)
    scratch = [
        pltpu.VMEM((bs, N_PAD), F32),
        pltpu.VMEM((bs, D_MODEL), F32),
        pltpu.VMEM((bs, 2 * GROUP_W), F32),
    ]
    return pl.pallas_call(
        _sample_kernel,
        grid=(DEPTH, nb),
        in_specs=in_specs,
        out_specs=out_specs,
        out_shape=out_shape,
        scratch_shapes=scratch,
        compiler_params=pltpu.CompilerParams(
            dimension_semantics=("arbitrary", "arbitrary"),
            vmem_limit_bytes=VMEM_LIMIT_BYTES),
        name="sample_layers",
    )(x, w_in_bf, conv_w, conv_b, gbias, g_ret, g_m, w_out_bf, ln_g, ln_b,
      rot, gam, esel, state_ret, state_c, state_n, m_pad, state_conv)


def _cast_weight_kernel(w_ref, o_ref):
    o_ref[0] = w_ref[0].astype(BF16)


def _cast_weight(w):
    depth, k, n = w.shape
    rows = WEIGHT_CAST_ROWS
    assert k % rows == 0
    return pl.pallas_call(
        _cast_weight_kernel,
        grid=(depth, k // rows),
        in_specs=[pl.BlockSpec((1, rows, n), lambda l, r: (l, r, 0))],
        out_specs=pl.BlockSpec((1, rows, n), lambda l, r: (l, r, 0)),
        out_shape=jax.ShapeDtypeStruct((depth, k, n), BF16),
        compiler_params=pltpu.CompilerParams(dimension_semantics=("parallel", "parallel")),
        name="cast_weight",
    )(w)


def _cast_transposed_weight_kernel(n_valid, wt_ref, o_ref):
    rows = wt_ref.shape[1]
    row_id = pl.program_id(1) * rows + lax.broadcasted_iota(jnp.int32, wt_ref.shape[1:], 0)
    wt = jnp.where(row_id < n_valid, wt_ref[0], 0.0)
    o_ref[0] = wt.T.astype(BF16)


def _cast_transposed_weight(wt, n_out):
    depth, n, k = wt.shape
    rows = WEIGHT_CAST_ROWS
    assert n_out % LANES == 0 and 0 <= n_out - n < LANES
    return pl.pallas_call(
        functools.partial(_cast_transposed_weight_kernel, n),
        grid=(depth, pl.cdiv(n_out, rows)),
        in_specs=[pl.BlockSpec((1, rows, k), lambda l, r: (l, r, 0))],
        out_specs=pl.BlockSpec((1, k, rows), lambda l, r: (l, 0, r)),
        out_shape=jax.ShapeDtypeStruct((depth, k, n_out), BF16),
        compiler_params=pltpu.CompilerParams(dimension_semantics=("parallel", "parallel")),
        name="cast_weight_t",
    )(wt)


def _rotary_tables(pos):
    half = HEAD_DIM // 2
    inv = np.float32(ROPE_BASE) ** (-np.arange(half, dtype=np.float32) / np.float32(half))
    ang = pos.astype(np.float32)[:, None] * inv[None, :].astype(np.float32)
    cos = np.cos(ang.astype(np.float64)).astype(np.float32)
    sin = np.sin(ang.astype(np.float64)).astype(np.float32)
    cos_t = np.concatenate([cos, cos], axis=-1)
    sin_t = np.concatenate([-sin, sin], axis=-1)
    scale = np.float32(QK_SCALE)
    return cos_t, sin_t, cos_t * scale, sin_t * scale


def _retention_tables():
    L = CHUNK
    f32 = np.float32
    log_gamma = np.log(f32(1.0) - f32(2.0) ** (f32(-5.0) - np.arange(N_HEADS, dtype=f32))).astype(f32)
    idx = np.arange(L, dtype=f32)
    diff = idx[:, None] - idx[None, :]
    decay = (np.exp(log_gamma[:, None, None] * np.maximum(diff, f32(0.0))) * (diff >= 0)).astype(f32)
    q_decay = np.exp(log_gamma[:, None] * (idx + f32(1.0))).astype(f32)
    k_decay = np.exp(log_gamma[:, None] * (f32(L - 1.0) - idx)).astype(f32)
    c_decay = np.exp(log_gamma * f32(L)).astype(f32)
    qd = np.ascontiguousarray(np.broadcast_to(q_decay[:, :, None], (N_HEADS, L, HEAD_DIM)))
    kd = np.ascontiguousarray(np.broadcast_to(k_decay[:, :, None], (N_HEADS, L, HEAD_DIM)))
    gamma1 = np.exp(log_gamma).astype(f32)
    return decay, qd, kd, c_decay, gamma1


def kernel(x_prompt, x_sample, state_ret, state_mlstm_C, state_mlstm_n, state_mlstm_m, state_conv,
           w_in, conv_w, conv_b, b_i, b_f, g_ret, g_m, w_out, ln_g, ln_b):
    B, T, _ = x_prompt.shape
    Bs, Ts, _ = x_sample.shape
    assert Ts == 1

    w_in_bf = _cast_transposed_weight(jnp.swapaxes(w_in, 1, 2), N_PAD)
    w_out_bf = _cast_weight(w_out)
    gbias = jnp.pad(jnp.concatenate([b_i, b_f], axis=-1), ((0, 0), (0, LANES - 2 * N_HEADS)))
    gbias = gbias.reshape(DEPTH, 1, LANES)
    conv_b3 = conv_b.reshape(DEPTH, 1, 2 * GROUP_W)
    g_ret3 = g_ret.reshape(DEPTH, 1, GROUP_W)
    g_m3 = g_m.reshape(DEPTH, 1, GROUP_W)
    ln_g3 = ln_g.reshape(DEPTH, 1, D_MODEL)
    ln_b3 = ln_b.reshape(DEPTH, 1, D_MODEL)

    decay, qd, kd, c_decay, gamma1 = _retention_tables()
    idx = np.arange(CHUNK)
    causal = idx[:, None] >= idx[None, :]
    triu = jnp.asarray(causal.T, BF16)
    mask_add = np.where(causal, 0.0, -np.inf).astype(np.float32)
    e0 = jnp.asarray(np.broadcast_to(np.arange(LANES)[None, :] == 0, (CHUNK, LANES)), BF16)
    tabs_p = _rotary_tables(np.arange(T)) + (decay, qd, kd, c_decay, triu, mask_add, e0)
    tabs_p = tuple(jnp.asarray(a) for a in tabs_p)

    xp = x_prompt
    rp, cp, np_, mp, vp = [], [], [], [], []
    for l in range(DEPTH):
        xp, s, c, n, m, cv = _prompt_layer(
            l, xp, w_in_bf, conv_w[l], conv_b3[l], gbias[l], g_ret3[l], g_m3[l],
            w_out_bf, ln_g3[l], ln_b3[l], tabs_p)
        rp.append(s); cp.append(c); np_.append(n); mp.append(m); vp.append(cv)

    rot = jnp.asarray(np.concatenate(_rotary_tables(PAST_LEN + np.arange(Ts)), axis=0))
    bb = SAMPLE_BLOCK_B
    esel = jnp.asarray(np.arange(LANES)[:, None] == (np.arange(bb * HEAD_DIM)[None, :] // HEAD_DIM), BF16)
    gamma1 = jnp.asarray(gamma1)
    m_pad = jnp.pad(state_mlstm_m, ((0, 0), (0, 0), (N_HEADS, LANES - 2 * N_HEADS)))
    ys, rs, cs, ns, ms_pad, vs = _sample_layers(
        x_sample.reshape(Bs, D_MODEL), w_in_bf, conv_w, conv_b3, gbias, g_ret3, g_m3, w_out_bf,
        ln_g3, ln_b3, rot, gamma1, esel, state_ret, state_mlstm_C, state_mlstm_n, m_pad,
        jnp.swapaxes(state_conv, 1, 2))
    vs = jnp.swapaxes(vs, 1, 2)
    ms = ms_pad[:, :, N_HEADS:2 * N_HEADS]

    return (xp, ys.reshape(Bs, Ts, D_MODEL),
            jnp.stack(rp), jnp.stack(cp), jnp.stack(np_), jnp.stack(mp), jnp.stack(vp),
            rs, cs, ns, ms, vs)
```

```python
import functools

import jax
import jax.numpy as jnp
import numpy as np
from jax import lax
from jax.experimental import pallas as pl
from jax.experimental.pallas import tpu as pltpu

F32 = jnp.float32
BF16 = jnp.bfloat16

D_MODEL = 1024
DEPTH = 2
PAST_LEN = 16384
N_HEADS = 4
HEAD_DIM = 128
GROUP_W = N_HEADS * HEAD_DIM
CONV_W = 4
CHUNK = 128
ROPE_BASE = 10000.0
LN_EPS = 1e-5
GN_EPS = 1e-5
ALPHA = (2 * DEPTH) ** 0.25
QK_SCALE = HEAD_DIM ** -0.5

LANES = 128
SUBLANES = 8

COL_RQ = 0
COL_RK = COL_RQ + GROUP_W
COL_RV = COL_RK + GROUP_W
COL_RZ = COL_RV + GROUP_W
COL_MQK = COL_RZ + GROUP_W
COL_MV = COL_MQK + 2 * GROUP_W
COL_MO = COL_MV + GROUP_W
COL_MZ = COL_MO + GROUP_W
COL_GATE = COL_MZ + GROUP_W
N_IN = COL_GATE + 2 * N_HEADS
N_PAD = COL_GATE + LANES

PROMPT_BLOCK_T = 256
PROJ_PIECE_COLS = 256
PROJ_PIECE_ROWS = 256
SAMPLE_BLOCK_B = 16
WEIGHT_CAST_ROWS = 512
VMEM_LIMIT_BYTES = 56 * 1024 * 1024


def _sigmoid(x):
    return 1.0 / (1.0 + jnp.exp(-x))


def _silu(x):
    return x * _sigmoid(x)


def _log_sigmoid(x):
    return jnp.minimum(x, 0.0) - jnp.log1p(jnp.exp(-jnp.abs(x)))


def _dot(a, b):
    return jnp.dot(a, b, preferred_element_type=F32)


def _dot_nt(a, b):
    return lax.dot_general(a, b, (((1,), (1,)), ((), ())), preferred_element_type=F32)


def _dot_tn(a, b):
    return lax.dot_general(a, b, (((0,), (0,)), ((), ())), preferred_element_type=F32)


def _rotary(x, cos_t, sin_t):
    return x * cos_t + pltpu.roll(x, HEAD_DIM // 2, axis=1) * sin_t


def _head_norm(h, g):
    mu = jnp.mean(h, axis=-1, keepdims=True)
    d = h - mu
    var = jnp.mean(d * d, axis=-1, keepdims=True)
    return d * lax.rsqrt(var + GN_EPS) * g


def _layer_norm(x, g, b):
    mu = jnp.mean(x, axis=-1, keepdims=True)
    d = x - mu
    var = jnp.mean(d * d, axis=-1, keepdims=True)
    return d * lax.rsqrt(var + LN_EPS) * g + b


def _cumsum_lanes(triu_bf, x):
    hi = x.astype(BF16)
    r1 = x - hi.astype(F32)
    mid = r1.astype(BF16)
    lo = (r1 - mid.astype(F32)).astype(BF16)
    return _dot(hi, triu_bf) + _dot(mid, triu_bf) + _dot(lo, triu_bf)


RQK_Q, RQK_QDEC, RQK_K, RQK_KDEC = 0, GROUP_W, 2 * GROUP_W, 3 * GROUP_W
VV_RET, VV_M = 0, GROUP_W
GZ_RET, GZ_M = 0, GROUP_W
GT_COLS, GT_ROWS, GT_PER_CHUNK = 0, 1, 2
GT_ROWS_USED = SUBLANES
EMITS_PER_CHUNK = 9 * N_HEADS + 1
TAIL_PIECES = 0


def _prompt_kernel(nt, n_blocks, xn_ref, x_ref, w_in_ref, conv_w_ref, conv_b_ref, gbias_ref,
                   g_ret_ref, g_m_ref, w_out_ref, ln_g_ref, ln_b_ref,
                   cos_ref, sin_ref, kcos_ref, ksin_ref,
                   decay_ref, qd_ref, kd_ref, cdec_ref, triu_ref, mask_ref, e0_ref,
                   y_ref, s_out_ref, c_out_ref, n_out_ref, m_out_ref, conv_out_ref,
                   rqk_a, vv_a, gz_a, qkm_a, gt_a, rqk_b, vv_b, gz_b, qkm_b, gt_b,
                   xb_scr, u_scr, mix_scr, s_scr, caug_scr, m_scr):
    g = pl.program_id(0)
    t = lax.rem(jnp.maximum(g - 1, 0), nt)
    tn = lax.rem(jnp.minimum(g, n_blocks - 1), nt)
    block_t = x_ref.shape[1]
    n_chunks = block_t // CHUNK
    carry_rows = CONV_W - 1
    heads = range(N_HEADS)

    @pl.when(g == 0)
    def _init_pipeline():
        for ref in (rqk_b, vv_b, gz_b, qkm_b, gt_b):
            ref[...] = jnp.zeros_like(ref)

    @pl.when(t == 0)
    def _init_state():
        s_scr[...] = jnp.zeros_like(s_scr)
        caug_scr[...] = jnp.zeros_like(caug_scr)
        m_scr[...] = jnp.zeros_like(m_scr)

    @pl.when(tn == 0)
    def _init_conv_carry():
        u_scr[0:SUBLANES, :] = jnp.zeros((SUBLANES, 2 * GROUP_W), F32)

    def hcols(base, h):
        return slice(base + h * HEAD_DIM, base + (h + 1) * HEAD_DIM)

    def step_body(set_in, set_cur):
        rqk_i, vv_i, gz_i, qkm_i, gt_i = set_in
        rqk_c, vv_c, gz_c, qkm_c, gt_c = set_cur
        triu_bf = triu_ref[...]
        causal_add = mask_ref[...]
        ones_col = e0_ref[...]
        sub_id = lax.broadcasted_iota(jnp.int32, (SUBLANES, LANES), 0)
        pad_rows = jnp.zeros((CHUNK - GT_ROWS_USED, LANES), F32)
        xb_scr[...] = xn_ref[0].astype(BF16)
        pw = PROJ_PIECE_COLS
        heads_per_piece = pw // HEAD_DIM

        pr = PROJ_PIECE_ROWS
        chunks_per_piece = pr // CHUNK

        def group_rows(p):
            return slice(p * pr, (p + 1) * pr)

        def proj(p, col, width=pw):
            return _dot(xb_scr[group_rows(p), :], w_in_ref[:, col:col + width])

        def piece_rot(p, i, col_base, cos_r, sin_r, dec_ref, dst, dst_dec):
            res = proj(p, col_base + i * pw)
            for hh in range(heads_per_piece):
                h = i * heads_per_piece + hh
                for cc in range(chunks_per_piece):
                    rows = slice(p * pr + cc * CHUNK, p * pr + (cc + 1) * CHUNK)
                    r = _rotary(res[cc * CHUNK:(cc + 1) * CHUNK, hh * HEAD_DIM:(hh + 1) * HEAD_DIM],
                                cos_r[rows, :], sin_r[rows, :])
                    rqk_i[rows, hcols(dst, h)] = r.astype(BF16)
                    rqk_i[rows, hcols(dst_dec, h)] = (r * dec_ref[h]).astype(BF16)

        def piece_cast(p, i, col_base, dst):
            vv_i[group_rows(p), dst + i * pw:dst + (i + 1) * pw] = proj(p, col_base + i * pw).astype(BF16)

        def piece_rz(p, i):
            gz_i[group_rows(p), GZ_RET + i * pw:GZ_RET + (i + 1) * pw] = _silu(proj(p, COL_RZ + i * pw))

        def piece_moz(p, i):
            og = _sigmoid(proj(p, COL_MO + i * pw))
            gz_i[group_rows(p), GZ_M + i * pw:GZ_M + (i + 1) * pw] = og * _silu(proj(p, COL_MZ + i * pw))

        def piece_conv(p, i):
            base = SUBLANES + p * pr
            u_scr[base:base + pr, i * pw:(i + 1) * pw] = proj(p, COL_MQK + i * pw)
            for cs in range(i * pw, (i + 1) * pw, LANES):
                cols = slice(cs, cs + LANES)
                acc = conv_b_ref[:, cols]
                for j in range(CONV_W):
                    r0 = base - carry_rows + j
                    acc = acc + u_scr[r0:r0 + pr, cols] * conv_w_ref[j:j + 1, cols]
                act = _silu(acc)
                if cs >= GROUP_W:
                    act = act * QK_SCALE
                qkm_i[group_rows(p), cols] = act

        def piece_gate(p):
            res = proj(p, COL_GATE, LANES)
            for cc in range(chunks_per_piece):
                c = p * chunks_per_piece + cc
                gates = res[cc * CHUNK:(cc + 1) * CHUNK, :] + gbias_ref[...]
                g8 = gates.T[0:SUBLANES, :]
                rows8 = jnp.where(sub_id < N_HEADS, g8, _cumsum_lanes(triu_bf, _log_sigmoid(g8)))
                gt_i[c * GT_PER_CHUNK + GT_ROWS, 0:GT_ROWS_USED, :] = rows8
                gt_i[c * GT_PER_CHUNK + GT_COLS] = jnp.concatenate([rows8, pad_rows], axis=0).T

        n_col_pieces = GROUP_W // pw
        pieces = []
        for p in range(block_t // pr):
            pieces.append(functools.partial(piece_gate, p))
            for i in range(n_col_pieces):
                pieces.append(functools.partial(piece_conv, p, 2 * i))
                pieces.append(functools.partial(piece_cast, p, i, COL_RV, VV_RET))
                pieces.append(functools.partial(piece_rot, p, i, COL_RQ, cos_ref, sin_ref, qd_ref,
                                                RQK_Q, RQK_QDEC))
                pieces.append(functools.partial(piece_rz, p, i))
                pieces.append(functools.partial(piece_conv, p, 2 * i + 1))
                pieces.append(functools.partial(piece_cast, p, i, COL_MV, VV_M))
                pieces.append(functools.partial(piece_rot, p, i, COL_RK, kcos_ref, ksin_ref, kd_ref,
                                                RQK_K, RQK_KDEC))
                pieces.append(functools.partial(piece_moz, p, i))
        n_pieces = len(pieces)
        n_slots = EMITS_PER_CHUNK * n_chunks
        slots_done = [0]

        def emit_proj_pieces():
            slots_done[0] += 1
            target = -(-(slots_done[0] * (n_pieces - TAIL_PIECES)) // n_slots)
            while n_pieces - len(pieces) < target:
                pieces.pop(0)()

        def per_head(fn):
            out = []
            for h in heads:
                emit_proj_pieces()
                out.append(fn(h))
            return out

        for c in range(n_chunks):
            rows = slice(c * CHUNK, (c + 1) * CHUNK)

            q_bf = [rqk_c[rows, hcols(RQK_Q, h)] for h in heads]
            k_bf = [rqk_c[rows, hcols(RQK_K, h)] for h in heads]
            v_bf = [vv_c[rows, hcols(VV_RET, h)] for h in heads]
            sc = per_head(lambda h: _dot_nt(q_bf[h], k_bf[h]))
            state = [s_scr[h] for h in heads]
            upd = per_head(lambda h: _dot_tn(rqk_c[rows, hcols(RQK_KDEC, h)], v_bf[h]))
            for h in heads:
                s_scr[h] = state[h] * cdec_ref[h] + upd[h]
            o = per_head(lambda h: _dot(
                jnp.concatenate([(sc[h] * decay_ref[h]).astype(BF16),
                                 rqk_c[rows, hcols(RQK_QDEC, h)]], axis=1),
                jnp.concatenate([v_bf[h], state[h].astype(BF16)], axis=0)))

            def ret_out(h):
                hn = _head_norm(o[h], g_ret_ref[:, hcols(0, h)])
                mix_scr[rows, hcols(0, h)] = (hn * gz_c[rows, hcols(GZ_RET, h)]).astype(BF16)

            per_head(ret_out)

            cols_t = gt_c[c * GT_PER_CHUNK + GT_COLS]
            rows_t = gt_c[c * GT_PER_CHUNK + GT_ROWS, 0:GT_ROWS_USED, :]
            q = [qkm_c[rows, hcols(0, h)] for h in heads]
            k = [qkm_c[rows, hcols(GROUP_W, h)] for h in heads]
            vaug_bf = [jnp.concatenate([vv_c[rows, hcols(VV_M, h)], ones_col], axis=1)
                       for h in heads]
            qk = per_head(lambda h: _dot_nt(q[h].astype(BF16), k[h].astype(BF16)))
            caug = [caug_scr[h] for h in heads]
            m_old = [m_scr[h, 0:1, 0:1] for h in heads]
            i_row = [rows_t[h:h + 1, :] for h in heads]
            b_row = [rows_t[N_HEADS + h:N_HEADS + h + 1, :] for h in heads]
            i_col = [cols_t[:, h:h + 1] for h in heads]
            b_col = [cols_t[:, N_HEADS + h:N_HEADS + h + 1] for h in heads]
            a_col = [b_col[h] + m_old[h] for h in heads]
            dm = per_head(lambda h: (b_col[h] + (i_row[h] - b_row[h])) + causal_add)
            mt = [jnp.maximum(a_col[h], jnp.max(dm[h], axis=-1, keepdims=True)) for h in heads]
            w_inter = [jnp.exp(a_col[h] - mt[h]) for h in heads]
            res = per_head(lambda h: _dot(
                jnp.concatenate([(qk[h] * jnp.exp(dm[h] - mt[h])).astype(BF16),
                                 (q[h] * w_inter[h]).astype(BF16)], axis=1),
                jnp.concatenate([vaug_bf[h], caug[h].astype(BF16)], axis=0)))
            b_last = [b_col[h][CHUNK - 1:CHUNK, :] for h in heads]
            g_col = [b_last[h] - b_col[h] + i_col[h] for h in heads]
            m_new = [jnp.maximum(b_last[h] + m_old[h], jnp.max(g_col[h], axis=0, keepdims=True))
                     for h in heads]
            wk = [jnp.exp(g_col[h] - m_new[h]) for h in heads]
            wc = [jnp.exp(b_last[h] + m_old[h] - m_new[h]) for h in heads]

            def mlstm_update(h):
                caug_scr[h] = caug[h] * wc[h] + _dot_tn((k[h] * wk[h]).astype(BF16), vaug_bf[h])
                m_scr[h] = jnp.broadcast_to(m_new[h], (SUBLANES, LANES))

            per_head(mlstm_update)

            def mlstm_out(h):
                den = jnp.maximum(jnp.abs(res[h][:, HEAD_DIM:HEAD_DIM + 1]), jnp.exp(-mt[h]))
                hn = _head_norm(res[h][:, :HEAD_DIM] / den, g_m_ref[:, hcols(0, h)])
                mix_scr[rows, hcols(GROUP_W, h)] = (hn * gz_c[rows, hcols(GZ_M, h)]).astype(BF16)

            per_head(mlstm_out)

            emit_proj_pieces()
            mix = _dot(mix_scr[rows, :], w_out_ref[...])
            y_ref[0, rows, :] = _layer_norm(ALPHA * x_ref[0, rows, :] + mix, ln_g_ref[...], ln_b_ref[...])

        assert slots_done[0] == n_slots and len(pieces) == TAIL_PIECES
        while pieces:
            pieces.pop(0)()

    set_a = (rqk_a, vv_a, gz_a, qkm_a, gt_a)
    set_b = (rqk_b, vv_b, gz_b, qkm_b, gt_b)
    parity = lax.rem(g, 2)

    @pl.when(parity == 0)
    def _even_step():
        step_body(set_a, set_b)

    @pl.when(parity == 1)
    def _odd_step():
        step_body(set_b, set_a)

    @pl.when(jnp.logical_and(tn == nt - 1, g < n_blocks))
    def _write_conv_state():
        conv_out_ref[0] = u_scr[SUBLANES + block_t - carry_rows:SUBLANES + block_t, :]

    @pl.when(jnp.logical_and(t == nt - 1, g > 0))
    def _write_state():
        s_out_ref[0] = s_scr[...]
        m_out_ref[0] = jnp.zeros((SUBLANES, LANES), F32)
        for h in heads:
            caug = caug_scr[h]
            c_out_ref[0, h] = caug[:, :HEAD_DIM]
            n_out_ref[0, h:h + 1, :] = caug[:, HEAD_DIM:].T[0:1, :]
            m_out_ref[0, h:h + 1, :] = m_scr[h, 0:1, :]

    u_scr[0:SUBLANES, :] = u_scr[block_t:block_t + SUBLANES, :]


def _const_spec(shape):
    nd = len(shape)
    return pl.BlockSpec(shape, lambda g: (0,) * nd)


def _layer_spec(shape, layer):
    return pl.BlockSpec((None,) + tuple(shape[1:]), lambda g: (layer,) + (0,) * (len(shape) - 1))


def _prompt_layer(layer, x, w_in_bf, conv_w, conv_b, gbias, g_ret, g_m, w_out_bf, ln_g, ln_b, tabs):
    B, T, _ = x.shape
    bt = PROMPT_BLOCK_T
    assert T % bt == 0 and bt % CHUNK == 0 and GROUP_W % PROJ_PIECE_COLS == 0
    nt = T // bt
    n_blocks = B * nt
    n_chunks = bt // CHUNK
    cos_t, sin_t, kcos_t, ksin_t, decay, qd, kd, cdec, triu, mask, e0 = tabs

    def nxt(g):
        return jnp.minimum(g, n_blocks - 1)

    def cur(g):
        return jnp.maximum(g - 1, 0)

    row_spec = pl.BlockSpec((bt, LANES), lambda g: (nxt(g) % nt, 0))
    in_specs = [
        pl.BlockSpec((1, bt, D_MODEL), lambda g: (nxt(g) // nt, nxt(g) % nt, 0)),
        pl.BlockSpec((1, bt, D_MODEL), lambda g: (cur(g) // nt, cur(g) % nt, 0)),
        _layer_spec(w_in_bf.shape, layer), _const_spec(conv_w.shape), _const_spec(conv_b.shape),
        _const_spec(gbias.shape), _const_spec(g_ret.shape), _const_spec(g_m.shape),
        _layer_spec(w_out_bf.shape, layer), _const_spec(ln_g.shape), _const_spec(ln_b.shape),
        row_spec, row_spec, row_spec, row_spec,
        _const_spec(decay.shape), _const_spec(qd.shape), _const_spec(kd.shape),
        pl.BlockSpec(memory_space=pltpu.SMEM),
        _const_spec(triu.shape), _const_spec(mask.shape), _const_spec(e0.shape),
    ]
    out_shape = (
        jax.ShapeDtypeStruct((B, T, D_MODEL), F32),
        jax.ShapeDtypeStruct((B, N_HEADS, HEAD_DIM, HEAD_DIM), F32),
        jax.ShapeDtypeStruct((B, N_HEADS, HEAD_DIM, HEAD_DIM), F32),
        jax.ShapeDtypeStruct((B, N_HEADS, HEAD_DIM), F32),
        jax.ShapeDtypeStruct((B, SUBLANES, LANES), F32),
        jax.ShapeDtypeStruct((B, CONV_W - 1, 2 * GROUP_W), F32),
    )
    out_specs = (
        pl.BlockSpec((1, bt, D_MODEL), lambda g: (cur(g) // nt, cur(g) % nt, 0)),
        pl.BlockSpec((1, N_HEADS, HEAD_DIM, HEAD_DIM), lambda g: (cur(g) // nt, 0, 0, 0)),
        pl.BlockSpec((1, N_HEADS, HEAD_DIM, HEAD_DIM), lambda g: (cur(g) // nt, 0, 0, 0)),
        pl.BlockSpec((1, N_HEADS, HEAD_DIM), lambda g: (cur(g) // nt, 0, 0)),
        pl.BlockSpec((1, SUBLANES, LANES), lambda g: (cur(g) // nt, 0, 0)),
        pl.BlockSpec((1, CONV_W - 1, 2 * GROUP_W), lambda g: (nxt(g) // nt, 0, 0)),
    )
    operand_set = [
        pltpu.VMEM((bt, 4 * GROUP_W), BF16),
        pltpu.VMEM((bt, 2 * GROUP_W), BF16),
        pltpu.VMEM((bt, 2 * GROUP_W), F32),
        pltpu.VMEM((bt, 2 * GROUP_W), F32),
        pltpu.VMEM((n_chunks * GT_PER_CHUNK, CHUNK, LANES), F32),
    ]
    scratch = operand_set + operand_set + [
        pltpu.VMEM((bt, D_MODEL), BF16),
        pltpu.VMEM((SUBLANES + bt, 2 * GROUP_W), F32),
        pltpu.VMEM((bt, 2 * GROUP_W), BF16),
        pltpu.VMEM((N_HEADS, HEAD_DIM, HEAD_DIM), F32),
        pltpu.VMEM((N_HEADS, HEAD_DIM, 2 * HEAD_DIM), F32),
        pltpu.VMEM((N_HEADS, SUBLANES, LANES), F32),
    ]
    y, s, c, n, m, cv = pl.pallas_call(
        functools.partial(_prompt_kernel, nt, n_blocks),
        grid=(n_blocks + 1,),
        in_specs=in_specs,
        out_specs=out_specs,
        out_shape=out_shape,
        scratch_shapes=scratch,
        compiler_params=pltpu.CompilerParams(
            dimension_semantics=("arbitrary",),
            vmem_limit_bytes=VMEM_LIMIT_BYTES),
        name="prompt_layer",
    )(x, x, w_in_bf, conv_w, conv_b, gbias, g_ret, g_m, w_out_bf, ln_g, ln_b,
      cos_t, sin_t, kcos_t, ksin_t, decay, qd, kd, cdec, triu, mask, e0)
    return y, s, c, n, m[:, :N_HEADS, 0], cv


def _sample_kernel(x_ref, w_in_ref, conv_w_ref, conv_b_ref, gbias_ref, g_ret_ref, g_m_ref,
                   w_out_ref, ln_g_ref, ln_b_ref, rot_ref, gam_ref, esel_ref,
                   s_ref, c_ref, n_ref, m_ref, cv_ref,
                   y_ref, s_out_ref, c_out_ref, n_out_ref, m_out_ref, cv_out_ref,
                   proj_scr, xcur_scr, mix_scr):
    layer = pl.program_id(0)
    j = pl.program_id(1)
    bb = s_ref.shape[1]

    @pl.when(jnp.logical_and(layer == 0, j == 0))
    def _load_x():
        xcur_scr[...] = x_ref[...]

    @pl.when(j == 0)
    def _project():
        xb = xcur_scr[...].astype(BF16)
        for lo in range(0, N_PAD, 512):
            hi = min(lo + 512, N_PAD)
            proj_scr[:, lo:hi] = _dot(xb, w_in_ref[0, :, lo:hi])

    r0 = pl.multiple_of(j * bb, bb)
    rows = pl.ds(r0, bb)
    cos_t = rot_ref[0:1, :]
    sin_t = rot_ref[1:2, :]
    kcos_t = rot_ref[2:3, :]
    ksin_t = rot_ref[3:4, :]
    esel = esel_ref[...]
    row_id = lax.broadcasted_iota(jnp.int32, (bb, bb * HEAD_DIM), 0)
    blk_id = jnp.right_shift(lax.broadcasted_iota(jnp.int32, (bb, bb * HEAD_DIM), 1), 7)
    pad_rows = jnp.zeros((LANES - bb, HEAD_DIM), F32)
    pad_wide = jnp.zeros((LANES - bb, bb * HEAD_DIM), BF16)

    def col_form(x8):
        return jnp.concatenate([x8, pad_rows], axis=0).T.astype(BF16)

    def outer_all(k8, v8):
        vt = jnp.concatenate([v8] * bb, axis=1)
        vsel = jnp.where(row_id == blk_id, vt, 0.0).astype(BF16)
        return _dot(col_form(k8), jnp.concatenate([vsel, pad_wide], axis=0))

    def col_bcast_all(q8):
        return _dot(col_form(q8), esel)

    for h in range(N_HEADS):
        hc = h * HEAD_DIM
        q8 = _rotary(proj_scr[rows, COL_RQ + hc:COL_RQ + hc + HEAD_DIM], cos_t, sin_t)
        k8 = _rotary(proj_scr[rows, COL_RK + hc:COL_RK + hc + HEAD_DIM], kcos_t, ksin_t)
        v8 = proj_scr[rows, COL_RV + hc:COL_RV + hc + HEAD_DIM]
        kv_all = outer_all(k8, v8)
        qc_all = col_bcast_all(q8)
        gamma = gam_ref[h]
        o_rows = []
        for r in range(bb):
            blk = slice(r * HEAD_DIM, (r + 1) * HEAD_DIM)
            s_new = s_ref[0, r, h] * gamma + kv_all[:, blk]
            s_out_ref[0, r, h] = s_new
            o_rows.append(jnp.sum(qc_all[:, blk] * s_new, axis=0, keepdims=True))
        o8 = _head_norm(jnp.concatenate(o_rows, axis=0), g_ret_ref[0, :, hc:hc + HEAD_DIM])
        z8 = proj_scr[rows, COL_RZ + hc:COL_RZ + hc + HEAD_DIM]
        mix_scr[rows, hc:hc + HEAD_DIM] = o8 * _silu(z8)

    u8 = proj_scr[rows, COL_MQK:COL_MQK + 2 * GROUP_W]
    acc = conv_b_ref[0] + u8 * conv_w_ref[0, CONV_W - 1:CONV_W, :]
    for jj in range(CONV_W - 1):
        acc = acc + cv_ref[0, jj] * conv_w_ref[0, jj:jj + 1, :]
    for jj in range(1, CONV_W - 1):
        cv_out_ref[0, jj - 1] = cv_ref[0, jj]
    cv_out_ref[0, CONV_W - 2] = u8
    qk8 = _silu(acc)

    gates = proj_scr[rows, COL_GATE:COL_GATE + LANES] + gbias_ref[0]
    i_al = pltpu.roll(gates, N_HEADS, axis=1)
    bm = _log_sigmoid(gates) + m_ref[0]
    m_new = jnp.maximum(bm, i_al)
    wk = jnp.exp(i_al - m_new)
    wc = jnp.exp(bm - m_new)
    einv = jnp.exp(-m_new)
    m_out_ref[0] = m_new
    for h in range(N_HEADS):
        hc = h * HEAD_DIM
        gl = N_HEADS + h
        q8 = qk8[:, hc:hc + HEAD_DIM]
        k8 = qk8[:, GROUP_W + hc:GROUP_W + hc + HEAD_DIM] * QK_SCALE
        v8 = proj_scr[rows, COL_MV + hc:COL_MV + hc + HEAD_DIM]
        wk_h = wk[:, gl:gl + 1]
        wc_h = jnp.broadcast_to(wc[:, gl:gl + 1], (bb, HEAD_DIM))
        kw8 = k8 * wk_h
        kv_all = outer_all(kw8, v8)
        qc_all = col_bcast_all(q8)
        n_new = n_ref[0, :, h, :] * wc_h + kw8
        n_out_ref[0, :, h, :] = n_new
        num_rows = []
        for r in range(bb):
            blk = slice(r * HEAD_DIM, (r + 1) * HEAD_DIM)
            c_new = c_ref[0, r, h] * wc_h[r:r + 1, :] + kv_all[:, blk]
            c_out_ref[0, r, h] = c_new
            num_rows.append(jnp.sum(qc_all[:, blk] * c_new, axis=0, keepdims=True))
        num = jnp.concatenate(num_rows, axis=0)
        q_bf = q8.astype(BF16).astype(F32)
        den = jnp.sum(q_bf * n_new, axis=-1, keepdims=True)
        hout = num / jnp.maximum(jnp.abs(den), einv[:, gl:gl + 1])
        hn = _head_norm(hout, g_m_ref[0, :, hc:hc + HEAD_DIM])
        og = _sigmoid(proj_scr[rows, COL_MO + hc:COL_MO + hc + HEAD_DIM])
        zg = _silu(proj_scr[rows, COL_MZ + hc:COL_MZ + hc + HEAD_DIM])
        mix_scr[rows, GROUP_W + hc:GROUP_W + hc + HEAD_DIM] = hn * og * zg

    @pl.when(j == pl.num_programs(1) - 1)
    def _finish_layer():
        mix = _dot(mix_scr[...].astype(BF16), w_out_ref[0])
        y = _layer_norm(ALPHA * xcur_scr[...] + mix, ln_g_ref[0], ln_b_ref[0])
        xcur_scr[...] = y
        y_ref[...] = y


def _sample_layers(x, w_in_bf, conv_w, conv_b, gbias, g_ret, g_m, w_out_bf, ln_g, ln_b,
                   rot, gam, esel, state_ret, state_c, state_n, m_pad, state_conv):
    bs = x.shape[0]
    bb = SAMPLE_BLOCK_B
    assert bs % bb == 0
    nb = bs // bb

    def lspec(shape, buffers=None):
        nd = len(shape)
        mode = None if buffers is None else pl.Buffered(buffers)
        return pl.BlockSpec((1,) + tuple(shape[1:]), lambda l, j: (l,) + (0,) * (nd - 1),
                            pipeline_mode=mode)

    def cspec(shape):
        nd = len(shape)
        return pl.BlockSpec(tuple(shape), lambda l, j: (0,) * nd)

    mat_spec = pl.BlockSpec((1, bb, N_HEADS, HEAD_DIM, HEAD_DIM), lambda l, j: (l, j, 0, 0, 0))
    n_spec = pl.BlockSpec((1, bb, N_HEADS, HEAD_DIM), lambda l, j: (l, j, 0, 0))
    m_spec = pl.BlockSpec((1, bb, LANES), lambda l, j: (l, j, 0))
    cv_spec = pl.BlockSpec((1, CONV_W - 1, bb, 2 * GROUP_W), lambda l, j: (l, 0, j, 0))
    in_specs = [
        cspec(x.shape), lspec(w_in_bf.shape, 1), lspec(conv_w.shape), lspec(conv_b.shape),
        lspec(gbias.shape), lspec(g_ret.shape), lspec(g_m.shape), lspec(w_out_bf.shape, 1),
        lspec(ln_g.shape), lspec(ln_b.shape), cspec(rot.shape),
        pl.BlockSpec(memory_space=pltpu.SMEM), cspec(esel.shape),
        mat_spec, mat_spec, n_spec, m_spec, cv_spec,
    ]
    out_shape = (
        jax.ShapeDtypeStruct(x.shape, F32),
        jax.ShapeDtypeStruct(state_ret.shape, F32),
        jax.ShapeDtypeStruct(state_c.shape, F32),
        jax.ShapeDtypeStruct(state_n.shape, F32),
        jax.ShapeDtypeStruct(m_pad.shape, F32),
        jax.ShapeDtypeStruct(state_conv.shape, F32),
    )
    out_specs = (cspec(x.shape), mat_spec, mat_spec, n_spec, m_spec, cv_spec)
    scratch = [
        pltpu.VMEM((bs, N_PAD), F32),
        pltpu.VMEM((bs, D_MODEL), F32),
        pltpu.VMEM((bs, 2 * GROUP_W), F32),
    ]
    return pl.pallas_call(
        _sample_kernel,
        grid=(DEPTH, nb),
        in_specs=in_specs,
        out_specs=out_specs,
        out_shape=out_shape,
        scratch_shapes=scratch,
        compiler_params=pltpu.CompilerParams(
            dimension_semantics=("arbitrary", "arbitrary"),
            vmem_limit_bytes=VMEM_LIMIT_BYTES),
        name="sample_layers",
    )(x, w_in_bf, conv_w, conv_b, gbias, g_ret, g_m, w_out_bf, ln_g, ln_b,
      rot, gam, esel, state_ret, state_c, state_n, m_pad, state_conv)


def _cast_weight_kernel(w_ref, o_ref):
    o_ref[0] = w_ref[0].astype(BF16)


def _cast_weight(w):
    depth, k, n = w.shape
    rows = WEIGHT_CAST_ROWS
    assert k % rows == 0
    return pl.pallas_call(
        _cast_weight_kernel,
        grid=(depth, k // rows),
        in_specs=[pl.BlockSpec((1, rows, n), lambda l, r: (l, r, 0))],
        out_specs=pl.BlockSpec((1, rows, n), lambda l, r: (l, r, 0)),
        out_shape=jax.ShapeDtypeStruct((depth, k, n), BF16),
        compiler_params=pltpu.CompilerParams(dimension_semantics=("parallel", "parallel")),
        name="cast_weight",
    )(w)


def _cast_transposed_weight_kernel(n_valid, wt_ref, o_ref):
    rows = wt_ref.shape[1]
    row_id = pl.program_id(1) * rows + lax.broadcasted_iota(jnp.int32, wt_ref.shape[1:], 0)
    wt = jnp.where(row_id < n_valid, wt_ref[0], 0.0)
    o_ref[0] = wt.astype(BF16).T


def _cast_transposed_weight(wt, n_out):
    depth, n, k = wt.shape
    rows = WEIGHT_CAST_ROWS
    assert n_out % LANES == 0 and 0 <= n_out - n < LANES
    return pl.pallas_call(
        functools.partial(_cast_transposed_weight_kernel, n),
        grid=(depth, pl.cdiv(n_out, rows)),
        in_specs=[pl.BlockSpec((1, rows, k), lambda l, r: (l, r, 0))],
        out_specs=pl.BlockSpec((1, k, rows), lambda l, r: (l, 0, r)),
        out_shape=jax.ShapeDtypeStruct((depth, k, n_out), BF16),
        compiler_params=pltpu.CompilerParams(dimension_semantics=("parallel", "parallel")),
        name="cast_weight_t",
    )(wt)


def _rotary_tables(pos):
    half = HEAD_DIM // 2
    inv = np.float32(ROPE_BASE) ** (-np.arange(half, dtype=np.float32) / np.float32(half))
    ang = pos.astype(np.float32)[:, None] * inv[None, :].astype(np.float32)
    cos = np.cos(ang.astype(np.float64)).astype(np.float32)
    sin = np.sin(ang.astype(np.float64)).astype(np.float32)
    cos_t = np.concatenate([cos, cos], axis=-1)
    sin_t = np.concatenate([-sin, sin], axis=-1)
    scale = np.float32(QK_SCALE)
    return cos_t, sin_t, cos_t * scale, sin_t * scale


def _retention_tables():
    L = CHUNK
    f32 = np.float32
    log_gamma = np.log(f32(1.0) - f32(2.0) ** (f32(-5.0) - np.arange(N_HEADS, dtype=f32))).astype(f32)
    idx = np.arange(L, dtype=f32)
    diff = idx[:, None] - idx[None, :]
    decay = (np.exp(log_gamma[:, None, None] * np.maximum(diff, f32(0.0))) * (diff >= 0)).astype(f32)
    q_decay = np.exp(log_gamma[:, None] * (idx + f32(1.0))).astype(f32)
    k_decay = np.exp(log_gamma[:, None] * (f32(L - 1.0) - idx)).astype(f32)
    c_decay = np.exp(log_gamma * f32(L)).astype(f32)
    qd = np.ascontiguousarray(np.broadcast_to(q_decay[:, :, None], (N_HEADS, L, HEAD_DIM)))
    kd = np.ascontiguousarray(np.broadcast_to(k_decay[:, :, None], (N_HEADS, L, HEAD_DIM)))
    gamma1 = np.exp(log_gamma).astype(f32)
    return decay, qd, kd, c_decay, gamma1


def kernel(x_prompt, x_sample, state_ret, state_mlstm_C, state_mlstm_n, state_mlstm_m, state_conv,
           w_in, conv_w, conv_b, b_i, b_f, g_ret, g_m, w_out, ln_g, ln_b):
    B, T, _ = x_prompt.shape
    Bs, Ts, _ = x_sample.shape
    assert Ts == 1

    w_in_bf = _cast_transposed_weight(jnp.swapaxes(w_in, 1, 2), N_PAD)
    w_out_bf = _cast_weight(w_out)
    gbias = jnp.pad(jnp.concatenate([b_i, b_f], axis=-1), ((0, 0), (0, LANES - 2 * N_HEADS)))
    gbias = gbias.reshape(DEPTH, 1, LANES)
    conv_b3 = conv_b.reshape(DEPTH, 1, 2 * GROUP_W)
    g_ret3 = g_ret.reshape(DEPTH, 1, GROUP_W)
    g_m3 = g_m.reshape(DEPTH, 1, GROUP_W)
    ln_g3 = ln_g.reshape(DEPTH, 1, D_MODEL)
    ln_b3 = ln_b.reshape(DEPTH, 1, D_MODEL)

    decay, qd, kd, c_decay, gamma1 = _retention_tables()
    idx = np.arange(CHUNK)
    causal = idx[:, None] >= idx[None, :]
    triu = jnp.asarray(causal.T, BF16)
    mask_add = np.where(causal, 0.0, -np.inf).astype(np.float32)
    e0 = jnp.asarray(np.broadcast_to(np.arange(LANES)[None, :] == 0, (CHUNK, LANES)), BF16)
    tabs_p = _rotary_tables(np.arange(T)) + (decay, qd, kd, c_decay, triu, mask_add, e0)
    tabs_p = tuple(jnp.asarray(a) for a in tabs_p)

    xp = x_prompt
    rp, cp, np_, mp, vp = [], [], [], [], []
    for l in range(DEPTH):
        xp, s, c, n, m, cv = _prompt_layer(
            l, xp, w_in_bf, conv_w[l], conv_b3[l], gbias[l], g_ret3[l], g_m3[l],
            w_out_bf, ln_g3[l], ln_b3[l], tabs_p)
        rp.append(s); cp.append(c); np_.append(n); mp.append(m); vp.append(cv)

    rot = jnp.asarray(np.concatenate(_rotary_tables(PAST_LEN + np.arange(Ts)), axis=0))
    bb = SAMPLE_BLOCK_B
    esel = jnp.asarray(np.arange(LANES)[:, None] == (np.arange(bb * HEAD_DIM)[None, :] // HEAD_DIM), BF16)
    gamma1 = jnp.asarray(gamma1)
    m_pad = jnp.pad(state_mlstm_m, ((0, 0), (0, 0), (N_HEADS, LANES - 2 * N_HEADS)))
    ys, rs, cs, ns, ms_pad, vs = _sample_layers(
        x_sample.reshape(Bs, D_MODEL), w_in_bf, conv_w, conv_b3, gbias, g_ret3, g_m3, w_out_bf,
        ln_g3, ln_b3, rot, gamma1, esel, state_ret, state_mlstm_C, state_mlstm_n, m_pad,
        jnp.swapaxes(state_conv, 1, 2))
    vs = jnp.swapaxes(vs, 1, 2)
    ms = ms_pad[:, :, N_HEADS:2 * N_HEADS]

    return (xp, ys.reshape(Bs, Ts, D_MODEL),
            jnp.stack(rp), jnp.stack(cp), jnp.stack(np_), jnp.stack(mp), jnp.stack(vp),
            rs, cs, ns, ms, vs)
```

```python
import functools

import jax
import jax.numpy as jnp
import numpy as np
from jax import lax
from jax.experimental import pallas as pl
from jax.experimental.pallas import tpu as pltpu

F32 = jnp.float32
BF16 = jnp.bfloat16

D_MODEL = 1024
DEPTH = 2
PAST_LEN = 16384
N_HEADS = 4
HEAD_DIM = 128
GROUP_W = N_HEADS * HEAD_DIM
CONV_W = 4
CHUNK = 128
ROPE_BASE = 10000.0
LN_EPS = 1e-5
GN_EPS = 1e-5
ALPHA = (2 * DEPTH) ** 0.25
QK_SCALE = HEAD_DIM ** -0.5

LANES = 128
SUBLANES = 8

COL_RQ = 0
COL_RK = COL_RQ + GROUP_W
COL_RV = COL_RK + GROUP_W
COL_RZ = COL_RV + GROUP_W
COL_MQK = COL_RZ + GROUP_W
COL_MV = COL_MQK + 2 * GROUP_W
COL_MO = COL_MV + GROUP_W
COL_MZ = COL_MO + GROUP_W
COL_GATE = COL_MZ + GROUP_W
N_IN = COL_GATE + 2 * N_HEADS
N_PAD = COL_GATE + LANES

PROMPT_BLOCK_T = 256
PROJ_PIECE_COLS = 256
PROJ_PIECE_ROWS = 256
SAMPLE_BLOCK_B = 16
WEIGHT_CAST_ROWS = 1024
VMEM_LIMIT_BYTES = 56 * 1024 * 1024


def _sigmoid(x):
    return 1.0 / (1.0 + jnp.exp(-x))


def _silu(x):
    return x * _sigmoid(x)


def _log_sigmoid(x):
    return jnp.minimum(x, 0.0) - jnp.log1p(jnp.exp(-jnp.abs(x)))


def _dot(a, b):
    return jnp.dot(a, b, preferred_element_type=F32)


def _dot_nt(a, b):
    return lax.dot_general(a, b, (((1,), (1,)), ((), ())), preferred_element_type=F32)


def _dot_tn(a, b):
    return lax.dot_general(a, b, (((0,), (0,)), ((), ())), preferred_element_type=F32)


def _rotary(x, cos_t, sin_t):
    return x * cos_t + pltpu.roll(x, HEAD_DIM // 2, axis=1) * sin_t


def _head_norm(h, g):
    mu = jnp.mean(h, axis=-1, keepdims=True)
    d = h - mu
    var = jnp.mean(d * d, axis=-1, keepdims=True)
    return d * lax.rsqrt(var + GN_EPS) * g


def _layer_norm(x, g, b):
    mu = jnp.mean(x, axis=-1, keepdims=True)
    d = x - mu
    var = jnp.mean(d * d, axis=-1, keepdims=True)
    return d * lax.rsqrt(var + LN_EPS) * g + b


def _cumsum_lanes(triu_bf, x):
    hi = x.astype(BF16)
    r1 = x - hi.astype(F32)
    mid = r1.astype(BF16)
    lo = (r1 - mid.astype(F32)).astype(BF16)
    return _dot(hi, triu_bf) + _dot(mid, triu_bf) + _dot(lo, triu_bf)


RQK_Q, RQK_QDEC, RQK_K, RQK_KDEC = 0, GROUP_W, 2 * GROUP_W, 3 * GROUP_W
VV_RET, VV_M = 0, GROUP_W
GZ_RET, GZ_M = 0, GROUP_W
GT_COLS, GT_ROWS, GT_PER_CHUNK = 0, 1, 2
GT_ROWS_USED = SUBLANES
EMITS_PER_CHUNK = 9 * N_HEADS + 1
TAIL_PIECES = 0


def _prompt_kernel(nt, n_blocks, xn_ref, x_ref, w_in_ref, conv_w_ref, conv_b_ref, gbias_ref,
                   g_ret_ref, g_m_ref, w_out_ref, ln_g_ref, ln_b_ref,
                   cos_ref, sin_ref, kcos_ref, ksin_ref,
                   decay_ref, qd_ref, kd_ref, cdec_ref, triu_ref, mask_ref, e0_ref,
                   y_ref, s_out_ref, c_out_ref, n_out_ref, m_out_ref, conv_out_ref,
                   rqk_a, vv_a, gz_a, qkm_a, gt_a, rqk_b, vv_b, gz_b, qkm_b, gt_b,
                   xb_scr, u_scr, mix_scr, s_scr, caug_scr, m_scr):
    g = pl.program_id(0)
    t = lax.rem(jnp.maximum(g - 1, 0), nt)
    tn = lax.rem(jnp.minimum(g, n_blocks - 1), nt)
    block_t = x_ref.shape[1]
    n_chunks = block_t // CHUNK
    carry_rows = CONV_W - 1
    heads = range(N_HEADS)

    @pl.when(g == 0)
    def _init_pipeline():
        for ref in (rqk_b, vv_b, gz_b, qkm_b, gt_b):
            ref[...] = jnp.zeros_like(ref)

    @pl.when(t == 0)
    def _init_state():
        s_scr[...] = jnp.zeros_like(s_scr)
        caug_scr[...] = jnp.zeros_like(caug_scr)
        m_scr[...] = jnp.zeros_like(m_scr)

    @pl.when(tn == 0)
    def _init_conv_carry():
        u_scr[0:SUBLANES, :] = jnp.zeros((SUBLANES, 2 * GROUP_W), F32)

    def hcols(base, h):
        return slice(base + h * HEAD_DIM, base + (h + 1) * HEAD_DIM)

    def step_body(set_in, set_cur):
        rqk_i, vv_i, gz_i, qkm_i, gt_i = set_in
        rqk_c, vv_c, gz_c, qkm_c, gt_c = set_cur
        triu_bf = triu_ref[...]
        causal_add = mask_ref[...]
        ones_col = e0_ref[...]
        sub_id = lax.broadcasted_iota(jnp.int32, (SUBLANES, LANES), 0)
        pad_rows = jnp.zeros((CHUNK - GT_ROWS_USED, LANES), F32)
        xb_scr[...] = xn_ref[0].astype(BF16)
        pw = PROJ_PIECE_COLS
        heads_per_piece = pw // HEAD_DIM

        pr = PROJ_PIECE_ROWS
        chunks_per_piece = pr // CHUNK

        def group_rows(p):
            return slice(p * pr, (p + 1) * pr)

        def proj(p, col, width=pw):
            return _dot(xb_scr[group_rows(p), :], w_in_ref[:, col:col + width])

        def piece_rot(p, i, col_base, cos_r, sin_r, dec_ref, dst, dst_dec):
            res = proj(p, col_base + i * pw)
            for hh in range(heads_per_piece):
                h = i * heads_per_piece + hh
                for cc in range(chunks_per_piece):
                    rows = slice(p * pr + cc * CHUNK, p * pr + (cc + 1) * CHUNK)
                    r = _rotary(res[cc * CHUNK:(cc + 1) * CHUNK, hh * HEAD_DIM:(hh + 1) * HEAD_DIM],
                                cos_r[rows, :], sin_r[rows, :])
                    rqk_i[rows, hcols(dst, h)] = r.astype(BF16)
                    rqk_i[rows, hcols(dst_dec, h)] = (r * dec_ref[h]).astype(BF16)

        def piece_cast(p, i, col_base, dst):
            vv_i[group_rows(p), dst + i * pw:dst + (i + 1) * pw] = proj(p, col_base + i * pw).astype(BF16)

        def piece_rz(p, i):
            gz_i[group_rows(p), GZ_RET + i * pw:GZ_RET + (i + 1) * pw] = _silu(proj(p, COL_RZ + i * pw))

        def piece_moz(p, i):
            og = _sigmoid(proj(p, COL_MO + i * pw))
            gz_i[group_rows(p), GZ_M + i * pw:GZ_M + (i + 1) * pw] = og * _silu(proj(p, COL_MZ + i * pw))

        def piece_conv(p, i):
            base = SUBLANES + p * pr
            u_scr[base:base + pr, i * pw:(i + 1) * pw] = proj(p, COL_MQK + i * pw)
            for cs in range(i * pw, (i + 1) * pw, LANES):
                cols = slice(cs, cs + LANES)
                acc = conv_b_ref[:, cols]
                for j in range(CONV_W):
                    r0 = base - carry_rows + j
                    acc = acc + u_scr[r0:r0 + pr, cols] * conv_w_ref[j:j + 1, cols]
                act = _silu(acc)
                if cs >= GROUP_W:
                    act = act * QK_SCALE
                qkm_i[group_rows(p), cols] = act

        def piece_gate(p):
            res = proj(p, COL_GATE, LANES)
            for cc in range(chunks_per_piece):
                c = p * chunks_per_piece + cc
                gates = res[cc * CHUNK:(cc + 1) * CHUNK, :] + gbias_ref[...]
                g8 = gates.T[0:SUBLANES, :]
                rows8 = jnp.where(sub_id < N_HEADS, g8, _cumsum_lanes(triu_bf, _log_sigmoid(g8)))
                gt_i[c * GT_PER_CHUNK + GT_ROWS, 0:GT_ROWS_USED, :] = rows8
                gt_i[c * GT_PER_CHUNK + GT_COLS] = jnp.concatenate([rows8, pad_rows], axis=0).T

        n_col_pieces = GROUP_W // pw
        pieces = []
        for p in range(block_t // pr):
            pieces.append(functools.partial(piece_gate, p))
            for i in range(n_col_pieces):
                pieces.append(functools.partial(piece_conv, p, 2 * i))
                pieces.append(functools.partial(piece_cast, p, i, COL_RV, VV_RET))
                pieces.append(functools.partial(piece_rot, p, i, COL_RQ, cos_ref, sin_ref, qd_ref,
                                                RQK_Q, RQK_QDEC))
                pieces.append(functools.partial(piece_rz, p, i))
                pieces.append(functools.partial(piece_conv, p, 2 * i + 1))
                pieces.append(functools.partial(piece_cast, p, i, COL_MV, VV_M))
                pieces.append(functools.partial(piece_rot, p, i, COL_RK, kcos_ref, ksin_ref, kd_ref,
                                                RQK_K, RQK_KDEC))
                pieces.append(functools.partial(piece_moz, p, i))
        n_pieces = len(pieces)
        n_slots = EMITS_PER_CHUNK * n_chunks
        slots_done = [0]

        def emit_proj_pieces():
            slots_done[0] += 1
            target = -(-(slots_done[0] * (n_pieces - TAIL_PIECES)) // n_slots)
            while n_pieces - len(pieces) < target:
                pieces.pop(0)()

        def per_head(fn):
            out = []
            for h in heads:
                emit_proj_pieces()
                out.append(fn(h))
            return out

        for c in range(n_chunks):
            rows = slice(c * CHUNK, (c + 1) * CHUNK)

            q_bf = [rqk_c[rows, hcols(RQK_Q, h)] for h in heads]
            k_bf = [rqk_c[rows, hcols(RQK_K, h)] for h in heads]
            v_bf = [vv_c[rows, hcols(VV_RET, h)] for h in heads]
            sc = per_head(lambda h: _dot_nt(q_bf[h], k_bf[h]))
            state = [s_scr[h] for h in heads]
            upd = per_head(lambda h: _dot_tn(rqk_c[rows, hcols(RQK_KDEC, h)], v_bf[h]))
            for h in heads:
                s_scr[h] = state[h] * cdec_ref[h] + upd[h]
            o = per_head(lambda h: _dot(
                jnp.concatenate([(sc[h] * decay_ref[h]).astype(BF16),
                                 rqk_c[rows, hcols(RQK_QDEC, h)]], axis=1),
                jnp.concatenate([v_bf[h], state[h].astype(BF16)], axis=0)))

            def ret_out(h):
                hn = _head_norm(o[h], g_ret_ref[:, hcols(0, h)])
                mix_scr[rows, hcols(0, h)] = (hn * gz_c[rows, hcols(GZ_RET, h)]).astype(BF16)

            per_head(ret_out)

            cols_t = gt_c[c * GT_PER_CHUNK + GT_COLS]
            rows_t = gt_c[c * GT_PER_CHUNK + GT_ROWS, 0:GT_ROWS_USED, :]
            q = [qkm_c[rows, hcols(0, h)] for h in heads]
            k = [qkm_c[rows, hcols(GROUP_W, h)] for h in heads]
            vaug_bf = [jnp.concatenate([vv_c[rows, hcols(VV_M, h)], ones_col], axis=1)
                       for h in heads]
            qk = per_head(lambda h: _dot_nt(q[h].astype(BF16), k[h].astype(BF16)))
            caug = [caug_scr[h] for h in heads]
            m_old = [m_scr[h, 0:1, 0:1] for h in heads]
            i_row = [rows_t[h:h + 1, :] for h in heads]
            b_row = [rows_t[N_HEADS + h:N_HEADS + h + 1, :] for h in heads]
            i_col = [cols_t[:, h:h + 1] for h in heads]
            b_col = [cols_t[:, N_HEADS + h:N_HEADS + h + 1] for h in heads]
            a_col = [b_col[h] + m_old[h] for h in heads]
            dm = per_head(lambda h: (b_col[h] + (i_row[h] - b_row[h])) + causal_add)
            mt = [jnp.maximum(a_col[h], jnp.max(dm[h], axis=-1, keepdims=True)) for h in heads]
            w_inter = [jnp.exp(a_col[h] - mt[h]) for h in heads]
            res = per_head(lambda h: _dot(
                jnp.concatenate([(qk[h] * jnp.exp(dm[h] - mt[h])).astype(BF16),
                                 (q[h] * w_inter[h]).astype(BF16)], axis=1),
                jnp.concatenate([vaug_bf[h], caug[h].astype(BF16)], axis=0)))
            b_last = [b_col[h][CHUNK - 1:CHUNK, :] for h in heads]
            g_col = [b_last[h] - b_col[h] + i_col[h] for h in heads]
            m_new = [jnp.maximum(b_last[h] + m_old[h], jnp.max(g_col[h], axis=0, keepdims=True))
                     for h in heads]
            wk = [jnp.exp(g_col[h] - m_new[h]) for h in heads]
            wc = [jnp.exp(b_last[h] + m_old[h] - m_new[h]) for h in heads]

            def mlstm_update(h):
                caug_scr[h] = caug[h] * wc[h] + _dot_tn((k[h] * wk[h]).astype(BF16), vaug_bf[h])
                m_scr[h] = jnp.broadcast_to(m_new[h], (SUBLANES, LANES))

            per_head(mlstm_update)

            def mlstm_out(h):
                den = jnp.maximum(jnp.abs(res[h][:, HEAD_DIM:HEAD_DIM + 1]), jnp.exp(-mt[h]))
                hn = _head_norm(res[h][:, :HEAD_DIM] / den, g_m_ref[:, hcols(0, h)])
                mix_scr[rows, hcols(GROUP_W, h)] = (hn * gz_c[rows, hcols(GZ_M, h)]).astype(BF16)

            per_head(mlstm_out)

            emit_proj_pieces()
            mix = _dot(mix_scr[rows, :], w_out_ref[...])
            y_ref[0, rows, :] = _layer_norm(ALPHA * x_ref[0, rows, :] + mix, ln_g_ref[...], ln_b_ref[...])

        assert slots_done[0] == n_slots and len(pieces) == TAIL_PIECES
        while pieces:
            pieces.pop(0)()

    set_a = (rqk_a, vv_a, gz_a, qkm_a, gt_a)
    set_b = (rqk_b, vv_b, gz_b, qkm_b, gt_b)
    parity = lax.rem(g, 2)

    @pl.when(parity == 0)
    def _even_step():
        step_body(set_a, set_b)

    @pl.when(parity == 1)
    def _odd_step():
        step_body(set_b, set_a)

    @pl.when(jnp.logical_and(tn == nt - 1, g < n_blocks))
    def _write_conv_state():
        conv_out_ref[0] = u_scr[SUBLANES + block_t - carry_rows:SUBLANES + block_t, :]

    @pl.when(jnp.logical_and(t == nt - 1, g > 0))
    def _write_state():
        s_out_ref[0] = s_scr[...]
        m_out_ref[0] = jnp.zeros((SUBLANES, LANES), F32)
        for h in heads:
            caug = caug_scr[h]
            c_out_ref[0, h] = caug[:, :HEAD_DIM]
            n_out_ref[0, h:h + 1, :] = caug[:, HEAD_DIM:].T[0:1, :]
            m_out_ref[0, h:h + 1, :] = m_scr[h, 0:1, :]

    u_scr[0:SUBLANES, :] = u_scr[block_t:block_t + SUBLANES, :]


def _const_spec(shape):
    nd = len(shape)
    return pl.BlockSpec(shape, lambda g: (0,) * nd)


def _layer_spec(shape, layer):
    return pl.BlockSpec((None,) + tuple(shape[1:]), lambda g: (layer,) + (0,) * (len(shape) - 1))


def _prompt_layer(layer, x, w_in_bf, conv_w, conv_b, gbias, g_ret, g_m, w_out_bf, ln_g, ln_b, tabs,
                  prev_states):
    B, T, _ = x.shape
    bt = PROMPT_BLOCK_T
    assert T % bt == 0 and bt % CHUNK == 0 and GROUP_W % PROJ_PIECE_COLS == 0
    nt = T // bt
    n_blocks = B * nt
    n_chunks = bt // CHUNK
    cos_t, sin_t, kcos_t, ksin_t, decay, qd, kd, cdec, triu, mask, e0 = tabs

    def nxt(g):
        return jnp.minimum(g, n_blocks - 1)

    def cur(g):
        return jnp.maximum(g - 1, 0)

    row_spec = pl.BlockSpec((bt, LANES), lambda g: (nxt(g) % nt, 0))
    in_specs = [
        pl.BlockSpec((1, bt, D_MODEL), lambda g: (nxt(g) // nt, nxt(g) % nt, 0)),
        pl.BlockSpec((1, bt, D_MODEL), lambda g: (cur(g) // nt, cur(g) % nt, 0)),
        _layer_spec(w_in_bf.shape, layer), _layer_spec(conv_w.shape, layer), _layer_spec(conv_b.shape, layer),
        _layer_spec(gbias.shape, layer), _layer_spec(g_ret.shape, layer), _layer_spec(g_m.shape, layer),
        _layer_spec(w_out_bf.shape, layer), _layer_spec(ln_g.shape, layer), _layer_spec(ln_b.shape, layer),
        row_spec, row_spec, row_spec, row_spec,
        _const_spec(decay.shape), _const_spec(qd.shape), _const_spec(kd.shape),
        pl.BlockSpec(memory_space=pltpu.SMEM),
        _const_spec(triu.shape), _const_spec(mask.shape), _const_spec(e0.shape),
    ]
    out_shape = (
        jax.ShapeDtypeStruct((B, T, D_MODEL), F32),
        jax.ShapeDtypeStruct((DEPTH, B, N_HEADS, HEAD_DIM, HEAD_DIM), F32),
        jax.ShapeDtypeStruct((DEPTH, B, N_HEADS, HEAD_DIM, HEAD_DIM), F32),
        jax.ShapeDtypeStruct((DEPTH, B, N_HEADS, HEAD_DIM), F32),
        jax.ShapeDtypeStruct((DEPTH, B, SUBLANES, LANES), F32),
        jax.ShapeDtypeStruct((DEPTH, B, CONV_W - 1, 2 * GROUP_W), F32),
    )
    out_specs = (
        pl.BlockSpec((1, bt, D_MODEL), lambda g: (cur(g) // nt, cur(g) % nt, 0)),
        pl.BlockSpec((None, 1, N_HEADS, HEAD_DIM, HEAD_DIM), lambda g: (layer, cur(g) // nt, 0, 0, 0)),
        pl.BlockSpec((None, 1, N_HEADS, HEAD_DIM, HEAD_DIM), lambda g: (layer, cur(g) // nt, 0, 0, 0)),
        pl.BlockSpec((None, 1, N_HEADS, HEAD_DIM), lambda g: (layer, cur(g) // nt, 0, 0)),
        pl.BlockSpec((None, 1, SUBLANES, LANES), lambda g: (layer, cur(g) // nt, 0, 0)),
        pl.BlockSpec((None, 1, CONV_W - 1, 2 * GROUP_W), lambda g: (layer, nxt(g) // nt, 0, 0)),
    )
    n_in = len(in_specs)
    n_prev = len(prev_states)
    in_specs = in_specs + [pl.BlockSpec(memory_space=pl.ANY)] * n_prev
    aliases = {n_in + i: 1 + i for i in range(n_prev)}

    def body(*refs):
        _prompt_kernel(nt, n_blocks, *refs[:n_in], *refs[n_in + n_prev:])
    operand_set = [
        pltpu.VMEM((bt, 4 * GROUP_W), BF16),
        pltpu.VMEM((bt, 2 * GROUP_W), BF16),
        pltpu.VMEM((bt, 2 * GROUP_W), F32),
        pltpu.VMEM((bt, 2 * GROUP_W), F32),
        pltpu.VMEM((n_chunks * GT_PER_CHUNK, CHUNK, LANES), F32),
    ]
    scratch = operand_set + operand_set + [
        pltpu.VMEM((bt, D_MODEL), BF16),
        pltpu.VMEM((SUBLANES + bt, 2 * GROUP_W), F32),
        pltpu.VMEM((bt, 2 * GROUP_W), BF16),
        pltpu.VMEM((N_HEADS, HEAD_DIM, HEAD_DIM), F32),
        pltpu.VMEM((N_HEADS, HEAD_DIM, 2 * HEAD_DIM), F32),
        pltpu.VMEM((N_HEADS, SUBLANES, LANES), F32),
    ]
    y, *states = pl.pallas_call(
        body,
        grid=(n_blocks + 1,),
        in_specs=in_specs,
        out_specs=out_specs,
        out_shape=out_shape,
        scratch_shapes=scratch,
        input_output_aliases=aliases,
        compiler_params=pltpu.CompilerParams(
            dimension_semantics=("arbitrary",),
            vmem_limit_bytes=VMEM_LIMIT_BYTES),
        name="prompt_layer",
    )(x, x, w_in_bf, conv_w, conv_b, gbias, g_ret, g_m, w_out_bf, ln_g, ln_b,
      cos_t, sin_t, kcos_t, ksin_t, decay, qd, kd, cdec, triu, mask, e0, *prev_states)
    return y, states


def _sample_kernel(x_ref, w_in_ref, conv_w_ref, conv_b_ref, gbias_ref, g_ret_ref, g_m_ref,
                   w_out_ref, ln_g_ref, ln_b_ref, rot_ref, gam_ref, esel_ref,
                   s_ref, c_ref, n_ref, m_ref, cv_ref,
                   y_ref, s_out_ref, c_out_ref, n_out_ref, m_out_ref, cv_out_ref,
                   proj_scr, xcur_scr, mix_scr):
    layer = pl.program_id(0)
    j = pl.program_id(1)
    bb = s_ref.shape[1]

    @pl.when(jnp.logical_and(layer == 0, j == 0))
    def _load_x():
        xcur_scr[...] = x_ref[...]

    @pl.when(j == 0)
    def _project():
        xb = xcur_scr[...].astype(BF16)
        for lo in range(0, N_PAD, 512):
            hi = min(lo + 512, N_PAD)
            proj_scr[:, lo:hi] = _dot(xb, w_in_ref[0, :, lo:hi])

    r0 = pl.multiple_of(j * bb, bb)
    rows = pl.ds(r0, bb)
    cos_t = rot_ref[0:1, :]
    sin_t = rot_ref[1:2, :]
    kcos_t = rot_ref[2:3, :]
    ksin_t = rot_ref[3:4, :]
    esel = esel_ref[...]
    row_id = lax.broadcasted_iota(jnp.int32, (bb, bb * HEAD_DIM), 0)
    blk_id = jnp.right_shift(lax.broadcasted_iota(jnp.int32, (bb, bb * HEAD_DIM), 1), 7)
    pad_rows = jnp.zeros((LANES - bb, HEAD_DIM), F32)
    pad_wide = jnp.zeros((LANES - bb, bb * HEAD_DIM), BF16)

    def col_form(x8):
        return jnp.concatenate([x8, pad_rows], axis=0).T.astype(BF16)

    def outer_all(k8, v8):
        vt = jnp.concatenate([v8] * bb, axis=1)
        vsel = jnp.where(row_id == blk_id, vt, 0.0).astype(BF16)
        return _dot(col_form(k8), jnp.concatenate([vsel, pad_wide], axis=0))

    def col_bcast_all(q8):
        return _dot(col_form(q8), esel)

    for h in range(N_HEADS):
        hc = h * HEAD_DIM
        q8 = _rotary(proj_scr[rows, COL_RQ + hc:COL_RQ + hc + HEAD_DIM], cos_t, sin_t)
        k8 = _rotary(proj_scr[rows, COL_RK + hc:COL_RK + hc + HEAD_DIM], kcos_t, ksin_t)
        v8 = proj_scr[rows, COL_RV + hc:COL_RV + hc + HEAD_DIM]
        kv_all = outer_all(k8, v8)
        qc_all = col_bcast_all(q8)
        gamma = gam_ref[h]
        o_rows = []
        for r in range(bb):
            blk = slice(r * HEAD_DIM, (r + 1) * HEAD_DIM)
            s_new = s_ref[0, r, h] * gamma + kv_all[:, blk]
            s_out_ref[0, r, h] = s_new
            o_rows.append(jnp.sum(qc_all[:, blk] * s_new, axis=0, keepdims=True))
        o8 = _head_norm(jnp.concatenate(o_rows, axis=0), g_ret_ref[0, :, hc:hc + HEAD_DIM])
        z8 = proj_scr[rows, COL_RZ + hc:COL_RZ + hc + HEAD_DIM]
        mix_scr[rows, hc:hc + HEAD_DIM] = o8 * _silu(z8)

    u8 = proj_scr[rows, COL_MQK:COL_MQK + 2 * GROUP_W]
    acc = conv_b_ref[0] + u8 * conv_w_ref[0, CONV_W - 1:CONV_W, :]
    for jj in range(CONV_W - 1):
        acc = acc + cv_ref[0, jj] * conv_w_ref[0, jj:jj + 1, :]
    for jj in range(1, CONV_W - 1):
        cv_out_ref[0, jj - 1] = cv_ref[0, jj]
    cv_out_ref[0, CONV_W - 2] = u8
    qk8 = _silu(acc)

    gates = proj_scr[rows, COL_GATE:COL_GATE + LANES] + gbias_ref[0]
    i_al = pltpu.roll(gates, N_HEADS, axis=1)
    bm = _log_sigmoid(gates) + m_ref[0]
    m_new = jnp.maximum(bm, i_al)
    wk = jnp.exp(i_al - m_new)
    wc = jnp.exp(bm - m_new)
    einv = jnp.exp(-m_new)
    m_out_ref[0] = m_new
    for h in range(N_HEADS):
        hc = h * HEAD_DIM
        gl = N_HEADS + h
        q8 = qk8[:, hc:hc + HEAD_DIM]
        k8 = qk8[:, GROUP_W + hc:GROUP_W + hc + HEAD_DIM] * QK_SCALE
        v8 = proj_scr[rows, COL_MV + hc:COL_MV + hc + HEAD_DIM]
        wk_h = wk[:, gl:gl + 1]
        wc_h = jnp.broadcast_to(wc[:, gl:gl + 1], (bb, HEAD_DIM))
        kw8 = k8 * wk_h
        kv_all = outer_all(kw8, v8)
        qc_all = col_bcast_all(q8)
        n_new = n_ref[0, :, h, :] * wc_h + kw8
        n_out_ref[0, :, h, :] = n_new
        num_rows = []
        for r in range(bb):
            blk = slice(r * HEAD_DIM, (r + 1) * HEAD_DIM)
            c_new = c_ref[0, r, h] * wc_h[r:r + 1, :] + kv_all[:, blk]
            c_out_ref[0, r, h] = c_new
            num_rows.append(jnp.sum(qc_all[:, blk] * c_new, axis=0, keepdims=True))
        num = jnp.concatenate(num_rows, axis=0)
        q_bf = q8.astype(BF16).astype(F32)
        den = jnp.sum(q_bf * n_new, axis=-1, keepdims=True)
        hout = num / jnp.maximum(jnp.abs(den), einv[:, gl:gl + 1])
        hn = _head_norm(hout, g_m_ref[0, :, hc:hc + HEAD_DIM])
        og = _sigmoid(proj_scr[rows, COL_MO + hc:COL_MO + hc + HEAD_DIM])
        zg = _silu(proj_scr[rows, COL_MZ + hc:COL_MZ + hc + HEAD_DIM])
        mix_scr[rows, GROUP_W + hc:GROUP_W + hc + HEAD_DIM] = hn * og * zg

    @pl.when(j == pl.num_programs(1) - 1)
    def _finish_layer():
        mix = _dot(mix_scr[...].astype(BF16), w_out_ref[0])
        y = _layer_norm(ALPHA * xcur_scr[...] + mix, ln_g_ref[0], ln_b_ref[0])
        xcur_scr[...] = y
        y_ref[...] = y


def _sample_layers(x, w_in_bf, conv_w, conv_b, gbias, g_ret, g_m, w_out_bf, ln_g, ln_b,
                   rot, gam, esel, state_ret, state_c, state_n, m_pad, state_conv):
    bs = x.shape[0]
    bb = SAMPLE_BLOCK_B
    assert bs % bb == 0
    nb = bs // bb

    def lspec(shape, buffers=None):
        nd = len(shape)
        mode = None if buffers is None else pl.Buffered(buffers)
        return pl.BlockSpec((1,) + tuple(shape[1:]), lambda l, j: (l,) + (0,) * (nd - 1),
                            pipeline_mode=mode)

    def cspec(shape):
        nd = len(shape)
        return pl.BlockSpec(tuple(shape), lambda l, j: (0,) * nd)

    mat_spec = pl.BlockSpec((1, bb, N_HEADS, HEAD_DIM, HEAD_DIM), lambda l, j: (l, j, 0, 0, 0))
    n_spec = pl.BlockSpec((1, bb, N_HEADS, HEAD_DIM), lambda l, j: (l, j, 0, 0))
    m_spec = pl.BlockSpec((1, bb, LANES), lambda l, j: (l, j, 0))
    cv_spec = pl.BlockSpec((1, CONV_W - 1, bb, 2 * GROUP_W), lambda l, j: (l, 0, j, 0))
    in_specs = [
        cspec(x.shape), lspec(w_in_bf.shape, 1), lspec(conv_w.shape), lspec(conv_b.shape),
        lspec(gbias.shape), lspec(g_ret.shape), lspec(g_m.shape), lspec(w_out_bf.shape, 1),
        lspec(ln_g.shape), lspec(ln_b.shape), cspec(rot.shape),
        pl.BlockSpec(memory_space=pltpu.SMEM), cspec(esel.shape),
        mat_spec, mat_spec, n_spec, m_spec, cv_spec,
    ]
    out_shape = (
        jax.ShapeDtypeStruct(x.shape, F32),
        jax.ShapeDtypeStruct(state_ret.shape, F32),
        jax.ShapeDtypeStruct(state_c.shape, F32),
        jax.ShapeDtypeStruct(state_n.shape, F32),
        jax.ShapeDtypeStruct(m_pad.shape, F32),
        jax.ShapeDtypeStruct(state_conv.shape, F32),
    )
    out_specs = (cspec(x.shape), mat_spec, mat_spec, n_spec, m_spec, cv_spec)
    scratch = [
        pltpu.VMEM((bs, N_PAD), F32),
        pltpu.VMEM((bs, D_MODEL), F32),
        pltpu.VMEM((bs, 2 * GROUP_W), F32),
    ]
    return pl.pallas_call(
        _sample_kernel,
        grid=(DEPTH, nb),
        in_specs=in_specs,
        out_specs=out_specs,
        out_shape=out_shape,
        scratch_shapes=scratch,
        compiler_params=pltpu.CompilerParams(
            dimension_semantics=("arbitrary", "arbitrary"),
            vmem_limit_bytes=VMEM_LIMIT_BYTES),
        name="sample_layers",
    )(x, w_in_bf, conv_w, conv_b, gbias, g_ret, g_m, w_out_bf, ln_g, ln_b,
      rot, gam, esel, state_ret, state_c, state_n, m_pad, state_conv)


def _cast_weight_kernel(w_ref, o_ref):
    o_ref[0] = w_ref[0].astype(BF16)


def _cast_weight(w):
    depth, k, n = w.shape
    rows = WEIGHT_CAST_ROWS
    assert k % rows == 0
    return pl.pallas_call(
        _cast_weight_kernel,
        grid=(depth, k // rows),
        in_specs=[pl.BlockSpec((1, rows, n), lambda l, r: (l, r, 0))],
        out_specs=pl.BlockSpec((1, rows, n), lambda l, r: (l, r, 0)),
        out_shape=jax.ShapeDtypeStruct((depth, k, n), BF16),
        compiler_params=pltpu.CompilerParams(dimension_semantics=("parallel", "parallel")),
        name="cast_weight",
    )(w)


def _cast_transposed_weight_kernel(n_valid, wt_ref, o_ref):
    rows = wt_ref.shape[1]
    row_id = pl.program_id(1) * rows + lax.broadcasted_iota(jnp.int32, wt_ref.shape[1:], 0)
    wt = jnp.where(row_id < n_valid, wt_ref[0], 0.0)
    o_ref[0] = wt.astype(BF16).T


def _cast_transposed_weight(wt, n_out):
    depth, n, k = wt.shape
    rows = WEIGHT_CAST_ROWS
    assert n_out % LANES == 0 and 0 <= n_out - n < LANES
    return pl.pallas_call(
        functools.partial(_cast_transposed_weight_kernel, n),
        grid=(depth, pl.cdiv(n_out, rows)),
        in_specs=[pl.BlockSpec((1, rows, k), lambda l, r: (l, r, 0))],
        out_specs=pl.BlockSpec((1, k, rows), lambda l, r: (l, 0, r)),
        out_shape=jax.ShapeDtypeStruct((depth, k, n_out), BF16),
        compiler_params=pltpu.CompilerParams(dimension_semantics=("parallel", "parallel")),
        name="cast_weight_t",
    )(wt)


def _rotary_tables(pos):
    half = HEAD_DIM // 2
    inv = np.float32(ROPE_BASE) ** (-np.arange(half, dtype=np.float32) / np.float32(half))
    ang = pos.astype(np.float32)[:, None] * inv[None, :].astype(np.float32)
    cos = np.cos(ang.astype(np.float64)).astype(np.float32)
    sin = np.sin(ang.astype(np.float64)).astype(np.float32)
    cos_t = np.concatenate([cos, cos], axis=-1)
    sin_t = np.concatenate([-sin, sin], axis=-1)
    scale = np.float32(QK_SCALE)
    return cos_t, sin_t, cos_t * scale, sin_t * scale


def _retention_tables():
    L = CHUNK
    f32 = np.float32
    log_gamma = np.log(f32(1.0) - f32(2.0) ** (f32(-5.0) - np.arange(N_HEADS, dtype=f32))).astype(f32)
    idx = np.arange(L, dtype=f32)
    diff = idx[:, None] - idx[None, :]
    decay = (np.exp(log_gamma[:, None, None] * np.maximum(diff, f32(0.0))) * (diff >= 0)).astype(f32)
    q_decay = np.exp(log_gamma[:, None] * (idx + f32(1.0))).astype(f32)
    k_decay = np.exp(log_gamma[:, None] * (f32(L - 1.0) - idx)).astype(f32)
    c_decay = np.exp(log_gamma * f32(L)).astype(f32)
    qd = np.ascontiguousarray(np.broadcast_to(q_decay[:, :, None], (N_HEADS, L, HEAD_DIM)))
    kd = np.ascontiguousarray(np.broadcast_to(k_decay[:, :, None], (N_HEADS, L, HEAD_DIM)))
    gamma1 = np.exp(log_gamma).astype(f32)
    return decay, qd, kd, c_decay, gamma1


def kernel(x_prompt, x_sample, state_ret, state_mlstm_C, state_mlstm_n, state_mlstm_m, state_conv,
           w_in, conv_w, conv_b, b_i, b_f, g_ret, g_m, w_out, ln_g, ln_b):
    B, T, _ = x_prompt.shape
    Bs, Ts, _ = x_sample.shape
    assert Ts == 1

    w_in_bf = _cast_transposed_weight(jnp.swapaxes(w_in, 1, 2), N_PAD)
    w_out_bf = _cast_weight(w_out)
    gbias = jnp.pad(jnp.concatenate([b_i, b_f], axis=-1), ((0, 0), (0, LANES - 2 * N_HEADS)))
    gbias = gbias.reshape(DEPTH, 1, LANES)
    conv_b3 = conv_b.reshape(DEPTH, 1, 2 * GROUP_W)
    g_ret3 = g_ret.reshape(DEPTH, 1, GROUP_W)
    g_m3 = g_m.reshape(DEPTH, 1, GROUP_W)
    ln_g3 = ln_g.reshape(DEPTH, 1, D_MODEL)
    ln_b3 = ln_b.reshape(DEPTH, 1, D_MODEL)

    decay, qd, kd, c_decay, gamma1 = _retention_tables()
    idx = np.arange(CHUNK)
    causal = idx[:, None] >= idx[None, :]
    triu = jnp.asarray(causal.T, BF16)
    mask_add = np.where(causal, 0.0, -np.inf).astype(np.float32)
    e0 = jnp.asarray(np.broadcast_to(np.arange(LANES)[None, :] == 0, (CHUNK, LANES)), BF16)
    tabs_p = _rotary_tables(np.arange(T)) + (decay, qd, kd, c_decay, triu, mask_add, e0)
    tabs_p = tuple(jnp.asarray(a) for a in tabs_p)

    xp = x_prompt
    prompt_states = []
    for l in range(DEPTH):
        xp, prompt_states = _prompt_layer(
            l, xp, w_in_bf, conv_w, conv_b3, gbias, g_ret3, g_m3, w_out_bf, ln_g3, ln_b3, tabs_p,
            prompt_states)
    rp, cp, np_, mp_pad, vp = prompt_states
    mp = mp_pad[:, :, :N_HEADS, 0]

    rot = jnp.asarray(np.concatenate(_rotary_tables(PAST_LEN + np.arange(Ts)), axis=0))
    bb = SAMPLE_BLOCK_B
    esel = jnp.asarray(np.arange(LANES)[:, None] == (np.arange(bb * HEAD_DIM)[None, :] // HEAD_DIM), BF16)
    gamma1 = jnp.asarray(gamma1)
    m_pad = jnp.pad(state_mlstm_m, ((0, 0), (0, 0), (N_HEADS, LANES - 2 * N_HEADS)))
    ys, rs, cs, ns, ms_pad, vs = _sample_layers(
        x_sample.reshape(Bs, D_MODEL), w_in_bf, conv_w, conv_b3, gbias, g_ret3, g_m3, w_out_bf,
        ln_g3, ln_b3, rot, gamma1, esel, state_ret, state_mlstm_C, state_mlstm_n, m_pad,
        jnp.swapaxes(state_conv, 1, 2))
    vs = jnp.swapaxes(vs, 1, 2)
    ms = ms_pad[:, :, N_HEADS:2 * N_HEADS]

    return (xp, ys.reshape(Bs, Ts, D_MODEL),
            rp, cp, np_, mp, vp, rs, cs, ns, ms, vs)
```

```python
import functools

import jax
import jax.numpy as jnp
import numpy as np
from jax import lax
from jax.experimental import pallas as pl
from jax.experimental.pallas import tpu as pltpu

F32 = jnp.float32
BF16 = jnp.bfloat16

D_MODEL = 1024
DEPTH = 2
PAST_LEN = 16384
N_HEADS = 4
HEAD_DIM = 128
GROUP_W = N_HEADS * HEAD_DIM
CONV_W = 4
CHUNK = 128
ROPE_BASE = 10000.0
LN_EPS = 1e-5
GN_EPS = 1e-5
ALPHA = (2 * DEPTH) ** 0.25
QK_SCALE = HEAD_DIM ** -0.5

LANES = 128
SUBLANES = 8

COL_RQ = 0
COL_RK = COL_RQ + GROUP_W
COL_RV = COL_RK + GROUP_W
COL_RZ = COL_RV + GROUP_W
COL_MQK = COL_RZ + GROUP_W
COL_MV = COL_MQK + 2 * GROUP_W
COL_MO = COL_MV + GROUP_W
COL_MZ = COL_MO + GROUP_W
COL_GATE = COL_MZ + GROUP_W
N_IN = COL_GATE + 2 * N_HEADS
N_PAD = COL_GATE + LANES

PROMPT_BLOCK_T = 256
PROJ_PIECE_COLS = 256
PROJ_PIECE_ROWS = 256
SAMPLE_BLOCK_B = 16
WEIGHT_CAST_ROWS = 1024
VMEM_LIMIT_BYTES = 56 * 1024 * 1024


def _sigmoid(x):
    return 1.0 / (1.0 + jnp.exp(-x))


def _silu(x):
    return x * _sigmoid(x)


def _log_sigmoid(x):
    return jnp.minimum(x, 0.0) - jnp.log1p(jnp.exp(-jnp.abs(x)))


def _dot(a, b):
    return jnp.dot(a, b, preferred_element_type=F32)


def _dot_nt(a, b):
    return lax.dot_general(a, b, (((1,), (1,)), ((), ())), preferred_element_type=F32)


def _dot_tn(a, b):
    return lax.dot_general(a, b, (((0,), (0,)), ((), ())), preferred_element_type=F32)


def _rotary(x, cos_t, sin_t):
    return x * cos_t + pltpu.roll(x, HEAD_DIM // 2, axis=1) * sin_t


def _head_norm(h, g):
    mu = jnp.mean(h, axis=-1, keepdims=True)
    d = h - mu
    var = jnp.mean(d * d, axis=-1, keepdims=True)
    return d * lax.rsqrt(var + GN_EPS) * g


def _layer_norm(x, g, b):
    mu = jnp.mean(x, axis=-1, keepdims=True)
    d = x - mu
    var = jnp.mean(d * d, axis=-1, keepdims=True)
    return d * lax.rsqrt(var + LN_EPS) * g + b


def _cumsum_lanes(triu_bf, x):
    hi = x.astype(BF16)
    r1 = x - hi.astype(F32)
    mid = r1.astype(BF16)
    lo = (r1 - mid.astype(F32)).astype(BF16)
    return _dot(hi, triu_bf) + _dot(mid, triu_bf) + _dot(lo, triu_bf)


RQK_Q, RQK_QDEC, RQK_K, RQK_KDEC = 0, GROUP_W, 2 * GROUP_W, 3 * GROUP_W
VV_RET, VV_M = 0, GROUP_W
GZ_RET, GZ_M = 0, GROUP_W
GT_COLS, GT_ROWS, GT_PER_CHUNK = 0, 1, 2
GT_ROWS_USED = SUBLANES
EMITS_PER_CHUNK = 9 * N_HEADS + 1
TAIL_PIECES = 0
HEADS_PER_STAGE = 2


def _prompt_kernel(nt, n_blocks, xn_ref, x_ref, w_in_ref, conv_w_ref, conv_b_ref, gbias_ref,
                   g_ret_ref, g_m_ref, w_out_ref, ln_g_ref, ln_b_ref,
                   cos_ref, sin_ref, kcos_ref, ksin_ref,
                   decay_ref, qd_ref, kd_ref, cdec_ref, triu_ref, mask_ref, e0_ref,
                   y_ref, s_out_ref, c_out_ref, n_out_ref, m_out_ref, conv_out_ref,
                   rqk_a, vv_a, gz_a, qkm_a, gt_a, rqk_b, vv_b, gz_b, qkm_b, gt_b,
                   xb_scr, u_scr, mix_scr, s_scr, caug_scr, m_scr):
    g = pl.program_id(0)
    t = lax.rem(jnp.maximum(g - 1, 0), nt)
    tn = lax.rem(jnp.minimum(g, n_blocks - 1), nt)
    block_t = x_ref.shape[1]
    n_chunks = block_t // CHUNK
    carry_rows = CONV_W - 1
    heads = range(N_HEADS)

    @pl.when(g == 0)
    def _init_pipeline():
        for ref in (rqk_b, vv_b, gz_b, qkm_b, gt_b):
            ref[...] = jnp.zeros_like(ref)

    @pl.when(t == 0)
    def _init_state():
        s_scr[...] = jnp.zeros_like(s_scr)
        caug_scr[...] = jnp.zeros_like(caug_scr)
        m_scr[...] = jnp.zeros_like(m_scr)

    @pl.when(tn == 0)
    def _init_conv_carry():
        u_scr[0:SUBLANES, :] = jnp.zeros((SUBLANES, 2 * GROUP_W), F32)

    def hcols(base, h):
        return slice(base + h * HEAD_DIM, base + (h + 1) * HEAD_DIM)

    def step_body(set_in, set_cur):
        rqk_i, vv_i, gz_i, qkm_i, gt_i = set_in
        rqk_c, vv_c, gz_c, qkm_c, gt_c = set_cur
        triu_bf = triu_ref[...]
        causal_add = mask_ref[...]
        ones_col = e0_ref[...]
        sub_id = lax.broadcasted_iota(jnp.int32, (SUBLANES, LANES), 0)
        pad_rows = jnp.zeros((CHUNK - GT_ROWS_USED, LANES), F32)
        xb_scr[...] = xn_ref[0].astype(BF16)
        pw = PROJ_PIECE_COLS
        heads_per_piece = pw // HEAD_DIM

        pr = PROJ_PIECE_ROWS
        chunks_per_piece = pr // CHUNK

        def group_rows(p):
            return slice(p * pr, (p + 1) * pr)

        def proj(p, col, width=pw):
            return _dot(xb_scr[group_rows(p), :], w_in_ref[:, col:col + width])

        def piece_rot(p, i, col_base, cos_r, sin_r, dec_ref, dst, dst_dec):
            res = proj(p, col_base + i * pw)
            for hh in range(heads_per_piece):
                h = i * heads_per_piece + hh
                for cc in range(chunks_per_piece):
                    rows = slice(p * pr + cc * CHUNK, p * pr + (cc + 1) * CHUNK)
                    r = _rotary(res[cc * CHUNK:(cc + 1) * CHUNK, hh * HEAD_DIM:(hh + 1) * HEAD_DIM],
                                cos_r[rows, :], sin_r[rows, :])
                    rqk_i[rows, hcols(dst, h)] = r.astype(BF16)
                    rqk_i[rows, hcols(dst_dec, h)] = (r * dec_ref[h]).astype(BF16)

        def piece_cast(p, i, col_base, dst):
            vv_i[group_rows(p), dst + i * pw:dst + (i + 1) * pw] = proj(p, col_base + i * pw).astype(BF16)

        def piece_rz(p, i):
            gz_i[group_rows(p), GZ_RET + i * pw:GZ_RET + (i + 1) * pw] = _silu(proj(p, COL_RZ + i * pw))

        def piece_moz(p, i):
            og = _sigmoid(proj(p, COL_MO + i * pw))
            gz_i[group_rows(p), GZ_M + i * pw:GZ_M + (i + 1) * pw] = og * _silu(proj(p, COL_MZ + i * pw))

        def piece_conv(p, i):
            base = SUBLANES + p * pr
            u_scr[base:base + pr, i * pw:(i + 1) * pw] = proj(p, COL_MQK + i * pw)
            for cs in range(i * pw, (i + 1) * pw, LANES):
                cols = slice(cs, cs + LANES)
                acc = conv_b_ref[:, cols]
                for j in range(CONV_W):
                    r0 = base - carry_rows + j
                    acc = acc + u_scr[r0:r0 + pr, cols] * conv_w_ref[j:j + 1, cols]
                act = _silu(acc)
                if cs >= GROUP_W:
                    act = act * QK_SCALE
                qkm_i[group_rows(p), cols] = act

        def piece_gate(p):
            res = proj(p, COL_GATE, LANES)
            for cc in range(chunks_per_piece):
                c = p * chunks_per_piece + cc
                gates = res[cc * CHUNK:(cc + 1) * CHUNK, :] + gbias_ref[...]
                g8 = gates.T[0:SUBLANES, :]
                rows8 = jnp.where(sub_id < N_HEADS, g8, _cumsum_lanes(triu_bf, _log_sigmoid(g8)))
                gt_i[c * GT_PER_CHUNK + GT_ROWS, 0:GT_ROWS_USED, :] = rows8
                gt_i[c * GT_PER_CHUNK + GT_COLS] = jnp.concatenate([rows8, pad_rows], axis=0).T

        n_col_pieces = GROUP_W // pw
        pieces = []
        for p in range(block_t // pr):
            pieces.append(functools.partial(piece_gate, p))
            for i in range(n_col_pieces):
                pieces.append(functools.partial(piece_conv, p, 2 * i))
                pieces.append(functools.partial(piece_cast, p, i, COL_RV, VV_RET))
                pieces.append(functools.partial(piece_rot, p, i, COL_RQ, cos_ref, sin_ref, qd_ref,
                                                RQK_Q, RQK_QDEC))
                pieces.append(functools.partial(piece_rz, p, i))
                pieces.append(functools.partial(piece_conv, p, 2 * i + 1))
                pieces.append(functools.partial(piece_cast, p, i, COL_MV, VV_M))
                pieces.append(functools.partial(piece_rot, p, i, COL_RK, kcos_ref, ksin_ref, kd_ref,
                                                RQK_K, RQK_KDEC))
                pieces.append(functools.partial(piece_moz, p, i))
        n_pieces = len(pieces)
        n_slots = EMITS_PER_CHUNK * n_chunks
        slots_done = [0]

        def emit_proj_pieces():
            slots_done[0] += 1
            target = -(-(slots_done[0] * (n_pieces - TAIL_PIECES)) // n_slots)
            while n_pieces - len(pieces) < target:
                pieces.pop(0)()

        def per_head(group, fn):
            out = {}
            for h in group:
                emit_proj_pieces()
                out[h] = fn(h)
            return out

        head_groups = [range(h0, h0 + HEADS_PER_STAGE) for h0 in range(0, N_HEADS, HEADS_PER_STAGE)]

        for c in range(n_chunks):
            rows = slice(c * CHUNK, (c + 1) * CHUNK)

            for grp in head_groups:
                q_bf = {h: rqk_c[rows, hcols(RQK_Q, h)] for h in grp}
                k_bf = {h: rqk_c[rows, hcols(RQK_K, h)] for h in grp}
                v_bf = {h: vv_c[rows, hcols(VV_RET, h)] for h in grp}
                sc = per_head(grp, lambda h: _dot_nt(q_bf[h], k_bf[h]))
                state = {h: s_scr[h] for h in grp}
                upd = per_head(grp, lambda h: _dot_tn(rqk_c[rows, hcols(RQK_KDEC, h)], v_bf[h]))
                for h in grp:
                    s_scr[h] = state[h] * cdec_ref[h] + upd[h]
                o = per_head(grp, lambda h: _dot(
                    jnp.concatenate([(sc[h] * decay_ref[h]).astype(BF16),
                                     rqk_c[rows, hcols(RQK_QDEC, h)]], axis=1),
                    jnp.concatenate([v_bf[h], state[h].astype(BF16)], axis=0)))

                def ret_out(h):
                    hn = _head_norm(o[h], g_ret_ref[:, hcols(0, h)])
                    mix_scr[rows, hcols(0, h)] = (hn * gz_c[rows, hcols(GZ_RET, h)]).astype(BF16)

                per_head(grp, ret_out)

            cols_t = gt_c[c * GT_PER_CHUNK + GT_COLS]
            rows_t = gt_c[c * GT_PER_CHUNK + GT_ROWS, 0:GT_ROWS_USED, :]
            for grp in head_groups:
                q = {h: qkm_c[rows, hcols(0, h)] for h in grp}
                k = {h: qkm_c[rows, hcols(GROUP_W, h)] for h in grp}
                vaug_bf = {h: jnp.concatenate([vv_c[rows, hcols(VV_M, h)], ones_col], axis=1)
                           for h in grp}
                qk = per_head(grp, lambda h: _dot_nt(q[h].astype(BF16), k[h].astype(BF16)))
                caug = {h: caug_scr[h] for h in grp}
                m_old = {h: m_scr[h, 0:1, 0:1] for h in grp}
                i_row = {h: rows_t[h:h + 1, :] for h in grp}
                b_row = {h: rows_t[N_HEADS + h:N_HEADS + h + 1, :] for h in grp}
                i_col = {h: cols_t[:, h:h + 1] for h in grp}
                b_col = {h: cols_t[:, N_HEADS + h:N_HEADS + h + 1] for h in grp}
                a_col = {h: b_col[h] + m_old[h] for h in grp}
                dm = per_head(grp, lambda h: (b_col[h] + (i_row[h] - b_row[h])) + causal_add)
                mt = {h: jnp.maximum(a_col[h], jnp.max(dm[h], axis=-1, keepdims=True)) for h in grp}
                w_inter = {h: jnp.exp(a_col[h] - mt[h]) for h in grp}
                res = per_head(grp, lambda h: _dot(
                    jnp.concatenate([(qk[h] * jnp.exp(dm[h] - mt[h])).astype(BF16),
                                     (q[h] * w_inter[h]).astype(BF16)], axis=1),
                    jnp.concatenate([vaug_bf[h], caug[h].astype(BF16)], axis=0)))
                b_last = {h: b_col[h][CHUNK - 1:CHUNK, :] for h in grp}
                g_col = {h: b_last[h] - b_col[h] + i_col[h] for h in grp}
                m_new = {h: jnp.maximum(b_last[h] + m_old[h], jnp.max(g_col[h], axis=0, keepdims=True))
                         for h in grp}
                wk = {h: jnp.exp(g_col[h] - m_new[h]) for h in grp}
                wc = {h: jnp.exp(b_last[h] + m_old[h] - m_new[h]) for h in grp}

                def mlstm_update(h):
                    caug_scr[h] = caug[h] * wc[h] + _dot_tn((k[h] * wk[h]).astype(BF16), vaug_bf[h])
                    m_scr[h] = jnp.broadcast_to(m_new[h], (SUBLANES, LANES))

                per_head(grp, mlstm_update)

                def mlstm_out(h):
                    den = jnp.maximum(jnp.abs(res[h][:, HEAD_DIM:HEAD_DIM + 1]), jnp.exp(-mt[h]))
                    hn = _head_norm(res[h][:, :HEAD_DIM] / den, g_m_ref[:, hcols(0, h)])
                    mix_scr[rows, hcols(GROUP_W, h)] = (hn * gz_c[rows, hcols(GZ_M, h)]).astype(BF16)

                per_head(grp, mlstm_out)

            emit_proj_pieces()
            mix = _dot(mix_scr[rows, :], w_out_ref[...])
            y_ref[0, rows, :] = _layer_norm(ALPHA * x_ref[0, rows, :] + mix, ln_g_ref[...], ln_b_ref[...])

        assert slots_done[0] == n_slots and len(pieces) == TAIL_PIECES
        while pieces:
            pieces.pop(0)()

    set_a = (rqk_a, vv_a, gz_a, qkm_a, gt_a)
    set_b = (rqk_b, vv_b, gz_b, qkm_b, gt_b)
    parity = lax.rem(g, 2)

    @pl.when(parity == 0)
    def _even_step():
        step_body(set_a, set_b)

    @pl.when(parity == 1)
    def _odd_step():
        step_body(set_b, set_a)

    @pl.when(jnp.logical_and(tn == nt - 1, g < n_blocks))
    def _write_conv_state():
        conv_out_ref[0] = u_scr[SUBLANES + block_t - carry_rows:SUBLANES + block_t, :]

    @pl.when(jnp.logical_and(t == nt - 1, g > 0))
    def _write_state():
        s_out_ref[0] = s_scr[...]
        m_out_ref[0] = jnp.zeros((SUBLANES, LANES), F32)
        for h in heads:
            caug = caug_scr[h]
            c_out_ref[0, h] = caug[:, :HEAD_DIM]
            n_out_ref[0, h:h + 1, :] = caug[:, HEAD_DIM:].T[0:1, :]
            m_out_ref[0, h:h + 1, :] = m_scr[h, 0:1, :]

    u_scr[0:SUBLANES, :] = u_scr[block_t:block_t + SUBLANES, :]


def _const_spec(shape):
    nd = len(shape)
    return pl.BlockSpec(shape, lambda g: (0,) * nd)


def _layer_spec(shape, layer):
    return pl.BlockSpec((None,) + tuple(shape[1:]), lambda g: (layer,) + (0,) * (len(shape) - 1))


def _prompt_layer(layer, x, w_in_bf, conv_w, conv_b, gbias, g_ret, g_m, w_out_bf, ln_g, ln_b, tabs,
                  prev_states):
    B, T, _ = x.shape
    bt = PROMPT_BLOCK_T
    assert T % bt == 0 and bt % CHUNK == 0 and GROUP_W % PROJ_PIECE_COLS == 0
    nt = T // bt
    n_blocks = B * nt
    n_chunks = bt // CHUNK
    cos_t, sin_t, kcos_t, ksin_t, decay, qd, kd, cdec, triu, mask, e0 = tabs

    def nxt(g):
        return jnp.minimum(g, n_blocks - 1)

    def cur(g):
        return jnp.maximum(g - 1, 0)

    row_spec = pl.BlockSpec((bt, LANES), lambda g: (nxt(g) % nt, 0))
    in_specs = [
        pl.BlockSpec((1, bt, D_MODEL), lambda g: (nxt(g) // nt, nxt(g) % nt, 0)),
        pl.BlockSpec((1, bt, D_MODEL), lambda g: (cur(g) // nt, cur(g) % nt, 0)),
        _layer_spec(w_in_bf.shape, layer), _layer_spec(conv_w.shape, layer), _layer_spec(conv_b.shape, layer),
        _layer_spec(gbias.shape, layer), _layer_spec(g_ret.shape, layer), _layer_spec(g_m.shape, layer),
        _layer_spec(w_out_bf.shape, layer), _layer_spec(ln_g.shape, layer), _layer_spec(ln_b.shape, layer),
        row_spec, row_spec, row_spec, row_spec,
        _const_spec(decay.shape), _const_spec(qd.shape), _const_spec(kd.shape),
        pl.BlockSpec(memory_space=pltpu.SMEM),
        _const_spec(triu.shape), _const_spec(mask.shape), _const_spec(e0.shape),
    ]
    out_shape = (
        jax.ShapeDtypeStruct((B, T, D_MODEL), F32),
        jax.ShapeDtypeStruct((DEPTH, B, N_HEADS, HEAD_DIM, HEAD_DIM), F32),
        jax.ShapeDtypeStruct((DEPTH, B, N_HEADS, HEAD_DIM, HEAD_DIM), F32),
        jax.ShapeDtypeStruct((DEPTH, B, N_HEADS, HEAD_DIM), F32),
        jax.ShapeDtypeStruct((DEPTH, B, SUBLANES, LANES), F32),
        jax.ShapeDtypeStruct((DEPTH, B, CONV_W - 1, 2 * GROUP_W), F32),
    )
    out_specs = (
        pl.BlockSpec((1, bt, D_MODEL), lambda g: (cur(g) // nt, cur(g) % nt, 0)),
        pl.BlockSpec((None, 1, N_HEADS, HEAD_DIM, HEAD_DIM), lambda g: (layer, cur(g) // nt, 0, 0, 0)),
        pl.BlockSpec((None, 1, N_HEADS, HEAD_DIM, HEAD_DIM), lambda g: (layer, cur(g) // nt, 0, 0, 0)),
        pl.BlockSpec((None, 1, N_HEADS, HEAD_DIM), lambda g: (layer, cur(g) // nt, 0, 0)),
        pl.BlockSpec((None, 1, SUBLANES, LANES), lambda g: (layer, cur(g) // nt, 0, 0)),
        pl.BlockSpec((None, 1, CONV_W - 1, 2 * GROUP_W), lambda g: (layer, nxt(g) // nt, 0, 0)),
    )
    n_in = len(in_specs)
    n_prev = len(prev_states)
    in_specs = in_specs + [pl.BlockSpec(memory_space=pl.ANY)] * n_prev
    aliases = {n_in + i: 1 + i for i in range(n_prev)}

    def body(*refs):
        _prompt_kernel(nt, n_blocks, *refs[:n_in], *refs[n_in + n_prev:])
    operand_set = [
        pltpu.VMEM((bt, 4 * GROUP_W), BF16),
        pltpu.VMEM((bt, 2 * GROUP_W), BF16),
        pltpu.VMEM((bt, 2 * GROUP_W), F32),
        pltpu.VMEM((bt, 2 * GROUP_W), F32),
        pltpu.VMEM((n_chunks * GT_PER_CHUNK, CHUNK, LANES), F32),
    ]
    scratch = operand_set + operand_set + [
        pltpu.VMEM((bt, D_MODEL), BF16),
        pltpu.VMEM((SUBLANES + bt, 2 * GROUP_W), F32),
        pltpu.VMEM((bt, 2 * GROUP_W), BF16),
        pltpu.VMEM((N_HEADS, HEAD_DIM, HEAD_DIM), F32),
        pltpu.VMEM((N_HEADS, HEAD_DIM, 2 * HEAD_DIM), F32),
        pltpu.VMEM((N_HEADS, SUBLANES, LANES), F32),
    ]
    y, *states = pl.pallas_call(
        body,
        grid=(n_blocks + 1,),
        in_specs=in_specs,
        out_specs=out_specs,
        out_shape=out_shape,
        scratch_shapes=scratch,
        input_output_aliases=aliases,
        compiler_params=pltpu.CompilerParams(
            dimension_semantics=("arbitrary",),
            vmem_limit_bytes=VMEM_LIMIT_BYTES),
        name="prompt_layer",
    )(x, x, w_in_bf, conv_w, conv_b, gbias, g_ret, g_m, w_out_bf, ln_g, ln_b,
      cos_t, sin_t, kcos_t, ksin_t, decay, qd, kd, cdec, triu, mask, e0, *prev_states)
    return y, states


def _sample_kernel(x_ref, w_in_ref, conv_w_ref, conv_b_ref, gbias_ref, g_ret_ref, g_m_ref,
                   w_out_ref, ln_g_ref, ln_b_ref, rot_ref, gam_ref, esel_ref,
                   s_ref, c_ref, n_ref, m_ref, cv_ref,
                   y_ref, s_out_ref, c_out_ref, n_out_ref, m_out_ref, cv_out_ref,
                   proj_scr, xcur_scr, mix_scr):
    layer = pl.program_id(0)
    j = pl.program_id(1)
    bb = s_ref.shape[1]

    @pl.when(jnp.logical_and(layer == 0, j == 0))
    def _load_x():
        xcur_scr[...] = x_ref[...]

    @pl.when(j == 0)
    def _project():
        xb = xcur_scr[...].astype(BF16)
        for lo in range(0, N_PAD, 512):
            hi = min(lo + 512, N_PAD)
            proj_scr[:, lo:hi] = _dot(xb, w_in_ref[0, :, lo:hi])

    r0 = pl.multiple_of(j * bb, bb)
    rows = pl.ds(r0, bb)
    cos_t = rot_ref[0:1, :]
    sin_t = rot_ref[1:2, :]
    kcos_t = rot_ref[2:3, :]
    ksin_t = rot_ref[3:4, :]
    esel = esel_ref[...]
    row_id = lax.broadcasted_iota(jnp.int32, (bb, bb * HEAD_DIM), 0)
    blk_id = jnp.right_shift(lax.broadcasted_iota(jnp.int32, (bb, bb * HEAD_DIM), 1), 7)
    pad_rows = jnp.zeros((LANES - bb, HEAD_DIM), F32)
    pad_wide = jnp.zeros((LANES - bb, bb * HEAD_DIM), BF16)

    def col_form(x8):
        return jnp.concatenate([x8, pad_rows], axis=0).T.astype(BF16)

    def outer_all(k8, v8):
        vt = jnp.concatenate([v8] * bb, axis=1)
        vsel = jnp.where(row_id == blk_id, vt, 0.0).astype(BF16)
        return _dot(col_form(k8), jnp.concatenate([vsel, pad_wide], axis=0))

    def col_bcast_all(q8):
        return _dot(col_form(q8), esel)

    for h in range(N_HEADS):
        hc = h * HEAD_DIM
        q8 = _rotary(proj_scr[rows, COL_RQ + hc:COL_RQ + hc + HEAD_DIM], cos_t, sin_t)
        k8 = _rotary(proj_scr[rows, COL_RK + hc:COL_RK + hc + HEAD_DIM], kcos_t, ksin_t)
        v8 = proj_scr[rows, COL_RV + hc:COL_RV + hc + HEAD_DIM]
        kv_all = outer_all(k8, v8)
        qc_all = col_bcast_all(q8)
        gamma = gam_ref[h]
        o_rows = []
        for r in range(bb):
            blk = slice(r * HEAD_DIM, (r + 1) * HEAD_DIM)
            s_new = s_ref[0, r, h] * gamma + kv_all[:, blk]
            s_out_ref[0, r, h] = s_new
            o_rows.append(jnp.sum(qc_all[:, blk] * s_new, axis=0, keepdims=True))
        o8 = _head_norm(jnp.concatenate(o_rows, axis=0), g_ret_ref[0, :, hc:hc + HEAD_DIM])
        z8 = proj_scr[rows, COL_RZ + hc:COL_RZ + hc + HEAD_DIM]
        mix_scr[rows, hc:hc + HEAD_DIM] = o8 * _silu(z8)

    u8 = proj_scr[rows, COL_MQK:COL_MQK + 2 * GROUP_W]
    acc = conv_b_ref[0] + u8 * conv_w_ref[0, CONV_W - 1:CONV_W, :]
    for jj in range(CONV_W - 1):
        acc = acc + cv_ref[0, jj] * conv_w_ref[0, jj:jj + 1, :]
    for jj in range(1, CONV_W - 1):
        cv_out_ref[0, jj - 1] = cv_ref[0, jj]
    cv_out_ref[0, CONV_W - 2] = u8
    qk8 = _silu(acc)

    gates = proj_scr[rows, COL_GATE:COL_GATE + LANES] + gbias_ref[0]
    i_al = pltpu.roll(gates, N_HEADS, axis=1)
    bm = _log_sigmoid(gates) + m_ref[0]
    m_new = jnp.maximum(bm, i_al)
    wk = jnp.exp(i_al - m_new)
    wc = jnp.exp(bm - m_new)
    einv = jnp.exp(-m_new)
    m_out_ref[0] = m_new
    for h in range(N_HEADS):
        hc = h * HEAD_DIM
        gl = N_HEADS + h
        q8 = qk8[:, hc:hc + HEAD_DIM]
        k8 = qk8[:, GROUP_W + hc:GROUP_W + hc + HEAD_DIM] * QK_SCALE
        v8 = proj_scr[rows, COL_MV + hc:COL_MV + hc + HEAD_DIM]
        wk_h = wk[:, gl:gl + 1]
        wc_h = jnp.broadcast_to(wc[:, gl:gl + 1], (bb, HEAD_DIM))
        kw8 = k8 * wk_h
        kv_all = outer_all(kw8, v8)
        qc_all = col_bcast_all(q8)
        n_new = n_ref[0, :, h, :] * wc_h + kw8
        n_out_ref[0, :, h, :] = n_new
        num_rows = []
        for r in range(bb):
            blk = slice(r * HEAD_DIM, (r + 1) * HEAD_DIM)
            c_new = c_ref[0, r, h] * wc_h[r:r + 1, :] + kv_all[:, blk]
            c_out_ref[0, r, h] = c_new
            num_rows.append(jnp.sum(qc_all[:, blk] * c_new, axis=0, keepdims=True))
        num = jnp.concatenate(num_rows, axis=0)
        q_bf = q8.astype(BF16).astype(F32)
        den = jnp.sum(q_bf * n_new, axis=-1, keepdims=True)
        hout = num / jnp.maximum(jnp.abs(den), einv[:, gl:gl + 1])
        hn = _head_norm(hout, g_m_ref[0, :, hc:hc + HEAD_DIM])
        og = _sigmoid(proj_scr[rows, COL_MO + hc:COL_MO + hc + HEAD_DIM])
        zg = _silu(proj_scr[rows, COL_MZ + hc:COL_MZ + hc + HEAD_DIM])
        mix_scr[rows, GROUP_W + hc:GROUP_W + hc + HEAD_DIM] = hn * og * zg

    @pl.when(j == pl.num_programs(1) - 1)
    def _finish_layer():
        mix = _dot(mix_scr[...].astype(BF16), w_out_ref[0])
        y = _layer_norm(ALPHA * xcur_scr[...] + mix, ln_g_ref[0], ln_b_ref[0])
        xcur_scr[...] = y
        y_ref[...] = y


def _sample_layers(x, w_in_bf, conv_w, conv_b, gbias, g_ret, g_m, w_out_bf, ln_g, ln_b,
                   rot, gam, esel, state_ret, state_c, state_n, m_pad, state_conv):
    bs = x.shape[0]
    bb = SAMPLE_BLOCK_B
    assert bs % bb == 0
    nb = bs // bb

    def lspec(shape, buffers=None):
        nd = len(shape)
        mode = None if buffers is None else pl.Buffered(buffers)
        return pl.BlockSpec((1,) + tuple(shape[1:]), lambda l, j: (l,) + (0,) * (nd - 1),
                            pipeline_mode=mode)

    def cspec(shape):
        nd = len(shape)
        return pl.BlockSpec(tuple(shape), lambda l, j: (0,) * nd)

    mat_spec = pl.BlockSpec((1, bb, N_HEADS, HEAD_DIM, HEAD_DIM), lambda l, j: (l, j, 0, 0, 0))
    n_spec = pl.BlockSpec((1, bb, N_HEADS, HEAD_DIM), lambda l, j: (l, j, 0, 0))
    m_spec = pl.BlockSpec((1, bb, LANES), lambda l, j: (l, j, 0))
    cv_spec = pl.BlockSpec((1, CONV_W - 1, bb, 2 * GROUP_W), lambda l, j: (l, 0, j, 0))
    in_specs = [
        cspec(x.shape), lspec(w_in_bf.shape, 1), lspec(conv_w.shape), lspec(conv_b.shape),
        lspec(gbias.shape), lspec(g_ret.shape), lspec(g_m.shape), lspec(w_out_bf.shape, 1),
        lspec(ln_g.shape), lspec(ln_b.shape), cspec(rot.shape),
        pl.BlockSpec(memory_space=pltpu.SMEM), cspec(esel.shape),
        mat_spec, mat_spec, n_spec, m_spec, cv_spec,
    ]
    out_shape = (
        jax.ShapeDtypeStruct(x.shape, F32),
        jax.ShapeDtypeStruct(state_ret.shape, F32),
        jax.ShapeDtypeStruct(state_c.shape, F32),
        jax.ShapeDtypeStruct(state_n.shape, F32),
        jax.ShapeDtypeStruct(m_pad.shape, F32),
        jax.ShapeDtypeStruct(state_conv.shape, F32),
    )
    out_specs = (cspec(x.shape), mat_spec, mat_spec, n_spec, m_spec, cv_spec)
    scratch = [
        pltpu.VMEM((bs, N_PAD), F32),
        pltpu.VMEM((bs, D_MODEL), F32),
        pltpu.VMEM((bs, 2 * GROUP_W), F32),
    ]
    return pl.pallas_call(
        _sample_kernel,
        grid=(DEPTH, nb),
        in_specs=in_specs,
        out_specs=out_specs,
        out_shape=out_shape,
        scratch_shapes=scratch,
        compiler_params=pltpu.CompilerParams(
            dimension_semantics=("arbitrary", "arbitrary"),
            vmem_limit_bytes=VMEM_LIMIT_BYTES),
        name="sample_layers",
    )(x, w_in_bf, conv_w, conv_b, gbias, g_ret, g_m, w_out_bf, ln_g, ln_b,
      rot, gam, esel, state_ret, state_c, state_n, m_pad, state_conv)


def _cast_weight_kernel(w_ref, o_ref):
    o_ref[0] = w_ref[0].astype(BF16)


def _cast_weight(w):
    depth, k, n = w.shape
    rows = WEIGHT_CAST_ROWS
    assert k % rows == 0
    return pl.pallas_call(
        _cast_weight_kernel,
        grid=(depth, k // rows),
        in_specs=[pl.BlockSpec((1, rows, n), lambda l, r: (l, r, 0))],
        out_specs=pl.BlockSpec((1, rows, n), lambda l, r: (l, r, 0)),
        out_shape=jax.ShapeDtypeStruct((depth, k, n), BF16),
        compiler_params=pltpu.CompilerParams(dimension_semantics=("parallel", "parallel")),
        name="cast_weight",
    )(w)


def _cast_transposed_weight_kernel(n_valid, wt_ref, o_ref):
    rows = wt_ref.shape[1]
    row_id = pl.program_id(1) * rows + lax.broadcasted_iota(jnp.int32, wt_ref.shape[1:], 0)
    wt = jnp.where(row_id < n_valid, wt_ref[0], 0.0)
    o_ref[0] = wt.astype(BF16).T


def _cast_transposed_weight(wt, n_out):
    depth, n, k = wt.shape
    rows = WEIGHT_CAST_ROWS
    assert n_out % LANES == 0 and 0 <= n_out - n < LANES
    return pl.pallas_call(
        functools.partial(_cast_transposed_weight_kernel, n),
        grid=(depth, pl.cdiv(n_out, rows)),
        in_specs=[pl.BlockSpec((1, rows, k), lambda l, r: (l, r, 0))],
        out_specs=pl.BlockSpec((1, k, rows), lambda l, r: (l, 0, r)),
        out_shape=jax.ShapeDtypeStruct((depth, k, n_out), BF16),
        compiler_params=pltpu.CompilerParams(dimension_semantics=("parallel", "parallel")),
        name="cast_weight_t",
    )(wt)


def _rotary_tables(pos):
    half = HEAD_DIM // 2
    inv = np.float64(ROPE_BASE) ** (-np.arange(half, dtype=np.float64) / half)
    ang = pos.astype(np.float64)[:, None] * inv[None, :]
    cos = np.cos(ang)
    sin = np.sin(ang)
    cos_t = np.concatenate([cos, cos], axis=-1)
    sin_t = np.concatenate([-sin, sin], axis=-1)
    tabs = (cos_t, sin_t, cos_t * QK_SCALE, sin_t * QK_SCALE)
    return tuple(t.astype(np.float32) for t in tabs)


def _retention_tables():
    L = CHUNK
    f32 = np.float32
    log_gamma = np.log(1.0 - 2.0 ** (-5.0 - np.arange(N_HEADS, dtype=np.float64)))
    idx = np.arange(L, dtype=np.float64)
    diff = idx[:, None] - idx[None, :]
    decay = (np.exp(log_gamma[:, None, None] * np.maximum(diff, 0.0)) * (diff >= 0)).astype(f32)
    q_decay = np.exp(log_gamma[:, None] * (idx + 1.0)).astype(f32)
    k_decay = np.exp(log_gamma[:, None] * (L - 1.0 - idx)).astype(f32)
    c_decay = np.exp(log_gamma * L).astype(f32)
    qd = np.ascontiguousarray(np.broadcast_to(q_decay[:, :, None], (N_HEADS, L, HEAD_DIM)))
    kd = np.ascontiguousarray(np.broadcast_to(k_decay[:, :, None], (N_HEADS, L, HEAD_DIM)))
    gamma1 = np.exp(log_gamma).astype(f32)
    return decay, qd, kd, c_decay, gamma1


def kernel(x_prompt, x_sample, state_ret, state_mlstm_C, state_mlstm_n, state_mlstm_m, state_conv,
           w_in, conv_w, conv_b, b_i, b_f, g_ret, g_m, w_out, ln_g, ln_b):
    B, T, _ = x_prompt.shape
    Bs, Ts, _ = x_sample.shape
    assert Ts == 1

    w_in_bf = _cast_transposed_weight(jnp.swapaxes(w_in, 1, 2), N_PAD)
    w_out_bf = _cast_weight(w_out)
    gbias = jnp.pad(jnp.concatenate([b_i, b_f], axis=-1), ((0, 0), (0, LANES - 2 * N_HEADS)))
    gbias = gbias.reshape(DEPTH, 1, LANES)
    conv_b3 = conv_b.reshape(DEPTH, 1, 2 * GROUP_W)
    g_ret3 = g_ret.reshape(DEPTH, 1, GROUP_W)
    g_m3 = g_m.reshape(DEPTH, 1, GROUP_W)
    ln_g3 = ln_g.reshape(DEPTH, 1, D_MODEL)
    ln_b3 = ln_b.reshape(DEPTH, 1, D_MODEL)

    decay, qd, kd, c_decay, gamma1 = _retention_tables()
    idx = np.arange(CHUNK)
    causal = idx[:, None] >= idx[None, :]
    triu = jnp.asarray(causal.T, BF16)
    mask_add = np.where(causal, 0.0, -np.inf).astype(np.float32)
    e0 = jnp.asarray(np.broadcast_to(np.arange(LANES)[None, :] == 0, (CHUNK, LANES)), BF16)
    tabs_p = _rotary_tables(np.arange(T)) + (decay, qd, kd, c_decay, triu, mask_add, e0)
    tabs_p = tuple(jnp.asarray(a) for a in tabs_p)

    xp = x_prompt
    prompt_states = []
    for l in range(DEPTH):
        xp, prompt_states = _prompt_layer(
            l, xp, w_in_bf, conv_w, conv_b3, gbias, g_ret3, g_m3, w_out_bf, ln_g3, ln_b3, tabs_p,
            prompt_states)
    rp, cp, np_, mp_pad, vp = prompt_states
    mp = mp_pad[:, :, :N_HEADS, 0]

    rot = jnp.asarray(np.concatenate(_rotary_tables(PAST_LEN + np.arange(Ts)), axis=0))
    bb = SAMPLE_BLOCK_B
    esel = jnp.asarray(np.arange(LANES)[:, None] == (np.arange(bb * HEAD_DIM)[None, :] // HEAD_DIM), BF16)
    gamma1 = jnp.asarray(gamma1)
    m_pad = jnp.pad(state_mlstm_m, ((0, 0), (0, 0), (N_HEADS, LANES - 2 * N_HEADS)))
    ys, rs, cs, ns, ms_pad, vs = _sample_layers(
        x_sample.reshape(Bs, D_MODEL), w_in_bf, conv_w, conv_b3, gbias, g_ret3, g_m3, w_out_bf,
        ln_g3, ln_b3, rot, gamma1, esel, state_ret, state_mlstm_C, state_mlstm_n, m_pad,
        jnp.swapaxes(state_conv, 1, 2))
    vs = jnp.swapaxes(vs, 1, 2)
    ms = ms_pad[:, :, N_HEADS:2 * N_HEADS]

    return (xp, ys.reshape(Bs, Ts, D_MODEL),
            rp, cp, np_, mp, vp, rs, cs, ns, ms, vs)
```

```python
import functools

import jax
import jax.numpy as jnp
import numpy as np
from jax import lax
from jax.experimental import pallas as pl
from jax.experimental.pallas import tpu as pltpu

F32 = jnp.float32
BF16 = jnp.bfloat16

D_MODEL = 1024
DEPTH = 2
PAST_LEN = 16384
N_HEADS = 4
HEAD_DIM = 128
GROUP_W = N_HEADS * HEAD_DIM
CONV_W = 4
CHUNK = 128
ROPE_BASE = 10000.0
LN_EPS = 1e-5
GN_EPS = 1e-5
ALPHA = (2 * DEPTH) ** 0.25
QK_SCALE = HEAD_DIM ** -0.5

LANES = 128
SUBLANES = 8

COL_RQ = 0
COL_RK = COL_RQ + GROUP_W
COL_RV = COL_RK + GROUP_W
COL_RZ = COL_RV + GROUP_W
COL_MQK = COL_RZ + GROUP_W
COL_MV = COL_MQK + 2 * GROUP_W
COL_MO = COL_MV + GROUP_W
COL_MZ = COL_MO + GROUP_W
COL_GATE = COL_MZ + GROUP_W
N_IN = COL_GATE + 2 * N_HEADS
N_PAD = COL_GATE + LANES

PROMPT_BLOCK_T = 256
PROJ_PIECE_COLS = 256
PROJ_PIECE_ROWS = 256
SAMPLE_BLOCK_B = 16
WEIGHT_CAST_ROWS = 1024
VMEM_LIMIT_BYTES = 56 * 1024 * 1024


def _sigmoid(x):
    return 1.0 / (1.0 + jnp.exp(-x))


def _silu(x):
    return x * _sigmoid(x)


def _log_sigmoid(x):
    return jnp.minimum(x, 0.0) - jnp.log1p(jnp.exp(-jnp.abs(x)))


def _dot(a, b):
    return jnp.dot(a, b, preferred_element_type=F32)


def _dot_nt(a, b):
    return lax.dot_general(a, b, (((1,), (1,)), ((), ())), preferred_element_type=F32)


def _dot_tn(a, b):
    return lax.dot_general(a, b, (((0,), (0,)), ((), ())), preferred_element_type=F32)


def _rotary(x, cos_t, sin_t):
    return x * cos_t + pltpu.roll(x, HEAD_DIM // 2, axis=1) * sin_t


def _head_norm(h, g):
    mu = jnp.mean(h, axis=-1, keepdims=True)
    d = h - mu
    var = jnp.mean(d * d, axis=-1, keepdims=True)
    return d * lax.rsqrt(var + GN_EPS) * g


def _layer_norm(x, g, b):
    mu = jnp.mean(x, axis=-1, keepdims=True)
    d = x - mu
    var = jnp.mean(d * d, axis=-1, keepdims=True)
    return d * lax.rsqrt(var + LN_EPS) * g + b


def _cumsum_lanes(triu_bf, x):
    hi = x.astype(BF16)
    r1 = x - hi.astype(F32)
    mid = r1.astype(BF16)
    lo = (r1 - mid.astype(F32)).astype(BF16)
    return _dot(hi, triu_bf) + _dot(mid, triu_bf) + _dot(lo, triu_bf)


RQK_Q, RQK_QDEC, RQK_K, RQK_KDEC = 0, GROUP_W, 2 * GROUP_W, 3 * GROUP_W
VV_RET, VV_M = 0, GROUP_W
GZ_RET, GZ_M = 0, GROUP_W
GT_COLS, GT_ROWS, GT_PER_CHUNK = 0, 1, 2
GT_ROWS_USED = SUBLANES
EMITS_PER_CHUNK = 9 * N_HEADS + 1
TAIL_PIECES = 0
N_STATE_OUTPUTS = 5
HEADS_PER_STAGE = 2


def _prompt_kernel(nt, n_blocks, layer, xn_ref, x_ref, w_in_ref, conv_w_ref, conv_b_ref, gbias_ref,
                   g_ret_ref, g_m_ref, w_out_ref, ln_g_ref, ln_b_ref,
                   cos_ref, sin_ref, kcos_ref, ksin_ref,
                   decay_ref, qd_ref, kd_ref, cdec_ref, triu_ref, mask_ref, e0_ref, *refs):
    prev_refs = refs[:N_STATE_OUTPUTS] if layer > 0 else ()
    refs = refs[len(prev_refs):]
    y_ref, s_out_ref, c_out_ref, n_out_ref, m_out_ref, conv_out_ref = refs[:1 + N_STATE_OUTPUTS]
    (rqk_a, vv_a, gz_a, qkm_a, gt_a, rqk_b, vv_b, gz_b, qkm_b, gt_b,
     xb_scr, u_scr, mix_scr, s_scr, caug_scr, m_scr) = refs[1 + N_STATE_OUTPUTS:]
    g = pl.program_id(0)
    t = lax.rem(jnp.maximum(g - 1, 0), nt)
    tn = lax.rem(jnp.minimum(g, n_blocks - 1), nt)
    block_t = x_ref.shape[1]
    n_chunks = block_t // CHUNK
    carry_rows = CONV_W - 1
    heads = range(N_HEADS)

    @pl.when(g == 0)
    def _init_pipeline():
        for ref in (rqk_b, vv_b, gz_b, qkm_b, gt_b):
            ref[...] = jnp.zeros_like(ref)

    @pl.when(t == 0)
    def _init_state():
        s_scr[...] = jnp.zeros_like(s_scr)
        caug_scr[...] = jnp.zeros_like(caug_scr)
        m_scr[...] = jnp.zeros_like(m_scr)

    @pl.when(tn == 0)
    def _init_conv_carry():
        u_scr[0:SUBLANES, :] = jnp.zeros((SUBLANES, 2 * GROUP_W), F32)

    def hcols(base, h):
        return slice(base + h * HEAD_DIM, base + (h + 1) * HEAD_DIM)

    def step_body(set_in, set_cur):
        rqk_i, vv_i, gz_i, qkm_i, gt_i = set_in
        rqk_c, vv_c, gz_c, qkm_c, gt_c = set_cur
        triu_bf = triu_ref[...]
        causal_add = mask_ref[...]
        ones_col = e0_ref[...]
        sub_id = lax.broadcasted_iota(jnp.int32, (SUBLANES, LANES), 0)
        pad_rows = jnp.zeros((CHUNK - GT_ROWS_USED, LANES), F32)
        xb_scr[...] = xn_ref[0].astype(BF16)
        pw = PROJ_PIECE_COLS
        heads_per_piece = pw // HEAD_DIM

        pr = PROJ_PIECE_ROWS
        chunks_per_piece = pr // CHUNK

        def group_rows(p):
            return slice(p * pr, (p + 1) * pr)

        def proj(p, col, width=pw):
            return _dot(xb_scr[group_rows(p), :], w_in_ref[:, col:col + width])

        def piece_rot(p, i, col_base, cos_r, sin_r, dec_ref, dst, dst_dec):
            res = proj(p, col_base + i * pw)
            for hh in range(heads_per_piece):
                h = i * heads_per_piece + hh
                for cc in range(chunks_per_piece):
                    rows = slice(p * pr + cc * CHUNK, p * pr + (cc + 1) * CHUNK)
                    r = _rotary(res[cc * CHUNK:(cc + 1) * CHUNK, hh * HEAD_DIM:(hh + 1) * HEAD_DIM],
                                cos_r[rows, :], sin_r[rows, :])
                    rqk_i[rows, hcols(dst, h)] = r.astype(BF16)
                    rqk_i[rows, hcols(dst_dec, h)] = (r * dec_ref[h]).astype(BF16)

        def piece_cast(p, i, col_base, dst):
            vv_i[group_rows(p), dst + i * pw:dst + (i + 1) * pw] = proj(p, col_base + i * pw).astype(BF16)

        def piece_rz(p, i):
            gz_i[group_rows(p), GZ_RET + i * pw:GZ_RET + (i + 1) * pw] = _silu(proj(p, COL_RZ + i * pw))

        def piece_moz(p, i):
            og = _sigmoid(proj(p, COL_MO + i * pw))
            gz_i[group_rows(p), GZ_M + i * pw:GZ_M + (i + 1) * pw] = og * _silu(proj(p, COL_MZ + i * pw))

        def piece_conv(p, i):
            base = SUBLANES + p * pr
            u_scr[base:base + pr, i * pw:(i + 1) * pw] = proj(p, COL_MQK + i * pw)
            for cs in range(i * pw, (i + 1) * pw, LANES):
                cols = slice(cs, cs + LANES)
                acc = conv_b_ref[:, cols]
                for j in range(CONV_W):
                    r0 = base - carry_rows + j
                    acc = acc + u_scr[r0:r0 + pr, cols] * conv_w_ref[j:j + 1, cols]
                act = _silu(acc)
                if cs >= GROUP_W:
                    act = act * QK_SCALE
                qkm_i[group_rows(p), cols] = act

        def piece_gate(p):
            res = proj(p, COL_GATE, LANES)
            for cc in range(chunks_per_piece):
                c = p * chunks_per_piece + cc
                gates = res[cc * CHUNK:(cc + 1) * CHUNK, :] + gbias_ref[...]
                g8 = gates.T[0:SUBLANES, :]
                rows8 = jnp.where(sub_id < N_HEADS, g8, _cumsum_lanes(triu_bf, _log_sigmoid(g8)))
                gt_i[c * GT_PER_CHUNK + GT_ROWS, 0:GT_ROWS_USED, :] = rows8
                gt_i[c * GT_PER_CHUNK + GT_COLS] = jnp.concatenate([rows8, pad_rows], axis=0).T

        n_col_pieces = GROUP_W // pw
        pieces = []
        for p in range(block_t // pr):
            pieces.append(functools.partial(piece_gate, p))
            for i in range(n_col_pieces):
                pieces.append(functools.partial(piece_conv, p, 2 * i))
                pieces.append(functools.partial(piece_cast, p, i, COL_RV, VV_RET))
                pieces.append(functools.partial(piece_rot, p, i, COL_RQ, cos_ref, sin_ref, qd_ref,
                                                RQK_Q, RQK_QDEC))
                pieces.append(functools.partial(piece_rz, p, i))
                pieces.append(functools.partial(piece_conv, p, 2 * i + 1))
                pieces.append(functools.partial(piece_cast, p, i, COL_MV, VV_M))
                pieces.append(functools.partial(piece_rot, p, i, COL_RK, kcos_ref, ksin_ref, kd_ref,
                                                RQK_K, RQK_KDEC))
                pieces.append(functools.partial(piece_moz, p, i))
        n_pieces = len(pieces)
        n_slots = EMITS_PER_CHUNK * n_chunks
        slots_done = [0]

        def emit_proj_pieces():
            slots_done[0] += 1
            target = -(-(slots_done[0] * (n_pieces - TAIL_PIECES)) // n_slots)
            while n_pieces - len(pieces) < target:
                pieces.pop(0)()

        def per_head(group, fn):
            out = {}
            for h in group:
                emit_proj_pieces()
                out[h] = fn(h)
            return out

        head_groups = [range(h0, h0 + HEADS_PER_STAGE) for h0 in range(0, N_HEADS, HEADS_PER_STAGE)]

        for c in range(n_chunks):
            rows = slice(c * CHUNK, (c + 1) * CHUNK)

            for grp in head_groups:
                q_bf = {h: rqk_c[rows, hcols(RQK_Q, h)] for h in grp}
                k_bf = {h: rqk_c[rows, hcols(RQK_K, h)] for h in grp}
                v_bf = {h: vv_c[rows, hcols(VV_RET, h)] for h in grp}
                sc = per_head(grp, lambda h: _dot_nt(q_bf[h], k_bf[h]))
                state = {h: s_scr[h] for h in grp}
                upd = per_head(grp, lambda h: _dot_tn(rqk_c[rows, hcols(RQK_KDEC, h)], v_bf[h]))
                for h in grp:
                    s_scr[h] = state[h] * cdec_ref[h] + upd[h]
                o = per_head(grp, lambda h: _dot(
                    jnp.concatenate([(sc[h] * decay_ref[h]).astype(BF16),
                                     rqk_c[rows, hcols(RQK_QDEC, h)]], axis=1),
                    jnp.concatenate([v_bf[h], state[h].astype(BF16)], axis=0)))

                def ret_out(h):
                    hn = _head_norm(o[h], g_ret_ref[:, hcols(0, h)])
                    mix_scr[rows, hcols(0, h)] = (hn * gz_c[rows, hcols(GZ_RET, h)]).astype(BF16)

                per_head(grp, ret_out)

            cols_t = gt_c[c * GT_PER_CHUNK + GT_COLS]
            rows_t = gt_c[c * GT_PER_CHUNK + GT_ROWS, 0:GT_ROWS_USED, :]
            for grp in head_groups:
                q = {h: qkm_c[rows, hcols(0, h)] for h in grp}
                k = {h: qkm_c[rows, hcols(GROUP_W, h)] for h in grp}
                vaug_bf = {h: jnp.concatenate([vv_c[rows, hcols(VV_M, h)], ones_col], axis=1)
                           for h in grp}
                qk = per_head(grp, lambda h: _dot_nt(q[h].astype(BF16), k[h].astype(BF16)))
                caug = {h: caug_scr[h] for h in grp}
                m_old = {h: m_scr[h, 0:1, 0:1] for h in grp}
                i_row = {h: rows_t[h:h + 1, :] for h in grp}
                b_row = {h: rows_t[N_HEADS + h:N_HEADS + h + 1, :] for h in grp}
                i_col = {h: cols_t[:, h:h + 1] for h in grp}
                b_col = {h: cols_t[:, N_HEADS + h:N_HEADS + h + 1] for h in grp}
                a_col = {h: b_col[h] + m_old[h] for h in grp}
                dm = per_head(grp, lambda h: (b_col[h] + (i_row[h] - b_row[h])) + causal_add)
                mt = {h: jnp.maximum(a_col[h], jnp.max(dm[h], axis=-1, keepdims=True)) for h in grp}
                w_inter = {h: jnp.exp(a_col[h] - mt[h]) for h in grp}
                res = per_head(grp, lambda h: _dot(
                    jnp.concatenate([(qk[h] * jnp.exp(dm[h] - mt[h])).astype(BF16),
                                     (q[h] * w_inter[h]).astype(BF16)], axis=1),
                    jnp.concatenate([vaug_bf[h], caug[h].astype(BF16)], axis=0)))
                b_last = {h: b_col[h][CHUNK - 1:CHUNK, :] for h in grp}
                g_col = {h: b_last[h] - b_col[h] + i_col[h] for h in grp}
                m_new = {h: jnp.maximum(b_last[h] + m_old[h], jnp.max(g_col[h], axis=0, keepdims=True))
                         for h in grp}
                wk = {h: jnp.exp(g_col[h] - m_new[h]) for h in grp}
                wc = {h: jnp.exp(b_last[h] + m_old[h] - m_new[h]) for h in grp}

                def mlstm_update(h):
                    caug_scr[h] = caug[h] * wc[h] + _dot_tn((k[h] * wk[h]).astype(BF16), vaug_bf[h])
                    m_scr[h] = jnp.broadcast_to(m_new[h], (SUBLANES, LANES))

                per_head(grp, mlstm_update)

                def mlstm_out(h):
                    den = jnp.maximum(jnp.abs(res[h][:, HEAD_DIM:HEAD_DIM + 1]), jnp.exp(-mt[h]))
                    hn = _head_norm(res[h][:, :HEAD_DIM] / den, g_m_ref[:, hcols(0, h)])
                    mix_scr[rows, hcols(GROUP_W, h)] = (hn * gz_c[rows, hcols(GZ_M, h)]).astype(BF16)

                per_head(grp, mlstm_out)

            emit_proj_pieces()
            mix = _dot(mix_scr[rows, :], w_out_ref[...])
            y_ref[0, rows, :] = _layer_norm(ALPHA * x_ref[0, rows, :] + mix, ln_g_ref[...], ln_b_ref[...])

        assert slots_done[0] == n_slots and len(pieces) == TAIL_PIECES
        while pieces:
            pieces.pop(0)()

    set_a = (rqk_a, vv_a, gz_a, qkm_a, gt_a)
    set_b = (rqk_b, vv_b, gz_b, qkm_b, gt_b)
    parity = lax.rem(g, 2)

    @pl.when(parity == 0)
    def _even_step():
        step_body(set_a, set_b)

    @pl.when(parity == 1)
    def _odd_step():
        step_body(set_b, set_a)

    @pl.when(jnp.logical_and(tn == nt - 1, g < n_blocks))
    def _write_conv_state():
        if layer > 0:
            conv_out_ref[0:layer] = prev_refs[-1][...]
        conv_out_ref[layer, 0] = u_scr[SUBLANES + block_t - carry_rows:SUBLANES + block_t, :]

    @pl.when(jnp.logical_and(t == nt - 1, g > 0))
    def _write_state():
        if layer > 0:
            for out_ref, prev_ref in zip((s_out_ref, c_out_ref, n_out_ref, m_out_ref), prev_refs):
                out_ref[0:layer] = prev_ref[...]
        s_out_ref[layer, 0] = s_scr[...]
        m_out_ref[layer, 0] = jnp.zeros((SUBLANES, LANES), F32)
        for h in heads:
            caug = caug_scr[h]
            c_out_ref[layer, 0, h] = caug[:, :HEAD_DIM]
            n_out_ref[layer, 0, h:h + 1, :] = caug[:, HEAD_DIM:].T[0:1, :]
            m_out_ref[layer, 0, h:h + 1, :] = m_scr[h, 0:1, :]

    u_scr[0:SUBLANES, :] = u_scr[block_t:block_t + SUBLANES, :]


def _const_spec(shape):
    nd = len(shape)
    return pl.BlockSpec(shape, lambda g: (0,) * nd)


def _layer_spec(shape, layer):
    return pl.BlockSpec((None,) + tuple(shape[1:]), lambda g: (layer,) + (0,) * (len(shape) - 1))


def _prompt_layer(layer, x, w_in_bf, conv_w, conv_b, gbias, g_ret, g_m, w_out_bf, ln_g, ln_b, tabs,
                  prev_states):
    B, T, _ = x.shape
    bt = PROMPT_BLOCK_T
    assert T % bt == 0 and bt % CHUNK == 0 and GROUP_W % PROJ_PIECE_COLS == 0
    nt = T // bt
    n_blocks = B * nt
    n_chunks = bt // CHUNK
    cos_t, sin_t, kcos_t, ksin_t, decay, qd, kd, cdec, triu, mask, e0 = tabs

    def nxt(g):
        return jnp.minimum(g, n_blocks - 1)

    def cur(g):
        return jnp.maximum(g - 1, 0)

    row_spec = pl.BlockSpec((bt, LANES), lambda g: (nxt(g) % nt, 0))
    in_specs = [
        pl.BlockSpec((1, bt, D_MODEL), lambda g: (nxt(g) // nt, nxt(g) % nt, 0)),
        pl.BlockSpec((1, bt, D_MODEL), lambda g: (cur(g) // nt, cur(g) % nt, 0)),
        _layer_spec(w_in_bf.shape, layer), _layer_spec(conv_w.shape, layer), _layer_spec(conv_b.shape, layer),
        _layer_spec(gbias.shape, layer), _layer_spec(g_ret.shape, layer), _layer_spec(g_m.shape, layer),
        _layer_spec(w_out_bf.shape, layer), _layer_spec(ln_g.shape, layer), _layer_spec(ln_b.shape, layer),
        row_spec, row_spec, row_spec, row_spec,
        _const_spec(decay.shape), _const_spec(qd.shape), _const_spec(kd.shape),
        pl.BlockSpec(memory_space=pltpu.SMEM),
        _const_spec(triu.shape), _const_spec(mask.shape), _const_spec(e0.shape),
    ]
    state_tails = ((N_HEADS, HEAD_DIM, HEAD_DIM), (N_HEADS, HEAD_DIM, HEAD_DIM), (N_HEADS, HEAD_DIM),
                   (SUBLANES, LANES), (CONV_W - 1, 2 * GROUP_W))
    assert len(state_tails) == N_STATE_OUTPUTS and len(prev_states) in (0, N_STATE_OUTPUTS)

    def state_spec(depth, tail, batch_of):
        zeros = (0,) * len(tail)
        return pl.BlockSpec((depth, 1) + tail, lambda g: (0, batch_of(g)) + zeros)

    batch_ofs = [lambda g: cur(g) // nt] * (N_STATE_OUTPUTS - 1) + [lambda g: nxt(g) // nt]
    out_shape = (jax.ShapeDtypeStruct((B, T, D_MODEL), F32),) + tuple(
        jax.ShapeDtypeStruct((layer + 1, B) + tail, F32) for tail in state_tails)
    out_specs = (pl.BlockSpec((1, bt, D_MODEL), lambda g: (cur(g) // nt, cur(g) % nt, 0)),) + tuple(
        state_spec(layer + 1, tail, bo) for tail, bo in zip(state_tails, batch_ofs))
    if prev_states:
        in_specs = in_specs + [state_spec(layer, tail, bo) for tail, bo in zip(state_tails, batch_ofs)]
    operand_set = [
        pltpu.VMEM((bt, 4 * GROUP_W), BF16),
        pltpu.VMEM((bt, 2 * GROUP_W), BF16),
        pltpu.VMEM((bt, 2 * GROUP_W), F32),
        pltpu.VMEM((bt, 2 * GROUP_W), F32),
        pltpu.VMEM((n_chunks * GT_PER_CHUNK, CHUNK, LANES), F32),
    ]
    scratch = operand_set + operand_set + [
        pltpu.VMEM((bt, D_MODEL), BF16),
        pltpu.VMEM((SUBLANES + bt, 2 * GROUP_W), F32),
        pltpu.VMEM((bt, 2 * GROUP_W), BF16),
        pltpu.VMEM((N_HEADS, HEAD_DIM, HEAD_DIM), F32),
        pltpu.VMEM((N_HEADS, HEAD_DIM, 2 * HEAD_DIM), F32),
        pltpu.VMEM((N_HEADS, SUBLANES, LANES), F32),
    ]
    y, *states = pl.pallas_call(
        functools.partial(_prompt_kernel, nt, n_blocks, layer),
        grid=(n_blocks + 1,),
        in_specs=in_specs,
        out_specs=out_specs,
        out_shape=out_shape,
        scratch_shapes=scratch,
        compiler_params=pltpu.CompilerParams(
            dimension_semantics=("arbitrary",),
            vmem_limit_bytes=VMEM_LIMIT_BYTES),
        name="prompt_layer",
    )(x, x, w_in_bf, conv_w, conv_b, gbias, g_ret, g_m, w_out_bf, ln_g, ln_b,
      cos_t, sin_t, kcos_t, ksin_t, decay, qd, kd, cdec, triu, mask, e0, *prev_states)
    return y, states


def _sample_kernel(x_ref, w_in_ref, conv_w_ref, conv_b_ref, gbias_ref, g_ret_ref, g_m_ref,
                   w_out_ref, ln_g_ref, ln_b_ref, rot_ref, gam_ref, esel_ref,
                   s_ref, c_ref, n_ref, m_ref, cv_ref,
                   y_ref, s_out_ref, c_out_ref, n_out_ref, m_out_ref, cv_out_ref,
                   proj_scr, xcur_scr, mix_scr):
    layer = pl.program_id(0)
    j = pl.program_id(1)
    bb = s_ref.shape[1]

    @pl.when(jnp.logical_and(layer == 0, j == 0))
    def _load_x():
        xcur_scr[...] = x_ref[...]

    @pl.when(j == 0)
    def _project():
        xb = xcur_scr[...].astype(BF16)
        for lo in range(0, N_PAD, 512):
            hi = min(lo + 512, N_PAD)
            proj_scr[:, lo:hi] = _dot(xb, w_in_ref[0, :, lo:hi])

    r0 = pl.multiple_of(j * bb, bb)
    rows = pl.ds(r0, bb)
    cos_t = rot_ref[0:1, :]
    sin_t = rot_ref[1:2, :]
    kcos_t = rot_ref[2:3, :]
    ksin_t = rot_ref[3:4, :]
    esel = esel_ref[...]
    row_id = lax.broadcasted_iota(jnp.int32, (bb, bb * HEAD_DIM), 0)
    blk_id = jnp.right_shift(lax.broadcasted_iota(jnp.int32, (bb, bb * HEAD_DIM), 1), 7)
    pad_rows = jnp.zeros((LANES - bb, HEAD_DIM), F32)
    pad_wide = jnp.zeros((LANES - bb, bb * HEAD_DIM), BF16)

    def col_form(x8):
        return jnp.concatenate([x8, pad_rows], axis=0).T.astype(BF16)

    def outer_all(k8, v8):
        vt = jnp.concatenate([v8] * bb, axis=1)
        vsel = jnp.where(row_id == blk_id, vt, 0.0).astype(BF16)
        return _dot(col_form(k8), jnp.concatenate([vsel, pad_wide], axis=0))

    def col_bcast_all(q8):
        return _dot(col_form(q8), esel)

    for h in range(N_HEADS):
        hc = h * HEAD_DIM
        q8 = _rotary(proj_scr[rows, COL_RQ + hc:COL_RQ + hc + HEAD_DIM], cos_t, sin_t)
        k8 = _rotary(proj_scr[rows, COL_RK + hc:COL_RK + hc + HEAD_DIM], kcos_t, ksin_t)
        v8 = proj_scr[rows, COL_RV + hc:COL_RV + hc + HEAD_DIM]
        kv_all = outer_all(k8, v8)
        qc_all = col_bcast_all(q8)
        gamma = gam_ref[h]
        o_rows = []
        for r in range(bb):
            blk = slice(r * HEAD_DIM, (r + 1) * HEAD_DIM)
            s_new = s_ref[0, r, h] * gamma + kv_all[:, blk]
            s_out_ref[0, r, h] = s_new
            o_rows.append(jnp.sum(qc_all[:, blk] * s_new, axis=0, keepdims=True))
        o8 = _head_norm(jnp.concatenate(o_rows, axis=0), g_ret_ref[0, :, hc:hc + HEAD_DIM])
        z8 = proj_scr[rows, COL_RZ + hc:COL_RZ + hc + HEAD_DIM]
        mix_scr[rows, hc:hc + HEAD_DIM] = o8 * _silu(z8)

    u8 = proj_scr[rows, COL_MQK:COL_MQK + 2 * GROUP_W]
    acc = conv_b_ref[0] + u8 * conv_w_ref[0, CONV_W - 1:CONV_W, :]
    for jj in range(CONV_W - 1):
        acc = acc + cv_ref[0, jj] * conv_w_ref[0, jj:jj + 1, :]
    for jj in range(1, CONV_W - 1):
        cv_out_ref[0, jj - 1] = cv_ref[0, jj]
    cv_out_ref[0, CONV_W - 2] = u8
    qk8 = _silu(acc)

    gates = proj_scr[rows, COL_GATE:COL_GATE + LANES] + gbias_ref[0]
    i_al = pltpu.roll(gates, N_HEADS, axis=1)
    bm = _log_sigmoid(gates) + m_ref[0]
    m_new = jnp.maximum(bm, i_al)
    wk = jnp.exp(i_al - m_new)
    wc = jnp.exp(bm - m_new)
    einv = jnp.exp(-m_new)
    m_out_ref[0] = m_new
    for h in range(N_HEADS):
        hc = h * HEAD_DIM
        gl = N_HEADS + h
        q8 = qk8[:, hc:hc + HEAD_DIM]
        k8 = qk8[:, GROUP_W + hc:GROUP_W + hc + HEAD_DIM] * QK_SCALE
        v8 = proj_scr[rows, COL_MV + hc:COL_MV + hc + HEAD_DIM]
        wk_h = wk[:, gl:gl + 1]
        wc_h = jnp.broadcast_to(wc[:, gl:gl + 1], (bb, HEAD_DIM))
        kw8 = k8 * wk_h
        kv_all = outer_all(kw8, v8)
        qc_all = col_bcast_all(q8)
        n_new = n_ref[0, :, h, :] * wc_h + kw8
        n_out_ref[0, :, h, :] = n_new
        num_rows = []
        for r in range(bb):
            blk = slice(r * HEAD_DIM, (r + 1) * HEAD_DIM)
            c_new = c_ref[0, r, h] * wc_h[r:r + 1, :] + kv_all[:, blk]
            c_out_ref[0, r, h] = c_new
            num_rows.append(jnp.sum(qc_all[:, blk] * c_new, axis=0, keepdims=True))
        num = jnp.concatenate(num_rows, axis=0)
        q_bf = q8.astype(BF16).astype(F32)
        den = jnp.sum(q_bf * n_new, axis=-1, keepdims=True)
        hout = num / jnp.maximum(jnp.abs(den), einv[:, gl:gl + 1])
        hn = _head_norm(hout, g_m_ref[0, :, hc:hc + HEAD_DIM])
        og = _sigmoid(proj_scr[rows, COL_MO + hc:COL_MO + hc + HEAD_DIM])
        zg = _silu(proj_scr[rows, COL_MZ + hc:COL_MZ + hc + HEAD_DIM])
        mix_scr[rows, GROUP_W + hc:GROUP_W + hc + HEAD_DIM] = hn * og * zg

    @pl.when(j == pl.num_programs(1) - 1)
    def _finish_layer():
        mix = _dot(mix_scr[...].astype(BF16), w_out_ref[0])
        y = _layer_norm(ALPHA * xcur_scr[...] + mix, ln_g_ref[0], ln_b_ref[0])
        xcur_scr[...] = y
        y_ref[...] = y


def _sample_layers(x, w_in_bf, conv_w, conv_b, gbias, g_ret, g_m, w_out_bf, ln_g, ln_b,
                   rot, gam, esel, state_ret, state_c, state_n, m_pad, state_conv):
    bs = x.shape[0]
    bb = SAMPLE_BLOCK_B
    assert bs % bb == 0
    nb = bs // bb

    def lspec(shape, buffers=None):
        nd = len(shape)
        mode = None if buffers is None else pl.Buffered(buffers)
        return pl.BlockSpec((1,) + tuple(shape[1:]), lambda l, j: (l,) + (0,) * (nd - 1),
                            pipeline_mode=mode)

    def cspec(shape):
        nd = len(shape)
        return pl.BlockSpec(tuple(shape), lambda l, j: (0,) * nd)

    mat_spec = pl.BlockSpec((1, bb, N_HEADS, HEAD_DIM, HEAD_DIM), lambda l, j: (l, j, 0, 0, 0))
    n_spec = pl.BlockSpec((1, bb, N_HEADS, HEAD_DIM), lambda l, j: (l, j, 0, 0))
    m_spec = pl.BlockSpec((1, bb, LANES), lambda l, j: (l, j, 0))
    cv_spec = pl.BlockSpec((1, CONV_W - 1, bb, 2 * GROUP_W), lambda l, j: (l, 0, j, 0))
    in_specs = [
        cspec(x.shape), lspec(w_in_bf.shape, 1), lspec(conv_w.shape), lspec(conv_b.shape),
        lspec(gbias.shape), lspec(g_ret.shape), lspec(g_m.shape), lspec(w_out_bf.shape, 1),
        lspec(ln_g.shape), lspec(ln_b.shape), cspec(rot.shape),
        pl.BlockSpec(memory_space=pltpu.SMEM), cspec(esel.shape),
        mat_spec, mat_spec, n_spec, m_spec, cv_spec,
    ]
    out_shape = (
        jax.ShapeDtypeStruct(x.shape, F32),
        jax.ShapeDtypeStruct(state_ret.shape, F32),
        jax.ShapeDtypeStruct(state_c.shape, F32),
        jax.ShapeDtypeStruct(state_n.shape, F32),
        jax.ShapeDtypeStruct(m_pad.shape, F32),
        jax.ShapeDtypeStruct(state_conv.shape, F32),
    )
    out_specs = (cspec(x.shape), mat_spec, mat_spec, n_spec, m_spec, cv_spec)
    scratch = [
        pltpu.VMEM((bs, N_PAD), F32),
        pltpu.VMEM((bs, D_MODEL), F32),
        pltpu.VMEM((bs, 2 * GROUP_W), F32),
    ]
    return pl.pallas_call(
        _sample_kernel,
        grid=(DEPTH, nb),
        in_specs=in_specs,
        out_specs=out_specs,
        out_shape=out_shape,
        scratch_shapes=scratch,
        compiler_params=pltpu.CompilerParams(
            dimension_semantics=("arbitrary", "arbitrary"),
            vmem_limit_bytes=VMEM_LIMIT_BYTES),
        name="sample_layers",
    )(x, w_in_bf, conv_w, conv_b, gbias, g_ret, g_m, w_out_bf, ln_g, ln_b,
      rot, gam, esel, state_ret, state_c, state_n, m_pad, state_conv)


def _cast_weight_kernel(w_ref, o_ref):
    o_ref[0] = w_ref[0].astype(BF16)


def _cast_weight(w):
    depth, k, n = w.shape
    rows = WEIGHT_CAST_ROWS
    assert k % rows == 0
    return pl.pallas_call(
        _cast_weight_kernel,
        grid=(depth, k // rows),
        in_specs=[pl.BlockSpec((1, rows, n), lambda l, r: (l, r, 0))],
        out_specs=pl.BlockSpec((1, rows, n), lambda l, r: (l, r, 0)),
        out_shape=jax.ShapeDtypeStruct((depth, k, n), BF16),
        compiler_params=pltpu.CompilerParams(dimension_semantics=("parallel", "parallel")),
        name="cast_weight",
    )(w)


def _cast_transposed_weight_kernel(n_valid, wt_ref, o_ref):
    rows = wt_ref.shape[1]
    row_id = pl.program_id(1) * rows + lax.broadcasted_iota(jnp.int32, wt_ref.shape[1:], 0)
    wt = jnp.where(row_id < n_valid, wt_ref[0], 0.0)
    o_ref[0] = wt.astype(BF16).T


def _cast_transposed_weight(wt, n_out):
    depth, n, k = wt.shape
    rows = WEIGHT_CAST_ROWS
    assert n_out % LANES == 0 and 0 <= n_out - n < LANES
    return pl.pallas_call(
        functools.partial(_cast_transposed_weight_kernel, n),
        grid=(depth, pl.cdiv(n_out, rows)),
        in_specs=[pl.BlockSpec((1, rows, k), lambda l, r: (l, r, 0))],
        out_specs=pl.BlockSpec((1, k, rows), lambda l, r: (l, 0, r)),
        out_shape=jax.ShapeDtypeStruct((depth, k, n_out), BF16),
        compiler_params=pltpu.CompilerParams(dimension_semantics=("parallel", "parallel")),
        name="cast_weight_t",
    )(wt)


def _rotary_tables(pos):
    half = HEAD_DIM // 2
    inv = np.float64(ROPE_BASE) ** (-np.arange(half, dtype=np.float64) / half)
    ang = pos.astype(np.float64)[:, None] * inv[None, :]
    cos = np.cos(ang)
    sin = np.sin(ang)
    cos_t = np.concatenate([cos, cos], axis=-1)
    sin_t = np.concatenate([-sin, sin], axis=-1)
    tabs = (cos_t, sin_t, cos_t * QK_SCALE, sin_t * QK_SCALE)
    return tuple(t.astype(np.float32) for t in tabs)


def _retention_tables():
    L = CHUNK
    f32 = np.float32
    log_gamma = np.log(1.0 - 2.0 ** (-5.0 - np.arange(N_HEADS, dtype=np.float64)))
    idx = np.arange(L, dtype=np.float64)
    diff = idx[:, None] - idx[None, :]
    decay = (np.exp(log_gamma[:, None, None] * np.maximum(diff, 0.0)) * (diff >= 0)).astype(f32)
    q_decay = np.exp(log_gamma[:, None] * (idx + 1.0)).astype(f32)
    k_decay = np.exp(log_gamma[:, None] * (L - 1.0 - idx)).astype(f32)
    c_decay = np.exp(log_gamma * L).astype(f32)
    qd = np.ascontiguousarray(np.broadcast_to(q_decay[:, :, None], (N_HEADS, L, HEAD_DIM)))
    kd = np.ascontiguousarray(np.broadcast_to(k_decay[:, :, None], (N_HEADS, L, HEAD_DIM)))
    gamma1 = np.exp(log_gamma).astype(f32)
    return decay, qd, kd, c_decay, gamma1


def kernel(x_prompt, x_sample, state_ret, state_mlstm_C, state_mlstm_n, state_mlstm_m, state_conv,
           w_in, conv_w, conv_b, b_i, b_f, g_ret, g_m, w_out, ln_g, ln_b):
    B, T, _ = x_prompt.shape
    Bs, Ts, _ = x_sample.shape
    assert Ts == 1

    w_in_bf = _cast_transposed_weight(jnp.swapaxes(w_in, 1, 2), N_PAD)
    w_out_bf = _cast_weight(w_out)
    gbias = jnp.pad(jnp.concatenate([b_i, b_f], axis=-1), ((0, 0), (0, LANES - 2 * N_HEADS)))
    gbias = gbias.reshape(DEPTH, 1, LANES)
    conv_b3 = conv_b.reshape(DEPTH, 1, 2 * GROUP_W)
    g_ret3 = g_ret.reshape(DEPTH, 1, GROUP_W)
    g_m3 = g_m.reshape(DEPTH, 1, GROUP_W)
    ln_g3 = ln_g.reshape(DEPTH, 1, D_MODEL)
    ln_b3 = ln_b.reshape(DEPTH, 1, D_MODEL)

    decay, qd, kd, c_decay, gamma1 = _retention_tables()
    idx = np.arange(CHUNK)
    causal = idx[:, None] >= idx[None, :]
    triu = jnp.asarray(causal.T, BF16)
    mask_add = np.where(causal, 0.0, -np.inf).astype(np.float32)
    e0 = jnp.asarray(np.broadcast_to(np.arange(LANES)[None, :] == 0, (CHUNK, LANES)), BF16)
    tabs_p = _rotary_tables(np.arange(T)) + (decay, qd, kd, c_decay, triu, mask_add, e0)
    tabs_p = tuple(jnp.asarray(a) for a in tabs_p)

    xp = x_prompt
    prompt_states = []
    for l in range(DEPTH):
        xp, prompt_states = _prompt_layer(
            l, xp, w_in_bf, conv_w, conv_b3, gbias, g_ret3, g_m3, w_out_bf, ln_g3, ln_b3, tabs_p,
            prompt_states)
    rp, cp, np_, mp_pad, vp = prompt_states
    mp = mp_pad[:, :, :N_HEADS, 0]

    rot = jnp.asarray(np.concatenate(_rotary_tables(PAST_LEN + np.arange(Ts)), axis=0))
    bb = SAMPLE_BLOCK_B
    esel = jnp.asarray(np.arange(LANES)[:, None] == (np.arange(bb * HEAD_DIM)[None, :] // HEAD_DIM), BF16)
    gamma1 = jnp.asarray(gamma1)
    m_pad = jnp.pad(state_mlstm_m, ((0, 0), (0, 0), (N_HEADS, LANES - 2 * N_HEADS)))
    ys, rs, cs, ns, ms_pad, vs = _sample_layers(
        x_sample.reshape(Bs, D_MODEL), w_in_bf, conv_w, conv_b3, gbias, g_ret3, g_m3, w_out_bf,
        ln_g3, ln_b3, rot, gamma1, esel, state_ret, state_mlstm_C, state_mlstm_n, m_pad,
        jnp.swapaxes(state_conv, 1, 2))
    vs = jnp.swapaxes(vs, 1, 2)
    ms = ms_pad[:, :, N_HEADS:2 * N_HEADS]

    return (xp, ys.reshape(Bs, Ts, D_MODEL),
            rp, cp, np_, mp, vp, rs, cs, ns, ms, vs)
```

```python
import functools

import jax
import jax.numpy as jnp
import numpy as np
from jax import lax
from jax.experimental import pallas as pl
from jax.experimental.pallas import tpu as pltpu

F32 = jnp.float32
BF16 = jnp.bfloat16

D_MODEL = 1024
DEPTH = 2
PAST_LEN = 16384
N_HEADS = 4
HEAD_DIM = 128
GROUP_W = N_HEADS * HEAD_DIM
CONV_W = 4
CHUNK = 128
ROPE_BASE = 10000.0
LN_EPS = 1e-5
GN_EPS = 1e-5
ALPHA = (2 * DEPTH) ** 0.25
QK_SCALE = HEAD_DIM ** -0.5

LANES = 128
SUBLANES = 8

COL_RQ = 0
COL_RK = COL_RQ + GROUP_W
COL_RV = COL_RK + GROUP_W
COL_RZ = COL_RV + GROUP_W
COL_MQK = COL_RZ + GROUP_W
COL_MV = COL_MQK + 2 * GROUP_W
COL_MO = COL_MV + GROUP_W
COL_MZ = COL_MO + GROUP_W
COL_GATE = COL_MZ + GROUP_W
N_IN = COL_GATE + 2 * N_HEADS
N_PAD = COL_GATE + LANES
GATE_ROWS = 16

PROMPT_BLOCK_T = 256
PROJ_PIECE_COLS = 256
PROJ_PIECE_ROWS = 256
SAMPLE_BLOCK_B = 16
WEIGHT_CAST_ROWS = 1024
VMEM_LIMIT_BYTES = 56 * 1024 * 1024


def _sigmoid(x):
    return 1.0 / (1.0 + jnp.exp(-x))


def _silu(x):
    return x * _sigmoid(x)


def _log_sigmoid(x):
    return jnp.minimum(x, 0.0) - jnp.log1p(jnp.exp(-jnp.abs(x)))


def _dot(a, b):
    return jnp.dot(a, b, preferred_element_type=F32)


def _dot_nt(a, b):
    return lax.dot_general(a, b, (((1,), (1,)), ((), ())), preferred_element_type=F32)


def _dot_tn(a, b):
    return lax.dot_general(a, b, (((0,), (0,)), ((), ())), preferred_element_type=F32)


def _rotary(x, cos_t, sin_t):
    return x * cos_t + pltpu.roll(x, HEAD_DIM // 2, axis=1) * sin_t


def _head_norm(h, g):
    mu = jnp.mean(h, axis=-1, keepdims=True)
    d = h - mu
    var = jnp.mean(d * d, axis=-1, keepdims=True)
    return d * lax.rsqrt(var + GN_EPS) * g


def _layer_norm(x, g, b):
    mu = jnp.mean(x, axis=-1, keepdims=True)
    d = x - mu
    var = jnp.mean(d * d, axis=-1, keepdims=True)
    return d * lax.rsqrt(var + LN_EPS) * g + b


def _cumsum_lanes(triu_bf, x):
    hi = x.astype(BF16)
    r1 = x - hi.astype(F32)
    mid = r1.astype(BF16)
    lo = (r1 - mid.astype(F32)).astype(BF16)
    return _dot(hi, triu_bf) + _dot(mid, triu_bf) + _dot(lo, triu_bf)


RQK_Q, RQK_QDEC, RQK_K, RQK_KDEC = 0, GROUP_W, 2 * GROUP_W, 3 * GROUP_W
VV_RET, VV_M = 0, GROUP_W
GZ_RET, GZ_M = 0, GROUP_W
GT_COLS, GT_ROWS, GT_PER_CHUNK = 0, 1, 2
GT_ROWS_USED = SUBLANES
EMITS_PER_CHUNK = 9 * N_HEADS + 1
TAIL_PIECES = 0
N_STATE_OUTPUTS = 5
HEADS_PER_STAGE = 2


def _prompt_kernel(nt, n_blocks, layer, xn_ref, x_ref, w_in_ref, wg_ref, conv_w_ref, conv_b_ref, gbias_ref,
                   g_ret_ref, g_m_ref, w_out_ref, ln_g_ref, ln_b_ref,
                   cos_ref, sin_ref, kcos_ref, ksin_ref,
                   decay_ref, qd_ref, kd_ref, cdec_ref, triu_ref, mask_ref, e0_ref, *refs):
    prev_refs = refs[:N_STATE_OUTPUTS] if layer > 0 else ()
    refs = refs[len(prev_refs):]
    y_ref, s_out_ref, c_out_ref, n_out_ref, m_out_ref, conv_out_ref = refs[:1 + N_STATE_OUTPUTS]
    (rqk_a, vv_a, gz_a, qkm_a, gt_a, rqk_b, vv_b, gz_b, qkm_b, gt_b,
     xb_scr, u_scr, mix_scr, s_scr, caug_scr, m_scr) = refs[1 + N_STATE_OUTPUTS:]
    g = pl.program_id(0)
    t = lax.rem(jnp.maximum(g - 1, 0), nt)
    tn = lax.rem(jnp.minimum(g, n_blocks - 1), nt)
    block_t = x_ref.shape[1]
    n_chunks = block_t // CHUNK
    carry_rows = CONV_W - 1
    heads = range(N_HEADS)

    @pl.when(g == 0)
    def _init_pipeline():
        for ref in (rqk_b, vv_b, gz_b, qkm_b, gt_b):
            ref[...] = jnp.zeros_like(ref)

    @pl.when(t == 0)
    def _init_state():
        s_scr[...] = jnp.zeros_like(s_scr)
        caug_scr[...] = jnp.zeros_like(caug_scr)
        m_scr[...] = jnp.zeros_like(m_scr)

    @pl.when(tn == 0)
    def _init_conv_carry():
        u_scr[0:SUBLANES, :] = jnp.zeros((SUBLANES, 2 * GROUP_W), F32)

    def hcols(base, h):
        return slice(base + h * HEAD_DIM, base + (h + 1) * HEAD_DIM)

    def step_body(set_in, set_cur):
        rqk_i, vv_i, gz_i, qkm_i, gt_i = set_in
        rqk_c, vv_c, gz_c, qkm_c, gt_c = set_cur
        triu_bf = triu_ref[...]
        causal_add = mask_ref[...]
        ones_col = e0_ref[...]
        sub_id = lax.broadcasted_iota(jnp.int32, (SUBLANES, LANES), 0)
        pad_rows = jnp.zeros((CHUNK - GT_ROWS_USED, LANES), F32)
        xb_scr[...] = xn_ref[0].astype(BF16)
        pw = PROJ_PIECE_COLS
        heads_per_piece = pw // HEAD_DIM

        pr = PROJ_PIECE_ROWS
        chunks_per_piece = pr // CHUNK

        def group_rows(p):
            return slice(p * pr, (p + 1) * pr)

        def proj(p, col, width=pw):
            return _dot(xb_scr[group_rows(p), :], w_in_ref[:, col:col + width])

        def piece_rot(p, i, col_base, cos_r, sin_r, dec_ref, dst, dst_dec):
            res = proj(p, col_base + i * pw)
            for hh in range(heads_per_piece):
                h = i * heads_per_piece + hh
                for cc in range(chunks_per_piece):
                    rows = slice(p * pr + cc * CHUNK, p * pr + (cc + 1) * CHUNK)
                    r = _rotary(res[cc * CHUNK:(cc + 1) * CHUNK, hh * HEAD_DIM:(hh + 1) * HEAD_DIM],
                                cos_r[rows, :], sin_r[rows, :])
                    rqk_i[rows, hcols(dst, h)] = r.astype(BF16)
                    rqk_i[rows, hcols(dst_dec, h)] = (r * dec_ref[h]).astype(BF16)

        def piece_cast(p, i, col_base, dst):
            vv_i[group_rows(p), dst + i * pw:dst + (i + 1) * pw] = proj(p, col_base + i * pw).astype(BF16)

        def piece_rz(p, i):
            gz_i[group_rows(p), GZ_RET + i * pw:GZ_RET + (i + 1) * pw] = _silu(proj(p, COL_RZ + i * pw))

        def piece_moz(p, i):
            og = _sigmoid(proj(p, COL_MO + i * pw))
            gz_i[group_rows(p), GZ_M + i * pw:GZ_M + (i + 1) * pw] = og * _silu(proj(p, COL_MZ + i * pw))

        def piece_conv(p, i):
            base = SUBLANES + p * pr
            u_scr[base:base + pr, i * pw:(i + 1) * pw] = proj(p, COL_MQK + i * pw)
            for cs in range(i * pw, (i + 1) * pw, LANES):
                cols = slice(cs, cs + LANES)
                acc = conv_b_ref[:, cols]
                for j in range(CONV_W):
                    r0 = base - carry_rows + j
                    acc = acc + u_scr[r0:r0 + pr, cols] * conv_w_ref[j:j + 1, cols]
                act = _silu(acc)
                if cs >= GROUP_W:
                    act = act * QK_SCALE
                qkm_i[group_rows(p), cols] = act

        def piece_gate(p):
            gates_t = _dot_nt(wg_ref[...], xb_scr[group_rows(p), :])
            for cc in range(chunks_per_piece):
                c = p * chunks_per_piece + cc
                g8 = gates_t[0:SUBLANES, cc * CHUNK:(cc + 1) * CHUNK] + gbias_ref[...]
                rows8 = jnp.where(sub_id < N_HEADS, g8, _cumsum_lanes(triu_bf, _log_sigmoid(g8)))
                gt_i[c * GT_PER_CHUNK + GT_ROWS, 0:GT_ROWS_USED, :] = rows8
                gt_i[c * GT_PER_CHUNK + GT_COLS] = jnp.concatenate([rows8, pad_rows], axis=0).T

        n_col_pieces = GROUP_W // pw
        pieces = []
        for p in range(block_t // pr):
            pieces.append(functools.partial(piece_gate, p))
            for i in range(n_col_pieces):
                pieces.append(functools.partial(piece_conv, p, 2 * i))
                pieces.append(functools.partial(piece_cast, p, i, COL_RV, VV_RET))
                pieces.append(functools.partial(piece_rot, p, i, COL_RQ, cos_ref, sin_ref, qd_ref,
                                                RQK_Q, RQK_QDEC))
                pieces.append(functools.partial(piece_rz, p, i))
                pieces.append(functools.partial(piece_conv, p, 2 * i + 1))
                pieces.append(functools.partial(piece_cast, p, i, COL_MV, VV_M))
                pieces.append(functools.partial(piece_rot, p, i, COL_RK, kcos_ref, ksin_ref, kd_ref,
                                                RQK_K, RQK_KDEC))
                pieces.append(functools.partial(piece_moz, p, i))
        n_pieces = len(pieces)
        n_slots = EMITS_PER_CHUNK * n_chunks
        slots_done = [0]

        def emit_proj_pieces():
            slots_done[0] += 1
            target = -(-(slots_done[0] * (n_pieces - TAIL_PIECES)) // n_slots)
            while n_pieces - len(pieces) < target:
                pieces.pop(0)()

        def per_head(group, fn):
            out = {}
            for h in group:
                emit_proj_pieces()
                out[h] = fn(h)
            return out

        head_groups = [range(h0, h0 + HEADS_PER_STAGE) for h0 in range(0, N_HEADS, HEADS_PER_STAGE)]

        for c in range(n_chunks):
            rows = slice(c * CHUNK, (c + 1) * CHUNK)

            for grp in head_groups:
                q_bf = {h: rqk_c[rows, hcols(RQK_Q, h)] for h in grp}
                k_bf = {h: rqk_c[rows, hcols(RQK_K, h)] for h in grp}
                v_bf = {h: vv_c[rows, hcols(VV_RET, h)] for h in grp}
                sc = per_head(grp, lambda h: _dot_nt(q_bf[h], k_bf[h]))
                state = {h: s_scr[h] for h in grp}
                upd = per_head(grp, lambda h: _dot_tn(rqk_c[rows, hcols(RQK_KDEC, h)], v_bf[h]))
                for h in grp:
                    s_scr[h] = state[h] * cdec_ref[h] + upd[h]
                o = per_head(grp, lambda h: _dot(
                    jnp.concatenate([(sc[h] * decay_ref[h]).astype(BF16),
                                     rqk_c[rows, hcols(RQK_QDEC, h)]], axis=1),
                    jnp.concatenate([v_bf[h], state[h].astype(BF16)], axis=0)))

                def ret_out(h):
                    hn = _head_norm(o[h], g_ret_ref[:, hcols(0, h)])
                    mix_scr[rows, hcols(0, h)] = (hn * gz_c[rows, hcols(GZ_RET, h)]).astype(BF16)

                per_head(grp, ret_out)

            cols_t = gt_c[c * GT_PER_CHUNK + GT_COLS]
            rows_t = gt_c[c * GT_PER_CHUNK + GT_ROWS, 0:GT_ROWS_USED, :]
            for grp in head_groups:
                q = {h: qkm_c[rows, hcols(0, h)] for h in grp}
                k = {h: qkm_c[rows, hcols(GROUP_W, h)] for h in grp}
                vaug_bf = {h: jnp.concatenate([vv_c[rows, hcols(VV_M, h)], ones_col], axis=1)
                           for h in grp}
                qk = per_head(grp, lambda h: _dot_nt(q[h].astype(BF16), k[h].astype(BF16)))
                caug = {h: caug_scr[h] for h in grp}
                m_old = {h: m_scr[h, 0:1, 0:1] for h in grp}
                i_row = {h: rows_t[h:h + 1, :] for h in grp}
                b_row = {h: rows_t[N_HEADS + h:N_HEADS + h + 1, :] for h in grp}
                i_col = {h: cols_t[:, h:h + 1] for h in grp}
                b_col = {h: cols_t[:, N_HEADS + h:N_HEADS + h + 1] for h in grp}
                a_col = {h: b_col[h] + m_old[h] for h in grp}
                dm = per_head(grp, lambda h: (b_col[h] + (i_row[h] - b_row[h])) + causal_add)
                mt = {h: jnp.maximum(a_col[h], jnp.max(dm[h], axis=-1, keepdims=True)) for h in grp}
                w_inter = {h: jnp.exp(a_col[h] - mt[h]) for h in grp}
                res = per_head(grp, lambda h: _dot(
                    jnp.concatenate([(qk[h] * jnp.exp(dm[h] - mt[h])).astype(BF16),
                                     (q[h] * w_inter[h]).astype(BF16)], axis=1),
                    jnp.concatenate([vaug_bf[h], caug[h].astype(BF16)], axis=0)))
                b_last = {h: b_col[h][CHUNK - 1:CHUNK, :] for h in grp}
                g_col = {h: b_last[h] - b_col[h] + i_col[h] for h in grp}
                m_new = {h: jnp.maximum(b_last[h] + m_old[h], jnp.max(g_col[h], axis=0, keepdims=True))
                         for h in grp}
                wk = {h: jnp.exp(g_col[h] - m_new[h]) for h in grp}
                wc = {h: jnp.exp(b_last[h] + m_old[h] - m_new[h]) for h in grp}

                def mlstm_update(h):
                    caug_scr[h] = caug[h] * wc[h] + _dot_tn((k[h] * wk[h]).astype(BF16), vaug_bf[h])
                    m_scr[h] = jnp.broadcast_to(m_new[h], (SUBLANES, LANES))

                per_head(grp, mlstm_update)

                def mlstm_out(h):
                    den = jnp.maximum(jnp.abs(res[h][:, HEAD_DIM:HEAD_DIM + 1]), jnp.exp(-mt[h]))
                    hn = _head_norm(res[h][:, :HEAD_DIM] / den, g_m_ref[:, hcols(0, h)])
                    mix_scr[rows, hcols(GROUP_W, h)] = (hn * gz_c[rows, hcols(GZ_M, h)]).astype(BF16)

                per_head(grp, mlstm_out)

            emit_proj_pieces()
            mix = _dot(mix_scr[rows, :], w_out_ref[...])
            y_ref[0, rows, :] = _layer_norm(ALPHA * x_ref[0, rows, :] + mix, ln_g_ref[...], ln_b_ref[...])

        assert slots_done[0] == n_slots and len(pieces) == TAIL_PIECES
        while pieces:
            pieces.pop(0)()

    set_a = (rqk_a, vv_a, gz_a, qkm_a, gt_a)
    set_b = (rqk_b, vv_b, gz_b, qkm_b, gt_b)
    parity = lax.rem(g, 2)

    @pl.when(parity == 0)
    def _even_step():
        step_body(set_a, set_b)

    @pl.when(parity == 1)
    def _odd_step():
        step_body(set_b, set_a)

    @pl.when(jnp.logical_and(tn == nt - 1, g < n_blocks))
    def _write_conv_state():
        if layer > 0:
            conv_out_ref[0:layer] = prev_refs[-1][...]
        conv_out_ref[layer, 0] = u_scr[SUBLANES + block_t - carry_rows:SUBLANES + block_t, :]

    @pl.when(jnp.logical_and(t == nt - 1, g > 0))
    def _write_state():
        if layer > 0:
            for out_ref, prev_ref in zip((s_out_ref, c_out_ref, n_out_ref, m_out_ref), prev_refs):
                out_ref[0:layer] = prev_ref[...]
        s_out_ref[layer, 0] = s_scr[...]
        m_out_ref[layer, 0] = jnp.zeros((SUBLANES, LANES), F32)
        for h in heads:
            caug = caug_scr[h]
            c_out_ref[layer, 0, h] = caug[:, :HEAD_DIM]
            n_out_ref[layer, 0, h:h + 1, :] = caug[:, HEAD_DIM:].T[0:1, :]
            m_out_ref[layer, 0, h:h + 1, :] = m_scr[h, 0:1, :]

    u_scr[0:SUBLANES, :] = u_scr[block_t:block_t + SUBLANES, :]


def _const_spec(shape):
    nd = len(shape)
    return pl.BlockSpec(shape, lambda g: (0,) * nd)


def _layer_spec(shape, layer):
    return pl.BlockSpec((None,) + tuple(shape[1:]), lambda g: (layer,) + (0,) * (len(shape) - 1))


def _prompt_layer(layer, x, w_in_bf, wg_bf, conv_w, conv_b, gbias, g_ret, g_m, w_out_bf, ln_g, ln_b, tabs,
                  prev_states):
    B, T, _ = x.shape
    bt = PROMPT_BLOCK_T
    assert T % bt == 0 and bt % CHUNK == 0 and GROUP_W % PROJ_PIECE_COLS == 0
    nt = T // bt
    n_blocks = B * nt
    n_chunks = bt // CHUNK
    cos_t, sin_t, kcos_t, ksin_t, decay, qd, kd, cdec, triu, mask, e0 = tabs

    def nxt(g):
        return jnp.minimum(g, n_blocks - 1)

    def cur(g):
        return jnp.maximum(g - 1, 0)

    row_spec = pl.BlockSpec((bt, LANES), lambda g: (nxt(g) % nt, 0))
    in_specs = [
        pl.BlockSpec((1, bt, D_MODEL), lambda g: (nxt(g) // nt, nxt(g) % nt, 0)),
        pl.BlockSpec((1, bt, D_MODEL), lambda g: (cur(g) // nt, cur(g) % nt, 0)),
        _layer_spec(w_in_bf.shape, layer), _layer_spec(wg_bf.shape, layer),
        _layer_spec(conv_w.shape, layer), _layer_spec(conv_b.shape, layer),
        _layer_spec(gbias.shape, layer), _layer_spec(g_ret.shape, layer), _layer_spec(g_m.shape, layer),
        _layer_spec(w_out_bf.shape, layer), _layer_spec(ln_g.shape, layer), _layer_spec(ln_b.shape, layer),
        row_spec, row_spec, row_spec, row_spec,
        _const_spec(decay.shape), _const_spec(qd.shape), _const_spec(kd.shape),
        pl.BlockSpec(memory_space=pltpu.SMEM),
        _const_spec(triu.shape), _const_spec(mask.shape), _const_spec(e0.shape),
    ]
    state_tails = ((N_HEADS, HEAD_DIM, HEAD_DIM), (N_HEADS, HEAD_DIM, HEAD_DIM), (N_HEADS, HEAD_DIM),
                   (SUBLANES, LANES), (CONV_W - 1, 2 * GROUP_W))
    assert len(state_tails) == N_STATE_OUTPUTS and len(prev_states) in (0, N_STATE_OUTPUTS)

    def state_spec(depth, tail, batch_of):
        zeros = (0,) * len(tail)
        return pl.BlockSpec((depth, 1) + tail, lambda g: (0, batch_of(g)) + zeros)

    batch_ofs = [lambda g: cur(g) // nt] * (N_STATE_OUTPUTS - 1) + [lambda g: nxt(g) // nt]
    out_shape = (jax.ShapeDtypeStruct((B, T, D_MODEL), F32),) + tuple(
        jax.ShapeDtypeStruct((layer + 1, B) + tail, F32) for tail in state_tails)
    out_specs = (pl.BlockSpec((1, bt, D_MODEL), lambda g: (cur(g) // nt, cur(g) % nt, 0)),) + tuple(
        state_spec(layer + 1, tail, bo) for tail, bo in zip(state_tails, batch_ofs))
    if prev_states:
        in_specs = in_specs + [state_spec(layer, tail, bo) for tail, bo in zip(state_tails, batch_ofs)]
    operand_set = [
        pltpu.VMEM((bt, 4 * GROUP_W), BF16),
        pltpu.VMEM((bt, 2 * GROUP_W), BF16),
        pltpu.VMEM((bt, 2 * GROUP_W), F32),
        pltpu.VMEM((bt, 2 * GROUP_W), F32),
        pltpu.VMEM((n_chunks * GT_PER_CHUNK, CHUNK, LANES), F32),
    ]
    scratch = operand_set + operand_set + [
        pltpu.VMEM((bt, D_MODEL), BF16),
        pltpu.VMEM((SUBLANES + bt, 2 * GROUP_W), F32),
        pltpu.VMEM((bt, 2 * GROUP_W), BF16),
        pltpu.VMEM((N_HEADS, HEAD_DIM, HEAD_DIM), F32),
        pltpu.VMEM((N_HEADS, HEAD_DIM, 2 * HEAD_DIM), F32),
        pltpu.VMEM((N_HEADS, SUBLANES, LANES), F32),
    ]
    y, *states = pl.pallas_call(
        functools.partial(_prompt_kernel, nt, n_blocks, layer),
        grid=(n_blocks + 1,),
        in_specs=in_specs,
        out_specs=out_specs,
        out_shape=out_shape,
        scratch_shapes=scratch,
        compiler_params=pltpu.CompilerParams(
            dimension_semantics=("arbitrary",),
            vmem_limit_bytes=VMEM_LIMIT_BYTES),
        name="prompt_layer",
    )(x, x, w_in_bf, wg_bf, conv_w, conv_b, gbias, g_ret, g_m, w_out_bf, ln_g, ln_b,
      cos_t, sin_t, kcos_t, ksin_t, decay, qd, kd, cdec, triu, mask, e0, *prev_states)
    return y, states


def _sample_kernel(x_ref, w_in_ref, wg_ref, conv_w_ref, conv_b_ref, gbias_ref, g_ret_ref, g_m_ref,
                   w_out_ref, ln_g_ref, ln_b_ref, rot_ref, gam_ref, esel_ref,
                   s_ref, c_ref, n_ref, m_ref, cv_ref,
                   y_ref, s_out_ref, c_out_ref, n_out_ref, m_out_ref, cv_out_ref,
                   proj_scr, xcur_scr, mix_scr):
    layer = pl.program_id(0)
    j = pl.program_id(1)
    bb = s_ref.shape[1]

    @pl.when(jnp.logical_and(layer == 0, j == 0))
    def _load_x():
        xcur_scr[...] = x_ref[...]

    @pl.when(j == 0)
    def _project():
        xb = xcur_scr[...].astype(BF16)
        for lo in range(0, COL_GATE, 512):
            proj_scr[:, lo:lo + 512] = _dot(xb, w_in_ref[0, :, lo:lo + 512])
        proj_scr[:, COL_GATE:N_PAD] = jnp.zeros((proj_scr.shape[0], N_PAD - COL_GATE), F32)
        proj_scr[:, COL_GATE:COL_GATE + GATE_ROWS] = _dot_nt(xb, wg_ref[0])

    r0 = pl.multiple_of(j * bb, bb)
    rows = pl.ds(r0, bb)
    cos_t = rot_ref[0:1, :]
    sin_t = rot_ref[1:2, :]
    kcos_t = rot_ref[2:3, :]
    ksin_t = rot_ref[3:4, :]
    esel = esel_ref[...]
    row_id = lax.broadcasted_iota(jnp.int32, (bb, bb * HEAD_DIM), 0)
    blk_id = jnp.right_shift(lax.broadcasted_iota(jnp.int32, (bb, bb * HEAD_DIM), 1), 7)
    pad_rows = jnp.zeros((LANES - bb, HEAD_DIM), F32)
    pad_wide = jnp.zeros((LANES - bb, bb * HEAD_DIM), BF16)

    def col_form(x8):
        return jnp.concatenate([x8, pad_rows], axis=0).T.astype(BF16)

    def outer_all(k8, v8):
        vt = jnp.concatenate([v8] * bb, axis=1)
        vsel = jnp.where(row_id == blk_id, vt, 0.0).astype(BF16)
        return _dot(col_form(k8), jnp.concatenate([vsel, pad_wide], axis=0))

    def col_bcast_all(q8):
        return _dot(col_form(q8), esel)

    for h in range(N_HEADS):
        hc = h * HEAD_DIM
        q8 = _rotary(proj_scr[rows, COL_RQ + hc:COL_RQ + hc + HEAD_DIM], cos_t, sin_t)
        k8 = _rotary(proj_scr[rows, COL_RK + hc:COL_RK + hc + HEAD_DIM], kcos_t, ksin_t)
        v8 = proj_scr[rows, COL_RV + hc:COL_RV + hc + HEAD_DIM]
        kv_all = outer_all(k8, v8)
        qc_all = col_bcast_all(q8)
        gamma = gam_ref[h]
        o_rows = []
        for r in range(bb):
            blk = slice(r * HEAD_DIM, (r + 1) * HEAD_DIM)
            s_new = s_ref[0, r, h] * gamma + kv_all[:, blk]
            s_out_ref[0, r, h] = s_new
            o_rows.append(jnp.sum(qc_all[:, blk] * s_new, axis=0, keepdims=True))
        o8 = _head_norm(jnp.concatenate(o_rows, axis=0), g_ret_ref[0, :, hc:hc + HEAD_DIM])
        z8 = proj_scr[rows, COL_RZ + hc:COL_RZ + hc + HEAD_DIM]
        mix_scr[rows, hc:hc + HEAD_DIM] = o8 * _silu(z8)

    u8 = proj_scr[rows, COL_MQK:COL_MQK + 2 * GROUP_W]
    acc = conv_b_ref[0] + u8 * conv_w_ref[0, CONV_W - 1:CONV_W, :]
    for jj in range(CONV_W - 1):
        acc = acc + cv_ref[0, jj] * conv_w_ref[0, jj:jj + 1, :]
    for jj in range(1, CONV_W - 1):
        cv_out_ref[0, jj - 1] = cv_ref[0, jj]
    cv_out_ref[0, CONV_W - 2] = u8
    qk8 = _silu(acc)

    gates = proj_scr[rows, COL_GATE:COL_GATE + LANES] + gbias_ref[0]
    i_al = pltpu.roll(gates, N_HEADS, axis=1)
    bm = _log_sigmoid(gates) + m_ref[0]
    m_new = jnp.maximum(bm, i_al)
    wk = jnp.exp(i_al - m_new)
    wc = jnp.exp(bm - m_new)
    einv = jnp.exp(-m_new)
    m_out_ref[0] = m_new
    for h in range(N_HEADS):
        hc = h * HEAD_DIM
        gl = N_HEADS + h
        q8 = qk8[:, hc:hc + HEAD_DIM]
        k8 = qk8[:, GROUP_W + hc:GROUP_W + hc + HEAD_DIM] * QK_SCALE
        v8 = proj_scr[rows, COL_MV + hc:COL_MV + hc + HEAD_DIM]
        wk_h = wk[:, gl:gl + 1]
        wc_h = jnp.broadcast_to(wc[:, gl:gl + 1], (bb, HEAD_DIM))
        kw8 = k8 * wk_h
        kv_all = outer_all(kw8, v8)
        qc_all = col_bcast_all(q8)
        n_new = n_ref[0, :, h, :] * wc_h + kw8
        n_out_ref[0, :, h, :] = n_new
        num_rows = []
        for r in range(bb):
            blk = slice(r * HEAD_DIM, (r + 1) * HEAD_DIM)
            c_new = c_ref[0, r, h] * wc_h[r:r + 1, :] + kv_all[:, blk]
            c_out_ref[0, r, h] = c_new
            num_rows.append(jnp.sum(qc_all[:, blk] * c_new, axis=0, keepdims=True))
        num = jnp.concatenate(num_rows, axis=0)
        q_bf = q8.astype(BF16).astype(F32)
        den = jnp.sum(q_bf * n_new, axis=-1, keepdims=True)
        hout = num / jnp.maximum(jnp.abs(den), einv[:, gl:gl + 1])
        hn = _head_norm(hout, g_m_ref[0, :, hc:hc + HEAD_DIM])
        og = _sigmoid(proj_scr[rows, COL_MO + hc:COL_MO + hc + HEAD_DIM])
        zg = _silu(proj_scr[rows, COL_MZ + hc:COL_MZ + hc + HEAD_DIM])
        mix_scr[rows, GROUP_W + hc:GROUP_W + hc + HEAD_DIM] = hn * og * zg

    @pl.when(j == pl.num_programs(1) - 1)
    def _finish_layer():
        mix = _dot(mix_scr[...].astype(BF16), w_out_ref[0])
        y = _layer_norm(ALPHA * xcur_scr[...] + mix, ln_g_ref[0], ln_b_ref[0])
        xcur_scr[...] = y
        y_ref[...] = y


def _sample_layers(x, w_in_bf, wg_bf, conv_w, conv_b, gbias, g_ret, g_m, w_out_bf, ln_g, ln_b,
                   rot, gam, esel, state_ret, state_c, state_n, m_pad, state_conv):
    bs = x.shape[0]
    bb = SAMPLE_BLOCK_B
    assert bs % bb == 0
    nb = bs // bb

    def lspec(shape, buffers=None):
        nd = len(shape)
        mode = None if buffers is None else pl.Buffered(buffers)
        return pl.BlockSpec((1,) + tuple(shape[1:]), lambda l, j: (l,) + (0,) * (nd - 1),
                            pipeline_mode=mode)

    def cspec(shape):
        nd = len(shape)
        return pl.BlockSpec(tuple(shape), lambda l, j: (0,) * nd)

    mat_spec = pl.BlockSpec((1, bb, N_HEADS, HEAD_DIM, HEAD_DIM), lambda l, j: (l, j, 0, 0, 0))
    n_spec = pl.BlockSpec((1, bb, N_HEADS, HEAD_DIM), lambda l, j: (l, j, 0, 0))
    m_spec = pl.BlockSpec((1, bb, LANES), lambda l, j: (l, j, 0))
    cv_spec = pl.BlockSpec((1, CONV_W - 1, bb, 2 * GROUP_W), lambda l, j: (l, 0, j, 0))
    in_specs = [
        cspec(x.shape), lspec(w_in_bf.shape, 1), lspec(wg_bf.shape),
        lspec(conv_w.shape), lspec(conv_b.shape),
        lspec(gbias.shape), lspec(g_ret.shape), lspec(g_m.shape), lspec(w_out_bf.shape, 1),
        lspec(ln_g.shape), lspec(ln_b.shape), cspec(rot.shape),
        pl.BlockSpec(memory_space=pltpu.SMEM), cspec(esel.shape),
        mat_spec, mat_spec, n_spec, m_spec, cv_spec,
    ]
    out_shape = (
        jax.ShapeDtypeStruct(x.shape, F32),
        jax.ShapeDtypeStruct(state_ret.shape, F32),
        jax.ShapeDtypeStruct(state_c.shape, F32),
        jax.ShapeDtypeStruct(state_n.shape, F32),
        jax.ShapeDtypeStruct(m_pad.shape, F32),
        jax.ShapeDtypeStruct(state_conv.shape, F32),
    )
    out_specs = (cspec(x.shape), mat_spec, mat_spec, n_spec, m_spec, cv_spec)
    scratch = [
        pltpu.VMEM((bs, N_PAD), F32),
        pltpu.VMEM((bs, D_MODEL), F32),
        pltpu.VMEM((bs, 2 * GROUP_W), F32),
    ]
    return pl.pallas_call(
        _sample_kernel,
        grid=(DEPTH, nb),
        in_specs=in_specs,
        out_specs=out_specs,
        out_shape=out_shape,
        scratch_shapes=scratch,
        compiler_params=pltpu.CompilerParams(
            dimension_semantics=("arbitrary", "arbitrary"),
            vmem_limit_bytes=VMEM_LIMIT_BYTES),
        name="sample_layers",
    )(x, w_in_bf, wg_bf, conv_w, conv_b, gbias, g_ret, g_m, w_out_bf, ln_g, ln_b,
      rot, gam, esel, state_ret, state_c, state_n, m_pad, state_conv)


def _cast_weight_kernel(w_ref, o_ref):
    o_ref[0] = w_ref[0].astype(BF16)


def _cast_weight(w):
    depth, k, n = w.shape
    rows = WEIGHT_CAST_ROWS
    assert k % rows == 0
    return pl.pallas_call(
        _cast_weight_kernel,
        grid=(depth, k // rows),
        in_specs=[pl.BlockSpec((1, rows, n), lambda l, r: (l, r, 0))],
        out_specs=pl.BlockSpec((1, rows, n), lambda l, r: (l, r, 0)),
        out_shape=jax.ShapeDtypeStruct((depth, k, n), BF16),
        compiler_params=pltpu.CompilerParams(dimension_semantics=("parallel", "parallel")),
        name="cast_weight",
    )(w)


def _cast_transposed_weight_kernel(n_valid, wt_ref, o_ref):
    rows = wt_ref.shape[1]
    row_id = pl.program_id(1) * rows + lax.broadcasted_iota(jnp.int32, wt_ref.shape[1:], 0)
    wt = jnp.where(row_id < n_valid, wt_ref[0], 0.0)
    o_ref[0] = wt.astype(BF16).T


def _cast_transposed_weight(wt, n_out):
    depth, n, k = wt.shape
    rows = WEIGHT_CAST_ROWS
    assert n_out % LANES == 0 and n_out - n < LANES
    return pl.pallas_call(
        functools.partial(_cast_transposed_weight_kernel, min(n, n_out)),
        grid=(depth, pl.cdiv(n_out, rows)),
        in_specs=[pl.BlockSpec((1, rows, k), lambda l, r: (l, r, 0))],
        out_specs=pl.BlockSpec((1, k, rows), lambda l, r: (l, 0, r)),
        out_shape=jax.ShapeDtypeStruct((depth, k, n_out), BF16),
        compiler_params=pltpu.CompilerParams(dimension_semantics=("parallel", "parallel")),
        name="cast_weight_t",
    )(wt)


def _rotary_tables(pos):
    half = HEAD_DIM // 2
    inv = np.float64(ROPE_BASE) ** (-np.arange(half, dtype=np.float64) / half)
    ang = pos.astype(np.float64)[:, None] * inv[None, :]
    cos = np.cos(ang)
    sin = np.sin(ang)
    cos_t = np.concatenate([cos, cos], axis=-1)
    sin_t = np.concatenate([-sin, sin], axis=-1)
    tabs = (cos_t, sin_t, cos_t * QK_SCALE, sin_t * QK_SCALE)
    return tuple(t.astype(np.float32) for t in tabs)


def _retention_tables():
    L = CHUNK
    f32 = np.float32
    log_gamma = np.log(1.0 - 2.0 ** (-5.0 - np.arange(N_HEADS, dtype=np.float64)))
    idx = np.arange(L, dtype=np.float64)
    diff = idx[:, None] - idx[None, :]
    decay = (np.exp(log_gamma[:, None, None] * np.maximum(diff, 0.0)) * (diff >= 0)).astype(f32)
    q_decay = np.exp(log_gamma[:, None] * (idx + 1.0)).astype(f32)
    k_decay = np.exp(log_gamma[:, None] * (L - 1.0 - idx)).astype(f32)
    c_decay = np.exp(log_gamma * L).astype(f32)
    qd = np.ascontiguousarray(np.broadcast_to(q_decay[:, :, None], (N_HEADS, L, HEAD_DIM)))
    kd = np.ascontiguousarray(np.broadcast_to(k_decay[:, :, None], (N_HEADS, L, HEAD_DIM)))
    gamma1 = np.exp(log_gamma).astype(f32)
    return decay, qd, kd, c_decay, gamma1


def kernel(x_prompt, x_sample, state_ret, state_mlstm_C, state_mlstm_n, state_mlstm_m, state_conv,
           w_in, conv_w, conv_b, b_i, b_f, g_ret, g_m, w_out, ln_g, ln_b):
    B, T, _ = x_prompt.shape
    Bs, Ts, _ = x_sample.shape
    assert Ts == 1

    w_in_t = jnp.swapaxes(w_in, 1, 2)
    w_in_bf = _cast_transposed_weight(w_in_t, COL_GATE)
    wg_bf = jnp.pad(w_in_t[:, COL_GATE:N_IN, :], ((0, 0), (0, GATE_ROWS - 2 * N_HEADS), (0, 0))).astype(BF16)
    w_out_bf = _cast_weight(w_out)
    gate_b = jnp.concatenate([b_i, b_f], axis=-1)
    gbias = jnp.pad(gate_b, ((0, 0), (0, LANES - 2 * N_HEADS))).reshape(DEPTH, 1, LANES)
    gbias_rows = jnp.broadcast_to(gate_b[:, :, None], (DEPTH, 2 * N_HEADS, LANES))
    conv_b3 = conv_b.reshape(DEPTH, 1, 2 * GROUP_W)
    g_ret3 = g_ret.reshape(DEPTH, 1, GROUP_W)
    g_m3 = g_m.reshape(DEPTH, 1, GROUP_W)
    ln_g3 = ln_g.reshape(DEPTH, 1, D_MODEL)
    ln_b3 = ln_b.reshape(DEPTH, 1, D_MODEL)

    decay, qd, kd, c_decay, gamma1 = _retention_tables()
    idx = np.arange(CHUNK)
    causal = idx[:, None] >= idx[None, :]
    triu = jnp.asarray(causal.T, BF16)
    mask_add = np.where(causal, 0.0, -np.inf).astype(np.float32)
    e0 = jnp.asarray(np.broadcast_to(np.arange(LANES)[None, :] == 0, (CHUNK, LANES)), BF16)
    tabs_p = _rotary_tables(np.arange(T)) + (decay, qd, kd, c_decay, triu, mask_add, e0)
    tabs_p = tuple(jnp.asarray(a) for a in tabs_p)

    xp = x_prompt
    prompt_states = []
    for l in range(DEPTH):
        xp, prompt_states = _prompt_layer(
            l, xp, w_in_bf, wg_bf, conv_w, conv_b3, gbias_rows, g_ret3, g_m3, w_out_bf, ln_g3, ln_b3, tabs_p,
            prompt_states)
    rp, cp, np_, mp_pad, vp = prompt_states
    mp = mp_pad[:, :, :N_HEADS, 0]

    rot = jnp.asarray(np.concatenate(_rotary_tables(PAST_LEN + np.arange(Ts)), axis=0))
    bb = SAMPLE_BLOCK_B
    esel = jnp.asarray(np.arange(LANES)[:, None] == (np.arange(bb * HEAD_DIM)[None, :] // HEAD_DIM), BF16)
    gamma1 = jnp.asarray(gamma1)
    m_pad = jnp.pad(state_mlstm_m, ((0, 0), (0, 0), (N_HEADS, LANES - 2 * N_HEADS)))
    ys, rs, cs, ns, ms_pad, vs = _sample_layers(
        x_sample.reshape(Bs, D_MODEL), w_in_bf, wg_bf, conv_w, conv_b3, gbias, g_ret3, g_m3, w_out_bf,
        ln_g3, ln_b3, rot, gamma1, esel, state_ret, state_mlstm_C, state_mlstm_n, m_pad,
        jnp.swapaxes(state_conv, 1, 2))
    vs = jnp.swapaxes(vs, 1, 2)
    ms = ms_pad[:, :, N_HEADS:2 * N_HEADS]

    return (xp, ys.reshape(Bs, Ts, D_MODEL),
            rp, cp, np_, mp, vp, rs, cs, ns, ms, vs)
```

```python
import functools

import jax
import jax.numpy as jnp
import numpy as np
from jax import lax
from jax.experimental import pallas as pl
from jax.experimental.pallas import tpu as pltpu

F32 = jnp.float32
BF16 = jnp.bfloat16

D_MODEL = 1024
DEPTH = 2
PAST_LEN = 16384
N_HEADS = 4
HEAD_DIM = 128
GROUP_W = N_HEADS * HEAD_DIM
CONV_W = 4
CHUNK = 128
ROPE_BASE = 10000.0
LN_EPS = 1e-5
GN_EPS = 1e-5
ALPHA = (2 * DEPTH) ** 0.25
QK_SCALE = HEAD_DIM ** -0.5

LANES = 128
SUBLANES = 8

COL_RQ = 0
COL_RK = COL_RQ + GROUP_W
COL_RV = COL_RK + GROUP_W
COL_RZ = COL_RV + GROUP_W
COL_MQK = COL_RZ + GROUP_W
COL_MV = COL_MQK + 2 * GROUP_W
COL_MO = COL_MV + GROUP_W
COL_MZ = COL_MO + GROUP_W
COL_GATE = COL_MZ + GROUP_W
N_IN = COL_GATE + 2 * N_HEADS
N_PAD = COL_GATE + LANES
GATE_ROWS = 16

PROMPT_BLOCK_T = 256
PROJ_PIECE_COLS = 256
PROJ_PIECE_ROWS = 256
SAMPLE_BLOCK_B = 16
WEIGHT_CAST_ROWS = 1024
VMEM_LIMIT_BYTES = 56 * 1024 * 1024


def _sigmoid(x):
    return 1.0 / (1.0 + jnp.exp(-x))


def _silu(x):
    return x * _sigmoid(x)


def _log_sigmoid(x):
    return jnp.minimum(x, 0.0) - jnp.log1p(jnp.exp(-jnp.abs(x)))


def _dot(a, b):
    return jnp.dot(a, b, preferred_element_type=F32)


def _dot_nt(a, b):
    return lax.dot_general(a, b, (((1,), (1,)), ((), ())), preferred_element_type=F32)


def _dot_tn(a, b):
    return lax.dot_general(a, b, (((0,), (0,)), ((), ())), preferred_element_type=F32)


def _rotary(x, cos_t, sin_t):
    return x * cos_t + pltpu.roll(x, HEAD_DIM // 2, axis=1) * sin_t


def _head_norm(h, g):
    mu = jnp.mean(h, axis=-1, keepdims=True)
    d = h - mu
    var = jnp.mean(d * d, axis=-1, keepdims=True)
    return d * lax.rsqrt(var + GN_EPS) * g


def _layer_norm(x, g, b):
    mu = jnp.mean(x, axis=-1, keepdims=True)
    d = x - mu
    var = jnp.mean(d * d, axis=-1, keepdims=True)
    return d * lax.rsqrt(var + LN_EPS) * g + b


def _cumsum_lanes(triu_bf, x):
    hi = x.astype(BF16)
    r1 = x - hi.astype(F32)
    mid = r1.astype(BF16)
    lo = (r1 - mid.astype(F32)).astype(BF16)
    return _dot(hi, triu_bf) + _dot(mid, triu_bf) + _dot(lo, triu_bf)


RQK_Q, RQK_QDEC, RQK_K, RQK_KDEC = 0, GROUP_W, 2 * GROUP_W, 3 * GROUP_W
VV_RET, VV_M = 0, GROUP_W
GZ_RET, GZ_M = 0, GROUP_W
GT_COLS, GT_ROWS, GT_PER_CHUNK = 0, 1, 2
GT_ROWS_USED = SUBLANES
EMITS_PER_CHUNK = 9 * N_HEADS + 1
N_STATE_OUTPUTS = 5
HEADS_PER_STAGE = 2


def _prompt_kernel(nt, n_blocks, layer, xn_ref, x_ref, w_in_ref, wg_ref, conv_w_ref, conv_b_ref, gbias_ref,
                   g_ret_ref, g_m_ref, w_out_ref, ln_g_ref, ln_b_ref,
                   cos_ref, sin_ref, kcos_ref, ksin_ref,
                   decay_ref, qd_ref, kd_ref, cdec_ref, triu_ref, mask_ref, e0_ref, *refs):
    prev_refs = refs[:N_STATE_OUTPUTS] if layer > 0 else ()
    refs = refs[len(prev_refs):]
    y_ref, s_out_ref, c_out_ref, n_out_ref, m_out_ref, conv_out_ref = refs[:1 + N_STATE_OUTPUTS]
    (rqk_a, vv_a, gz_a, qkm_a, gt_a, rqk_b, vv_b, gz_b, qkm_b, gt_b,
     xb_scr, u_scr, mix_scr, s_scr, caug_scr, m_scr) = refs[1 + N_STATE_OUTPUTS:]
    g = pl.program_id(0)
    t = lax.rem(jnp.maximum(g - 1, 0), nt)
    tn = lax.rem(jnp.minimum(g, n_blocks - 1), nt)
    block_t = x_ref.shape[1]
    n_chunks = block_t // CHUNK
    carry_rows = CONV_W - 1
    heads = range(N_HEADS)

    @pl.when(t == 0)
    def _init_state():
        s_scr[...] = jnp.zeros_like(s_scr)
        caug_scr[...] = jnp.zeros_like(caug_scr)
        m_scr[...] = jnp.zeros_like(m_scr)

    @pl.when(tn == 0)
    def _init_conv_carry():
        u_scr[0:SUBLANES, :] = jnp.zeros((SUBLANES, 2 * GROUP_W), F32)

    def hcols(base, h):
        return slice(base + h * HEAD_DIM, base + (h + 1) * HEAD_DIM)

    def step_body(set_in, set_cur, do_proj=True, do_rec=True):
        rqk_i, vv_i, gz_i, qkm_i, gt_i = set_in
        rqk_c, vv_c, gz_c, qkm_c, gt_c = set_cur
        triu_bf = triu_ref[...]
        causal_add = mask_ref[...]
        ones_col = e0_ref[...]
        sub_id = lax.broadcasted_iota(jnp.int32, (SUBLANES, LANES), 0)
        pad_rows = jnp.zeros((CHUNK - GT_ROWS_USED, LANES), F32)
        if do_proj:
            xb_scr[...] = xn_ref[0].astype(BF16)
        pw = PROJ_PIECE_COLS
        heads_per_piece = pw // HEAD_DIM

        pr = PROJ_PIECE_ROWS
        chunks_per_piece = pr // CHUNK

        def group_rows(p):
            return slice(p * pr, (p + 1) * pr)

        def proj(p, col, width=pw):
            return _dot(xb_scr[group_rows(p), :], w_in_ref[:, col:col + width])

        def piece_rot(p, i, col_base, cos_r, sin_r, dec_ref, dst, dst_dec):
            res = proj(p, col_base + i * pw)
            for hh in range(heads_per_piece):
                h = i * heads_per_piece + hh
                for cc in range(chunks_per_piece):
                    rows = slice(p * pr + cc * CHUNK, p * pr + (cc + 1) * CHUNK)
                    r = _rotary(res[cc * CHUNK:(cc + 1) * CHUNK, hh * HEAD_DIM:(hh + 1) * HEAD_DIM],
                                cos_r[rows, :], sin_r[rows, :])
                    rqk_i[rows, hcols(dst, h)] = r.astype(BF16)
                    rqk_i[rows, hcols(dst_dec, h)] = (r * dec_ref[h]).astype(BF16)

        def piece_cast(p, i, col_base, dst):
            vv_i[group_rows(p), dst + i * pw:dst + (i + 1) * pw] = proj(p, col_base + i * pw).astype(BF16)

        def piece_rz(p, i):
            gz_i[group_rows(p), GZ_RET + i * pw:GZ_RET + (i + 1) * pw] = _silu(proj(p, COL_RZ + i * pw))

        def piece_moz(p, i):
            og = _sigmoid(proj(p, COL_MO + i * pw))
            gz_i[group_rows(p), GZ_M + i * pw:GZ_M + (i + 1) * pw] = og * _silu(proj(p, COL_MZ + i * pw))

        def piece_conv(p, i):
            base = SUBLANES + p * pr
            u_scr[base:base + pr, i * pw:(i + 1) * pw] = proj(p, COL_MQK + i * pw)
            for cs in range(i * pw, (i + 1) * pw, LANES):
                cols = slice(cs, cs + LANES)
                acc = conv_b_ref[:, cols]
                for j in range(CONV_W):
                    r0 = base - carry_rows + j
                    acc = acc + u_scr[r0:r0 + pr, cols] * conv_w_ref[j:j + 1, cols]
                act = _silu(acc)
                if cs >= GROUP_W:
                    act = act * QK_SCALE
                qkm_i[group_rows(p), cols] = act

        def piece_gate(p):
            gates_t = _dot_nt(wg_ref[...], xb_scr[group_rows(p), :])
            for cc in range(chunks_per_piece):
                c = p * chunks_per_piece + cc
                g8 = gates_t[0:SUBLANES, cc * CHUNK:(cc + 1) * CHUNK] + gbias_ref[...]
                rows8 = jnp.where(sub_id < N_HEADS, g8, _cumsum_lanes(triu_bf, _log_sigmoid(g8)))
                gt_i[c * GT_PER_CHUNK + GT_ROWS, 0:GT_ROWS_USED, :] = rows8
                gt_i[c * GT_PER_CHUNK + GT_COLS] = jnp.concatenate([rows8, pad_rows], axis=0).T

        n_col_pieces = GROUP_W // pw
        pieces = []
        for p in range(block_t // pr):
            pieces.append(functools.partial(piece_gate, p))
            for i in range(n_col_pieces):
                pieces.append(functools.partial(piece_conv, p, 2 * i))
                pieces.append(functools.partial(piece_cast, p, i, COL_RV, VV_RET))
                pieces.append(functools.partial(piece_rot, p, i, COL_RQ, cos_ref, sin_ref, qd_ref,
                                                RQK_Q, RQK_QDEC))
                pieces.append(functools.partial(piece_rz, p, i))
                pieces.append(functools.partial(piece_conv, p, 2 * i + 1))
                pieces.append(functools.partial(piece_cast, p, i, COL_MV, VV_M))
                pieces.append(functools.partial(piece_rot, p, i, COL_RK, kcos_ref, ksin_ref, kd_ref,
                                                RQK_K, RQK_KDEC))
                pieces.append(functools.partial(piece_moz, p, i))
        if not do_proj:
            pieces = []
        n_pieces = len(pieces)
        n_slots = EMITS_PER_CHUNK * n_chunks
        slots_done = [0]
        if not do_rec:
            while pieces:
                pieces.pop(0)()
            return

        def emit_proj_pieces():
            slots_done[0] += 1
            target = -(-(slots_done[0] * n_pieces) // n_slots)
            while n_pieces - len(pieces) < target:
                pieces.pop(0)()

        def per_head(group, fn):
            out = {}
            for h in group:
                emit_proj_pieces()
                out[h] = fn(h)
            return out

        head_groups = [range(h0, h0 + HEADS_PER_STAGE) for h0 in range(0, N_HEADS, HEADS_PER_STAGE)]

        for c in range(n_chunks):
            rows = slice(c * CHUNK, (c + 1) * CHUNK)

            for grp in head_groups:
                q_bf = {h: rqk_c[rows, hcols(RQK_Q, h)] for h in grp}
                k_bf = {h: rqk_c[rows, hcols(RQK_K, h)] for h in grp}
                v_bf = {h: vv_c[rows, hcols(VV_RET, h)] for h in grp}
                sc = per_head(grp, lambda h: _dot_nt(q_bf[h], k_bf[h]))
                state = {h: s_scr[h] for h in grp}
                upd = per_head(grp, lambda h: _dot_tn(rqk_c[rows, hcols(RQK_KDEC, h)], v_bf[h]))
                for h in grp:
                    s_scr[h] = state[h] * cdec_ref[h] + upd[h]
                o = per_head(grp, lambda h: _dot(
                    jnp.concatenate([(sc[h] * decay_ref[h]).astype(BF16),
                                     rqk_c[rows, hcols(RQK_QDEC, h)]], axis=1),
                    jnp.concatenate([v_bf[h], state[h].astype(BF16)], axis=0)))

                def ret_out(h):
                    hn = _head_norm(o[h], g_ret_ref[:, hcols(0, h)])
                    mix_scr[rows, hcols(0, h)] = (hn * gz_c[rows, hcols(GZ_RET, h)]).astype(BF16)

                per_head(grp, ret_out)

            cols_t = gt_c[c * GT_PER_CHUNK + GT_COLS]
            rows_t = gt_c[c * GT_PER_CHUNK + GT_ROWS, 0:GT_ROWS_USED, :]
            for grp in head_groups:
                q = {h: qkm_c[rows, hcols(0, h)] for h in grp}
                k = {h: qkm_c[rows, hcols(GROUP_W, h)] for h in grp}
                vaug_bf = {h: jnp.concatenate([vv_c[rows, hcols(VV_M, h)], ones_col], axis=1)
                           for h in grp}
                qk = per_head(grp, lambda h: _dot_nt(q[h].astype(BF16), k[h].astype(BF16)))
                caug = {h: caug_scr[h] for h in grp}
                m_old = {h: m_scr[h, 0:1, 0:1] for h in grp}
                i_row = {h: rows_t[h:h + 1, :] for h in grp}
                b_row = {h: rows_t[N_HEADS + h:N_HEADS + h + 1, :] for h in grp}
                i_col = {h: cols_t[:, h:h + 1] for h in grp}
                b_col = {h: cols_t[:, N_HEADS + h:N_HEADS + h + 1] for h in grp}
                a_col = {h: b_col[h] + m_old[h] for h in grp}
                dm = per_head(grp, lambda h: (b_col[h] + (i_row[h] - b_row[h])) + causal_add)
                mt = {h: jnp.maximum(a_col[h], jnp.max(dm[h], axis=-1, keepdims=True)) for h in grp}
                w_inter = {h: jnp.exp(a_col[h] - mt[h]) for h in grp}
                res = per_head(grp, lambda h: _dot(
                    jnp.concatenate([(qk[h] * jnp.exp(dm[h] - mt[h])).astype(BF16),
                                     (q[h] * w_inter[h]).astype(BF16)], axis=1),
                    jnp.concatenate([vaug_bf[h], caug[h].astype(BF16)], axis=0)))
                b_last = {h: b_col[h][CHUNK - 1:CHUNK, :] for h in grp}
                g_col = {h: b_last[h] - b_col[h] + i_col[h] for h in grp}
                m_new = {h: jnp.maximum(b_last[h] + m_old[h], jnp.max(g_col[h], axis=0, keepdims=True))
                         for h in grp}
                wk = {h: jnp.exp(g_col[h] - m_new[h]) for h in grp}
                wc = {h: jnp.exp(b_last[h] + m_old[h] - m_new[h]) for h in grp}

                def mlstm_update(h):
                    caug_scr[h] = caug[h] * wc[h] + _dot_tn((k[h] * wk[h]).astype(BF16), vaug_bf[h])
                    m_scr[h] = jnp.broadcast_to(m_new[h], (SUBLANES, LANES))

                per_head(grp, mlstm_update)

                def mlstm_out(h):
                    den = jnp.maximum(jnp.abs(res[h][:, HEAD_DIM:HEAD_DIM + 1]), jnp.exp(-mt[h]))
                    hn = _head_norm(res[h][:, :HEAD_DIM] / den, g_m_ref[:, hcols(0, h)])
                    mix_scr[rows, hcols(GROUP_W, h)] = (hn * gz_c[rows, hcols(GZ_M, h)]).astype(BF16)

                per_head(grp, mlstm_out)

            emit_proj_pieces()
            mix = _dot(mix_scr[rows, :], w_out_ref[...])
            y_ref[0, rows, :] = _layer_norm(ALPHA * x_ref[0, rows, :] + mix, ln_g_ref[...], ln_b_ref[...])

        assert slots_done[0] == n_slots and not pieces

    set_a = (rqk_a, vv_a, gz_a, qkm_a, gt_a)
    set_b = (rqk_b, vv_b, gz_b, qkm_b, gt_b)
    parity = lax.rem(g, 2)
    interior = jnp.logical_and(g > 0, g < n_blocks)
    last_sets = (set_a, set_b) if n_blocks % 2 == 0 else (set_b, set_a)

    @pl.when(g == 0)
    def _first_step():
        step_body(set_a, set_b, do_rec=False)

    @pl.when(jnp.logical_and(interior, parity == 0))
    def _even_step():
        step_body(set_a, set_b)

    @pl.when(jnp.logical_and(interior, parity == 1))
    def _odd_step():
        step_body(set_b, set_a)

    @pl.when(g == n_blocks)
    def _last_step():
        step_body(*last_sets, do_proj=False)

    @pl.when(jnp.logical_and(tn == nt - 1, g < n_blocks))
    def _write_conv_state():
        if layer > 0:
            conv_out_ref[0:layer] = prev_refs[-1][...]
        conv_out_ref[layer, 0] = u_scr[SUBLANES + block_t - carry_rows:SUBLANES + block_t, :]

    @pl.when(jnp.logical_and(t == nt - 1, g > 0))
    def _write_state():
        if layer > 0:
            for out_ref, prev_ref in zip((s_out_ref, c_out_ref, n_out_ref, m_out_ref), prev_refs):
                out_ref[0:layer] = prev_ref[...]
        s_out_ref[layer, 0] = s_scr[...]
        m_out_ref[layer, 0] = jnp.zeros((SUBLANES, LANES), F32)
        for h in heads:
            caug = caug_scr[h]
            c_out_ref[layer, 0, h] = caug[:, :HEAD_DIM]
            n_out_ref[layer, 0, h:h + 1, :] = caug[:, HEAD_DIM:].T[0:1, :]
            m_out_ref[layer, 0, h:h + 1, :] = m_scr[h, 0:1, :]

    u_scr[0:SUBLANES, :] = u_scr[block_t:block_t + SUBLANES, :]


def _const_spec(shape):
    nd = len(shape)
    return pl.BlockSpec(shape, lambda g: (0,) * nd)


def _layer_spec(shape, layer):
    return pl.BlockSpec((None,) + tuple(shape[1:]), lambda g: (layer,) + (0,) * (len(shape) - 1))


def _prompt_layer(layer, x, w_in_bf, wg_bf, conv_w, conv_b, gbias, g_ret, g_m, w_out_bf, ln_g, ln_b, tabs,
                  prev_states):
    B, T, _ = x.shape
    bt = PROMPT_BLOCK_T
    assert T % bt == 0 and bt % CHUNK == 0 and GROUP_W % PROJ_PIECE_COLS == 0
    nt = T // bt
    n_blocks = B * nt
    n_chunks = bt // CHUNK
    cos_t, sin_t, kcos_t, ksin_t, decay, qd, kd, cdec, triu, mask, e0 = tabs

    def nxt(g):
        return jnp.minimum(g, n_blocks - 1)

    def cur(g):
        return jnp.maximum(g - 1, 0)

    row_spec = pl.BlockSpec((bt, LANES), lambda g: (nxt(g) % nt, 0))
    in_specs = [
        pl.BlockSpec((1, bt, D_MODEL), lambda g: (nxt(g) // nt, nxt(g) % nt, 0)),
        pl.BlockSpec((1, bt, D_MODEL), lambda g: (cur(g) // nt, cur(g) % nt, 0)),
        _layer_spec(w_in_bf.shape, layer), _layer_spec(wg_bf.shape, layer),
        _layer_spec(conv_w.shape, layer), _layer_spec(conv_b.shape, layer),
        _layer_spec(gbias.shape, layer), _layer_spec(g_ret.shape, layer), _layer_spec(g_m.shape, layer),
        _layer_spec(w_out_bf.shape, layer), _layer_spec(ln_g.shape, layer), _layer_spec(ln_b.shape, layer),
        row_spec, row_spec, row_spec, row_spec,
        _const_spec(decay.shape), _const_spec(qd.shape), _const_spec(kd.shape),
        pl.BlockSpec(memory_space=pltpu.SMEM),
        _const_spec(triu.shape), _const_spec(mask.shape), _const_spec(e0.shape),
    ]
    state_tails = ((N_HEADS, HEAD_DIM, HEAD_DIM), (N_HEADS, HEAD_DIM, HEAD_DIM), (N_HEADS, HEAD_DIM),
                   (SUBLANES, LANES), (CONV_W - 1, 2 * GROUP_W))
    assert len(state_tails) == N_STATE_OUTPUTS and len(prev_states) in (0, N_STATE_OUTPUTS)

    def state_spec(depth, tail, batch_of):
        zeros = (0,) * len(tail)
        return pl.BlockSpec((depth, 1) + tail, lambda g: (0, batch_of(g)) + zeros)

    batch_ofs = [lambda g: cur(g) // nt] * (N_STATE_OUTPUTS - 1) + [lambda g: nxt(g) // nt]
    out_shape = (jax.ShapeDtypeStruct((B, T, D_MODEL), F32),) + tuple(
        jax.ShapeDtypeStruct((layer + 1, B) + tail, F32) for tail in state_tails)
    out_specs = (pl.BlockSpec((1, bt, D_MODEL), lambda g: (cur(g) // nt, cur(g) % nt, 0)),) + tuple(
        state_spec(layer + 1, tail, bo) for tail, bo in zip(state_tails, batch_ofs))
    if prev_states:
        in_specs = in_specs + [state_spec(layer, tail, bo) for tail, bo in zip(state_tails, batch_ofs)]
    operand_set = [
        pltpu.VMEM((bt, 4 * GROUP_W), BF16),
        pltpu.VMEM((bt, 2 * GROUP_W), BF16),
        pltpu.VMEM((bt, 2 * GROUP_W), F32),
        pltpu.VMEM((bt, 2 * GROUP_W), F32),
        pltpu.VMEM((n_chunks * GT_PER_CHUNK, CHUNK, LANES), F32),
    ]
    scratch = operand_set + operand_set + [
        pltpu.VMEM((bt, D_MODEL), BF16),
        pltpu.VMEM((SUBLANES + bt, 2 * GROUP_W), F32),
        pltpu.VMEM((bt, 2 * GROUP_W), BF16),
        pltpu.VMEM((N_HEADS, HEAD_DIM, HEAD_DIM), F32),
        pltpu.VMEM((N_HEADS, HEAD_DIM, 2 * HEAD_DIM), F32),
        pltpu.VMEM((N_HEADS, SUBLANES, LANES), F32),
    ]
    y, *states = pl.pallas_call(
        functools.partial(_prompt_kernel, nt, n_blocks, layer),
        grid=(n_blocks + 1,),
        in_specs=in_specs,
        out_specs=out_specs,
        out_shape=out_shape,
        scratch_shapes=scratch,
        compiler_params=pltpu.CompilerParams(
            dimension_semantics=("arbitrary",),
            vmem_limit_bytes=VMEM_LIMIT_BYTES),
        name="prompt_layer",
    )(x, x, w_in_bf, wg_bf, conv_w, conv_b, gbias, g_ret, g_m, w_out_bf, ln_g, ln_b,
      cos_t, sin_t, kcos_t, ksin_t, decay, qd, kd, cdec, triu, mask, e0, *prev_states)
    return y, states


def _sample_kernel(x_ref, w_in_ref, wg_ref, conv_w_ref, conv_b_ref, gbias_ref, g_ret_ref, g_m_ref,
                   w_out_ref, ln_g_ref, ln_b_ref, rot_ref, gam_ref, esel_ref,
                   s_ref, c_ref, n_ref, m_ref, cv_ref,
                   y_ref, s_out_ref, c_out_ref, n_out_ref, m_out_ref, cv_out_ref,
                   proj_scr, xcur_scr, mix_scr):
    layer = pl.program_id(0)
    j = pl.program_id(1)
    bb = s_ref.shape[1]

    @pl.when(jnp.logical_and(layer == 0, j == 0))
    def _load_x():
        xcur_scr[...] = x_ref[...]

    @pl.when(j == 0)
    def _project():
        xb = xcur_scr[...].astype(BF16)
        for lo in range(0, COL_GATE, 512):
            proj_scr[:, lo:lo + 512] = _dot(xb, w_in_ref[0, :, lo:lo + 512])
        proj_scr[:, COL_GATE:N_PAD] = jnp.zeros((proj_scr.shape[0], N_PAD - COL_GATE), F32)
        proj_scr[:, COL_GATE:COL_GATE + GATE_ROWS] = _dot_nt(xb, wg_ref[0])

    r0 = pl.multiple_of(j * bb, bb)
    rows = pl.ds(r0, bb)
    cos_t = rot_ref[0:1, :]
    sin_t = rot_ref[1:2, :]
    kcos_t = rot_ref[2:3, :]
    ksin_t = rot_ref[3:4, :]
    esel = esel_ref[...]
    row_id = lax.broadcasted_iota(jnp.int32, (bb, bb * HEAD_DIM), 0)
    blk_id = jnp.right_shift(lax.broadcasted_iota(jnp.int32, (bb, bb * HEAD_DIM), 1), 7)
    pad_rows = jnp.zeros((LANES - bb, HEAD_DIM), F32)
    pad_wide = jnp.zeros((LANES - bb, bb * HEAD_DIM), BF16)

    def col_form(x8):
        return jnp.concatenate([x8, pad_rows], axis=0).T.astype(BF16)

    def outer_all(k8, v8):
        vt = jnp.concatenate([v8] * bb, axis=1)
        vsel = jnp.where(row_id == blk_id, vt, 0.0).astype(BF16)
        return _dot(col_form(k8), jnp.concatenate([vsel, pad_wide], axis=0))

    def col_bcast_all(q8):
        return _dot(col_form(q8), esel)

    for h in range(N_HEADS):
        hc = h * HEAD_DIM
        q8 = _rotary(proj_scr[rows, COL_RQ + hc:COL_RQ + hc + HEAD_DIM], cos_t, sin_t)
        k8 = _rotary(proj_scr[rows, COL_RK + hc:COL_RK + hc + HEAD_DIM], kcos_t, ksin_t)
        v8 = proj_scr[rows, COL_RV + hc:COL_RV + hc + HEAD_DIM]
        kv_all = outer_all(k8, v8)
        qc_all = col_bcast_all(q8)
        gamma = gam_ref[h]
        o_rows = []
        for r in range(bb):
            blk = slice(r * HEAD_DIM, (r + 1) * HEAD_DIM)
            s_new = s_ref[0, r, h] * gamma + kv_all[:, blk]
            s_out_ref[0, r, h] = s_new
            o_rows.append(jnp.sum(qc_all[:, blk] * s_new, axis=0, keepdims=True))
        o8 = _head_norm(jnp.concatenate(o_rows, axis=0), g_ret_ref[0, :, hc:hc + HEAD_DIM])
        z8 = proj_scr[rows, COL_RZ + hc:COL_RZ + hc + HEAD_DIM]
        mix_scr[rows, hc:hc + HEAD_DIM] = o8 * _silu(z8)

    u8 = proj_scr[rows, COL_MQK:COL_MQK + 2 * GROUP_W]
    acc = conv_b_ref[0] + u8 * conv_w_ref[0, CONV_W - 1:CONV_W, :]
    for jj in range(CONV_W - 1):
        acc = acc + cv_ref[0, jj] * conv_w_ref[0, jj:jj + 1, :]
    for jj in range(1, CONV_W - 1):
        cv_out_ref[0, jj - 1] = cv_ref[0, jj]
    cv_out_ref[0, CONV_W - 2] = u8
    qk8 = _silu(acc)

    gates = proj_scr[rows, COL_GATE:COL_GATE + LANES] + gbias_ref[0]
    i_al = pltpu.roll(gates, N_HEADS, axis=1)
    bm = _log_sigmoid(gates) + m_ref[0]
    m_new = jnp.maximum(bm, i_al)
    wk = jnp.exp(i_al - m_new)
    wc = jnp.exp(bm - m_new)
    einv = jnp.exp(-m_new)
    m_out_ref[0] = m_new
    for h in range(N_HEADS):
        hc = h * HEAD_DIM
        gl = N_HEADS + h
        q8 = qk8[:, hc:hc + HEAD_DIM]
        k8 = qk8[:, GROUP_W + hc:GROUP_W + hc + HEAD_DIM] * QK_SCALE
        v8 = proj_scr[rows, COL_MV + hc:COL_MV + hc + HEAD_DIM]
        wk_h = wk[:, gl:gl + 1]
        wc_h = jnp.broadcast_to(wc[:, gl:gl + 1], (bb, HEAD_DIM))
        kw8 = k8 * wk_h
        kv_all = outer_all(kw8, v8)
        qc_all = col_bcast_all(q8)
        n_new = n_ref[0, :, h, :] * wc_h + kw8
        n_out_ref[0, :, h, :] = n_new
        num_rows = []
        for r in range(bb):
            blk = slice(r * HEAD_DIM, (r + 1) * HEAD_DIM)
            c_new = c_ref[0, r, h] * wc_h[r:r + 1, :] + kv_all[:, blk]
            c_out_ref[0, r, h] = c_new
            num_rows.append(jnp.sum(qc_all[:, blk] * c_new, axis=0, keepdims=True))
        num = jnp.concatenate(num_rows, axis=0)
        q_bf = q8.astype(BF16).astype(F32)
        den = jnp.sum(q_bf * n_new, axis=-1, keepdims=True)
        hout = num / jnp.maximum(jnp.abs(den), einv[:, gl:gl + 1])
        hn = _head_norm(hout, g_m_ref[0, :, hc:hc + HEAD_DIM])
        og = _sigmoid(proj_scr[rows, COL_MO + hc:COL_MO + hc + HEAD_DIM])
        zg = _silu(proj_scr[rows, COL_MZ + hc:COL_MZ + hc + HEAD_DIM])
        mix_scr[rows, GROUP_W + hc:GROUP_W + hc + HEAD_DIM] = hn * og * zg

    @pl.when(j == pl.num_programs(1) - 1)
    def _finish_layer():
        mix = _dot(mix_scr[...].astype(BF16), w_out_ref[0])
        y = _layer_norm(ALPHA * xcur_scr[...] + mix, ln_g_ref[0], ln_b_ref[0])
        xcur_scr[...] = y
        y_ref[...] = y


def _sample_layers(x, w_in_bf, wg_bf, conv_w, conv_b, gbias, g_ret, g_m, w_out_bf, ln_g, ln_b,
                   rot, gam, esel, state_ret, state_c, state_n, m_pad, state_conv):
    bs = x.shape[0]
    bb = SAMPLE_BLOCK_B
    assert bs % bb == 0
    nb = bs // bb

    def lspec(shape, buffers=None):
        nd = len(shape)
        mode = None if buffers is None else pl.Buffered(buffers)
        return pl.BlockSpec((1,) + tuple(shape[1:]), lambda l, j: (l,) + (0,) * (nd - 1),
                            pipeline_mode=mode)

    def cspec(shape):
        nd = len(shape)
        return pl.BlockSpec(tuple(shape), lambda l, j: (0,) * nd)

    mat_spec = pl.BlockSpec((1, bb, N_HEADS, HEAD_DIM, HEAD_DIM), lambda l, j: (l, j, 0, 0, 0))
    n_spec = pl.BlockSpec((1, bb, N_HEADS, HEAD_DIM), lambda l, j: (l, j, 0, 0))
    m_spec = pl.BlockSpec((1, bb, LANES), lambda l, j: (l, j, 0))
    cv_spec = pl.BlockSpec((1, CONV_W - 1, bb, 2 * GROUP_W), lambda l, j: (l, 0, j, 0))
    in_specs = [
        cspec(x.shape), lspec(w_in_bf.shape, 1), lspec(wg_bf.shape),
        lspec(conv_w.shape), lspec(conv_b.shape),
        lspec(gbias.shape), lspec(g_ret.shape), lspec(g_m.shape), lspec(w_out_bf.shape, 1),
        lspec(ln_g.shape), lspec(ln_b.shape), cspec(rot.shape),
        pl.BlockSpec(memory_space=pltpu.SMEM), cspec(esel.shape),
        mat_spec, mat_spec, n_spec, m_spec, cv_spec,
    ]
    out_shape = (
        jax.ShapeDtypeStruct(x.shape, F32),
        jax.ShapeDtypeStruct(state_ret.shape, F32),
        jax.ShapeDtypeStruct(state_c.shape, F32),
        jax.ShapeDtypeStruct(state_n.shape, F32),
        jax.ShapeDtypeStruct(m_pad.shape, F32),
        jax.ShapeDtypeStruct(state_conv.shape, F32),
    )
    out_specs = (cspec(x.shape), mat_spec, mat_spec, n_spec, m_spec, cv_spec)
    scratch = [
        pltpu.VMEM((bs, N_PAD), F32),
        pltpu.VMEM((bs, D_MODEL), F32),
        pltpu.VMEM((bs, 2 * GROUP_W), F32),
    ]
    return pl.pallas_call(
        _sample_kernel,
        grid=(DEPTH, nb),
        in_specs=in_specs,
        out_specs=out_specs,
        out_shape=out_shape,
        scratch_shapes=scratch,
        compiler_params=pltpu.CompilerParams(
            dimension_semantics=("arbitrary", "arbitrary"),
            vmem_limit_bytes=VMEM_LIMIT_BYTES),
        name="sample_layers",
    )(x, w_in_bf, wg_bf, conv_w, conv_b, gbias, g_ret, g_m, w_out_bf, ln_g, ln_b,
      rot, gam, esel, state_ret, state_c, state_n, m_pad, state_conv)


def _cast_weight_kernel(w_ref, o_ref):
    o_ref[0] = w_ref[0].astype(BF16)


def _cast_weight(w):
    depth, k, n = w.shape
    rows = WEIGHT_CAST_ROWS
    assert k % rows == 0
    return pl.pallas_call(
        _cast_weight_kernel,
        grid=(depth, k // rows),
        in_specs=[pl.BlockSpec((1, rows, n), lambda l, r: (l, r, 0))],
        out_specs=pl.BlockSpec((1, rows, n), lambda l, r: (l, r, 0)),
        out_shape=jax.ShapeDtypeStruct((depth, k, n), BF16),
        compiler_params=pltpu.CompilerParams(dimension_semantics=("parallel", "parallel")),
        name="cast_weight",
    )(w)


def _cast_transposed_weight_kernel(n_valid, wt_ref, o_ref):
    rows = wt_ref.shape[1]
    row_id = pl.program_id(1) * rows + lax.broadcasted_iota(jnp.int32, wt_ref.shape[1:], 0)
    wt = jnp.where(row_id < n_valid, wt_ref[0], 0.0)
    o_ref[0] = wt.astype(BF16).T


def _cast_transposed_weight(wt, n_out):
    depth, n, k = wt.shape
    rows = WEIGHT_CAST_ROWS
    assert n_out % LANES == 0 and n_out - n < LANES
    return pl.pallas_call(
        functools.partial(_cast_transposed_weight_kernel, min(n, n_out)),
        grid=(depth, pl.cdiv(n_out, rows)),
        in_specs=[pl.BlockSpec((1, rows, k), lambda l, r: (l, r, 0))],
        out_specs=pl.BlockSpec((1, k, rows), lambda l, r: (l, 0, r)),
        out_shape=jax.ShapeDtypeStruct((depth, k, n_out), BF16),
        compiler_params=pltpu.CompilerParams(dimension_semantics=("parallel", "parallel")),
        name="cast_weight_t",
    )(wt)


def _rotary_tables(pos):
    half = HEAD_DIM // 2
    inv = np.float64(ROPE_BASE) ** (-np.arange(half, dtype=np.float64) / half)
    ang = pos.astype(np.float64)[:, None] * inv[None, :]
    cos = np.cos(ang)
    sin = np.sin(ang)
    cos_t = np.concatenate([cos, cos], axis=-1)
    sin_t = np.concatenate([-sin, sin], axis=-1)
    tabs = (cos_t, sin_t, cos_t * QK_SCALE, sin_t * QK_SCALE)
    return tuple(t.astype(np.float32) for t in tabs)


def _retention_tables():
    L = CHUNK
    f32 = np.float32
    log_gamma = np.log(1.0 - 2.0 ** (-5.0 - np.arange(N_HEADS, dtype=np.float64)))
    idx = np.arange(L, dtype=np.float64)
    diff = idx[:, None] - idx[None, :]
    decay = (np.exp(log_gamma[:, None, None] * np.maximum(diff, 0.0)) * (diff >= 0)).astype(f32)
    q_decay = np.exp(log_gamma[:, None] * (idx + 1.0)).astype(f32)
    k_decay = np.exp(log_gamma[:, None] * (L - 1.0 - idx)).astype(f32)
    c_decay = np.exp(log_gamma * L).astype(f32)
    qd = np.ascontiguousarray(np.broadcast_to(q_decay[:, :, None], (N_HEADS, L, HEAD_DIM)))
    kd = np.ascontiguousarray(np.broadcast_to(k_decay[:, :, None], (N_HEADS, L, HEAD_DIM)))
    gamma1 = np.exp(log_gamma).astype(f32)
    return decay, qd, kd, c_decay, gamma1


def kernel(x_prompt, x_sample, state_ret, state_mlstm_C, state_mlstm_n, state_mlstm_m, state_conv,
           w_in, conv_w, conv_b, b_i, b_f, g_ret, g_m, w_out, ln_g, ln_b):
    B, T, _ = x_prompt.shape
    Bs, Ts, _ = x_sample.shape
    assert Ts == 1

    w_in_t = jnp.swapaxes(w_in, 1, 2)
    w_in_bf = _cast_transposed_weight(w_in_t, COL_GATE)
    wg_bf = jnp.pad(w_in_t[:, COL_GATE:N_IN, :], ((0, 0), (0, GATE_ROWS - 2 * N_HEADS), (0, 0))).astype(BF16)
    w_out_bf = _cast_weight(w_out)
    gate_b = jnp.concatenate([b_i, b_f], axis=-1)
    gbias = jnp.pad(gate_b, ((0, 0), (0, LANES - 2 * N_HEADS))).reshape(DEPTH, 1, LANES)
    gbias_rows = jnp.broadcast_to(gate_b[:, :, None], (DEPTH, 2 * N_HEADS, LANES))
    conv_b3 = conv_b.reshape(DEPTH, 1, 2 * GROUP_W)
    g_ret3 = g_ret.reshape(DEPTH, 1, GROUP_W)
    g_m3 = g_m.reshape(DEPTH, 1, GROUP_W)
    ln_g3 = ln_g.reshape(DEPTH, 1, D_MODEL)
    ln_b3 = ln_b.reshape(DEPTH, 1, D_MODEL)

    decay, qd, kd, c_decay, gamma1 = _retention_tables()
    idx = np.arange(CHUNK)
    causal = idx[:, None] >= idx[None, :]
    triu = jnp.asarray(causal.T, BF16)
    mask_add = np.where(causal, 0.0, -np.inf).astype(np.float32)
    e0 = jnp.asarray(np.broadcast_to(np.arange(LANES)[None, :] == 0, (CHUNK, LANES)), BF16)
    tabs_p = _rotary_tables(np.arange(T)) + (decay, qd, kd, c_decay, triu, mask_add, e0)
    tabs_p = tuple(jnp.asarray(a) for a in tabs_p)

    xp = x_prompt
    prompt_states = []
    for l in range(DEPTH):
        xp, prompt_states = _prompt_layer(
            l, xp, w_in_bf, wg_bf, conv_w, conv_b3, gbias_rows, g_ret3, g_m3, w_out_bf, ln_g3, ln_b3, tabs_p,
            prompt_states)
    rp, cp, np_, mp_pad, vp = prompt_states
    mp = mp_pad[:, :, :N_HEADS, 0]

    rot = jnp.asarray(np.concatenate(_rotary_tables(PAST_LEN + np.arange(Ts)), axis=0))
    bb = SAMPLE_BLOCK_B
    esel = jnp.asarray(np.arange(LANES)[:, None] == (np.arange(bb * HEAD_DIM)[None, :] // HEAD_DIM), BF16)
    gamma1 = jnp.asarray(gamma1)
    m_pad = jnp.pad(state_mlstm_m, ((0, 0), (0, 0), (N_HEADS, LANES - 2 * N_HEADS)))
    ys, rs, cs, ns, ms_pad, vs = _sample_layers(
        x_sample.reshape(Bs, D_MODEL), w_in_bf, wg_bf, conv_w, conv_b3, gbias, g_ret3, g_m3, w_out_bf,
        ln_g3, ln_b3, rot, gamma1, esel, state_ret, state_mlstm_C, state_mlstm_n, m_pad,
        jnp.swapaxes(state_conv, 1, 2))
    vs = jnp.swapaxes(vs, 1, 2)
    ms = ms_pad[:, :, N_HEADS:2 * N_HEADS]

    return (xp, ys.reshape(Bs, Ts, D_MODEL),
            rp, cp, np_, mp, vp, rs, cs, ns, ms, vs)
```

```python
import functools

import jax
import jax.numpy as jnp
import numpy as np
from jax import lax
from jax.experimental import pallas as pl
from jax.experimental.pallas import tpu as pltpu

F32 = jnp.float32
BF16 = jnp.bfloat16

D_MODEL = 1024
DEPTH = 2
PAST_LEN = 16384
N_HEADS = 4
HEAD_DIM = 128
GROUP_W = N_HEADS * HEAD_DIM
CONV_W = 4
CHUNK = 128
ROPE_BASE = 10000.0
LN_EPS = 1e-5
GN_EPS = 1e-5
ALPHA = (2 * DEPTH) ** 0.25
QK_SCALE = HEAD_DIM ** -0.5

LANES = 128
SUBLANES = 8

COL_RQ = 0
COL_RK = COL_RQ + GROUP_W
COL_RV = COL_RK + GROUP_W
COL_RZ = COL_RV + GROUP_W
COL_MQK = COL_RZ + GROUP_W
COL_MV = COL_MQK + 2 * GROUP_W
COL_MO = COL_MV + GROUP_W
COL_MZ = COL_MO + GROUP_W
COL_GATE = COL_MZ + GROUP_W
N_IN = COL_GATE + 2 * N_HEADS
N_PAD = COL_GATE + LANES
GATE_ROWS = 16

PROMPT_BLOCK_T = 256
PROJ_PIECE_COLS = 256
PROJ_PIECE_ROWS = 256
SAMPLE_BLOCK_B = 16
WEIGHT_CAST_ROWS = 1024
VMEM_LIMIT_BYTES = 56 * 1024 * 1024


def _sigmoid(x):
    return 1.0 / (1.0 + jnp.exp(-x))


def _silu(x):
    return x * _sigmoid(x)


def _log_sigmoid(x):
    return jnp.minimum(x, 0.0) - jnp.log1p(jnp.exp(-jnp.abs(x)))


def _dot(a, b):
    return jnp.dot(a, b, preferred_element_type=F32)


def _dot_nt(a, b):
    return lax.dot_general(a, b, (((1,), (1,)), ((), ())), preferred_element_type=F32)


def _dot_tn(a, b):
    return lax.dot_general(a, b, (((0,), (0,)), ((), ())), preferred_element_type=F32)


def _rotary(x, cos_t, sin_t):
    return x * cos_t + pltpu.roll(x, HEAD_DIM // 2, axis=1) * sin_t


def _head_norm(h, g):
    mu = jnp.mean(h, axis=-1, keepdims=True)
    d = h - mu
    var = jnp.mean(d * d, axis=-1, keepdims=True)
    return d * lax.rsqrt(var + GN_EPS) * g


def _layer_norm(x, g, b):
    mu = jnp.mean(x, axis=-1, keepdims=True)
    d = x - mu
    var = jnp.mean(d * d, axis=-1, keepdims=True)
    return d * lax.rsqrt(var + LN_EPS) * g + b


def _cumsum_lanes(triu_bf, x):
    hi = x.astype(BF16)
    r1 = x - hi.astype(F32)
    mid = r1.astype(BF16)
    lo = (r1 - mid.astype(F32)).astype(BF16)
    return _dot(hi, triu_bf) + _dot(mid, triu_bf) + _dot(lo, triu_bf)


RQK_Q, RQK_QDEC, RQK_K, RQK_KDEC = 0, GROUP_W, 2 * GROUP_W, 3 * GROUP_W
VV_RET, VV_M = 0, GROUP_W
GZ_RET, GZ_M = 0, GROUP_W
GT_COLS, GT_ROWS, GT_PER_CHUNK = 0, 1, 2
GT_ROWS_USED = SUBLANES
EMITS_PER_CHUNK = 9 * N_HEADS + 1
N_STATE_OUTPUTS = 5
HEADS_PER_STAGE = 2


def _prompt_kernel(nt, n_blocks, layer, xn_ref, x_ref, w_in_ref, wg_ref, conv_w_ref, conv_b_ref, gbias_ref,
                   g_ret_ref, g_m_ref, w_out_ref, ln_g_ref, ln_b_ref,
                   cos_ref, sin_ref, kcos_ref, ksin_ref,
                   decay_ref, qd_ref, kd_ref, cdec_ref, triu_ref, mask_ref, e0_ref, *refs):
    prev_refs = refs[:N_STATE_OUTPUTS] if layer > 0 else ()
    refs = refs[len(prev_refs):]
    y_ref, s_out_ref, c_out_ref, n_out_ref, m_out_ref, conv_out_ref = refs[:1 + N_STATE_OUTPUTS]
    (rqk_a, vv_a, gz_a, qkm_a, gt_a, rqk_b, vv_b, gz_b, qkm_b, gt_b,
     xb_scr, u_scr, mix_scr, s_scr, caug_scr, m_scr) = refs[1 + N_STATE_OUTPUTS:]
    g = pl.program_id(0)
    t = lax.rem(jnp.maximum(g - 1, 0), nt)
    tn = lax.rem(jnp.minimum(g, n_blocks - 1), nt)
    block_t = x_ref.shape[1]
    n_chunks = block_t // CHUNK
    carry_rows = CONV_W - 1
    heads = range(N_HEADS)
    lrow = slice(layer, layer + 1)

    @pl.when(g == 0)
    def _init_pipeline():
        for ref in (rqk_b, vv_b, gz_b, qkm_b, gt_b):
            ref[...] = jnp.zeros_like(ref)

    @pl.when(t == 0)
    def _init_state():
        s_scr[...] = jnp.zeros_like(s_scr)
        caug_scr[...] = jnp.zeros_like(caug_scr)
        m_scr[...] = jnp.zeros_like(m_scr)

    @pl.when(tn == 0)
    def _init_conv_carry():
        u_scr[0:SUBLANES, :] = jnp.zeros((SUBLANES, 2 * GROUP_W), F32)

    def hcols(base, h):
        return slice(base + h * HEAD_DIM, base + (h + 1) * HEAD_DIM)

    def step_body(set_in, set_cur):
        rqk_i, vv_i, gz_i, qkm_i, gt_i = set_in
        rqk_c, vv_c, gz_c, qkm_c, gt_c = set_cur
        triu_bf = triu_ref[...]
        causal_add = mask_ref[...]
        ones_col = e0_ref[...]
        sub_id = lax.broadcasted_iota(jnp.int32, (SUBLANES, LANES), 0)
        pad_rows = jnp.zeros((CHUNK - GT_ROWS_USED, LANES), F32)
        xb_scr[...] = xn_ref[0].astype(BF16)
        pw = PROJ_PIECE_COLS
        heads_per_piece = pw // HEAD_DIM

        pr = PROJ_PIECE_ROWS
        chunks_per_piece = pr // CHUNK

        def group_rows(p):
            return slice(p * pr, (p + 1) * pr)

        def proj(p, col, width=pw):
            return _dot(xb_scr[group_rows(p), :], w_in_ref[:, col:col + width])

        def piece_rot(p, i, col_base, cos_r, sin_r, dec_ref, dst, dst_dec):
            res = proj(p, col_base + i * pw)
            for hh in range(heads_per_piece):
                h = i * heads_per_piece + hh
                for cc in range(chunks_per_piece):
                    rows = slice(p * pr + cc * CHUNK, p * pr + (cc + 1) * CHUNK)
                    r = _rotary(res[cc * CHUNK:(cc + 1) * CHUNK, hh * HEAD_DIM:(hh + 1) * HEAD_DIM],
                                cos_r[rows, :], sin_r[rows, :])
                    rqk_i[rows, hcols(dst, h)] = r.astype(BF16)
                    rqk_i[rows, hcols(dst_dec, h)] = (r * dec_ref[h]).astype(BF16)

        def piece_cast(p, i, col_base, dst):
            vv_i[group_rows(p), dst + i * pw:dst + (i + 1) * pw] = proj(p, col_base + i * pw).astype(BF16)

        def piece_rz(p, i):
            gz_i[group_rows(p), GZ_RET + i * pw:GZ_RET + (i + 1) * pw] = _silu(proj(p, COL_RZ + i * pw))

        def piece_moz(p, i):
            og = _sigmoid(proj(p, COL_MO + i * pw))
            gz_i[group_rows(p), GZ_M + i * pw:GZ_M + (i + 1) * pw] = og * _silu(proj(p, COL_MZ + i * pw))

        def piece_conv(p, i):
            base = SUBLANES + p * pr
            u_scr[base:base + pr, i * pw:(i + 1) * pw] = proj(p, COL_MQK + i * pw)
            for cs in range(i * pw, (i + 1) * pw, LANES):
                cols = slice(cs, cs + LANES)
                acc = conv_b_ref[lrow, cols]
                for j in range(CONV_W):
                    r0 = base - carry_rows + j
                    acc = acc + u_scr[r0:r0 + pr, cols] * conv_w_ref[j:j + 1, cols]
                act = _silu(acc)
                if cs >= GROUP_W:
                    act = act * QK_SCALE
                qkm_i[group_rows(p), cols] = act

        def piece_gate(p):
            gates_t = _dot_nt(wg_ref[...], xb_scr[group_rows(p), :])
            for cc in range(chunks_per_piece):
                c = p * chunks_per_piece + cc
                g8 = gates_t[0:SUBLANES, cc * CHUNK:(cc + 1) * CHUNK] + gbias_ref[...]
                rows8 = jnp.where(sub_id < N_HEADS, g8, _cumsum_lanes(triu_bf, _log_sigmoid(g8)))
                gt_i[c * GT_PER_CHUNK + GT_ROWS, 0:GT_ROWS_USED, :] = rows8
                gt_i[c * GT_PER_CHUNK + GT_COLS] = jnp.concatenate([rows8, pad_rows], axis=0).T

        n_col_pieces = GROUP_W // pw
        pieces = []
        for p in range(block_t // pr):
            pieces.append(functools.partial(piece_gate, p))
            for i in range(n_col_pieces):
                pieces.append(functools.partial(piece_conv, p, 2 * i))
                pieces.append(functools.partial(piece_cast, p, i, COL_RV, VV_RET))
                pieces.append(functools.partial(piece_rot, p, i, COL_RQ, cos_ref, sin_ref, qd_ref,
                                                RQK_Q, RQK_QDEC))
                pieces.append(functools.partial(piece_rz, p, i))
                pieces.append(functools.partial(piece_conv, p, 2 * i + 1))
                pieces.append(functools.partial(piece_cast, p, i, COL_MV, VV_M))
                pieces.append(functools.partial(piece_rot, p, i, COL_RK, kcos_ref, ksin_ref, kd_ref,
                                                RQK_K, RQK_KDEC))
                pieces.append(functools.partial(piece_moz, p, i))
        n_pieces = len(pieces)
        n_slots = EMITS_PER_CHUNK * n_chunks
        slots_done = [0]

        def emit_proj_pieces():
            slots_done[0] += 1
            target = -(-(slots_done[0] * n_pieces) // n_slots)
            while n_pieces - len(pieces) < target:
                pieces.pop(0)()

        def per_head(group, fn):
            out = {}
            for h in group:
                emit_proj_pieces()
                out[h] = fn(h)
            return out

        head_groups = [range(h0, h0 + HEADS_PER_STAGE) for h0 in range(0, N_HEADS, HEADS_PER_STAGE)]

        for c in range(n_chunks):
            rows = slice(c * CHUNK, (c + 1) * CHUNK)

            for grp in head_groups:
                q_bf = {h: rqk_c[rows, hcols(RQK_Q, h)] for h in grp}
                k_bf = {h: rqk_c[rows, hcols(RQK_K, h)] for h in grp}
                v_bf = {h: vv_c[rows, hcols(VV_RET, h)] for h in grp}
                sc = per_head(grp, lambda h: _dot_nt(q_bf[h], k_bf[h]))
                state = {h: s_scr[h] for h in grp}
                upd = per_head(grp, lambda h: _dot_tn(rqk_c[rows, hcols(RQK_KDEC, h)], v_bf[h]))
                for h in grp:
                    s_scr[h] = state[h] * cdec_ref[h] + upd[h]
                o = per_head(grp, lambda h: _dot(
                    jnp.concatenate([(sc[h] * decay_ref[h]).astype(BF16),
                                     rqk_c[rows, hcols(RQK_QDEC, h)]], axis=1),
                    jnp.concatenate([v_bf[h], state[h].astype(BF16)], axis=0)))

                def ret_out(h):
                    hn = _head_norm(o[h], g_ret_ref[lrow, hcols(0, h)])
                    mix_scr[rows, hcols(0, h)] = (hn * gz_c[rows, hcols(GZ_RET, h)]).astype(BF16)

                per_head(grp, ret_out)

            cols_t = gt_c[c * GT_PER_CHUNK + GT_COLS]
            rows_t = gt_c[c * GT_PER_CHUNK + GT_ROWS, 0:GT_ROWS_USED, :]
            for grp in head_groups:
                q = {h: qkm_c[rows, hcols(0, h)] for h in grp}
                k = {h: qkm_c[rows, hcols(GROUP_W, h)] for h in grp}
                vaug_bf = {h: jnp.concatenate([vv_c[rows, hcols(VV_M, h)], ones_col], axis=1)
                           for h in grp}
                qk = per_head(grp, lambda h: _dot_nt(q[h].astype(BF16), k[h].astype(BF16)))
                caug = {h: caug_scr[h] for h in grp}
                m_old = {h: m_scr[h, 0:1, 0:1] for h in grp}
                i_row = {h: rows_t[h:h + 1, :] for h in grp}
                b_row = {h: rows_t[N_HEADS + h:N_HEADS + h + 1, :] for h in grp}
                i_col = {h: cols_t[:, h:h + 1] for h in grp}
                b_col = {h: cols_t[:, N_HEADS + h:N_HEADS + h + 1] for h in grp}
                a_col = {h: b_col[h] + m_old[h] for h in grp}
                dm = per_head(grp, lambda h: (b_col[h] + (i_row[h] - b_row[h])) + causal_add)
                mt = {h: jnp.maximum(a_col[h], jnp.max(dm[h], axis=-1, keepdims=True)) for h in grp}
                w_inter = {h: jnp.exp(a_col[h] - mt[h]) for h in grp}
                res = per_head(grp, lambda h: _dot(
                    jnp.concatenate([(qk[h] * jnp.exp(dm[h] - mt[h])).astype(BF16),
                                     (q[h] * w_inter[h]).astype(BF16)], axis=1),
                    jnp.concatenate([vaug_bf[h], caug[h].astype(BF16)], axis=0)))
                b_last = {h: b_col[h][CHUNK - 1:CHUNK, :] for h in grp}
                g_col = {h: b_last[h] - b_col[h] + i_col[h] for h in grp}
                m_new = {h: jnp.maximum(b_last[h] + m_old[h], jnp.max(g_col[h], axis=0, keepdims=True))
                         for h in grp}
                wk = {h: jnp.exp(g_col[h] - m_new[h]) for h in grp}
                wc = {h: jnp.exp(b_last[h] + m_old[h] - m_new[h]) for h in grp}

                def mlstm_update(h):
                    caug_scr[h] = caug[h] * wc[h] + _dot_tn((k[h] * wk[h]).astype(BF16), vaug_bf[h])
                    m_scr[h] = jnp.broadcast_to(m_new[h], (SUBLANES, LANES))

                per_head(grp, mlstm_update)

                def mlstm_out(h):
                    den = jnp.maximum(jnp.abs(res[h][:, HEAD_DIM:HEAD_DIM + 1]), jnp.exp(-mt[h]))
                    hn = _head_norm(res[h][:, :HEAD_DIM] / den, g_m_ref[lrow, hcols(0, h)])
                    mix_scr[rows, hcols(GROUP_W, h)] = (hn * gz_c[rows, hcols(GZ_M, h)]).astype(BF16)

                per_head(grp, mlstm_out)

            emit_proj_pieces()
            mix = _dot(mix_scr[rows, :], w_out_ref[...])
            y_ref[0, rows, :] = _layer_norm(ALPHA * x_ref[0, rows, :] + mix, ln_g_ref[lrow, :], ln_b_ref[lrow, :])

        assert slots_done[0] == n_slots and not pieces

    set_a = (rqk_a, vv_a, gz_a, qkm_a, gt_a)
    set_b = (rqk_b, vv_b, gz_b, qkm_b, gt_b)
    parity = lax.rem(g, 2)

    @pl.when(parity == 0)
    def _even_step():
        step_body(set_a, set_b)

    @pl.when(parity == 1)
    def _odd_step():
        step_body(set_b, set_a)

    @pl.when(jnp.logical_and(tn == nt - 1, g < n_blocks))
    def _write_conv_state():
        if layer > 0:
            conv_out_ref[0:layer] = prev_refs[-1][...]
        conv_out_ref[layer, 0] = u_scr[SUBLANES + block_t - carry_rows:SUBLANES + block_t, :]

    @pl.when(jnp.logical_and(t == nt - 1, g > 0))
    def _write_state():
        if layer > 0:
            for out_ref, prev_ref in zip((s_out_ref, c_out_ref, n_out_ref, m_out_ref), prev_refs):
                out_ref[0:layer] = prev_ref[...]
        s_out_ref[layer, 0] = s_scr[...]
        m_out_ref[layer, 0] = jnp.zeros((SUBLANES, LANES), F32)
        for h in heads:
            caug = caug_scr[h]
            c_out_ref[layer, 0, h] = caug[:, :HEAD_DIM]
            n_out_ref[layer, 0, h:h + 1, :] = caug[:, HEAD_DIM:].T[0:1, :]
            m_out_ref[layer, 0, h:h + 1, :] = m_scr[h, 0:1, :]

    u_scr[0:SUBLANES, :] = u_scr[block_t:block_t + SUBLANES, :]


def _const_spec(shape):
    nd = len(shape)
    return pl.BlockSpec(shape, lambda g: (0,) * nd)


def _layer_spec(shape, layer):
    return pl.BlockSpec((None,) + tuple(shape[1:]), lambda g: (layer,) + (0,) * (len(shape) - 1))


def _prompt_layer(layer, x, w_in_bf, wg_bf, conv_w, conv_b, gbias, g_ret, g_m, w_out_bf, ln_g, ln_b, tabs,
                  prev_states):
    B, T, _ = x.shape
    bt = PROMPT_BLOCK_T
    assert T % bt == 0 and bt % CHUNK == 0 and GROUP_W % PROJ_PIECE_COLS == 0
    nt = T // bt
    n_blocks = B * nt
    n_chunks = bt // CHUNK
    cos_t, sin_t, kcos_t, ksin_t, decay, qd, kd, cdec, triu, mask, e0 = tabs

    def nxt(g):
        return jnp.minimum(g, n_blocks - 1)

    def cur(g):
        return jnp.maximum(g - 1, 0)

    row_spec = pl.BlockSpec((bt, LANES), lambda g: (nxt(g) % nt, 0))
    in_specs = [
        pl.BlockSpec((1, bt, D_MODEL), lambda g: (nxt(g) // nt, nxt(g) % nt, 0)),
        pl.BlockSpec((1, bt, D_MODEL), lambda g: (cur(g) // nt, cur(g) % nt, 0)),
        _layer_spec(w_in_bf.shape, layer), _layer_spec(wg_bf.shape, layer),
        _layer_spec(conv_w.shape, layer), _const_spec(conv_b.shape),
        _layer_spec(gbias.shape, layer), _const_spec(g_ret.shape), _const_spec(g_m.shape),
        _layer_spec(w_out_bf.shape, layer), _const_spec(ln_g.shape), _const_spec(ln_b.shape),
        row_spec, row_spec, row_spec, row_spec,
        _const_spec(decay.shape), _const_spec(qd.shape), _const_spec(kd.shape),
        pl.BlockSpec(memory_space=pltpu.SMEM),
        _const_spec(triu.shape), _const_spec(mask.shape), _const_spec(e0.shape),
    ]
    state_tails = ((N_HEADS, HEAD_DIM, HEAD_DIM), (N_HEADS, HEAD_DIM, HEAD_DIM), (N_HEADS, HEAD_DIM),
                   (SUBLANES, LANES), (CONV_W - 1, 2 * GROUP_W))
    assert len(state_tails) == N_STATE_OUTPUTS and len(prev_states) in (0, N_STATE_OUTPUTS)

    def state_spec(depth, tail, batch_of):
        zeros = (0,) * len(tail)
        return pl.BlockSpec((depth, 1) + tail, lambda g: (0, batch_of(g)) + zeros)

    batch_ofs = [lambda g: cur(g) // nt] * (N_STATE_OUTPUTS - 1) + [lambda g: nxt(g) // nt]
    out_shape = (jax.ShapeDtypeStruct((B, T, D_MODEL), F32),) + tuple(
        jax.ShapeDtypeStruct((layer + 1, B) + tail, F32) for tail in state_tails)
    out_specs = (pl.BlockSpec((1, bt, D_MODEL), lambda g: (cur(g) // nt, cur(g) % nt, 0)),) + tuple(
        state_spec(layer + 1, tail, bo) for tail, bo in zip(state_tails, batch_ofs))
    if prev_states:
        in_specs = in_specs + [state_spec(layer, tail, bo) for tail, bo in zip(state_tails, batch_ofs)]
    operand_set = [
        pltpu.VMEM((bt, 4 * GROUP_W), BF16),
        pltpu.VMEM((bt, 2 * GROUP_W), BF16),
        pltpu.VMEM((bt, 2 * GROUP_W), F32),
        pltpu.VMEM((bt, 2 * GROUP_W), F32),
        pltpu.VMEM((n_chunks * GT_PER_CHUNK, CHUNK, LANES), F32),
    ]
    scratch = operand_set + operand_set + [
        pltpu.VMEM((bt, D_MODEL), BF16),
        pltpu.VMEM((SUBLANES + bt, 2 * GROUP_W), F32),
        pltpu.VMEM((bt, 2 * GROUP_W), BF16),
        pltpu.VMEM((N_HEADS, HEAD_DIM, HEAD_DIM), F32),
        pltpu.VMEM((N_HEADS, HEAD_DIM, 2 * HEAD_DIM), F32),
        pltpu.VMEM((N_HEADS, SUBLANES, LANES), F32),
    ]
    y, *states = pl.pallas_call(
        functools.partial(_prompt_kernel, nt, n_blocks, layer),
        grid=(n_blocks + 1,),
        in_specs=in_specs,
        out_specs=out_specs,
        out_shape=out_shape,
        scratch_shapes=scratch,
        compiler_params=pltpu.CompilerParams(
            dimension_semantics=("arbitrary",),
            vmem_limit_bytes=VMEM_LIMIT_BYTES),
        name="prompt_layer",
    )(x, x, w_in_bf, wg_bf, conv_w, conv_b, gbias, g_ret, g_m, w_out_bf, ln_g, ln_b,
      cos_t, sin_t, kcos_t, ksin_t, decay, qd, kd, cdec, triu, mask, e0, *prev_states)
    return y, states


def _sample_kernel(x_ref, w_in_ref, wg_ref, conv_w_ref, conv_b_ref, gbias_ref, g_ret_ref, g_m_ref,
                   w_out_ref, ln_g_ref, ln_b_ref, rot_ref, gam_ref, esel_ref,
                   s_ref, c_ref, n_ref, m_ref, cv_ref,
                   y_ref, s_out_ref, c_out_ref, n_out_ref, m_out_ref, cv_out_ref,
                   proj_scr, xcur_scr, mix_scr):
    layer = pl.program_id(0)
    j = pl.program_id(1)
    bb = s_ref.shape[1]

    def layer_row(ref, cols=slice(None)):
        row = ref[0:1, cols]
        for l in range(1, DEPTH):
            row = jnp.where(layer == l, ref[l:l + 1, cols], row)
        return row

    @pl.when(jnp.logical_and(layer == 0, j == 0))
    def _load_x():
        xcur_scr[...] = x_ref[...]

    @pl.when(j == 0)
    def _project():
        xb = xcur_scr[...].astype(BF16)
        for lo in range(0, COL_GATE, 512):
            proj_scr[:, lo:lo + 512] = _dot(xb, w_in_ref[0, :, lo:lo + 512])
        proj_scr[:, COL_GATE:N_PAD] = jnp.zeros((proj_scr.shape[0], N_PAD - COL_GATE), F32)
        proj_scr[:, COL_GATE:COL_GATE + GATE_ROWS] = _dot_nt(xb, wg_ref[0])

    r0 = pl.multiple_of(j * bb, bb)
    rows = pl.ds(r0, bb)
    cos_t = rot_ref[0:1, :]
    sin_t = rot_ref[1:2, :]
    kcos_t = rot_ref[2:3, :]
    ksin_t = rot_ref[3:4, :]
    esel = esel_ref[...]
    row_id = lax.broadcasted_iota(jnp.int32, (bb, bb * HEAD_DIM), 0)
    blk_id = jnp.right_shift(lax.broadcasted_iota(jnp.int32, (bb, bb * HEAD_DIM), 1), 7)
    pad_rows = jnp.zeros((LANES - bb, HEAD_DIM), F32)
    pad_wide = jnp.zeros((LANES - bb, bb * HEAD_DIM), BF16)

    def col_form(x8):
        return jnp.concatenate([x8, pad_rows], axis=0).T.astype(BF16)

    def outer_all(k8, v8):
        vt = jnp.concatenate([v8] * bb, axis=1)
        vsel = jnp.where(row_id == blk_id, vt, 0.0).astype(BF16)
        return _dot(col_form(k8), jnp.concatenate([vsel, pad_wide], axis=0))

    def col_bcast_all(q8):
        return _dot(col_form(q8), esel)

    for h in range(N_HEADS):
        hc = h * HEAD_DIM
        q8 = _rotary(proj_scr[rows, COL_RQ + hc:COL_RQ + hc + HEAD_DIM], cos_t, sin_t)
        k8 = _rotary(proj_scr[rows, COL_RK + hc:COL_RK + hc + HEAD_DIM], kcos_t, ksin_t)
        v8 = proj_scr[rows, COL_RV + hc:COL_RV + hc + HEAD_DIM]
        kv_all = outer_all(k8, v8)
        qc_all = col_bcast_all(q8)
        gamma = gam_ref[h]
        o_rows = []
        for r in range(bb):
            blk = slice(r * HEAD_DIM, (r + 1) * HEAD_DIM)
            s_new = s_ref[0, r, h] * gamma + kv_all[:, blk]
            s_out_ref[0, r, h] = s_new
            o_rows.append(jnp.sum(qc_all[:, blk] * s_new, axis=0, keepdims=True))
        o8 = _head_norm(jnp.concatenate(o_rows, axis=0), layer_row(g_ret_ref, slice(hc, hc + HEAD_DIM)))
        z8 = proj_scr[rows, COL_RZ + hc:COL_RZ + hc + HEAD_DIM]
        mix_scr[rows, hc:hc + HEAD_DIM] = o8 * _silu(z8)

    u8 = proj_scr[rows, COL_MQK:COL_MQK + 2 * GROUP_W]
    acc = layer_row(conv_b_ref) + u8 * conv_w_ref[0, CONV_W - 1:CONV_W, :]
    for jj in range(CONV_W - 1):
        acc = acc + cv_ref[0, jj] * conv_w_ref[0, jj:jj + 1, :]
    for jj in range(1, CONV_W - 1):
        cv_out_ref[0, jj - 1] = cv_ref[0, jj]
    cv_out_ref[0, CONV_W - 2] = u8
    qk8 = _silu(acc)

    gates = proj_scr[rows, COL_GATE:COL_GATE + LANES] + gbias_ref[0]
    i_al = pltpu.roll(gates, N_HEADS, axis=1)
    bm = _log_sigmoid(gates) + m_ref[0]
    m_new = jnp.maximum(bm, i_al)
    wk = jnp.exp(i_al - m_new)
    wc = jnp.exp(bm - m_new)
    einv = jnp.exp(-m_new)
    m_out_ref[0] = m_new
    for h in range(N_HEADS):
        hc = h * HEAD_DIM
        gl = N_HEADS + h
        q8 = qk8[:, hc:hc + HEAD_DIM]
        k8 = qk8[:, GROUP_W + hc:GROUP_W + hc + HEAD_DIM] * QK_SCALE
        v8 = proj_scr[rows, COL_MV + hc:COL_MV + hc + HEAD_DIM]
        wk_h = wk[:, gl:gl + 1]
        wc_h = jnp.broadcast_to(wc[:, gl:gl + 1], (bb, HEAD_DIM))
        kw8 = k8 * wk_h
        kv_all = outer_all(kw8, v8)
        qc_all = col_bcast_all(q8)
        n_new = n_ref[0, :, h, :] * wc_h + kw8
        n_out_ref[0, :, h, :] = n_new
        num_rows = []
        for r in range(bb):
            blk = slice(r * HEAD_DIM, (r + 1) * HEAD_DIM)
            c_new = c_ref[0, r, h] * wc_h[r:r + 1, :] + kv_all[:, blk]
            c_out_ref[0, r, h] = c_new
            num_rows.append(jnp.sum(qc_all[:, blk] * c_new, axis=0, keepdims=True))
        num = jnp.concatenate(num_rows, axis=0)
        q_bf = q8.astype(BF16).astype(F32)
        den = jnp.sum(q_bf * n_new, axis=-1, keepdims=True)
        hout = num / jnp.maximum(jnp.abs(den), einv[:, gl:gl + 1])
        hn = _head_norm(hout, layer_row(g_m_ref, slice(hc, hc + HEAD_DIM)))
        og = _sigmoid(proj_scr[rows, COL_MO + hc:COL_MO + hc + HEAD_DIM])
        zg = _silu(proj_scr[rows, COL_MZ + hc:COL_MZ + hc + HEAD_DIM])
        mix_scr[rows, GROUP_W + hc:GROUP_W + hc + HEAD_DIM] = hn * og * zg

    @pl.when(j == pl.num_programs(1) - 1)
    def _finish_layer():
        mix = _dot(mix_scr[...].astype(BF16), w_out_ref[0])
        y = _layer_norm(ALPHA * xcur_scr[...] + mix, layer_row(ln_g_ref), layer_row(ln_b_ref))
        xcur_scr[...] = y
        y_ref[...] = y


def _sample_layers(x, w_in_bf, wg_bf, conv_w, conv_b, gbias, g_ret, g_m, w_out_bf, ln_g, ln_b,
                   rot, gam, esel, state_ret, state_c, state_n, m_pad, state_conv):
    bs = x.shape[0]
    bb = SAMPLE_BLOCK_B
    assert bs % bb == 0
    nb = bs // bb

    def lspec(shape, buffers=None):
        nd = len(shape)
        mode = None if buffers is None else pl.Buffered(buffers)
        return pl.BlockSpec((1,) + tuple(shape[1:]), lambda l, j: (l,) + (0,) * (nd - 1),
                            pipeline_mode=mode)

    def cspec(shape):
        nd = len(shape)
        return pl.BlockSpec(tuple(shape), lambda l, j: (0,) * nd)

    mat_spec = pl.BlockSpec((1, bb, N_HEADS, HEAD_DIM, HEAD_DIM), lambda l, j: (l, j, 0, 0, 0))
    n_spec = pl.BlockSpec((1, bb, N_HEADS, HEAD_DIM), lambda l, j: (l, j, 0, 0))
    m_spec = pl.BlockSpec((1, bb, LANES), lambda l, j: (l, j, 0))
    cv_spec = pl.BlockSpec((1, CONV_W - 1, bb, 2 * GROUP_W), lambda l, j: (l, 0, j, 0))
    in_specs = [
        cspec(x.shape), lspec(w_in_bf.shape, 1), lspec(wg_bf.shape),
        lspec(conv_w.shape), cspec(conv_b.shape),
        lspec(gbias.shape), cspec(g_ret.shape), cspec(g_m.shape), lspec(w_out_bf.shape, 1),
        cspec(ln_g.shape), cspec(ln_b.shape), cspec(rot.shape),
        pl.BlockSpec(memory_space=pltpu.SMEM), cspec(esel.shape),
        mat_spec, mat_spec, n_spec, m_spec, cv_spec,
    ]
    out_shape = (
        jax.ShapeDtypeStruct(x.shape, F32),
        jax.ShapeDtypeStruct(state_ret.shape, F32),
        jax.ShapeDtypeStruct(state_c.shape, F32),
        jax.ShapeDtypeStruct(state_n.shape, F32),
        jax.ShapeDtypeStruct(m_pad.shape, F32),
        jax.ShapeDtypeStruct(state_conv.shape, F32),
    )
    out_specs = (cspec(x.shape), mat_spec, mat_spec, n_spec, m_spec, cv_spec)
    scratch = [
        pltpu.VMEM((bs, N_PAD), F32),
        pltpu.VMEM((bs, D_MODEL), F32),
        pltpu.VMEM((bs, 2 * GROUP_W), F32),
    ]
    return pl.pallas_call(
        _sample_kernel,
        grid=(DEPTH, nb),
        in_specs=in_specs,
        out_specs=out_specs,
        out_shape=out_shape,
        scratch_shapes=scratch,
        compiler_params=pltpu.CompilerParams(
            dimension_semantics=("arbitrary", "arbitrary"),
            vmem_limit_bytes=VMEM_LIMIT_BYTES),
        name="sample_layers",
    )(x, w_in_bf, wg_bf, conv_w, conv_b, gbias, g_ret, g_m, w_out_bf, ln_g, ln_b,
      rot, gam, esel, state_ret, state_c, state_n, m_pad, state_conv)


def _cast_weight_kernel(w_ref, o_ref):
    o_ref[0] = w_ref[0].astype(BF16)


def _cast_weight(w):
    depth, k, n = w.shape
    rows = WEIGHT_CAST_ROWS
    assert k % rows == 0
    return pl.pallas_call(
        _cast_weight_kernel,
        grid=(depth, k // rows),
        in_specs=[pl.BlockSpec((1, rows, n), lambda l, r: (l, r, 0))],
        out_specs=pl.BlockSpec((1, rows, n), lambda l, r: (l, r, 0)),
        out_shape=jax.ShapeDtypeStruct((depth, k, n), BF16),
        compiler_params=pltpu.CompilerParams(dimension_semantics=("parallel", "parallel")),
        name="cast_weight",
    )(w)


def _cast_transposed_weight_kernel(n_valid, wt_ref, o_ref):
    rows = wt_ref.shape[1]
    row_id = pl.program_id(1) * rows + lax.broadcasted_iota(jnp.int32, wt_ref.shape[1:], 0)
    wt = jnp.where(row_id < n_valid, wt_ref[0], 0.0)
    o_ref[0] = wt.astype(BF16).T


def _cast_transposed_weight(wt, n_out):
    depth, n, k = wt.shape
    rows = WEIGHT_CAST_ROWS
    assert n_out % LANES == 0 and n_out - n < LANES
    return pl.pallas_call(
        functools.partial(_cast_transposed_weight_kernel, min(n, n_out)),
        grid=(depth, pl.cdiv(n_out, rows)),
        in_specs=[pl.BlockSpec((1, rows, k), lambda l, r: (l, r, 0))],
        out_specs=pl.BlockSpec((1, k, rows), lambda l, r: (l, 0, r)),
        out_shape=jax.ShapeDtypeStruct((depth, k, n_out), BF16),
        compiler_params=pltpu.CompilerParams(dimension_semantics=("parallel", "parallel")),
        name="cast_weight_t",
    )(wt)


def _rotary_tables(pos):
    half = HEAD_DIM // 2
    inv = np.float64(ROPE_BASE) ** (-np.arange(half, dtype=np.float64) / half)
    ang = pos.astype(np.float64)[:, None] * inv[None, :]
    cos = np.cos(ang)
    sin = np.sin(ang)
    cos_t = np.concatenate([cos, cos], axis=-1)
    sin_t = np.concatenate([-sin, sin], axis=-1)
    tabs = (cos_t, sin_t, cos_t * QK_SCALE, sin_t * QK_SCALE)
    return tuple(t.astype(np.float32) for t in tabs)


def _retention_tables():
    L = CHUNK
    f32 = np.float32
    log_gamma = np.log(1.0 - 2.0 ** (-5.0 - np.arange(N_HEADS, dtype=np.float64)))
    idx = np.arange(L, dtype=np.float64)
    diff = idx[:, None] - idx[None, :]
    decay = (np.exp(log_gamma[:, None, None] * np.maximum(diff, 0.0)) * (diff >= 0)).astype(f32)
    q_decay = np.exp(log_gamma[:, None] * (idx + 1.0)).astype(f32)
    k_decay = np.exp(log_gamma[:, None] * (L - 1.0 - idx)).astype(f32)
    c_decay = np.exp(log_gamma * L).astype(f32)
    qd = np.ascontiguousarray(np.broadcast_to(q_decay[:, :, None], (N_HEADS, L, HEAD_DIM)))
    kd = np.ascontiguousarray(np.broadcast_to(k_decay[:, :, None], (N_HEADS, L, HEAD_DIM)))
    gamma1 = np.exp(log_gamma).astype(f32)
    return decay, qd, kd, c_decay, gamma1


def kernel(x_prompt, x_sample, state_ret, state_mlstm_C, state_mlstm_n, state_mlstm_m, state_conv,
           w_in, conv_w, conv_b, b_i, b_f, g_ret, g_m, w_out, ln_g, ln_b):
    B, T, _ = x_prompt.shape
    Bs, Ts, _ = x_sample.shape
    assert Ts == 1

    w_in_t = jnp.swapaxes(w_in, 1, 2)
    w_in_bf = _cast_transposed_weight(w_in_t, COL_GATE)
    wg_bf = jnp.pad(w_in_t[:, COL_GATE:N_IN, :], ((0, 0), (0, GATE_ROWS - 2 * N_HEADS), (0, 0))).astype(BF16)
    w_out_bf = _cast_weight(w_out)
    gate_b = jnp.concatenate([b_i, b_f], axis=-1)
    gbias = jnp.pad(gate_b, ((0, 0), (0, LANES - 2 * N_HEADS))).reshape(DEPTH, 1, LANES)
    gbias_rows = jnp.broadcast_to(gate_b[:, :, None], (DEPTH, 2 * N_HEADS, LANES))

    decay, qd, kd, c_decay, gamma1 = _retention_tables()
    idx = np.arange(CHUNK)
    causal = idx[:, None] >= idx[None, :]
    triu = jnp.asarray(causal.T, BF16)
    mask_add = np.where(causal, 0.0, -np.inf).astype(np.float32)
    e0 = jnp.asarray(np.broadcast_to(np.arange(LANES)[None, :] == 0, (CHUNK, LANES)), BF16)
    tabs_p = _rotary_tables(np.arange(T)) + (decay, qd, kd, c_decay, triu, mask_add, e0)
    tabs_p = tuple(jnp.asarray(a) for a in tabs_p)

    xp = x_prompt
    prompt_states = []
    for l in range(DEPTH):
        xp, prompt_states = _prompt_layer(
            l, xp, w_in_bf, wg_bf, conv_w, conv_b, gbias_rows, g_ret, g_m, w_out_bf, ln_g, ln_b, tabs_p,
            prompt_states)
    rp, cp, np_, mp_pad, vp = prompt_states
    mp = mp_pad[:, :, :N_HEADS, 0]

    rot = jnp.asarray(np.concatenate(_rotary_tables(PAST_LEN + np.arange(Ts)), axis=0))
    bb = SAMPLE_BLOCK_B
    esel = jnp.asarray(np.arange(LANES)[:, None] == (np.arange(bb * HEAD_DIM)[None, :] // HEAD_DIM), BF16)
    gamma1 = jnp.asarray(gamma1)
    m_pad = jnp.pad(state_mlstm_m, ((0, 0), (0, 0), (N_HEADS, LANES - 2 * N_HEADS)))
    ys, rs, cs, ns, ms_pad, vs = _sample_layers(
        x_sample.reshape(Bs, D_MODEL), w_in_bf, wg_bf, conv_w, conv_b, gbias, g_ret, g_m, w_out_bf,
        ln_g, ln_b, rot, gamma1, esel, state_ret, state_mlstm_C, state_mlstm_n, m_pad,
        jnp.swapaxes(state_conv, 1, 2))
    vs = jnp.swapaxes(vs, 1, 2)
    ms = ms_pad[:, :, N_HEADS:2 * N_HEADS]

    return (xp, ys.reshape(Bs, Ts, D_MODEL),
            rp, cp, np_, mp, vp, rs, cs, ns, ms, vs)
```

```python
import functools

import jax
import jax.numpy as jnp
import numpy as np
from jax import lax
from jax.experimental import pallas as pl
from jax.experimental.pallas import tpu as pltpu

F32 = jnp.float32
BF16 = jnp.bfloat16

D_MODEL = 1024
DEPTH = 2
PAST_LEN = 16384
N_HEADS = 4
HEAD_DIM = 128
GROUP_W = N_HEADS * HEAD_DIM
CONV_W = 4
CHUNK = 128
ROPE_BASE = 10000.0
LN_EPS = 1e-5
GN_EPS = 1e-5
ALPHA = (2 * DEPTH) ** 0.25
QK_SCALE = HEAD_DIM ** -0.5

LANES = 128
SUBLANES = 8

COL_RQ = 0
COL_RK = COL_RQ + GROUP_W
COL_RV = COL_RK + GROUP_W
COL_RZ = COL_RV + GROUP_W
COL_MQK = COL_RZ + GROUP_W
COL_MV = COL_MQK + 2 * GROUP_W
COL_MO = COL_MV + GROUP_W
COL_MZ = COL_MO + GROUP_W
COL_GATE = COL_MZ + GROUP_W
N_IN = COL_GATE + 2 * N_HEADS
N_PAD = COL_GATE + LANES
GATE_ROWS = 16

PROMPT_BLOCK_T = 256
PROJ_PIECE_COLS = 256
PROJ_PIECE_ROWS = 256
SAMPLE_BLOCK_B = 16
WEIGHT_CAST_ROWS = 1024
VMEM_LIMIT_BYTES = 56 * 1024 * 1024


def _sigmoid(x):
    return 1.0 / (1.0 + jnp.exp(-x))


def _silu(x):
    return x * _sigmoid(x)


def _log_sigmoid(x):
    return jnp.minimum(x, 0.0) - jnp.log1p(jnp.exp(-jnp.abs(x)))


def _dot(a, b):
    return jnp.dot(a, b, preferred_element_type=F32)


def _dot_nt(a, b):
    return lax.dot_general(a, b, (((1,), (1,)), ((), ())), preferred_element_type=F32)


def _dot_tn(a, b):
    return lax.dot_general(a, b, (((0,), (0,)), ((), ())), preferred_element_type=F32)


def _rotary(x, cos_t, sin_t):
    return x * cos_t + pltpu.roll(x, HEAD_DIM // 2, axis=1) * sin_t


def _head_norm(h, g):
    mu = jnp.mean(h, axis=-1, keepdims=True)
    d = h - mu
    var = jnp.mean(d * d, axis=-1, keepdims=True)
    return d * lax.rsqrt(var + GN_EPS) * g


def _layer_norm(x, g, b):
    mu = jnp.mean(x, axis=-1, keepdims=True)
    d = x - mu
    var = jnp.mean(d * d, axis=-1, keepdims=True)
    return d * lax.rsqrt(var + LN_EPS) * g + b


def _cumsum_lanes(triu_bf, x):
    hi = x.astype(BF16)
    r1 = x - hi.astype(F32)
    mid = r1.astype(BF16)
    lo = (r1 - mid.astype(F32)).astype(BF16)
    return _dot(hi, triu_bf) + _dot(mid, triu_bf) + _dot(lo, triu_bf)


RQK_Q, RQK_QDEC, RQK_K, RQK_KDEC = 0, GROUP_W, 2 * GROUP_W, 3 * GROUP_W
VV_RET, VV_M = 0, GROUP_W
GZ_RET, GZ_M = 0, GROUP_W
GT_COLS, GT_ROWS, GT_PER_CHUNK = 0, 1, 2
GT_ROWS_USED = SUBLANES
EMITS_PER_CHUNK = 9 * N_HEADS + 1
N_STATE_OUTPUTS = 5
HEADS_PER_STAGE = 2
ROT_COS, ROT_SIN, ROT_KCOS, ROT_KSIN = 0, 1, 2, 3


def _prompt_kernel(nt, n_blocks, layer, xn_ref, x_ref, w_in_ref, wg_ref, conv_w_ref, conv_b_ref, gbias_ref,
                   g_ret_ref, g_m_ref, w_out_ref, ln_g_ref, ln_b_ref,
                   rot_ref,
                   decay_ref, qd_ref, kd_ref, cdec_ref, triu_ref, mask_ref, e0_ref, *refs):
    prev_refs = refs[:N_STATE_OUTPUTS] if layer > 0 else ()
    refs = refs[len(prev_refs):]
    y_ref, s_out_ref, c_out_ref, n_out_ref, m_out_ref, conv_out_ref = refs[:1 + N_STATE_OUTPUTS]
    (rqk_a, vv_a, gz_a, qkm_a, gt_a, rqk_b, vv_b, gz_b, qkm_b, gt_b,
     xb_scr, u_scr, mix_scr, s_scr, caug_scr, m_scr) = refs[1 + N_STATE_OUTPUTS:]
    g = pl.program_id(0)
    t = lax.rem(jnp.maximum(g - 1, 0), nt)
    tn = lax.rem(jnp.minimum(g, n_blocks - 1), nt)
    block_t = x_ref.shape[1]
    n_chunks = block_t // CHUNK
    carry_rows = CONV_W - 1
    heads = range(N_HEADS)
    lrow = slice(layer, layer + 1)

    @pl.when(g == 0)
    def _init_pipeline():
        for ref in (rqk_b, vv_b, gz_b, qkm_b, gt_b):
            ref[...] = jnp.zeros_like(ref)

    @pl.when(t == 0)
    def _init_state():
        s_scr[...] = jnp.zeros_like(s_scr)
        caug_scr[...] = jnp.zeros_like(caug_scr)
        m_scr[...] = jnp.zeros_like(m_scr)

    @pl.when(tn == 0)
    def _init_conv_carry():
        u_scr[0:SUBLANES, :] = jnp.zeros((SUBLANES, 2 * GROUP_W), F32)

    def hcols(base, h):
        return slice(base + h * HEAD_DIM, base + (h + 1) * HEAD_DIM)

    def step_body(set_in, set_cur):
        rqk_i, vv_i, gz_i, qkm_i, gt_i = set_in
        rqk_c, vv_c, gz_c, qkm_c, gt_c = set_cur
        triu_bf = triu_ref[...]
        causal_add = mask_ref[...]
        ones_col = e0_ref[...]
        sub_id = lax.broadcasted_iota(jnp.int32, (SUBLANES, LANES), 0)
        pad_rows = jnp.zeros((CHUNK - GT_ROWS_USED, LANES), F32)
        xb_scr[...] = xn_ref[0].astype(BF16)
        pw = PROJ_PIECE_COLS
        heads_per_piece = pw // HEAD_DIM

        pr = PROJ_PIECE_ROWS
        chunks_per_piece = pr // CHUNK

        def group_rows(p):
            return slice(p * pr, (p + 1) * pr)

        def proj(p, col, width=pw):
            return _dot(xb_scr[group_rows(p), :], w_in_ref[:, col:col + width])

        def piece_rot(p, i, col_base, cos_r, sin_r, dec_ref, dst, dst_dec):
            res = proj(p, col_base + i * pw)
            for hh in range(heads_per_piece):
                h = i * heads_per_piece + hh
                for cc in range(chunks_per_piece):
                    rows = slice(p * pr + cc * CHUNK, p * pr + (cc + 1) * CHUNK)
                    r = _rotary(res[cc * CHUNK:(cc + 1) * CHUNK, hh * HEAD_DIM:(hh + 1) * HEAD_DIM],
                                rot_ref[rows, cos_r * LANES:(cos_r + 1) * LANES],
                                rot_ref[rows, sin_r * LANES:(sin_r + 1) * LANES])
                    rqk_i[rows, hcols(dst, h)] = r.astype(BF16)
                    rqk_i[rows, hcols(dst_dec, h)] = (r * dec_ref[h]).astype(BF16)

        def piece_cast(p, i, col_base, dst):
            vv_i[group_rows(p), dst + i * pw:dst + (i + 1) * pw] = proj(p, col_base + i * pw).astype(BF16)

        def piece_rz(p, i):
            gz_i[group_rows(p), GZ_RET + i * pw:GZ_RET + (i + 1) * pw] = _silu(proj(p, COL_RZ + i * pw))

        def piece_moz(p, i):
            og = _sigmoid(proj(p, COL_MO + i * pw))
            gz_i[group_rows(p), GZ_M + i * pw:GZ_M + (i + 1) * pw] = og * _silu(proj(p, COL_MZ + i * pw))

        def piece_conv(p, i):
            base = SUBLANES + p * pr
            u_scr[base:base + pr, i * pw:(i + 1) * pw] = proj(p, COL_MQK + i * pw)
            for cs in range(i * pw, (i + 1) * pw, LANES):
                cols = slice(cs, cs + LANES)
                acc = conv_b_ref[lrow, cols]
                for j in range(CONV_W):
                    r0 = base - carry_rows + j
                    acc = acc + u_scr[r0:r0 + pr, cols] * conv_w_ref[j:j + 1, cols]
                act = _silu(acc)
                if cs >= GROUP_W:
                    act = act * QK_SCALE
                qkm_i[group_rows(p), cols] = act

        def piece_gate(p):
            gates_t = _dot_nt(wg_ref[...], xb_scr[group_rows(p), :])
            for cc in range(chunks_per_piece):
                c = p * chunks_per_piece + cc
                g8 = gates_t[0:SUBLANES, cc * CHUNK:(cc + 1) * CHUNK] + gbias_ref[...]
                rows8 = jnp.where(sub_id < N_HEADS, g8, _cumsum_lanes(triu_bf, _log_sigmoid(g8)))
                gt_i[c * GT_PER_CHUNK + GT_ROWS, 0:GT_ROWS_USED, :] = rows8
                gt_i[c * GT_PER_CHUNK + GT_COLS] = jnp.concatenate([rows8, pad_rows], axis=0).T

        n_col_pieces = GROUP_W // pw
        pieces = []
        for p in range(block_t // pr):
            pieces.append(functools.partial(piece_gate, p))
            for i in range(n_col_pieces):
                pieces.append(functools.partial(piece_conv, p, 2 * i))
                pieces.append(functools.partial(piece_cast, p, i, COL_RV, VV_RET))
                pieces.append(functools.partial(piece_rot, p, i, COL_RQ, ROT_COS, ROT_SIN, qd_ref,
                                                RQK_Q, RQK_QDEC))
                pieces.append(functools.partial(piece_rz, p, i))
                pieces.append(functools.partial(piece_conv, p, 2 * i + 1))
                pieces.append(functools.partial(piece_cast, p, i, COL_MV, VV_M))
                pieces.append(functools.partial(piece_rot, p, i, COL_RK, ROT_KCOS, ROT_KSIN, kd_ref,
                                                RQK_K, RQK_KDEC))
                pieces.append(functools.partial(piece_moz, p, i))
        n_pieces = len(pieces)
        n_slots = EMITS_PER_CHUNK * n_chunks
        slots_done = [0]

        def emit_proj_pieces():
            slots_done[0] += 1
            target = -(-(slots_done[0] * n_pieces) // n_slots)
            while n_pieces - len(pieces) < target:
                pieces.pop(0)()

        def per_head(group, fn):
            out = {}
            for h in group:
                emit_proj_pieces()
                out[h] = fn(h)
            return out

        head_groups = [range(h0, h0 + HEADS_PER_STAGE) for h0 in range(0, N_HEADS, HEADS_PER_STAGE)]

        for c in range(n_chunks):
            rows = slice(c * CHUNK, (c + 1) * CHUNK)

            for grp in head_groups:
                q_bf = {h: rqk_c[rows, hcols(RQK_Q, h)] for h in grp}
                k_bf = {h: rqk_c[rows, hcols(RQK_K, h)] for h in grp}
                v_bf = {h: vv_c[rows, hcols(VV_RET, h)] for h in grp}
                sc = per_head(grp, lambda h: _dot_nt(q_bf[h], k_bf[h]))
                state = {h: s_scr[h] for h in grp}
                upd = per_head(grp, lambda h: _dot_tn(rqk_c[rows, hcols(RQK_KDEC, h)], v_bf[h]))
                for h in grp:
                    s_scr[h] = state[h] * cdec_ref[h] + upd[h]
                o = per_head(grp, lambda h: _dot(
                    jnp.concatenate([(sc[h] * decay_ref[h]).astype(BF16),
                                     rqk_c[rows, hcols(RQK_QDEC, h)]], axis=1),
                    jnp.concatenate([v_bf[h], state[h].astype(BF16)], axis=0)))

                def ret_out(h):
                    hn = _head_norm(o[h], g_ret_ref[lrow, hcols(0, h)])
                    mix_scr[rows, hcols(0, h)] = (hn * gz_c[rows, hcols(GZ_RET, h)]).astype(BF16)

                per_head(grp, ret_out)

            cols_t = gt_c[c * GT_PER_CHUNK + GT_COLS]
            rows_t = gt_c[c * GT_PER_CHUNK + GT_ROWS, 0:GT_ROWS_USED, :]
            for grp in head_groups:
                q = {h: qkm_c[rows, hcols(0, h)] for h in grp}
                k = {h: qkm_c[rows, hcols(GROUP_W, h)] for h in grp}
                vaug_bf = {h: jnp.concatenate([vv_c[rows, hcols(VV_M, h)], ones_col], axis=1)
                           for h in grp}
                qk = per_head(grp, lambda h: _dot_nt(q[h].astype(BF16), k[h].astype(BF16)))
                caug = {h: caug_scr[h] for h in grp}
                m_old = {h: m_scr[h, 0:1, 0:1] for h in grp}
                i_row = {h: rows_t[h:h + 1, :] for h in grp}
                b_row = {h: rows_t[N_HEADS + h:N_HEADS + h + 1, :] for h in grp}
                i_col = {h: cols_t[:, h:h + 1] for h in grp}
                b_col = {h: cols_t[:, N_HEADS + h:N_HEADS + h + 1] for h in grp}
                a_col = {h: b_col[h] + m_old[h] for h in grp}
                dm = per_head(grp, lambda h: (b_col[h] + (i_row[h] - b_row[h])) + causal_add)
                mt = {h: jnp.maximum(a_col[h], jnp.max(dm[h], axis=-1, keepdims=True)) for h in grp}
                w_inter = {h: jnp.exp(a_col[h] - mt[h]) for h in grp}
                res = per_head(grp, lambda h: _dot(
                    jnp.concatenate([(qk[h] * jnp.exp(dm[h] - mt[h])).astype(BF16),
                                     (q[h] * w_inter[h]).astype(BF16)], axis=1),
                    jnp.concatenate([vaug_bf[h], caug[h].astype(BF16)], axis=0)))
                b_last = {h: b_col[h][CHUNK - 1:CHUNK, :] for h in grp}
                g_col = {h: b_last[h] - b_col[h] + i_col[h] for h in grp}
                m_new = {h: jnp.maximum(b_last[h] + m_old[h], jnp.max(g_col[h], axis=0, keepdims=True))
                         for h in grp}
                wk = {h: jnp.exp(g_col[h] - m_new[h]) for h in grp}
                wc = {h: jnp.exp(b_last[h] + m_old[h] - m_new[h]) for h in grp}

                def mlstm_update(h):
                    caug_scr[h] = caug[h] * wc[h] + _dot_tn((k[h] * wk[h]).astype(BF16), vaug_bf[h])
                    m_scr[h] = jnp.broadcast_to(m_new[h], (SUBLANES, LANES))

                per_head(grp, mlstm_update)

                def mlstm_out(h):
                    den = jnp.maximum(jnp.abs(res[h][:, HEAD_DIM:HEAD_DIM + 1]), jnp.exp(-mt[h]))
                    hn = _head_norm(res[h][:, :HEAD_DIM] / den, g_m_ref[lrow, hcols(0, h)])
                    mix_scr[rows, hcols(GROUP_W, h)] = (hn * gz_c[rows, hcols(GZ_M, h)]).astype(BF16)

                per_head(grp, mlstm_out)

            emit_proj_pieces()
            mix = _dot(mix_scr[rows, :], w_out_ref[...])
            y_ref[0, rows, :] = _layer_norm(ALPHA * x_ref[0, rows, :] + mix, ln_g_ref[lrow, :], ln_b_ref[lrow, :])

        assert slots_done[0] == n_slots and not pieces

    set_a = (rqk_a, vv_a, gz_a, qkm_a, gt_a)
    set_b = (rqk_b, vv_b, gz_b, qkm_b, gt_b)
    parity = lax.rem(g, 2)

    @pl.when(parity == 0)
    def _even_step():
        step_body(set_a, set_b)

    @pl.when(parity == 1)
    def _odd_step():
        step_body(set_b, set_a)

    @pl.when(jnp.logical_and(tn == nt - 1, g < n_blocks))
    def _write_conv_state():
        if layer > 0:
            conv_out_ref[0:layer] = prev_refs[-1][...]
        conv_out_ref[layer, 0] = u_scr[SUBLANES + block_t - carry_rows:SUBLANES + block_t, :]

    @pl.when(jnp.logical_and(t == nt - 1, g > 0))
    def _write_state():
        if layer > 0:
            for out_ref, prev_ref in zip((s_out_ref, c_out_ref, n_out_ref, m_out_ref), prev_refs):
                out_ref[0:layer] = prev_ref[...]
        s_out_ref[layer, 0] = s_scr[...]
        m_out_ref[layer, 0] = jnp.zeros((SUBLANES, LANES), F32)
        for h in heads:
            caug = caug_scr[h]
            c_out_ref[layer, 0, h] = caug[:, :HEAD_DIM]
            n_out_ref[layer, 0, h:h + 1, :] = caug[:, HEAD_DIM:].T[0:1, :]
            m_out_ref[layer, 0, h:h + 1, :] = m_scr[h, 0:1, :]

    u_scr[0:SUBLANES, :] = u_scr[block_t:block_t + SUBLANES, :]


def _const_spec(shape):
    nd = len(shape)
    return pl.BlockSpec(shape, lambda g: (0,) * nd)


def _layer_spec(shape, layer):
    return pl.BlockSpec((None,) + tuple(shape[1:]), lambda g: (layer,) + (0,) * (len(shape) - 1))


def _prompt_layer(layer, x, w_in_bf, wg_bf, conv_w, conv_b, gbias, g_ret, g_m, w_out_bf, ln_g, ln_b, tabs,
                  prev_states):
    B, T, _ = x.shape
    bt = PROMPT_BLOCK_T
    assert T % bt == 0 and bt % CHUNK == 0 and GROUP_W % PROJ_PIECE_COLS == 0
    nt = T // bt
    n_blocks = B * nt
    n_chunks = bt // CHUNK
    rot_tab, decay, qd, kd, cdec, triu, mask, e0 = tabs

    def nxt(g):
        return jnp.minimum(g, n_blocks - 1)

    def cur(g):
        return jnp.maximum(g - 1, 0)

    row_spec = pl.BlockSpec((bt, rot_tab.shape[1]), lambda g: (nxt(g) % nt, 0))
    in_specs = [
        pl.BlockSpec((1, bt, D_MODEL), lambda g: (nxt(g) // nt, nxt(g) % nt, 0)),
        pl.BlockSpec((1, bt, D_MODEL), lambda g: (cur(g) // nt, cur(g) % nt, 0)),
        _layer_spec(w_in_bf.shape, layer), _layer_spec(wg_bf.shape, layer),
        _layer_spec(conv_w.shape, layer), _const_spec(conv_b.shape),
        _layer_spec(gbias.shape, layer), _const_spec(g_ret.shape), _const_spec(g_m.shape),
        _layer_spec(w_out_bf.shape, layer), _const_spec(ln_g.shape), _const_spec(ln_b.shape),
        row_spec,
        _const_spec(decay.shape), _const_spec(qd.shape), _const_spec(kd.shape),
        pl.BlockSpec(memory_space=pltpu.SMEM),
        _const_spec(triu.shape), _const_spec(mask.shape), _const_spec(e0.shape),
    ]
    state_tails = ((N_HEADS, HEAD_DIM, HEAD_DIM), (N_HEADS, HEAD_DIM, HEAD_DIM), (N_HEADS, HEAD_DIM),
                   (SUBLANES, LANES), (CONV_W - 1, 2 * GROUP_W))
    assert len(state_tails) == N_STATE_OUTPUTS and len(prev_states) in (0, N_STATE_OUTPUTS)

    def state_spec(depth, tail, batch_of):
        zeros = (0,) * len(tail)
        return pl.BlockSpec((depth, 1) + tail, lambda g: (0, batch_of(g)) + zeros)

    batch_ofs = [lambda g: cur(g) // nt] * (N_STATE_OUTPUTS - 1) + [lambda g: nxt(g) // nt]
    out_shape = (jax.ShapeDtypeStruct((B, T, D_MODEL), F32),) + tuple(
        jax.ShapeDtypeStruct((layer + 1, B) + tail, F32) for tail in state_tails)
    out_specs = (pl.BlockSpec((1, bt, D_MODEL), lambda g: (cur(g) // nt, cur(g) % nt, 0)),) + tuple(
        state_spec(layer + 1, tail, bo) for tail, bo in zip(state_tails, batch_ofs))
    if prev_states:
        in_specs = in_specs + [state_spec(layer, tail, bo) for tail, bo in zip(state_tails, batch_ofs)]
    operand_set = [
        pltpu.VMEM((bt, 4 * GROUP_W), BF16),
        pltpu.VMEM((bt, 2 * GROUP_W), BF16),
        pltpu.VMEM((bt, 2 * GROUP_W), F32),
        pltpu.VMEM((bt, 2 * GROUP_W), F32),
        pltpu.VMEM((n_chunks * GT_PER_CHUNK, CHUNK, LANES), F32),
    ]
    scratch = operand_set + operand_set + [
        pltpu.VMEM((bt, D_MODEL), BF16),
        pltpu.VMEM((SUBLANES + bt, 2 * GROUP_W), F32),
        pltpu.VMEM((bt, 2 * GROUP_W), BF16),
        pltpu.VMEM((N_HEADS, HEAD_DIM, HEAD_DIM), F32),
        pltpu.VMEM((N_HEADS, HEAD_DIM, 2 * HEAD_DIM), F32),
        pltpu.VMEM((N_HEADS, SUBLANES, LANES), F32),
    ]
    y, *states = pl.pallas_call(
        functools.partial(_prompt_kernel, nt, n_blocks, layer),
        grid=(n_blocks + 1,),
        in_specs=in_specs,
        out_specs=out_specs,
        out_shape=out_shape,
        scratch_shapes=scratch,
        compiler_params=pltpu.CompilerParams(
            dimension_semantics=("arbitrary",),
            vmem_limit_bytes=VMEM_LIMIT_BYTES),
        name="prompt_layer",
    )(x, x, w_in_bf, wg_bf, conv_w, conv_b, gbias, g_ret, g_m, w_out_bf, ln_g, ln_b,
      rot_tab, decay, qd, kd, cdec, triu, mask, e0, *prev_states)
    return y, states


def _sample_kernel(x_ref, w_in_ref, wg_ref, conv_w_ref, conv_b_ref, gbias_ref, g_ret_ref, g_m_ref,
                   w_out_ref, ln_g_ref, ln_b_ref, rot_ref, gam_ref, esel_ref,
                   s_ref, c_ref, n_ref, m_ref, cv_ref,
                   y_ref, s_out_ref, c_out_ref, n_out_ref, m_out_ref, cv_out_ref,
                   proj_scr, xcur_scr, mix_scr):
    layer = pl.program_id(0)
    j = pl.program_id(1)
    bb = s_ref.shape[1]

    def layer_row(ref, cols=slice(None)):
        row = ref[0:1, cols]
        for l in range(1, DEPTH):
            row = jnp.where(layer == l, ref[l:l + 1, cols], row)
        return row

    @pl.when(jnp.logical_and(layer == 0, j == 0))
    def _load_x():
        xcur_scr[...] = x_ref[...]

    @pl.when(j == 0)
    def _project():
        xb = xcur_scr[...].astype(BF16)
        for lo in range(0, COL_GATE, 512):
            proj_scr[:, lo:lo + 512] = _dot(xb, w_in_ref[0, :, lo:lo + 512])
        proj_scr[:, COL_GATE:N_PAD] = jnp.zeros((proj_scr.shape[0], N_PAD - COL_GATE), F32)
        proj_scr[:, COL_GATE:COL_GATE + GATE_ROWS] = _dot_nt(xb, wg_ref[0])

    r0 = pl.multiple_of(j * bb, bb)
    rows = pl.ds(r0, bb)
    cos_t = rot_ref[0:1, :]
    sin_t = rot_ref[1:2, :]
    kcos_t = rot_ref[2:3, :]
    ksin_t = rot_ref[3:4, :]
    esel = esel_ref[...]
    row_id = lax.broadcasted_iota(jnp.int32, (bb, bb * HEAD_DIM), 0)
    blk_id = jnp.right_shift(lax.broadcasted_iota(jnp.int32, (bb, bb * HEAD_DIM), 1), 7)
    pad_rows = jnp.zeros((LANES - bb, HEAD_DIM), F32)
    pad_wide = jnp.zeros((LANES - bb, bb * HEAD_DIM), BF16)

    def col_form(x8):
        return jnp.concatenate([x8, pad_rows], axis=0).T.astype(BF16)

    def outer_all(k8, v8):
        vt = jnp.concatenate([v8] * bb, axis=1)
        vsel = jnp.where(row_id == blk_id, vt, 0.0).astype(BF16)
        return _dot(col_form(k8), jnp.concatenate([vsel, pad_wide], axis=0))

    def col_bcast_all(q8):
        return _dot(col_form(q8), esel)

    for h in range(N_HEADS):
        hc = h * HEAD_DIM
        q8 = _rotary(proj_scr[rows, COL_RQ + hc:COL_RQ + hc + HEAD_DIM], cos_t, sin_t)
        k8 = _rotary(proj_scr[rows, COL_RK + hc:COL_RK + hc + HEAD_DIM], kcos_t, ksin_t)
        v8 = proj_scr[rows, COL_RV + hc:COL_RV + hc + HEAD_DIM]
        kv_all = outer_all(k8, v8)
        qc_all = col_bcast_all(q8)
        gamma = gam_ref[h]
        o_rows = []
        for r in range(bb):
            blk = slice(r * HEAD_DIM, (r + 1) * HEAD_DIM)
            s_new = s_ref[0, r, h] * gamma + kv_all[:, blk]
            s_out_ref[0, r, h] = s_new
            o_rows.append(jnp.sum(qc_all[:, blk] * s_new, axis=0, keepdims=True))
        o8 = _head_norm(jnp.concatenate(o_rows, axis=0), layer_row(g_ret_ref, slice(hc, hc + HEAD_DIM)))
        z8 = proj_scr[rows, COL_RZ + hc:COL_RZ + hc + HEAD_DIM]
        mix_scr[rows, hc:hc + HEAD_DIM] = o8 * _silu(z8)

    u8 = proj_scr[rows, COL_MQK:COL_MQK + 2 * GROUP_W]
    acc = layer_row(conv_b_ref) + u8 * conv_w_ref[0, CONV_W - 1:CONV_W, :]
    for jj in range(CONV_W - 1):
        acc = acc + cv_ref[0, jj] * conv_w_ref[0, jj:jj + 1, :]
    for jj in range(1, CONV_W - 1):
        cv_out_ref[0, jj - 1] = cv_ref[0, jj]
    cv_out_ref[0, CONV_W - 2] = u8
    qk8 = _silu(acc)

    gates = proj_scr[rows, COL_GATE:COL_GATE + LANES] + gbias_ref[0]
    i_al = pltpu.roll(gates, N_HEADS, axis=1)
    bm = _log_sigmoid(gates) + m_ref[0]
    m_new = jnp.maximum(bm, i_al)
    wk = jnp.exp(i_al - m_new)
    wc = jnp.exp(bm - m_new)
    einv = jnp.exp(-m_new)
    m_out_ref[0] = m_new
    for h in range(N_HEADS):
        hc = h * HEAD_DIM
        gl = N_HEADS + h
        q8 = qk8[:, hc:hc + HEAD_DIM]
        k8 = qk8[:, GROUP_W + hc:GROUP_W + hc + HEAD_DIM] * QK_SCALE
        v8 = proj_scr[rows, COL_MV + hc:COL_MV + hc + HEAD_DIM]
        wk_h = wk[:, gl:gl + 1]
        wc_h = jnp.broadcast_to(wc[:, gl:gl + 1], (bb, HEAD_DIM))
        kw8 = k8 * wk_h
        kv_all = outer_all(kw8, v8)
        qc_all = col_bcast_all(q8)
        n_new = n_ref[0, :, h, :] * wc_h + kw8
        n_out_ref[0, :, h, :] = n_new
        num_rows = []
        for r in range(bb):
            blk = slice(r * HEAD_DIM, (r + 1) * HEAD_DIM)
            c_new = c_ref[0, r, h] * wc_h[r:r + 1, :] + kv_all[:, blk]
            c_out_ref[0, r, h] = c_new
            num_rows.append(jnp.sum(qc_all[:, blk] * c_new, axis=0, keepdims=True))
        num = jnp.concatenate(num_rows, axis=0)
        q_bf = q8.astype(BF16).astype(F32)
        den = jnp.sum(q_bf * n_new, axis=-1, keepdims=True)
        hout = num / jnp.maximum(jnp.abs(den), einv[:, gl:gl + 1])
        hn = _head_norm(hout, layer_row(g_m_ref, slice(hc, hc + HEAD_DIM)))
        og = _sigmoid(proj_scr[rows, COL_MO + hc:COL_MO + hc + HEAD_DIM])
        zg = _silu(proj_scr[rows, COL_MZ + hc:COL_MZ + hc + HEAD_DIM])
        mix_scr[rows, GROUP_W + hc:GROUP_W + hc + HEAD_DIM] = hn * og * zg

    @pl.when(j == pl.num_programs(1) - 1)
    def _finish_layer():
        mix = _dot(mix_scr[...].astype(BF16), w_out_ref[0])
        y = _layer_norm(ALPHA * xcur_scr[...] + mix, layer_row(ln_g_ref), layer_row(ln_b_ref))
        xcur_scr[...] = y
        y_ref[...] = y


def _sample_layers(x, w_in_bf, wg_bf, conv_w, conv_b, gbias, g_ret, g_m, w_out_bf, ln_g, ln_b,
                   rot, gam, esel, state_ret, state_c, state_n, m_pad, state_conv):
    bs = x.shape[0]
    bb = SAMPLE_BLOCK_B
    assert bs % bb == 0
    nb = bs // bb

    def lspec(shape, buffers=None):
        nd = len(shape)
        mode = None if buffers is None else pl.Buffered(buffers)
        return pl.BlockSpec((1,) + tuple(shape[1:]), lambda l, j: (l,) + (0,) * (nd - 1),
                            pipeline_mode=mode)

    def cspec(shape):
        nd = len(shape)
        return pl.BlockSpec(tuple(shape), lambda l, j: (0,) * nd)

    mat_spec = pl.BlockSpec((1, bb, N_HEADS, HEAD_DIM, HEAD_DIM), lambda l, j: (l, j, 0, 0, 0))
    n_spec = pl.BlockSpec((1, bb, N_HEADS, HEAD_DIM), lambda l, j: (l, j, 0, 0))
    m_spec = pl.BlockSpec((1, bb, LANES), lambda l, j: (l, j, 0))
    cv_spec = pl.BlockSpec((1, CONV_W - 1, bb, 2 * GROUP_W), lambda l, j: (l, 0, j, 0))
    in_specs = [
        cspec(x.shape), lspec(w_in_bf.shape, 1), lspec(wg_bf.shape),
        lspec(conv_w.shape), cspec(conv_b.shape),
        lspec(gbias.shape), cspec(g_ret.shape), cspec(g_m.shape), lspec(w_out_bf.shape, 1),
        cspec(ln_g.shape), cspec(ln_b.shape), cspec(rot.shape),
        pl.BlockSpec(memory_space=pltpu.SMEM), cspec(esel.shape),
        mat_spec, mat_spec, n_spec, m_spec, cv_spec,
    ]
    out_shape = (
        jax.ShapeDtypeStruct(x.shape, F32),
        jax.ShapeDtypeStruct(state_ret.shape, F32),
        jax.ShapeDtypeStruct(state_c.shape, F32),
        jax.ShapeDtypeStruct(state_n.shape, F32),
        jax.ShapeDtypeStruct(m_pad.shape, F32),
        jax.ShapeDtypeStruct(state_conv.shape, F32),
    )
    out_specs = (cspec(x.shape), mat_spec, mat_spec, n_spec, m_spec, cv_spec)
    scratch = [
        pltpu.VMEM((bs, N_PAD), F32),
        pltpu.VMEM((bs, D_MODEL), F32),
        pltpu.VMEM((bs, 2 * GROUP_W), F32),
    ]
    return pl.pallas_call(
        _sample_kernel,
        grid=(DEPTH, nb),
        in_specs=in_specs,
        out_specs=out_specs,
        out_shape=out_shape,
        scratch_shapes=scratch,
        compiler_params=pltpu.CompilerParams(
            dimension_semantics=("arbitrary", "arbitrary"),
            vmem_limit_bytes=VMEM_LIMIT_BYTES),
        name="sample_layers",
    )(x, w_in_bf, wg_bf, conv_w, conv_b, gbias, g_ret, g_m, w_out_bf, ln_g, ln_b,
      rot, gam, esel, state_ret, state_c, state_n, m_pad, state_conv)


def _cast_weight_kernel(w_ref, o_ref):
    o_ref[0] = w_ref[0].astype(BF16)


def _cast_weight(w):
    depth, k, n = w.shape
    rows = WEIGHT_CAST_ROWS
    assert k % rows == 0
    return pl.pallas_call(
        _cast_weight_kernel,
        grid=(depth, k // rows),
        in_specs=[pl.BlockSpec((1, rows, n), lambda l, r: (l, r, 0))],
        out_specs=pl.BlockSpec((1, rows, n), lambda l, r: (l, r, 0)),
        out_shape=jax.ShapeDtypeStruct((depth, k, n), BF16),
        compiler_params=pltpu.CompilerParams(dimension_semantics=("parallel", "parallel")),
        name="cast_weight",
    )(w)


def _cast_transposed_weight_kernel(n_valid, wt_ref, o_ref):
    rows = wt_ref.shape[1]
    row_id = pl.program_id(1) * rows + lax.broadcasted_iota(jnp.int32, wt_ref.shape[1:], 0)
    wt = jnp.where(row_id < n_valid, wt_ref[0], 0.0)
    o_ref[0] = wt.astype(BF16).T


def _cast_transposed_weight(wt, n_out):
    depth, n, k = wt.shape
    rows = WEIGHT_CAST_ROWS
    assert n_out % LANES == 0 and n_out - n < LANES
    return pl.pallas_call(
        functools.partial(_cast_transposed_weight_kernel, min(n, n_out)),
        grid=(depth, pl.cdiv(n_out, rows)),
        in_specs=[pl.BlockSpec((1, rows, k), lambda l, r: (l, r, 0))],
        out_specs=pl.BlockSpec((1, k, rows), lambda l, r: (l, 0, r)),
        out_shape=jax.ShapeDtypeStruct((depth, k, n_out), BF16),
        compiler_params=pltpu.CompilerParams(dimension_semantics=("parallel", "parallel")),
        name="cast_weight_t",
    )(wt)


def _rotary_tables(pos):
    half = HEAD_DIM // 2
    inv = np.float64(ROPE_BASE) ** (-np.arange(half, dtype=np.float64) / half)
    ang = pos.astype(np.float64)[:, None] * inv[None, :]
    cos = np.cos(ang)
    sin = np.sin(ang)
    cos_t = np.concatenate([cos, cos], axis=-1)
    sin_t = np.concatenate([-sin, sin], axis=-1)
    tabs = (cos_t, sin_t, cos_t * QK_SCALE, sin_t * QK_SCALE)
    return tuple(t.astype(np.float32) for t in tabs)


def _retention_tables():
    L = CHUNK
    f32 = np.float32
    log_gamma = np.log(1.0 - 2.0 ** (-5.0 - np.arange(N_HEADS, dtype=np.float64)))
    idx = np.arange(L, dtype=np.float64)
    diff = idx[:, None] - idx[None, :]
    decay = (np.exp(log_gamma[:, None, None] * np.maximum(diff, 0.0)) * (diff >= 0)).astype(f32)
    q_decay = np.exp(log_gamma[:, None] * (idx + 1.0)).astype(f32)
    k_decay = np.exp(log_gamma[:, None] * (L - 1.0 - idx)).astype(f32)
    c_decay = np.exp(log_gamma * L).astype(f32)
    qd = np.ascontiguousarray(np.broadcast_to(q_decay[:, :, None], (N_HEADS, L, HEAD_DIM)))
    kd = np.ascontiguousarray(np.broadcast_to(k_decay[:, :, None], (N_HEADS, L, HEAD_DIM)))
    gamma1 = np.exp(log_gamma).astype(f32)
    return decay, qd, kd, c_decay, gamma1


def kernel(x_prompt, x_sample, state_ret, state_mlstm_C, state_mlstm_n, state_mlstm_m, state_conv,
           w_in, conv_w, conv_b, b_i, b_f, g_ret, g_m, w_out, ln_g, ln_b):
    B, T, _ = x_prompt.shape
    Bs, Ts, _ = x_sample.shape
    assert Ts == 1

    w_in_t = jnp.swapaxes(w_in, 1, 2)
    w_in_bf = _cast_transposed_weight(w_in_t, COL_GATE)
    wg_bf = jnp.pad(w_in_t[:, COL_GATE:N_IN, :], ((0, 0), (0, GATE_ROWS - 2 * N_HEADS), (0, 0))).astype(BF16)
    w_out_bf = _cast_weight(w_out)
    gate_b = jnp.concatenate([b_i, b_f], axis=-1)
    gbias = jnp.pad(gate_b, ((0, 0), (0, LANES - 2 * N_HEADS))).reshape(DEPTH, 1, LANES)
    gbias_rows = jnp.broadcast_to(gate_b[:, :, None], (DEPTH, 2 * N_HEADS, LANES))

    decay, qd, kd, c_decay, gamma1 = _retention_tables()
    idx = np.arange(CHUNK)
    causal = idx[:, None] >= idx[None, :]
    triu = jnp.asarray(causal.T, BF16)
    mask_add = np.where(causal, 0.0, -np.inf).astype(np.float32)
    e0 = jnp.asarray(np.broadcast_to(np.arange(LANES)[None, :] == 0, (CHUNK, LANES)), BF16)
    rot_tab = np.concatenate(_rotary_tables(np.arange(T)), axis=1)
    tabs_p = (rot_tab, decay, qd, kd, c_decay, triu, mask_add, e0)
    tabs_p = tuple(jnp.asarray(a) for a in tabs_p)

    xp = x_prompt
    prompt_states = []
    for l in range(DEPTH):
        xp, prompt_states = _prompt_layer(
            l, xp, w_in_bf, wg_bf, conv_w, conv_b, gbias_rows, g_ret, g_m, w_out_bf, ln_g, ln_b, tabs_p,
            prompt_states)
    rp, cp, np_, mp_pad, vp = prompt_states
    mp = mp_pad[:, :, :N_HEADS, 0]

    rot = jnp.asarray(np.concatenate(_rotary_tables(PAST_LEN + np.arange(Ts)), axis=0))
    bb = SAMPLE_BLOCK_B
    esel = jnp.asarray(np.arange(LANES)[:, None] == (np.arange(bb * HEAD_DIM)[None, :] // HEAD_DIM), BF16)
    gamma1 = jnp.asarray(gamma1)
    m_pad = jnp.pad(state_mlstm_m, ((0, 0), (0, 0), (N_HEADS, LANES - 2 * N_HEADS)))
    ys, rs, cs, ns, ms_pad, vs = _sample_layers(
        x_sample.reshape(Bs, D_MODEL), w_in_bf, wg_bf, conv_w, conv_b, gbias, g_ret, g_m, w_out_bf,
        ln_g, ln_b, rot, gamma1, esel, state_ret, state_mlstm_C, state_mlstm_n, m_pad,
        jnp.swapaxes(state_conv, 1, 2))
    vs = jnp.swapaxes(vs, 1, 2)
    ms = ms_pad[:, :, N_HEADS:2 * N_HEADS]

    return (xp, ys.reshape(Bs, Ts, D_MODEL),
            rp, cp, np_, mp, vp, rs, cs, ns, ms, vs)
```

```python
import functools

import jax
import jax.numpy as jnp
import numpy as np
from jax import lax
from jax.experimental import pallas as pl
from jax.experimental.pallas import tpu as pltpu

F32 = jnp.float32
BF16 = jnp.bfloat16

D_MODEL = 1024
DEPTH = 2
PAST_LEN = 16384
N_HEADS = 4
HEAD_DIM = 128
GROUP_W = N_HEADS * HEAD_DIM
CONV_W = 4
CHUNK = 128
ROPE_BASE = 10000.0
LN_EPS = 1e-5
GN_EPS = 1e-5
ALPHA = (2 * DEPTH) ** 0.25
QK_SCALE = HEAD_DIM ** -0.5
LOG_QK_SCALE = -0.5 * float(np.log(HEAD_DIM))

LANES = 128
SUBLANES = 8

COL_RQ = 0
COL_RK = COL_RQ + GROUP_W
COL_RV = COL_RK + GROUP_W
COL_RZ = COL_RV + GROUP_W
COL_MQK = COL_RZ + GROUP_W
COL_MV = COL_MQK + 2 * GROUP_W
COL_MO = COL_MV + GROUP_W
COL_MZ = COL_MO + GROUP_W
COL_GATE = COL_MZ + GROUP_W
N_IN = COL_GATE + 2 * N_HEADS
N_PAD = COL_GATE + LANES
GATE_ROWS = 16

PROMPT_BLOCK_T = 256
PROJ_PIECE_COLS = 256
PROJ_PIECE_ROWS = 256
SAMPLE_BLOCK_B = 16
WEIGHT_CAST_ROWS = 1024
VMEM_LIMIT_BYTES = 56 * 1024 * 1024


def _sigmoid(x):
    return 1.0 / (1.0 + jnp.exp(-x))


def _silu(x):
    return x * _sigmoid(x)


def _log_sigmoid(x):
    return jnp.minimum(x, 0.0) - jnp.log1p(jnp.exp(-jnp.abs(x)))


def _dot(a, b):
    return jnp.dot(a, b, preferred_element_type=F32)


def _dot_nt(a, b):
    return lax.dot_general(a, b, (((1,), (1,)), ((), ())), preferred_element_type=F32)


def _dot_tn(a, b):
    return lax.dot_general(a, b, (((0,), (0,)), ((), ())), preferred_element_type=F32)


def _rotary(x, cos_t, sin_t):
    return x * cos_t + pltpu.roll(x, HEAD_DIM // 2, axis=1) * sin_t


def _head_norm(h, g):
    mu = jnp.mean(h, axis=-1, keepdims=True)
    d = h - mu
    var = jnp.mean(d * d, axis=-1, keepdims=True)
    return d * lax.rsqrt(var + GN_EPS) * g


def _layer_norm(x, g, b):
    mu = jnp.mean(x, axis=-1, keepdims=True)
    d = x - mu
    var = jnp.mean(d * d, axis=-1, keepdims=True)
    return d * lax.rsqrt(var + LN_EPS) * g + b


def _cumsum_lanes(triu_bf, x):
    hi = x.astype(BF16)
    r1 = x - hi.astype(F32)
    mid = r1.astype(BF16)
    lo = (r1 - mid.astype(F32)).astype(BF16)
    return _dot(hi, triu_bf) + _dot(mid, triu_bf) + _dot(lo, triu_bf)


RQK_Q, RQK_QDEC, RQK_K, RQK_KDEC = 0, GROUP_W, 2 * GROUP_W, 3 * GROUP_W
VV_RET, VV_M = 0, GROUP_W
GZ_RET, GZ_M = 0, GROUP_W
GT_COLS, GT_ROWS, GT_PER_CHUNK = 0, 1, 2
GT_ROWS_USED = SUBLANES
EMITS_PER_CHUNK = 9 * N_HEADS + 1
N_STATE_OUTPUTS = 5
HEADS_PER_STAGE = 2


def _prompt_kernel(nt, n_blocks, layer, xn_ref, x_ref, w_in_ref, wg_ref, conv_w_ref, conv_b_ref, gbias_ref,
                   g_ret_ref, g_m_ref, w_out_ref, ln_g_ref, ln_b_ref,
                   cos_ref, sin_ref, kcos_ref, ksin_ref,
                   decay_ref, qd_ref, kd_ref, cdec_ref, triu_ref, mask_ref, e0_ref, *refs):
    prev_refs = refs[:N_STATE_OUTPUTS] if layer > 0 else ()
    refs = refs[len(prev_refs):]
    y_ref, s_out_ref, c_out_ref, n_out_ref, m_out_ref, conv_out_ref = refs[:1 + N_STATE_OUTPUTS]
    (rqk_a, vv_a, gz_a, qkm_a, gt_a, rqk_b, vv_b, gz_b, qkm_b, gt_b,
     xb_scr, u_scr, mix_scr, s_scr, caug_scr, m_scr) = refs[1 + N_STATE_OUTPUTS:]
    g = pl.program_id(0)
    t = lax.rem(jnp.maximum(g - 1, 0), nt)
    tn = lax.rem(jnp.minimum(g, n_blocks - 1), nt)
    block_t = x_ref.shape[1]
    n_chunks = block_t // CHUNK
    carry_rows = CONV_W - 1
    heads = range(N_HEADS)
    lrow = slice(layer, layer + 1)

    @pl.when(g == 0)
    def _init_pipeline():
        for ref in (rqk_b, vv_b, gz_b, qkm_b, gt_b):
            ref[...] = jnp.zeros_like(ref)

    @pl.when(t == 0)
    def _init_state():
        s_scr[...] = jnp.zeros_like(s_scr)
        caug_scr[...] = jnp.zeros_like(caug_scr)
        m_scr[...] = jnp.zeros_like(m_scr)

    @pl.when(tn == 0)
    def _init_conv_carry():
        u_scr[0:SUBLANES, :] = jnp.zeros((SUBLANES, 2 * GROUP_W), F32)

    def hcols(base, h):
        return slice(base + h * HEAD_DIM, base + (h + 1) * HEAD_DIM)

    def step_body(set_in, set_cur):
        rqk_i, vv_i, gz_i, qkm_i, gt_i = set_in
        rqk_c, vv_c, gz_c, qkm_c, gt_c = set_cur
        triu_bf = triu_ref[...]
        causal_add = mask_ref[...]
        ones_col = e0_ref[...]
        sub_id = lax.broadcasted_iota(jnp.int32, (SUBLANES, LANES), 0)
        pad_rows = jnp.zeros((CHUNK - GT_ROWS_USED, LANES), F32)
        xb_scr[...] = xn_ref[0].astype(BF16)
        pw = PROJ_PIECE_COLS
        heads_per_piece = pw // HEAD_DIM

        pr = PROJ_PIECE_ROWS
        chunks_per_piece = pr // CHUNK

        def group_rows(p):
            return slice(p * pr, (p + 1) * pr)

        def proj(p, col, width=pw):
            return _dot(xb_scr[group_rows(p), :], w_in_ref[:, col:col + width])

        def piece_rot(p, i, col_base, cos_r, sin_r, dec_ref, dst, dst_dec):
            res = proj(p, col_base + i * pw)
            for hh in range(heads_per_piece):
                h = i * heads_per_piece + hh
                for cc in range(chunks_per_piece):
                    rows = slice(p * pr + cc * CHUNK, p * pr + (cc + 1) * CHUNK)
                    r = _rotary(res[cc * CHUNK:(cc + 1) * CHUNK, hh * HEAD_DIM:(hh + 1) * HEAD_DIM],
                                cos_r[rows, :], sin_r[rows, :])
                    rqk_i[rows, hcols(dst, h)] = r.astype(BF16)
                    rqk_i[rows, hcols(dst_dec, h)] = (r * dec_ref[h]).astype(BF16)

        def piece_cast(p, i, col_base, dst):
            vv_i[group_rows(p), dst + i * pw:dst + (i + 1) * pw] = proj(p, col_base + i * pw).astype(BF16)

        def piece_rz(p, i):
            gz_i[group_rows(p), GZ_RET + i * pw:GZ_RET + (i + 1) * pw] = _silu(proj(p, COL_RZ + i * pw))

        def piece_moz(p, i):
            og = _sigmoid(proj(p, COL_MO + i * pw))
            gz_i[group_rows(p), GZ_M + i * pw:GZ_M + (i + 1) * pw] = og * _silu(proj(p, COL_MZ + i * pw))

        def piece_conv(p, i):
            base = SUBLANES + p * pr
            u_scr[base:base + pr, i * pw:(i + 1) * pw] = proj(p, COL_MQK + i * pw)
            for cs in range(i * pw, (i + 1) * pw, LANES):
                cols = slice(cs, cs + LANES)
                acc = conv_b_ref[lrow, cols]
                for j in range(CONV_W):
                    r0 = base - carry_rows + j
                    acc = acc + u_scr[r0:r0 + pr, cols] * conv_w_ref[j:j + 1, cols]
                qkm_i[group_rows(p), cols] = _silu(acc)

        def piece_gate(p):
            gates_t = _dot_nt(wg_ref[...], xb_scr[group_rows(p), :])
            for cc in range(chunks_per_piece):
                c = p * chunks_per_piece + cc
                g8 = gates_t[0:SUBLANES, cc * CHUNK:(cc + 1) * CHUNK] + gbias_ref[...]
                rows8 = jnp.where(sub_id < N_HEADS, g8, _cumsum_lanes(triu_bf, _log_sigmoid(g8)))
                gt_i[c * GT_PER_CHUNK + GT_ROWS, 0:GT_ROWS_USED, :] = rows8
                gt_i[c * GT_PER_CHUNK + GT_COLS] = jnp.concatenate([rows8, pad_rows], axis=0).T

        n_col_pieces = GROUP_W // pw
        pieces = []
        for p in range(block_t // pr):
            pieces.append(functools.partial(piece_gate, p))
            for i in range(n_col_pieces):
                pieces.append(functools.partial(piece_conv, p, 2 * i))
                pieces.append(functools.partial(piece_cast, p, i, COL_RV, VV_RET))
                pieces.append(functools.partial(piece_rot, p, i, COL_RQ, cos_ref, sin_ref, qd_ref,
                                                RQK_Q, RQK_QDEC))
                pieces.append(functools.partial(piece_rz, p, i))
                pieces.append(functools.partial(piece_conv, p, 2 * i + 1))
                pieces.append(functools.partial(piece_cast, p, i, COL_MV, VV_M))
                pieces.append(functools.partial(piece_rot, p, i, COL_RK, kcos_ref, ksin_ref, kd_ref,
                                                RQK_K, RQK_KDEC))
                pieces.append(functools.partial(piece_moz, p, i))
        n_pieces = len(pieces)
        n_slots = EMITS_PER_CHUNK * n_chunks
        slots_done = [0]

        def emit_proj_pieces():
            slots_done[0] += 1
            target = -(-(slots_done[0] * n_pieces) // n_slots)
            while n_pieces - len(pieces) < target:
                pieces.pop(0)()

        def per_head(group, fn):
            out = {}
            for h in group:
                emit_proj_pieces()
                out[h] = fn(h)
            return out

        head_groups = [range(h0, h0 + HEADS_PER_STAGE) for h0 in range(0, N_HEADS, HEADS_PER_STAGE)]

        for c in range(n_chunks):
            rows = slice(c * CHUNK, (c + 1) * CHUNK)

            for grp in head_groups:
                q_bf = {h: rqk_c[rows, hcols(RQK_Q, h)] for h in grp}
                k_bf = {h: rqk_c[rows, hcols(RQK_K, h)] for h in grp}
                v_bf = {h: vv_c[rows, hcols(VV_RET, h)] for h in grp}
                sc = per_head(grp, lambda h: _dot_nt(q_bf[h], k_bf[h]))
                state = {h: s_scr[h] for h in grp}
                upd = per_head(grp, lambda h: _dot_tn(rqk_c[rows, hcols(RQK_KDEC, h)], v_bf[h]))
                for h in grp:
                    s_scr[h] = state[h] * cdec_ref[h] + upd[h]
                o = per_head(grp, lambda h: _dot(
                    jnp.concatenate([(sc[h] * decay_ref[h]).astype(BF16),
                                     rqk_c[rows, hcols(RQK_QDEC, h)]], axis=1),
                    jnp.concatenate([v_bf[h], state[h].astype(BF16)], axis=0)))

                def ret_out(h):
                    hn = _head_norm(o[h], g_ret_ref[lrow, hcols(0, h)])
                    mix_scr[rows, hcols(0, h)] = (hn * gz_c[rows, hcols(GZ_RET, h)]).astype(BF16)

                per_head(grp, ret_out)

            cols_t = gt_c[c * GT_PER_CHUNK + GT_COLS]
            rows_t = gt_c[c * GT_PER_CHUNK + GT_ROWS, 0:GT_ROWS_USED, :]
            for grp in head_groups:
                q = {h: qkm_c[rows, hcols(0, h)] for h in grp}
                k = {h: qkm_c[rows, hcols(GROUP_W, h)] for h in grp}
                vaug_bf = {h: jnp.concatenate([vv_c[rows, hcols(VV_M, h)], ones_col], axis=1)
                           for h in grp}
                qk = per_head(grp, lambda h: _dot_nt(q[h].astype(BF16), k[h].astype(BF16)))
                caug = {h: caug_scr[h] for h in grp}
                m_old = {h: m_scr[h, 0:1, 0:1] for h in grp}
                i_row = {h: rows_t[h:h + 1, :] for h in grp}
                b_row = {h: rows_t[N_HEADS + h:N_HEADS + h + 1, :] for h in grp}
                i_col = {h: cols_t[:, h:h + 1] for h in grp}
                b_col = {h: cols_t[:, N_HEADS + h:N_HEADS + h + 1] for h in grp}
                a_col = {h: b_col[h] + m_old[h] for h in grp}
                dm = per_head(grp, lambda h: (b_col[h] + (i_row[h] - b_row[h])) + causal_add)
                mt = {h: jnp.maximum(a_col[h], jnp.max(dm[h], axis=-1, keepdims=True)) for h in grp}
                w_inter = {h: jnp.exp(a_col[h] - mt[h]) for h in grp}
                res = per_head(grp, lambda h: _dot(
                    jnp.concatenate([(qk[h] * jnp.exp(dm[h] - mt[h])).astype(BF16),
                                     (q[h] * w_inter[h]).astype(BF16)], axis=1),
                    jnp.concatenate([vaug_bf[h], caug[h].astype(BF16)], axis=0)))
                b_last = {h: b_col[h][CHUNK - 1:CHUNK, :] for h in grp}
                g_col = {h: b_last[h] - b_col[h] + i_col[h] for h in grp}
                m_new = {h: jnp.maximum(b_last[h] + m_old[h], jnp.max(g_col[h], axis=0, keepdims=True))
                         for h in grp}
                wk = {h: jnp.exp(g_col[h] - (m_new[h] - LOG_QK_SCALE)) for h in grp}
                wc = {h: jnp.exp(b_last[h] + m_old[h] - m_new[h]) for h in grp}

                def mlstm_update(h):
                    caug_scr[h] = caug[h] * wc[h] + _dot_tn((k[h] * wk[h]).astype(BF16), vaug_bf[h])
                    m_scr[h] = jnp.broadcast_to(m_new[h], (SUBLANES, LANES))

                per_head(grp, mlstm_update)

                def mlstm_out(h):
                    den = jnp.maximum(jnp.abs(res[h][:, HEAD_DIM:HEAD_DIM + 1]), jnp.exp(-mt[h]))
                    hn = _head_norm(res[h][:, :HEAD_DIM] / den, g_m_ref[lrow, hcols(0, h)])
                    mix_scr[rows, hcols(GROUP_W, h)] = (hn * gz_c[rows, hcols(GZ_M, h)]).astype(BF16)

                per_head(grp, mlstm_out)

            emit_proj_pieces()
            mix = _dot(mix_scr[rows, :], w_out_ref[...])
            y_ref[0, rows, :] = _layer_norm(ALPHA * x_ref[0, rows, :] + mix, ln_g_ref[lrow, :], ln_b_ref[lrow, :])

        assert slots_done[0] == n_slots and not pieces

    set_a = (rqk_a, vv_a, gz_a, qkm_a, gt_a)
    set_b = (rqk_b, vv_b, gz_b, qkm_b, gt_b)
    parity = lax.rem(g, 2)

    @pl.when(parity == 0)
    def _even_step():
        step_body(set_a, set_b)

    @pl.when(parity == 1)
    def _odd_step():
        step_body(set_b, set_a)

    @pl.when(jnp.logical_and(tn == nt - 1, g < n_blocks))
    def _write_conv_state():
        if layer > 0:
            conv_out_ref[0:layer] = prev_refs[-1][...]
        conv_out_ref[layer, 0] = u_scr[SUBLANES + block_t - carry_rows:SUBLANES + block_t, :]

    @pl.when(jnp.logical_and(t == nt - 1, g > 0))
    def _write_state():
        if layer > 0:
            for out_ref, prev_ref in zip((s_out_ref, c_out_ref, n_out_ref, m_out_ref), prev_refs):
                out_ref[0:layer] = prev_ref[...]
        s_out_ref[layer, 0] = s_scr[...]
        m_out_ref[layer, 0] = jnp.zeros((SUBLANES, LANES), F32)
        for h in heads:
            caug = caug_scr[h]
            c_out_ref[layer, 0, h] = caug[:, :HEAD_DIM]
            n_out_ref[layer, 0, h:h + 1, :] = caug[:, HEAD_DIM:].T[0:1, :]
            m_out_ref[layer, 0, h:h + 1, :] = m_scr[h, 0:1, :]

    u_scr[0:SUBLANES, :] = u_scr[block_t:block_t + SUBLANES, :]


def _const_spec(shape):
    nd = len(shape)
    return pl.BlockSpec(shape, lambda g: (0,) * nd)


def _layer_spec(shape, layer):
    return pl.BlockSpec((None,) + tuple(shape[1:]), lambda g: (layer,) + (0,) * (len(shape) - 1))


def _prompt_layer(layer, x, w_in_bf, wg_bf, conv_w, conv_b, gbias, g_ret, g_m, w_out_bf, ln_g, ln_b, tabs,
                  prev_states):
    B, T, _ = x.shape
    bt = PROMPT_BLOCK_T
    assert T % bt == 0 and bt % CHUNK == 0 and GROUP_W % PROJ_PIECE_COLS == 0
    nt = T // bt
    n_blocks = B * nt
    n_chunks = bt // CHUNK
    cos_t, sin_t, kcos_t, ksin_t, decay, qd, kd, cdec, triu, mask, e0 = tabs

    def nxt(g):
        return jnp.minimum(g, n_blocks - 1)

    def cur(g):
        return jnp.maximum(g - 1, 0)

    row_spec = pl.BlockSpec((bt, LANES), lambda g: (nxt(g) % nt, 0))
    in_specs = [
        pl.BlockSpec((1, bt, D_MODEL), lambda g: (nxt(g) // nt, nxt(g) % nt, 0)),
        pl.BlockSpec((1, bt, D_MODEL), lambda g: (cur(g) // nt, cur(g) % nt, 0)),
        _layer_spec(w_in_bf.shape, layer), _layer_spec(wg_bf.shape, layer),
        _layer_spec(conv_w.shape, layer), _const_spec(conv_b.shape),
        _layer_spec(gbias.shape, layer), _const_spec(g_ret.shape), _const_spec(g_m.shape),
        _layer_spec(w_out_bf.shape, layer), _const_spec(ln_g.shape), _const_spec(ln_b.shape),
        row_spec, row_spec, row_spec, row_spec,
        _const_spec(decay.shape), _const_spec(qd.shape), _const_spec(kd.shape),
        pl.BlockSpec(memory_space=pltpu.SMEM),
        _const_spec(triu.shape), _const_spec(mask.shape), _const_spec(e0.shape),
    ]
    state_tails = ((N_HEADS, HEAD_DIM, HEAD_DIM), (N_HEADS, HEAD_DIM, HEAD_DIM), (N_HEADS, HEAD_DIM),
                   (SUBLANES, LANES), (CONV_W - 1, 2 * GROUP_W))
    assert len(state_tails) == N_STATE_OUTPUTS and len(prev_states) in (0, N_STATE_OUTPUTS)

    def state_spec(depth, tail, batch_of):
        zeros = (0,) * len(tail)
        return pl.BlockSpec((depth, 1) + tail, lambda g: (0, batch_of(g)) + zeros)

    batch_ofs = [lambda g: cur(g) // nt] * (N_STATE_OUTPUTS - 1) + [lambda g: nxt(g) // nt]
    out_shape = (jax.ShapeDtypeStruct((B, T, D_MODEL), F32),) + tuple(
        jax.ShapeDtypeStruct((layer + 1, B) + tail, F32) for tail in state_tails)
    out_specs = (pl.BlockSpec((1, bt, D_MODEL), lambda g: (cur(g) // nt, cur(g) % nt, 0)),) + tuple(
        state_spec(layer + 1, tail, bo) for tail, bo in zip(state_tails, batch_ofs))
    if prev_states:
        in_specs = in_specs + [state_spec(layer, tail, bo) for tail, bo in zip(state_tails, batch_ofs)]
    operand_set = [
        pltpu.VMEM((bt, 4 * GROUP_W), BF16),
        pltpu.VMEM((bt, 2 * GROUP_W), BF16),
        pltpu.VMEM((bt, 2 * GROUP_W), F32),
        pltpu.VMEM((bt, 2 * GROUP_W), F32),
        pltpu.VMEM((n_chunks * GT_PER_CHUNK, CHUNK, LANES), F32),
    ]
    scratch = operand_set + operand_set + [
        pltpu.VMEM((bt, D_MODEL), BF16),
        pltpu.VMEM((SUBLANES + bt, 2 * GROUP_W), F32),
        pltpu.VMEM((bt, 2 * GROUP_W), BF16),
        pltpu.VMEM((N_HEADS, HEAD_DIM, HEAD_DIM), F32),
        pltpu.VMEM((N_HEADS, HEAD_DIM, 2 * HEAD_DIM), F32),
        pltpu.VMEM((N_HEADS, SUBLANES, LANES), F32),
    ]
    y, *states = pl.pallas_call(
        functools.partial(_prompt_kernel, nt, n_blocks, layer),
        grid=(n_blocks + 1,),
        in_specs=in_specs,
        out_specs=out_specs,
        out_shape=out_shape,
        scratch_shapes=scratch,
        compiler_params=pltpu.CompilerParams(
            dimension_semantics=("arbitrary",),
            vmem_limit_bytes=VMEM_LIMIT_BYTES),
        name="prompt_layer",
    )(x, x, w_in_bf, wg_bf, conv_w, conv_b, gbias, g_ret, g_m, w_out_bf, ln_g, ln_b,
      cos_t, sin_t, kcos_t, ksin_t, decay, qd, kd, cdec, triu, mask, e0, *prev_states)
    return y, states


def _sample_kernel(x_ref, w_in_ref, wg_ref, conv_w_ref, conv_b_ref, gbias_ref, g_ret_ref, g_m_ref,
                   w_out_ref, ln_g_ref, ln_b_ref, rot_ref, gam_ref, esel_ref,
                   s_ref, c_ref, n_ref, m_ref, cv_ref,
                   y_ref, s_out_ref, c_out_ref, n_out_ref, m_out_ref, cv_out_ref,
                   proj_scr, xcur_scr, mix_scr):
    layer = pl.program_id(0)
    j = pl.program_id(1)
    bb = s_ref.shape[1]

    def layer_row(ref, cols=slice(None)):
        row = ref[0:1, cols]
        for l in range(1, DEPTH):
            row = jnp.where(layer == l, ref[l:l + 1, cols], row)
        return row

    @pl.when(jnp.logical_and(layer == 0, j == 0))
    def _load_x():
        xcur_scr[...] = x_ref[...]

    @pl.when(j == 0)
    def _project():
        xb = xcur_scr[...].astype(BF16)
        for lo in range(0, COL_GATE, 512):
            proj_scr[:, lo:lo + 512] = _dot(xb, w_in_ref[0, :, lo:lo + 512])
        proj_scr[:, COL_GATE:N_PAD] = jnp.zeros((proj_scr.shape[0], N_PAD - COL_GATE), F32)
        proj_scr[:, COL_GATE:COL_GATE + GATE_ROWS] = _dot_nt(xb, wg_ref[0])

    r0 = pl.multiple_of(j * bb, bb)
    rows = pl.ds(r0, bb)
    cos_t = rot_ref[0:1, :]
    sin_t = rot_ref[1:2, :]
    kcos_t = rot_ref[2:3, :]
    ksin_t = rot_ref[3:4, :]
    esel = esel_ref[...]
    row_id = lax.broadcasted_iota(jnp.int32, (bb, bb * HEAD_DIM), 0)
    blk_id = jnp.right_shift(lax.broadcasted_iota(jnp.int32, (bb, bb * HEAD_DIM), 1), 7)
    pad_rows = jnp.zeros((LANES - bb, HEAD_DIM), F32)
    pad_wide = jnp.zeros((LANES - bb, bb * HEAD_DIM), BF16)

    def col_form(x8):
        return jnp.concatenate([x8, pad_rows], axis=0).T.astype(BF16)

    def outer_all(k8, v8):
        vt = jnp.concatenate([v8] * bb, axis=1)
        vsel = jnp.where(row_id == blk_id, vt, 0.0).astype(BF16)
        return _dot(col_form(k8), jnp.concatenate([vsel, pad_wide], axis=0))

    def col_bcast_all(q8):
        return _dot(col_form(q8), esel)

    for h in range(N_HEADS):
        hc = h * HEAD_DIM
        q8 = _rotary(proj_scr[rows, COL_RQ + hc:COL_RQ + hc + HEAD_DIM], cos_t, sin_t)
        k8 = _rotary(proj_scr[rows, COL_RK + hc:COL_RK + hc + HEAD_DIM], kcos_t, ksin_t)
        v8 = proj_scr[rows, COL_RV + hc:COL_RV + hc + HEAD_DIM]
        kv_all = outer_all(k8, v8)
        qc_all = col_bcast_all(q8)
        gamma = gam_ref[h]
        o_rows = []
        for r in range(bb):
            blk = slice(r * HEAD_DIM, (r + 1) * HEAD_DIM)
            s_new = s_ref[0, r, h] * gamma + kv_all[:, blk]
            s_out_ref[0, r, h] = s_new
            o_rows.append(jnp.sum(qc_all[:, blk] * s_new, axis=0, keepdims=True))
        o8 = _head_norm(jnp.concatenate(o_rows, axis=0), layer_row(g_ret_ref, slice(hc, hc + HEAD_DIM)))
        z8 = proj_scr[rows, COL_RZ + hc:COL_RZ + hc + HEAD_DIM]
        mix_scr[rows, hc:hc + HEAD_DIM] = o8 * _silu(z8)

    u8 = proj_scr[rows, COL_MQK:COL_MQK + 2 * GROUP_W]
    acc = layer_row(conv_b_ref) + u8 * conv_w_ref[0, CONV_W - 1:CONV_W, :]
    for jj in range(CONV_W - 1):
        acc = acc + cv_ref[0, jj] * conv_w_ref[0, jj:jj + 1, :]
    for jj in range(1, CONV_W - 1):
        cv_out_ref[0, jj - 1] = cv_ref[0, jj]
    cv_out_ref[0, CONV_W - 2] = u8
    qk8 = _silu(acc)

    gates = proj_scr[rows, COL_GATE:COL_GATE + LANES] + gbias_ref[0]
    i_al = pltpu.roll(gates, N_HEADS, axis=1)
    bm = _log_sigmoid(gates) + m_ref[0]
    m_new = jnp.maximum(bm, i_al)
    wk = jnp.exp(i_al - m_new)
    wc = jnp.exp(bm - m_new)
    einv = jnp.exp(-m_new)
    m_out_ref[0] = m_new
    for h in range(N_HEADS):
        hc = h * HEAD_DIM
        gl = N_HEADS + h
        q8 = qk8[:, hc:hc + HEAD_DIM]
        k8 = qk8[:, GROUP_W + hc:GROUP_W + hc + HEAD_DIM] * QK_SCALE
        v8 = proj_scr[rows, COL_MV + hc:COL_MV + hc + HEAD_DIM]
        wk_h = wk[:, gl:gl + 1]
        wc_h = jnp.broadcast_to(wc[:, gl:gl + 1], (bb, HEAD_DIM))
        kw8 = k8 * wk_h
        kv_all = outer_all(kw8, v8)
        qc_all = col_bcast_all(q8)
        n_new = n_ref[0, :, h, :] * wc_h + kw8
        n_out_ref[0, :, h, :] = n_new
        num_rows = []
        for r in range(bb):
            blk = slice(r * HEAD_DIM, (r + 1) * HEAD_DIM)
            c_new = c_ref[0, r, h] * wc_h[r:r + 1, :] + kv_all[:, blk]
            c_out_ref[0, r, h] = c_new
            num_rows.append(jnp.sum(qc_all[:, blk] * c_new, axis=0, keepdims=True))
        num = jnp.concatenate(num_rows, axis=0)
        q_bf = q8.astype(BF16).astype(F32)
        den = jnp.sum(q_bf * n_new, axis=-1, keepdims=True)
        hout = num / jnp.maximum(jnp.abs(den), einv[:, gl:gl + 1])
        hn = _head_norm(hout, layer_row(g_m_ref, slice(hc, hc + HEAD_DIM)))
        og = _sigmoid(proj_scr[rows, COL_MO + hc:COL_MO + hc + HEAD_DIM])
        zg = _silu(proj_scr[rows, COL_MZ + hc:COL_MZ + hc + HEAD_DIM])
        mix_scr[rows, GROUP_W + hc:GROUP_W + hc + HEAD_DIM] = hn * og * zg

    @pl.when(j == pl.num_programs(1) - 1)
    def _finish_layer():
        mix = _dot(mix_scr[...].astype(BF16), w_out_ref[0])
        y = _layer_norm(ALPHA * xcur_scr[...] + mix, layer_row(ln_g_ref), layer_row(ln_b_ref))
        xcur_scr[...] = y
        y_ref[...] = y


def _sample_layers(x, w_in_bf, wg_bf, conv_w, conv_b, gbias, g_ret, g_m, w_out_bf, ln_g, ln_b,
                   rot, gam, esel, state_ret, state_c, state_n, m_pad, state_conv):
    bs = x.shape[0]
    bb = SAMPLE_BLOCK_B
    assert bs % bb == 0
    nb = bs // bb

    def lspec(shape, buffers=None):
        nd = len(shape)
        mode = None if buffers is None else pl.Buffered(buffers)
        return pl.BlockSpec((1,) + tuple(shape[1:]), lambda l, j: (l,) + (0,) * (nd - 1),
                            pipeline_mode=mode)

    def cspec(shape):
        nd = len(shape)
        return pl.BlockSpec(tuple(shape), lambda l, j: (0,) * nd)

    mat_spec = pl.BlockSpec((1, bb, N_HEADS, HEAD_DIM, HEAD_DIM), lambda l, j: (l, j, 0, 0, 0))
    n_spec = pl.BlockSpec((1, bb, N_HEADS, HEAD_DIM), lambda l, j: (l, j, 0, 0))
    m_spec = pl.BlockSpec((1, bb, LANES), lambda l, j: (l, j, 0))
    cv_spec = pl.BlockSpec((1, CONV_W - 1, bb, 2 * GROUP_W), lambda l, j: (l, 0, j, 0))
    in_specs = [
        cspec(x.shape), lspec(w_in_bf.shape, 1), lspec(wg_bf.shape),
        lspec(conv_w.shape), cspec(conv_b.shape),
        lspec(gbias.shape), cspec(g_ret.shape), cspec(g_m.shape), lspec(w_out_bf.shape, 1),
        cspec(ln_g.shape), cspec(ln_b.shape), cspec(rot.shape),
        pl.BlockSpec(memory_space=pltpu.SMEM), cspec(esel.shape),
        mat_spec, mat_spec, n_spec, m_spec, cv_spec,
    ]
    out_shape = (
        jax.ShapeDtypeStruct(x.shape, F32),
        jax.ShapeDtypeStruct(state_ret.shape, F32),
        jax.ShapeDtypeStruct(state_c.shape, F32),
        jax.ShapeDtypeStruct(state_n.shape, F32),
        jax.ShapeDtypeStruct(m_pad.shape, F32),
        jax.ShapeDtypeStruct(state_conv.shape, F32),
    )
    out_specs = (cspec(x.shape), mat_spec, mat_spec, n_spec, m_spec, cv_spec)
    scratch = [
        pltpu.VMEM((bs, N_PAD), F32),
        pltpu.VMEM((bs, D_MODEL), F32),
        pltpu.VMEM((bs, 2 * GROUP_W), F32),
    ]
    return pl.pallas_call(
        _sample_kernel,
        grid=(DEPTH, nb),
        in_specs=in_specs,
        out_specs=out_specs,
        out_shape=out_shape,
        scratch_shapes=scratch,
        compiler_params=pltpu.CompilerParams(
            dimension_semantics=("arbitrary", "arbitrary"),
            vmem_limit_bytes=VMEM_LIMIT_BYTES),
        name="sample_layers",
    )(x, w_in_bf, wg_bf, conv_w, conv_b, gbias, g_ret, g_m, w_out_bf, ln_g, ln_b,
      rot, gam, esel, state_ret, state_c, state_n, m_pad, state_conv)


def _cast_weight_kernel(w_ref, o_ref):
    o_ref[0] = w_ref[0].astype(BF16)


def _cast_weight(w):
    depth, k, n = w.shape
    rows = WEIGHT_CAST_ROWS
    assert k % rows == 0
    return pl.pallas_call(
        _cast_weight_kernel,
        grid=(depth, k // rows),
        in_specs=[pl.BlockSpec((1, rows, n), lambda l, r: (l, r, 0))],
        out_specs=pl.BlockSpec((1, rows, n), lambda l, r: (l, r, 0)),
        out_shape=jax.ShapeDtypeStruct((depth, k, n), BF16),
        compiler_params=pltpu.CompilerParams(dimension_semantics=("parallel", "parallel")),
        name="cast_weight",
    )(w)


def _cast_transposed_weight_kernel(n_valid, wt_ref, o_ref):
    rows = wt_ref.shape[1]
    row_id = pl.program_id(1) * rows + lax.broadcasted_iota(jnp.int32, wt_ref.shape[1:], 0)
    wt = jnp.where(row_id < n_valid, wt_ref[0], 0.0)
    o_ref[0] = wt.astype(BF16).T


def _cast_transposed_weight(wt, n_out):
    depth, n, k = wt.shape
    rows = WEIGHT_CAST_ROWS
    assert n_out % LANES == 0 and n_out - n < LANES
    return pl.pallas_call(
        functools.partial(_cast_transposed_weight_kernel, min(n, n_out)),
        grid=(depth, pl.cdiv(n_out, rows)),
        in_specs=[pl.BlockSpec((1, rows, k), lambda l, r: (l, r, 0))],
        out_specs=pl.BlockSpec((1, k, rows), lambda l, r: (l, 0, r)),
        out_shape=jax.ShapeDtypeStruct((depth, k, n_out), BF16),
        compiler_params=pltpu.CompilerParams(dimension_semantics=("parallel", "parallel")),
        name="cast_weight_t",
    )(wt)


def _rotary_tables(pos):
    half = HEAD_DIM // 2
    inv = np.float64(ROPE_BASE) ** (-np.arange(half, dtype=np.float64) / half)
    ang = pos.astype(np.float64)[:, None] * inv[None, :]
    cos = np.cos(ang)
    sin = np.sin(ang)
    cos_t = np.concatenate([cos, cos], axis=-1)
    sin_t = np.concatenate([-sin, sin], axis=-1)
    tabs = (cos_t, sin_t, cos_t * QK_SCALE, sin_t * QK_SCALE)
    return tuple(t.astype(np.float32) for t in tabs)


def _retention_tables():
    L = CHUNK
    f32 = np.float32
    log_gamma = np.log(1.0 - 2.0 ** (-5.0 - np.arange(N_HEADS, dtype=np.float64)))
    idx = np.arange(L, dtype=np.float64)
    diff = idx[:, None] - idx[None, :]
    decay = (np.exp(log_gamma[:, None, None] * np.maximum(diff, 0.0)) * (diff >= 0)).astype(f32)
    q_decay = np.exp(log_gamma[:, None] * (idx + 1.0)).astype(f32)
    k_decay = np.exp(log_gamma[:, None] * (L - 1.0 - idx)).astype(f32)
    c_decay = np.exp(log_gamma * L).astype(f32)
    qd = np.ascontiguousarray(np.broadcast_to(q_decay[:, :, None], (N_HEADS, L, HEAD_DIM)))
    kd = np.ascontiguousarray(np.broadcast_to(k_decay[:, :, None], (N_HEADS, L, HEAD_DIM)))
    gamma1 = np.exp(log_gamma).astype(f32)
    return decay, qd, kd, c_decay, gamma1


def kernel(x_prompt, x_sample, state_ret, state_mlstm_C, state_mlstm_n, state_mlstm_m, state_conv,
           w_in, conv_w, conv_b, b_i, b_f, g_ret, g_m, w_out, ln_g, ln_b):
    B, T, _ = x_prompt.shape
    Bs, Ts, _ = x_sample.shape
    assert Ts == 1

    w_in_t = jnp.swapaxes(w_in, 1, 2)
    w_in_bf = _cast_transposed_weight(w_in_t, COL_GATE)
    wg_bf = jnp.pad(w_in_t[:, COL_GATE:N_IN, :], ((0, 0), (0, GATE_ROWS - 2 * N_HEADS), (0, 0))).astype(BF16)
    w_out_bf = _cast_weight(w_out)
    gate_b = jnp.concatenate([b_i, b_f], axis=-1)
    gbias = jnp.pad(gate_b, ((0, 0), (0, LANES - 2 * N_HEADS))).reshape(DEPTH, 1, LANES)
    gbias_rows = jnp.broadcast_to(gate_b[:, :, None], (DEPTH, 2 * N_HEADS, LANES))

    decay, qd, kd, c_decay, gamma1 = _retention_tables()
    idx = np.arange(CHUNK)
    causal = idx[:, None] >= idx[None, :]
    triu = jnp.asarray(causal.T, BF16)
    mask_add = np.where(causal, LOG_QK_SCALE, -np.inf).astype(np.float32)
    e0 = jnp.asarray(np.broadcast_to(np.arange(LANES)[None, :] == 0, (CHUNK, LANES)), BF16)
    tabs_p = _rotary_tables(np.arange(T)) + (decay, qd, kd, c_decay, triu, mask_add, e0)
    tabs_p = tuple(jnp.asarray(a) for a in tabs_p)

    xp = x_prompt
    prompt_states = []
    for l in range(DEPTH):
        xp, prompt_states = _prompt_layer(
            l, xp, w_in_bf, wg_bf, conv_w, conv_b, gbias_rows, g_ret, g_m, w_out_bf, ln_g, ln_b, tabs_p,
            prompt_states)
    rp, cp, np_, mp_pad, vp = prompt_states
    mp = mp_pad[:, :, :N_HEADS, 0]

    rot = jnp.asarray(np.concatenate(_rotary_tables(PAST_LEN + np.arange(Ts)), axis=0))
    bb = SAMPLE_BLOCK_B
    esel = jnp.asarray(np.arange(LANES)[:, None] == (np.arange(bb * HEAD_DIM)[None, :] // HEAD_DIM), BF16)
    gamma1 = jnp.asarray(gamma1)
    m_pad = jnp.pad(state_mlstm_m, ((0, 0), (0, 0), (N_HEADS, LANES - 2 * N_HEADS)))
    ys, rs, cs, ns, ms_pad, vs = _sample_layers(
        x_sample.reshape(Bs, D_MODEL), w_in_bf, wg_bf, conv_w, conv_b, gbias, g_ret, g_m, w_out_bf,
        ln_g, ln_b, rot, gamma1, esel, state_ret, state_mlstm_C, state_mlstm_n, m_pad,
        jnp.swapaxes(state_conv, 1, 2))
    vs = jnp.swapaxes(vs, 1, 2)
    ms = ms_pad[:, :, N_HEADS:2 * N_HEADS]

    return (xp, ys.reshape(Bs, Ts, D_MODEL),
            rp, cp, np_, mp, vp, rs, cs, ns, ms, vs)
```

```python
import functools

import jax
import jax.numpy as jnp
import numpy as np
from jax import lax
from jax.experimental import pallas as pl
from jax.experimental.pallas import tpu as pltpu

F32 = jnp.float32
BF16 = jnp.bfloat16

D_MODEL = 1024
DEPTH = 2
PAST_LEN = 16384
N_HEADS = 4
HEAD_DIM = 128
GROUP_W = N_HEADS * HEAD_DIM
CONV_W = 4
CHUNK = 128
ROPE_BASE = 10000.0
LN_EPS = 1e-5
GN_EPS = 1e-5
ALPHA = (2 * DEPTH) ** 0.25
QK_SCALE = HEAD_DIM ** -0.5

LANES = 128
SUBLANES = 8

COL_RQ = 0
COL_RK = COL_RQ + GROUP_W
COL_RV = COL_RK + GROUP_W
COL_RZ = COL_RV + GROUP_W
COL_MQK = COL_RZ + GROUP_W
COL_MV = COL_MQK + 2 * GROUP_W
COL_MO = COL_MV + GROUP_W
COL_MZ = COL_MO + GROUP_W
COL_GATE = COL_MZ + GROUP_W
N_IN = COL_GATE + 2 * N_HEADS
N_PAD = COL_GATE + LANES
GATE_ROWS = 16

PROMPT_BLOCK_T = 256
PROJ_PIECE_COLS = 256
PROJ_PIECE_ROWS = 256
SAMPLE_BLOCK_B = 8
SAMPLE_RING_DEPTH = 4
WEIGHT_CAST_ROWS = 1024
VMEM_LIMIT_BYTES = 56 * 1024 * 1024


def _sigmoid(x):
    return 1.0 / (1.0 + jnp.exp(-x))


def _silu(x):
    return x * _sigmoid(x)


def _log_sigmoid(x):
    return jnp.minimum(x, 0.0) - jnp.log1p(jnp.exp(-jnp.abs(x)))


def _dot(a, b):
    return jnp.dot(a, b, preferred_element_type=F32)


def _dot_nt(a, b):
    return lax.dot_general(a, b, (((1,), (1,)), ((), ())), preferred_element_type=F32)


def _dot_tn(a, b):
    return lax.dot_general(a, b, (((0,), (0,)), ((), ())), preferred_element_type=F32)


def _rotary(x, cos_t, sin_t):
    return x * cos_t + pltpu.roll(x, HEAD_DIM // 2, axis=1) * sin_t


def _head_norm(h, g):
    mu = jnp.mean(h, axis=-1, keepdims=True)
    d = h - mu
    var = jnp.mean(d * d, axis=-1, keepdims=True)
    return d * lax.rsqrt(var + GN_EPS) * g


def _layer_norm(x, g, b):
    mu = jnp.mean(x, axis=-1, keepdims=True)
    d = x - mu
    var = jnp.mean(d * d, axis=-1, keepdims=True)
    return d * lax.rsqrt(var + LN_EPS) * g + b


def _cumsum_lanes(triu_bf, x):
    hi = x.astype(BF16)
    r1 = x - hi.astype(F32)
    mid = r1.astype(BF16)
    lo = (r1 - mid.astype(F32)).astype(BF16)
    return _dot(hi, triu_bf) + _dot(mid, triu_bf) + _dot(lo, triu_bf)


RQK_Q, RQK_QDEC, RQK_K, RQK_KDEC = 0, GROUP_W, 2 * GROUP_W, 3 * GROUP_W
VV_RET, VV_M = 0, GROUP_W
GZ_RET, GZ_M = 0, GROUP_W
GT_COLS, GT_ROWS, GT_PER_CHUNK = 0, 1, 2
GT_ROWS_USED = SUBLANES
EMITS_PER_CHUNK = 9 * N_HEADS + 1
N_STATE_OUTPUTS = 5
HEADS_PER_STAGE = 2


def _prompt_kernel(nt, n_blocks, layer, xn_ref, x_ref, w_in_ref, wg_ref, conv_w_ref, conv_b_ref, gbias_ref,
                   g_ret_ref, g_m_ref, w_out_ref, ln_g_ref, ln_b_ref,
                   cos_ref, sin_ref, kcos_ref, ksin_ref,
                   decay_ref, qd_ref, kd_ref, cdec_ref, triu_ref, mask_ref, e0_ref, *refs):
    prev_refs = refs[:N_STATE_OUTPUTS] if layer > 0 else ()
    refs = refs[len(prev_refs):]
    y_ref, s_out_ref, c_out_ref, n_out_ref, m_out_ref, conv_out_ref = refs[:1 + N_STATE_OUTPUTS]
    (rqk_a, vv_a, gz_a, qkm_a, gt_a, rqk_b, vv_b, gz_b, qkm_b, gt_b,
     xb_scr, u_scr, mix_scr, s_scr, caug_scr, m_scr) = refs[1 + N_STATE_OUTPUTS:]
    g = pl.program_id(0)
    t = lax.rem(jnp.maximum(g - 1, 0), nt)
    tn = lax.rem(jnp.minimum(g, n_blocks - 1), nt)
    block_t = x_ref.shape[1]
    n_chunks = block_t // CHUNK
    carry_rows = CONV_W - 1
    heads = range(N_HEADS)
    lrow = slice(layer, layer + 1)

    @pl.when(g == 0)
    def _init_pipeline():
        for ref in (rqk_b, vv_b, gz_b, qkm_b, gt_b):
            ref[...] = jnp.zeros_like(ref)

    @pl.when(t == 0)
    def _init_state():
        s_scr[...] = jnp.zeros_like(s_scr)
        caug_scr[...] = jnp.zeros_like(caug_scr)
        m_scr[...] = jnp.zeros_like(m_scr)

    @pl.when(tn == 0)
    def _init_conv_carry():
        u_scr[0:SUBLANES, :] = jnp.zeros((SUBLANES, 2 * GROUP_W), F32)

    def hcols(base, h):
        return slice(base + h * HEAD_DIM, base + (h + 1) * HEAD_DIM)

    def step_body(set_in, set_cur):
        rqk_i, vv_i, gz_i, qkm_i, gt_i = set_in
        rqk_c, vv_c, gz_c, qkm_c, gt_c = set_cur
        triu_bf = triu_ref[...]
        causal_add = mask_ref[...]
        ones_col = e0_ref[...]
        sub_id = lax.broadcasted_iota(jnp.int32, (SUBLANES, LANES), 0)
        pad_rows = jnp.zeros((CHUNK - GT_ROWS_USED, LANES), F32)
        xb_scr[...] = xn_ref[0].astype(BF16)
        pw = PROJ_PIECE_COLS
        heads_per_piece = pw // HEAD_DIM

        pr = PROJ_PIECE_ROWS
        chunks_per_piece = pr // CHUNK

        def group_rows(p):
            return slice(p * pr, (p + 1) * pr)

        def proj(p, col, width=pw):
            return _dot(xb_scr[group_rows(p), :], w_in_ref[:, col:col + width])

        def piece_rot(p, i, col_base, cos_r, sin_r, dec_ref, dst, dst_dec):
            res = proj(p, col_base + i * pw)
            for hh in range(heads_per_piece):
                h = i * heads_per_piece + hh
                for cc in range(chunks_per_piece):
                    rows = slice(p * pr + cc * CHUNK, p * pr + (cc + 1) * CHUNK)
                    r = _rotary(res[cc * CHUNK:(cc + 1) * CHUNK, hh * HEAD_DIM:(hh + 1) * HEAD_DIM],
                                cos_r[rows, :], sin_r[rows, :])
                    rqk_i[rows, hcols(dst, h)] = r.astype(BF16)
                    rqk_i[rows, hcols(dst_dec, h)] = (r * dec_ref[h]).astype(BF16)

        def piece_cast(p, i, col_base, dst):
            vv_i[group_rows(p), dst + i * pw:dst + (i + 1) * pw] = proj(p, col_base + i * pw).astype(BF16)

        def piece_rz(p, i):
            gz_i[group_rows(p), GZ_RET + i * pw:GZ_RET + (i + 1) * pw] = _silu(proj(p, COL_RZ + i * pw))

        def piece_moz(p, i):
            og = _sigmoid(proj(p, COL_MO + i * pw))
            gz_i[group_rows(p), GZ_M + i * pw:GZ_M + (i + 1) * pw] = og * _silu(proj(p, COL_MZ + i * pw))

        def piece_conv(p, i):
            base = SUBLANES + p * pr
            u_scr[base:base + pr, i * pw:(i + 1) * pw] = proj(p, COL_MQK + i * pw)
            for cs in range(i * pw, (i + 1) * pw, LANES):
                cols = slice(cs, cs + LANES)
                acc = conv_b_ref[lrow, cols]
                for j in range(CONV_W):
                    r0 = base - carry_rows + j
                    acc = acc + u_scr[r0:r0 + pr, cols] * conv_w_ref[j:j + 1, cols]
                act = _silu(acc)
                if cs >= GROUP_W:
                    act = act * QK_SCALE
                qkm_i[group_rows(p), cols] = act

        def piece_gate(p):
            gates_t = _dot_nt(wg_ref[...], xb_scr[group_rows(p), :])
            for cc in range(chunks_per_piece):
                c = p * chunks_per_piece + cc
                g8 = gates_t[0:SUBLANES, cc * CHUNK:(cc + 1) * CHUNK] + gbias_ref[...]
                rows8 = jnp.where(sub_id < N_HEADS, g8, _cumsum_lanes(triu_bf, _log_sigmoid(g8)))
                gt_i[c * GT_PER_CHUNK + GT_ROWS, 0:GT_ROWS_USED, :] = rows8
                gt_i[c * GT_PER_CHUNK + GT_COLS] = jnp.concatenate([rows8, pad_rows], axis=0).T

        n_col_pieces = GROUP_W // pw
        pieces = []
        for p in range(block_t // pr):
            pieces.append(functools.partial(piece_gate, p))
            for i in range(n_col_pieces):
                pieces.append(functools.partial(piece_conv, p, 2 * i))
                pieces.append(functools.partial(piece_cast, p, i, COL_RV, VV_RET))
                pieces.append(functools.partial(piece_rot, p, i, COL_RQ, cos_ref, sin_ref, qd_ref,
                                                RQK_Q, RQK_QDEC))
                pieces.append(functools.partial(piece_rz, p, i))
                pieces.append(functools.partial(piece_conv, p, 2 * i + 1))
                pieces.append(functools.partial(piece_cast, p, i, COL_MV, VV_M))
                pieces.append(functools.partial(piece_rot, p, i, COL_RK, kcos_ref, ksin_ref, kd_ref,
                                                RQK_K, RQK_KDEC))
                pieces.append(functools.partial(piece_moz, p, i))
        n_pieces = len(pieces)
        n_slots = EMITS_PER_CHUNK * n_chunks
        slots_done = [0]

        def emit_proj_pieces():
            slots_done[0] += 1
            target = -(-(slots_done[0] * n_pieces) // n_slots)
            while n_pieces - len(pieces) < target:
                pieces.pop(0)()

        def per_head(group, fn):
            out = {}
            for h in group:
                emit_proj_pieces()
                out[h] = fn(h)
            return out

        head_groups = [range(h0, h0 + HEADS_PER_STAGE) for h0 in range(0, N_HEADS, HEADS_PER_STAGE)]

        for c in range(n_chunks):
            rows = slice(c * CHUNK, (c + 1) * CHUNK)

            for grp in head_groups:
                q_bf = {h: rqk_c[rows, hcols(RQK_Q, h)] for h in grp}
                k_bf = {h: rqk_c[rows, hcols(RQK_K, h)] for h in grp}
                v_bf = {h: vv_c[rows, hcols(VV_RET, h)] for h in grp}
                sc = per_head(grp, lambda h: _dot_nt(q_bf[h], k_bf[h]))
                state = {h: s_scr[h] for h in grp}
                upd = per_head(grp, lambda h: _dot_tn(rqk_c[rows, hcols(RQK_KDEC, h)], v_bf[h]))
                for h in grp:
                    s_scr[h] = state[h] * cdec_ref[h] + upd[h]
                o = per_head(grp, lambda h: _dot(
                    jnp.concatenate([(sc[h] * decay_ref[h]).astype(BF16),
                                     rqk_c[rows, hcols(RQK_QDEC, h)]], axis=1),
                    jnp.concatenate([v_bf[h], state[h].astype(BF16)], axis=0)))

                def ret_out(h):
                    hn = _head_norm(o[h], g_ret_ref[lrow, hcols(0, h)])
                    mix_scr[rows, hcols(0, h)] = (hn * gz_c[rows, hcols(GZ_RET, h)]).astype(BF16)

                per_head(grp, ret_out)

            cols_t = gt_c[c * GT_PER_CHUNK + GT_COLS]
            rows_t = gt_c[c * GT_PER_CHUNK + GT_ROWS, 0:GT_ROWS_USED, :]
            for grp in head_groups:
                q = {h: qkm_c[rows, hcols(0, h)] for h in grp}
                k = {h: qkm_c[rows, hcols(GROUP_W, h)] for h in grp}
                vaug_bf = {h: jnp.concatenate([vv_c[rows, hcols(VV_M, h)], ones_col], axis=1)
                           for h in grp}
                qk = per_head(grp, lambda h: _dot_nt(q[h].astype(BF16), k[h].astype(BF16)))
                caug = {h: caug_scr[h] for h in grp}
                m_old = {h: m_scr[h, 0:1, 0:1] for h in grp}
                i_row = {h: rows_t[h:h + 1, :] for h in grp}
                b_row = {h: rows_t[N_HEADS + h:N_HEADS + h + 1, :] for h in grp}
                i_col = {h: cols_t[:, h:h + 1] for h in grp}
                b_col = {h: cols_t[:, N_HEADS + h:N_HEADS + h + 1] for h in grp}
                a_col = {h: b_col[h] + m_old[h] for h in grp}
                dm = per_head(grp, lambda h: (b_col[h] + (i_row[h] - b_row[h])) + causal_add)
                mt = {h: jnp.maximum(a_col[h], jnp.max(dm[h], axis=-1, keepdims=True)) for h in grp}
                w_inter = {h: jnp.exp(a_col[h] - mt[h]) for h in grp}
                res = per_head(grp, lambda h: _dot(
                    jnp.concatenate([(qk[h] * jnp.exp(dm[h] - mt[h])).astype(BF16),
                                     (q[h] * w_inter[h]).astype(BF16)], axis=1),
                    jnp.concatenate([vaug_bf[h], caug[h].astype(BF16)], axis=0)))
                b_last = {h: b_col[h][CHUNK - 1:CHUNK, :] for h in grp}
                g_col = {h: b_last[h] - b_col[h] + i_col[h] for h in grp}
                m_new = {h: jnp.maximum(b_last[h] + m_old[h], jnp.max(g_col[h], axis=0, keepdims=True))
                         for h in grp}
                wk = {h: jnp.exp(g_col[h] - m_new[h]) for h in grp}
                wc = {h: jnp.exp(b_last[h] + m_old[h] - m_new[h]) for h in grp}

                def mlstm_update(h):
                    caug_scr[h] = caug[h] * wc[h] + _dot_tn((k[h] * wk[h]).astype(BF16), vaug_bf[h])
                    m_scr[h] = jnp.broadcast_to(m_new[h], (SUBLANES, LANES))

                per_head(grp, mlstm_update)

                def mlstm_out(h):
                    den = jnp.maximum(jnp.abs(res[h][:, HEAD_DIM:HEAD_DIM + 1]), jnp.exp(-mt[h]))
                    hn = _head_norm(res[h][:, :HEAD_DIM] / den, g_m_ref[lrow, hcols(0, h)])
                    mix_scr[rows, hcols(GROUP_W, h)] = (hn * gz_c[rows, hcols(GZ_M, h)]).astype(BF16)

                per_head(grp, mlstm_out)

            emit_proj_pieces()
            mix = _dot(mix_scr[rows, :], w_out_ref[...])
            y_ref[0, rows, :] = _layer_norm(ALPHA * x_ref[0, rows, :] + mix, ln_g_ref[lrow, :], ln_b_ref[lrow, :])

        assert slots_done[0] == n_slots and not pieces

    set_a = (rqk_a, vv_a, gz_a, qkm_a, gt_a)
    set_b = (rqk_b, vv_b, gz_b, qkm_b, gt_b)
    parity = lax.rem(g, 2)

    @pl.when(parity == 0)
    def _even_step():
        step_body(set_a, set_b)

    @pl.when(parity == 1)
    def _odd_step():
        step_body(set_b, set_a)

    @pl.when(jnp.logical_and(tn == nt - 1, g < n_blocks))
    def _write_conv_state():
        if layer > 0:
            conv_out_ref[0:layer] = prev_refs[-1][...]
        conv_out_ref[layer, 0] = u_scr[SUBLANES + block_t - carry_rows:SUBLANES + block_t, :]

    @pl.when(jnp.logical_and(t == nt - 1, g > 0))
    def _write_state():
        if layer > 0:
            for out_ref, prev_ref in zip((s_out_ref, c_out_ref, n_out_ref, m_out_ref), prev_refs):
                out_ref[0:layer] = prev_ref[...]
        s_out_ref[layer, 0] = s_scr[...]
        m_out_ref[layer, 0] = jnp.zeros((SUBLANES, LANES), F32)
        for h in heads:
            caug = caug_scr[h]
            c_out_ref[layer, 0, h] = caug[:, :HEAD_DIM]
            n_out_ref[layer, 0, h:h + 1, :] = caug[:, HEAD_DIM:].T[0:1, :]
            m_out_ref[layer, 0, h:h + 1, :] = m_scr[h, 0:1, :]

    u_scr[0:SUBLANES, :] = u_scr[block_t:block_t + SUBLANES, :]


def _const_spec(shape):
    nd = len(shape)
    return pl.BlockSpec(shape, lambda g: (0,) * nd)


def _layer_spec(shape, layer):
    return pl.BlockSpec((None,) + tuple(shape[1:]), lambda g: (layer,) + (0,) * (len(shape) - 1))


def _prompt_layer(layer, x, w_in_bf, wg_bf, conv_w, conv_b, gbias, g_ret, g_m, w_out_bf, ln_g, ln_b, tabs,
                  prev_states):
    B, T, _ = x.shape
    bt = PROMPT_BLOCK_T
    assert T % bt == 0 and bt % CHUNK == 0 and GROUP_W % PROJ_PIECE_COLS == 0
    nt = T // bt
    n_blocks = B * nt
    n_chunks = bt // CHUNK
    cos_t, sin_t, kcos_t, ksin_t, decay, qd, kd, cdec, triu, mask, e0 = tabs

    def nxt(g):
        return jnp.minimum(g, n_blocks - 1)

    def cur(g):
        return jnp.maximum(g - 1, 0)

    row_spec = pl.BlockSpec((bt, LANES), lambda g: (nxt(g) % nt, 0))
    in_specs = [
        pl.BlockSpec((1, bt, D_MODEL), lambda g: (nxt(g) // nt, nxt(g) % nt, 0)),
        pl.BlockSpec((1, bt, D_MODEL), lambda g: (cur(g) // nt, cur(g) % nt, 0)),
        _layer_spec(w_in_bf.shape, layer), _layer_spec(wg_bf.shape, layer),
        _layer_spec(conv_w.shape, layer), _const_spec(conv_b.shape),
        _layer_spec(gbias.shape, layer), _const_spec(g_ret.shape), _const_spec(g_m.shape),
        _layer_spec(w_out_bf.shape, layer), _const_spec(ln_g.shape), _const_spec(ln_b.shape),
        row_spec, row_spec, row_spec, row_spec,
        _const_spec(decay.shape), _const_spec(qd.shape), _const_spec(kd.shape),
        pl.BlockSpec(memory_space=pltpu.SMEM),
        _const_spec(triu.shape), _const_spec(mask.shape), _const_spec(e0.shape),
    ]
    state_tails = ((N_HEADS, HEAD_DIM, HEAD_DIM), (N_HEADS, HEAD_DIM, HEAD_DIM), (N_HEADS, HEAD_DIM),
                   (SUBLANES, LANES), (CONV_W - 1, 2 * GROUP_W))
    assert len(state_tails) == N_STATE_OUTPUTS and len(prev_states) in (0, N_STATE_OUTPUTS)

    def state_spec(depth, tail, batch_of):
        zeros = (0,) * len(tail)
        return pl.BlockSpec((depth, 1) + tail, lambda g: (0, batch_of(g)) + zeros)

    batch_ofs = [lambda g: cur(g) // nt] * (N_STATE_OUTPUTS - 1) + [lambda g: nxt(g) // nt]
    out_shape = (jax.ShapeDtypeStruct((B, T, D_MODEL), F32),) + tuple(
        jax.ShapeDtypeStruct((layer + 1, B) + tail, F32) for tail in state_tails)
    out_specs = (pl.BlockSpec((1, bt, D_MODEL), lambda g: (cur(g) // nt, cur(g) % nt, 0)),) + tuple(
        state_spec(layer + 1, tail, bo) for tail, bo in zip(state_tails, batch_ofs))
    if prev_states:
        in_specs = in_specs + [state_spec(layer, tail, bo) for tail, bo in zip(state_tails, batch_ofs)]
    operand_set = [
        pltpu.VMEM((bt, 4 * GROUP_W), BF16),
        pltpu.VMEM((bt, 2 * GROUP_W), BF16),
        pltpu.VMEM((bt, 2 * GROUP_W), F32),
        pltpu.VMEM((bt, 2 * GROUP_W), F32),
        pltpu.VMEM((n_chunks * GT_PER_CHUNK, CHUNK, LANES), F32),
    ]
    scratch = operand_set + operand_set + [
        pltpu.VMEM((bt, D_MODEL), BF16),
        pltpu.VMEM((SUBLANES + bt, 2 * GROUP_W), F32),
        pltpu.VMEM((bt, 2 * GROUP_W), BF16),
        pltpu.VMEM((N_HEADS, HEAD_DIM, HEAD_DIM), F32),
        pltpu.VMEM((N_HEADS, HEAD_DIM, 2 * HEAD_DIM), F32),
        pltpu.VMEM((N_HEADS, SUBLANES, LANES), F32),
    ]
    y, *states = pl.pallas_call(
        functools.partial(_prompt_kernel, nt, n_blocks, layer),
        grid=(n_blocks + 1,),
        in_specs=in_specs,
        out_specs=out_specs,
        out_shape=out_shape,
        scratch_shapes=scratch,
        compiler_params=pltpu.CompilerParams(
            dimension_semantics=("arbitrary",),
            vmem_limit_bytes=VMEM_LIMIT_BYTES),
        name="prompt_layer",
    )(x, x, w_in_bf, wg_bf, conv_w, conv_b, gbias, g_ret, g_m, w_out_bf, ln_g, ln_b,
      cos_t, sin_t, kcos_t, ksin_t, decay, qd, kd, cdec, triu, mask, e0, *prev_states)
    return y, states


def _sample_kernel(x_ref, w_in_ref, wg_ref, conv_w_ref, conv_b_ref, gbias_ref, g_ret_ref, g_m_ref,
                   w_out_ref, ln_g_ref, ln_b_ref, rot_ref, gam_ref, esel_ref,
                   s_hbm, c_hbm, n_ref, m_ref, cv_ref,
                   y_ref, s_out_ref, c_out_ref, n_out_ref, m_out_ref, cv_out_ref,
                   proj_scr, xcur_scr, mix_scr, s_ring, c_ring, ring_sem):
    layer = pl.program_id(0)
    j = pl.program_id(1)
    nb = pl.num_programs(1)
    bb = s_out_ref.shape[1]

    n_steps = DEPTH * nb
    step = layer * nb + j
    depth = s_ring.shape[0]
    lookahead = depth - 1

    def ring_copies(target):
        slot = lax.rem(target, depth)
        t_layer = target // nb
        row0 = pl.multiple_of(lax.rem(target, nb) * bb, bb)
        return [pltpu.make_async_copy(hbm.at[t_layer, pl.ds(row0, bb)], ring.at[slot], ring_sem.at[slot, i])
                for i, (hbm, ring) in enumerate(((s_hbm, s_ring), (c_hbm, c_ring)))]

    @pl.when(step == 0)
    def _prime_ring():
        for ahead in range(lookahead):
            for copy in ring_copies(ahead):
                copy.start()

    @pl.when(step + lookahead < n_steps)
    def _prefetch():
        for copy in ring_copies(step + lookahead):
            copy.start()

    for copy in ring_copies(step):
        copy.wait()
    slot = lax.rem(step, depth)

    def layer_row(ref, cols=slice(None)):
        row = ref[0:1, cols]
        for l in range(1, DEPTH):
            row = jnp.where(layer == l, ref[l:l + 1, cols], row)
        return row

    @pl.when(jnp.logical_and(layer == 0, j == 0))
    def _load_x():
        xcur_scr[...] = x_ref[...]

    @pl.when(j == 0)
    def _project():
        xb = xcur_scr[...].astype(BF16)
        for lo in range(0, COL_GATE, 512):
            proj_scr[:, lo:lo + 512] = _dot(xb, w_in_ref[0, :, lo:lo + 512])
        proj_scr[:, COL_GATE:N_PAD] = jnp.zeros((proj_scr.shape[0], N_PAD - COL_GATE), F32)
        proj_scr[:, COL_GATE:COL_GATE + GATE_ROWS] = _dot_nt(xb, wg_ref[0])

    r0 = pl.multiple_of(j * bb, bb)
    rows = pl.ds(r0, bb)
    cos_t = rot_ref[0:1, :]
    sin_t = rot_ref[1:2, :]
    kcos_t = rot_ref[2:3, :]
    ksin_t = rot_ref[3:4, :]
    esel = esel_ref[...]
    row_id = lax.broadcasted_iota(jnp.int32, (bb, bb * HEAD_DIM), 0)
    blk_id = jnp.right_shift(lax.broadcasted_iota(jnp.int32, (bb, bb * HEAD_DIM), 1), 7)
    pad_rows = jnp.zeros((LANES - bb, HEAD_DIM), F32)
    pad_wide = jnp.zeros((LANES - bb, bb * HEAD_DIM), BF16)

    def col_form(x8):
        return jnp.concatenate([x8, pad_rows], axis=0).T.astype(BF16)

    def outer_all(k8, v8):
        vt = jnp.concatenate([v8] * bb, axis=1)
        vsel = jnp.where(row_id == blk_id, vt, 0.0).astype(BF16)
        return _dot(col_form(k8), jnp.concatenate([vsel, pad_wide], axis=0))

    def col_bcast_all(q8):
        return _dot(col_form(q8), esel)

    for h in range(N_HEADS):
        hc = h * HEAD_DIM
        q8 = _rotary(proj_scr[rows, COL_RQ + hc:COL_RQ + hc + HEAD_DIM], cos_t, sin_t)
        k8 = _rotary(proj_scr[rows, COL_RK + hc:COL_RK + hc + HEAD_DIM], kcos_t, ksin_t)
        v8 = proj_scr[rows, COL_RV + hc:COL_RV + hc + HEAD_DIM]
        kv_all = outer_all(k8, v8)
        qc_all = col_bcast_all(q8)
        gamma = gam_ref[h]
        o_rows = []
        for r in range(bb):
            blk = slice(r * HEAD_DIM, (r + 1) * HEAD_DIM)
            s_new = s_ring[slot, r, h] * gamma + kv_all[:, blk]
            s_out_ref[0, r, h] = s_new
            o_rows.append(jnp.sum(qc_all[:, blk] * s_new, axis=0, keepdims=True))
        o8 = _head_norm(jnp.concatenate(o_rows, axis=0), layer_row(g_ret_ref, slice(hc, hc + HEAD_DIM)))
        z8 = proj_scr[rows, COL_RZ + hc:COL_RZ + hc + HEAD_DIM]
        mix_scr[rows, hc:hc + HEAD_DIM] = o8 * _silu(z8)

    u8 = proj_scr[rows, COL_MQK:COL_MQK + 2 * GROUP_W]
    acc = layer_row(conv_b_ref) + u8 * conv_w_ref[0, CONV_W - 1:CONV_W, :]
    for jj in range(CONV_W - 1):
        acc = acc + cv_ref[0, jj] * conv_w_ref[0, jj:jj + 1, :]
    for jj in range(1, CONV_W - 1):
        cv_out_ref[0, jj - 1] = cv_ref[0, jj]
    cv_out_ref[0, CONV_W - 2] = u8
    qk8 = _silu(acc)

    gates = proj_scr[rows, COL_GATE:COL_GATE + LANES] + gbias_ref[0]
    i_al = pltpu.roll(gates, N_HEADS, axis=1)
    bm = _log_sigmoid(gates) + m_ref[0]
    m_new = jnp.maximum(bm, i_al)
    wk = jnp.exp(i_al - m_new)
    wc = jnp.exp(bm - m_new)
    einv = jnp.exp(-m_new)
    m_out_ref[0] = m_new
    for h in range(N_HEADS):
        hc = h * HEAD_DIM
        gl = N_HEADS + h
        q8 = qk8[:, hc:hc + HEAD_DIM]
        k8 = qk8[:, GROUP_W + hc:GROUP_W + hc + HEAD_DIM] * QK_SCALE
        v8 = proj_scr[rows, COL_MV + hc:COL_MV + hc + HEAD_DIM]
        wk_h = wk[:, gl:gl + 1]
        wc_h = jnp.broadcast_to(wc[:, gl:gl + 1], (bb, HEAD_DIM))
        kw8 = k8 * wk_h
        kv_all = outer_all(kw8, v8)
        qc_all = col_bcast_all(q8)
        n_new = n_ref[0, :, h, :] * wc_h + kw8
        n_out_ref[0, :, h, :] = n_new
        num_rows = []
        for r in range(bb):
            blk = slice(r * HEAD_DIM, (r + 1) * HEAD_DIM)
            c_new = c_ring[slot, r, h] * wc_h[r:r + 1, :] + kv_all[:, blk]
            c_out_ref[0, r, h] = c_new
            num_rows.append(jnp.sum(qc_all[:, blk] * c_new, axis=0, keepdims=True))
        num = jnp.concatenate(num_rows, axis=0)
        q_bf = q8.astype(BF16).astype(F32)
        den = jnp.sum(q_bf * n_new, axis=-1, keepdims=True)
        hout = num / jnp.maximum(jnp.abs(den), einv[:, gl:gl + 1])
        hn = _head_norm(hout, layer_row(g_m_ref, slice(hc, hc + HEAD_DIM)))
        og = _sigmoid(proj_scr[rows, COL_MO + hc:COL_MO + hc + HEAD_DIM])
        zg = _silu(proj_scr[rows, COL_MZ + hc:COL_MZ + hc + HEAD_DIM])
        mix_scr[rows, GROUP_W + hc:GROUP_W + hc + HEAD_DIM] = hn * og * zg

    @pl.when(j == pl.num_programs(1) - 1)
    def _finish_layer():
        mix = _dot(mix_scr[...].astype(BF16), w_out_ref[0])
        y = _layer_norm(ALPHA * xcur_scr[...] + mix, layer_row(ln_g_ref), layer_row(ln_b_ref))
        xcur_scr[...] = y
        y_ref[...] = y


def _sample_layers(x, w_in_bf, wg_bf, conv_w, conv_b, gbias, g_ret, g_m, w_out_bf, ln_g, ln_b,
                   rot, gam, esel, state_ret, state_c, state_n, m_pad, state_conv):
    bs = x.shape[0]
    bb = SAMPLE_BLOCK_B
    assert bs % bb == 0
    nb = bs // bb

    def lspec(shape, buffers=None):
        nd = len(shape)
        mode = None if buffers is None else pl.Buffered(buffers)
        return pl.BlockSpec((1,) + tuple(shape[1:]), lambda l, j: (l,) + (0,) * (nd - 1),
                            pipeline_mode=mode)

    def cspec(shape):
        nd = len(shape)
        return pl.BlockSpec(tuple(shape), lambda l, j: (0,) * nd)

    mat_spec = pl.BlockSpec((1, bb, N_HEADS, HEAD_DIM, HEAD_DIM), lambda l, j: (l, j, 0, 0, 0))
    n_spec = pl.BlockSpec((1, bb, N_HEADS, HEAD_DIM), lambda l, j: (l, j, 0, 0))
    m_spec = pl.BlockSpec((1, bb, LANES), lambda l, j: (l, j, 0))
    cv_spec = pl.BlockSpec((1, CONV_W - 1, bb, 2 * GROUP_W), lambda l, j: (l, 0, j, 0))
    in_specs = [
        cspec(x.shape), lspec(w_in_bf.shape, 1), lspec(wg_bf.shape),
        lspec(conv_w.shape), cspec(conv_b.shape),
        lspec(gbias.shape), cspec(g_ret.shape), cspec(g_m.shape), lspec(w_out_bf.shape, 1),
        cspec(ln_g.shape), cspec(ln_b.shape), cspec(rot.shape),
        pl.BlockSpec(memory_space=pltpu.SMEM), cspec(esel.shape),
        pl.BlockSpec(memory_space=pl.ANY), pl.BlockSpec(memory_space=pl.ANY), n_spec, m_spec, cv_spec,
    ]
    out_shape = (
        jax.ShapeDtypeStruct(x.shape, F32),
        jax.ShapeDtypeStruct(state_ret.shape, F32),
        jax.ShapeDtypeStruct(state_c.shape, F32),
        jax.ShapeDtypeStruct(state_n.shape, F32),
        jax.ShapeDtypeStruct(m_pad.shape, F32),
        jax.ShapeDtypeStruct(state_conv.shape, F32),
    )
    out_specs = (cspec(x.shape), mat_spec, mat_spec, n_spec, m_spec, cv_spec)
    scratch = [
        pltpu.VMEM((bs, N_PAD), F32),
        pltpu.VMEM((bs, D_MODEL), F32),
        pltpu.VMEM((bs, 2 * GROUP_W), F32),
        pltpu.VMEM((SAMPLE_RING_DEPTH, bb, N_HEADS, HEAD_DIM, HEAD_DIM), F32),
        pltpu.VMEM((SAMPLE_RING_DEPTH, bb, N_HEADS, HEAD_DIM, HEAD_DIM), F32),
        pltpu.SemaphoreType.DMA((SAMPLE_RING_DEPTH, 2)),
    ]
    assert DEPTH * nb >= SAMPLE_RING_DEPTH - 1
    return pl.pallas_call(
        _sample_kernel,
        grid=(DEPTH, nb),
        in_specs=in_specs,
        out_specs=out_specs,
        out_shape=out_shape,
        scratch_shapes=scratch,
        compiler_params=pltpu.CompilerParams(
            dimension_semantics=("arbitrary", "arbitrary"),
            vmem_limit_bytes=VMEM_LIMIT_BYTES),
        name="sample_layers",
    )(x, w_in_bf, wg_bf, conv_w, conv_b, gbias, g_ret, g_m, w_out_bf, ln_g, ln_b,
      rot, gam, esel, state_ret, state_c, state_n, m_pad, state_conv)


def _cast_weight_kernel(w_ref, o_ref):
    o_ref[0] = w_ref[0].astype(BF16)


def _cast_weight(w):
    depth, k, n = w.shape
    rows = WEIGHT_CAST_ROWS
    assert k % rows == 0
    return pl.pallas_call(
        _cast_weight_kernel,
        grid=(depth, k // rows),
        in_specs=[pl.BlockSpec((1, rows, n), lambda l, r: (l, r, 0))],
        out_specs=pl.BlockSpec((1, rows, n), lambda l, r: (l, r, 0)),
        out_shape=jax.ShapeDtypeStruct((depth, k, n), BF16),
        compiler_params=pltpu.CompilerParams(dimension_semantics=("parallel", "parallel")),
        name="cast_weight",
    )(w)


def _cast_transposed_weight_kernel(n_valid, wt_ref, o_ref):
    rows = wt_ref.shape[1]
    row_id = pl.program_id(1) * rows + lax.broadcasted_iota(jnp.int32, wt_ref.shape[1:], 0)
    wt = jnp.where(row_id < n_valid, wt_ref[0], 0.0)
    o_ref[0] = wt.astype(BF16).T


def _cast_transposed_weight(wt, n_out):
    depth, n, k = wt.shape
    rows = WEIGHT_CAST_ROWS
    assert n_out % LANES == 0 and n_out - n < LANES
    return pl.pallas_call(
        functools.partial(_cast_transposed_weight_kernel, min(n, n_out)),
        grid=(depth, pl.cdiv(n_out, rows)),
        in_specs=[pl.BlockSpec((1, rows, k), lambda l, r: (l, r, 0))],
        out_specs=pl.BlockSpec((1, k, rows), lambda l, r: (l, 0, r)),
        out_shape=jax.ShapeDtypeStruct((depth, k, n_out), BF16),
        compiler_params=pltpu.CompilerParams(dimension_semantics=("parallel", "parallel")),
        name="cast_weight_t",
    )(wt)


def _rotary_tables(pos):
    half = HEAD_DIM // 2
    inv = np.float64(ROPE_BASE) ** (-np.arange(half, dtype=np.float64) / half)
    ang = pos.astype(np.float64)[:, None] * inv[None, :]
    cos = np.cos(ang)
    sin = np.sin(ang)
    cos_t = np.concatenate([cos, cos], axis=-1)
    sin_t = np.concatenate([-sin, sin], axis=-1)
    tabs = (cos_t, sin_t, cos_t * QK_SCALE, sin_t * QK_SCALE)
    return tuple(t.astype(np.float32) for t in tabs)


def _retention_tables():
    L = CHUNK
    f32 = np.float32
    log_gamma = np.log(1.0 - 2.0 ** (-5.0 - np.arange(N_HEADS, dtype=np.float64)))
    idx = np.arange(L, dtype=np.float64)
    diff = idx[:, None] - idx[None, :]
    decay = (np.exp(log_gamma[:, None, None] * np.maximum(diff, 0.0)) * (diff >= 0)).astype(f32)
    q_decay = np.exp(log_gamma[:, None] * (idx + 1.0)).astype(f32)
    k_decay = np.exp(log_gamma[:, None] * (L - 1.0 - idx)).astype(f32)
    c_decay = np.exp(log_gamma * L).astype(f32)
    qd = np.ascontiguousarray(np.broadcast_to(q_decay[:, :, None], (N_HEADS, L, HEAD_DIM)))
    kd = np.ascontiguousarray(np.broadcast_to(k_decay[:, :, None], (N_HEADS, L, HEAD_DIM)))
    gamma1 = np.exp(log_gamma).astype(f32)
    return decay, qd, kd, c_decay, gamma1


def kernel(x_prompt, x_sample, state_ret, state_mlstm_C, state_mlstm_n, state_mlstm_m, state_conv,
           w_in, conv_w, conv_b, b_i, b_f, g_ret, g_m, w_out, ln_g, ln_b):
    B, T, _ = x_prompt.shape
    Bs, Ts, _ = x_sample.shape
    assert Ts == 1

    w_in_t = jnp.swapaxes(w_in, 1, 2)
    w_in_bf = _cast_transposed_weight(w_in_t, COL_GATE)
    wg_bf = jnp.pad(w_in_t[:, COL_GATE:N_IN, :], ((0, 0), (0, GATE_ROWS - 2 * N_HEADS), (0, 0))).astype(BF16)
    w_out_bf = _cast_weight(w_out)
    gate_b = jnp.concatenate([b_i, b_f], axis=-1)
    gbias = jnp.pad(gate_b, ((0, 0), (0, LANES - 2 * N_HEADS))).reshape(DEPTH, 1, LANES)
    gbias_rows = jnp.broadcast_to(gate_b[:, :, None], (DEPTH, 2 * N_HEADS, LANES))

    decay, qd, kd, c_decay, gamma1 = _retention_tables()
    idx = np.arange(CHUNK)
    causal = idx[:, None] >= idx[None, :]
    triu = jnp.asarray(causal.T, BF16)
    mask_add = np.where(causal, 0.0, -np.inf).astype(np.float32)
    e0 = jnp.asarray(np.broadcast_to(np.arange(LANES)[None, :] == 0, (CHUNK, LANES)), BF16)
    tabs_p = _rotary_tables(np.arange(T)) + (decay, qd, kd, c_decay, triu, mask_add, e0)
    tabs_p = tuple(jnp.asarray(a) for a in tabs_p)

    xp = x_prompt
    prompt_states = []
    for l in range(DEPTH):
        xp, prompt_states = _prompt_layer(
            l, xp, w_in_bf, wg_bf, conv_w, conv_b, gbias_rows, g_ret, g_m, w_out_bf, ln_g, ln_b, tabs_p,
            prompt_states)
    rp, cp, np_, mp_pad, vp = prompt_states
    mp = mp_pad[:, :, :N_HEADS, 0]

    rot = jnp.asarray(np.concatenate(_rotary_tables(PAST_LEN + np.arange(Ts)), axis=0))
    bb = SAMPLE_BLOCK_B
    esel = jnp.asarray(np.arange(LANES)[:, None] == (np.arange(bb * HEAD_DIM)[None, :] // HEAD_DIM), BF16)
    gamma1 = jnp.asarray(gamma1)
    m_pad = jnp.pad(state_mlstm_m, ((0, 0), (0, 0), (N_HEADS, LANES - 2 * N_HEADS)))
    ys, rs, cs, ns, ms_pad, vs = _sample_layers(
        x_sample.reshape(Bs, D_MODEL), w_in_bf, wg_bf, conv_w, conv_b, gbias, g_ret, g_m, w_out_bf,
        ln_g, ln_b, rot, gamma1, esel, state_ret, state_mlstm_C, state_mlstm_n, m_pad,
        jnp.swapaxes(state_conv, 1, 2))
    vs = jnp.swapaxes(vs, 1, 2)
    ms = ms_pad[:, :, N_HEADS:2 * N_HEADS]

    return (xp, ys.reshape(Bs, Ts, D_MODEL),
            rp, cp, np_, mp, vp, rs, cs, ns, ms, vs)
```

```python
import functools

import jax
import jax.numpy as jnp
import numpy as np
from jax import lax
from jax.experimental import pallas as pl
from jax.experimental.pallas import tpu as pltpu

F32 = jnp.float32
BF16 = jnp.bfloat16

D_MODEL = 1024
DEPTH = 2
PAST_LEN = 16384
N_HEADS = 4
HEAD_DIM = 128
GROUP_W = N_HEADS * HEAD_DIM
CONV_W = 4
CHUNK = 128
ROPE_BASE = 10000.0
LN_EPS = 1e-5
GN_EPS = 1e-5
ALPHA = (2 * DEPTH) ** 0.25
QK_SCALE = HEAD_DIM ** -0.5

LANES = 128
SUBLANES = 8

COL_RQ = 0
COL_RK = COL_RQ + GROUP_W
COL_RV = COL_RK + GROUP_W
COL_RZ = COL_RV + GROUP_W
COL_MQK = COL_RZ + GROUP_W
COL_MV = COL_MQK + 2 * GROUP_W
COL_MO = COL_MV + GROUP_W
COL_MZ = COL_MO + GROUP_W
COL_GATE = COL_MZ + GROUP_W
N_IN = COL_GATE + 2 * N_HEADS
N_PAD = COL_GATE + LANES
GATE_ROWS = 16

PROMPT_BLOCK_T = 256
PROJ_PIECE_COLS = 256
PROJ_PIECE_ROWS = 256
SAMPLE_BLOCK_B = 8
SAMPLE_RING_DEPTH = 4
WEIGHT_CAST_ROWS = 1024
VMEM_LIMIT_BYTES = 56 * 1024 * 1024


def _sigmoid(x):
    return 1.0 / (1.0 + jnp.exp(-x))


def _silu(x):
    return x * _sigmoid(x)


def _log_sigmoid(x):
    return jnp.minimum(x, 0.0) - jnp.log1p(jnp.exp(-jnp.abs(x)))


def _dot(a, b):
    return jnp.dot(a, b, preferred_element_type=F32)


def _dot_nt(a, b):
    return lax.dot_general(a, b, (((1,), (1,)), ((), ())), preferred_element_type=F32)


def _dot_tn(a, b):
    return lax.dot_general(a, b, (((0,), (0,)), ((), ())), preferred_element_type=F32)


def _rotary(x, cos_t, sin_t):
    return x * cos_t + pltpu.roll(x, HEAD_DIM // 2, axis=1) * sin_t


def _head_norm(h, g):
    mu = jnp.mean(h, axis=-1, keepdims=True)
    d = h - mu
    var = jnp.mean(d * d, axis=-1, keepdims=True)
    return d * lax.rsqrt(var + GN_EPS) * g


def _layer_norm(x, g, b):
    mu = jnp.mean(x, axis=-1, keepdims=True)
    d = x - mu
    var = jnp.mean(d * d, axis=-1, keepdims=True)
    return d * lax.rsqrt(var + LN_EPS) * g + b


def _cumsum_lanes(triu_bf, x):
    hi = x.astype(BF16)
    r1 = x - hi.astype(F32)
    mid = r1.astype(BF16)
    lo = (r1 - mid.astype(F32)).astype(BF16)
    return _dot(hi, triu_bf) + _dot(mid, triu_bf) + _dot(lo, triu_bf)


RQK_Q, RQK_QDEC, RQK_K, RQK_KDEC = 0, GROUP_W, 2 * GROUP_W, 3 * GROUP_W
VV_RET, VV_M = 0, GROUP_W
GZ_RET, GZ_M = 0, GROUP_W
GT_COLS, GT_ROWS, GT_PER_CHUNK = 0, 1, 2
GT_ROWS_USED = SUBLANES
EMITS_PER_CHUNK = 9 * N_HEADS + 1
N_STATE_OUTPUTS = 5
HEADS_PER_STAGE = 2


def _prompt_kernel(nt, n_blocks, layer, xn_ref, x_ref, w_in_ref, wg_ref, conv_w_ref, conv_b_ref, gbias_ref,
                   g_ret_ref, g_m_ref, w_out_ref, ln_g_ref, ln_b_ref,
                   cos_ref, sin_ref, kcos_ref, ksin_ref,
                   decay_ref, qd_ref, kd_ref, cdec_ref, triu_ref, mask_ref, e0_ref, *refs):
    prev_refs = refs[:N_STATE_OUTPUTS] if layer > 0 else ()
    refs = refs[len(prev_refs):]
    y_ref, s_out_ref, c_out_ref, n_out_ref, m_out_ref, conv_out_ref = refs[:1 + N_STATE_OUTPUTS]
    (rqk_a, vv_a, gz_a, qkm_a, gt_a, rqk_b, vv_b, gz_b, qkm_b, gt_b,
     xb_scr, u_scr, mix_scr, s_scr, caug_scr, m_scr) = refs[1 + N_STATE_OUTPUTS:]
    g = pl.program_id(0)
    t = lax.rem(jnp.maximum(g - 1, 0), nt)
    tn = lax.rem(jnp.minimum(g, n_blocks - 1), nt)
    block_t = x_ref.shape[1]
    n_chunks = block_t // CHUNK
    carry_rows = CONV_W - 1
    heads = range(N_HEADS)
    lrow = slice(layer, layer + 1)

    @pl.when(g == 0)
    def _init_pipeline():
        for ref in (rqk_b, vv_b, gz_b, qkm_b, gt_b):
            ref[...] = jnp.zeros_like(ref)

    @pl.when(t == 0)
    def _init_state():
        s_scr[...] = jnp.zeros_like(s_scr)
        caug_scr[...] = jnp.zeros_like(caug_scr)
        m_scr[...] = jnp.zeros_like(m_scr)

    @pl.when(tn == 0)
    def _init_conv_carry():
        u_scr[0:SUBLANES, :] = jnp.zeros((SUBLANES, 2 * GROUP_W), F32)

    def hcols(base, h):
        return slice(base + h * HEAD_DIM, base + (h + 1) * HEAD_DIM)

    def step_body(set_in, set_cur):
        rqk_i, vv_i, gz_i, qkm_i, gt_i = set_in
        rqk_c, vv_c, gz_c, qkm_c, gt_c = set_cur
        triu_bf = triu_ref[...]
        causal_add = mask_ref[...]
        ones_col = e0_ref[...]
        sub_id = lax.broadcasted_iota(jnp.int32, (SUBLANES, LANES), 0)
        pad_rows = jnp.zeros((CHUNK - GT_ROWS_USED, LANES), F32)
        xb_scr[...] = xn_ref[0].astype(BF16)
        pw = PROJ_PIECE_COLS
        heads_per_piece = pw // HEAD_DIM

        pr = PROJ_PIECE_ROWS
        chunks_per_piece = pr // CHUNK

        def group_rows(p):
            return slice(p * pr, (p + 1) * pr)

        def proj(p, col, width=pw):
            return _dot(xb_scr[group_rows(p), :], w_in_ref[:, col:col + width])

        def piece_rot(p, i, col_base, cos_r, sin_r, dec_ref, dst, dst_dec):
            res = proj(p, col_base + i * pw)
            for hh in range(heads_per_piece):
                h = i * heads_per_piece + hh
                for cc in range(chunks_per_piece):
                    rows = slice(p * pr + cc * CHUNK, p * pr + (cc + 1) * CHUNK)
                    r = _rotary(res[cc * CHUNK:(cc + 1) * CHUNK, hh * HEAD_DIM:(hh + 1) * HEAD_DIM],
                                cos_r[rows, :], sin_r[rows, :])
                    rqk_i[rows, hcols(dst, h)] = r.astype(BF16)
                    rqk_i[rows, hcols(dst_dec, h)] = (r * dec_ref[h]).astype(BF16)

        def piece_cast(p, i, col_base, dst):
            vv_i[group_rows(p), dst + i * pw:dst + (i + 1) * pw] = proj(p, col_base + i * pw).astype(BF16)

        def piece_rz(p, i):
            gz_i[group_rows(p), GZ_RET + i * pw:GZ_RET + (i + 1) * pw] = _silu(proj(p, COL_RZ + i * pw))

        def piece_moz(p, i):
            og = _sigmoid(proj(p, COL_MO + i * pw))
            gz_i[group_rows(p), GZ_M + i * pw:GZ_M + (i + 1) * pw] = og * _silu(proj(p, COL_MZ + i * pw))

        def piece_conv(p, i):
            base = SUBLANES + p * pr
            u_scr[base:base + pr, i * pw:(i + 1) * pw] = proj(p, COL_MQK + i * pw)
            for cs in range(i * pw, (i + 1) * pw, LANES):
                cols = slice(cs, cs + LANES)
                acc = conv_b_ref[lrow, cols]
                for j in range(CONV_W):
                    r0 = base - carry_rows + j
                    acc = acc + u_scr[r0:r0 + pr, cols] * conv_w_ref[j:j + 1, cols]
                act = _silu(acc)
                if cs >= GROUP_W:
                    act = act * QK_SCALE
                qkm_i[group_rows(p), cols] = act

        def piece_gate(p):
            gates_t = _dot_nt(wg_ref[...], xb_scr[group_rows(p), :])
            for cc in range(chunks_per_piece):
                c = p * chunks_per_piece + cc
                g8 = gates_t[0:SUBLANES, cc * CHUNK:(cc + 1) * CHUNK] + gbias_ref[...]
                rows8 = jnp.where(sub_id < N_HEADS, g8, _cumsum_lanes(triu_bf, _log_sigmoid(g8)))
                gt_i[c * GT_PER_CHUNK + GT_ROWS, 0:GT_ROWS_USED, :] = rows8
                gt_i[c * GT_PER_CHUNK + GT_COLS] = jnp.concatenate([rows8, pad_rows], axis=0).T

        n_col_pieces = GROUP_W // pw
        pieces = []
        for p in range(block_t // pr):
            pieces.append(functools.partial(piece_gate, p))
            for i in range(n_col_pieces):
                pieces.append(functools.partial(piece_conv, p, 2 * i))
                pieces.append(functools.partial(piece_cast, p, i, COL_RV, VV_RET))
                pieces.append(functools.partial(piece_rot, p, i, COL_RQ, cos_ref, sin_ref, qd_ref,
                                                RQK_Q, RQK_QDEC))
                pieces.append(functools.partial(piece_rz, p, i))
                pieces.append(functools.partial(piece_conv, p, 2 * i + 1))
                pieces.append(functools.partial(piece_cast, p, i, COL_MV, VV_M))
                pieces.append(functools.partial(piece_rot, p, i, COL_RK, kcos_ref, ksin_ref, kd_ref,
                                                RQK_K, RQK_KDEC))
                pieces.append(functools.partial(piece_moz, p, i))
        n_pieces = len(pieces)
        n_slots = EMITS_PER_CHUNK * n_chunks
        slots_done = [0]

        def emit_proj_pieces():
            slots_done[0] += 1
            target = -(-(slots_done[0] * n_pieces) // n_slots)
            while n_pieces - len(pieces) < target:
                pieces.pop(0)()

        def per_head(group, fn):
            out = {}
            for h in group:
                emit_proj_pieces()
                out[h] = fn(h)
            return out

        head_groups = [range(h0, h0 + HEADS_PER_STAGE) for h0 in range(0, N_HEADS, HEADS_PER_STAGE)]

        for c in range(n_chunks):
            rows = slice(c * CHUNK, (c + 1) * CHUNK)

            for grp in head_groups:
                q_bf = {h: rqk_c[rows, hcols(RQK_Q, h)] for h in grp}
                k_bf = {h: rqk_c[rows, hcols(RQK_K, h)] for h in grp}
                v_bf = {h: vv_c[rows, hcols(VV_RET, h)] for h in grp}
                sc = per_head(grp, lambda h: _dot_nt(q_bf[h], k_bf[h]))
                state = {h: s_scr[h] for h in grp}
                upd = per_head(grp, lambda h: _dot_tn(rqk_c[rows, hcols(RQK_KDEC, h)], v_bf[h]))
                for h in grp:
                    s_scr[h] = state[h] * cdec_ref[h] + upd[h]
                o = per_head(grp, lambda h: _dot(
                    jnp.concatenate([(sc[h] * decay_ref[h]).astype(BF16),
                                     rqk_c[rows, hcols(RQK_QDEC, h)]], axis=1),
                    jnp.concatenate([v_bf[h], state[h].astype(BF16)], axis=0)))

                def ret_out(h):
                    hn = _head_norm(o[h], g_ret_ref[lrow, hcols(0, h)])
                    mix_scr[rows, hcols(0, h)] = (hn * gz_c[rows, hcols(GZ_RET, h)]).astype(BF16)

                per_head(grp, ret_out)

            cols_t = gt_c[c * GT_PER_CHUNK + GT_COLS]
            rows_t = gt_c[c * GT_PER_CHUNK + GT_ROWS, 0:GT_ROWS_USED, :]
            for grp in head_groups:
                q = {h: qkm_c[rows, hcols(0, h)] for h in grp}
                k = {h: qkm_c[rows, hcols(GROUP_W, h)] for h in grp}
                vaug_bf = {h: jnp.concatenate([vv_c[rows, hcols(VV_M, h)], ones_col], axis=1)
                           for h in grp}
                qk = per_head(grp, lambda h: _dot_nt(q[h].astype(BF16), k[h].astype(BF16)))
                caug = {h: caug_scr[h] for h in grp}
                m_old = {h: m_scr[h, 0:1, 0:1] for h in grp}
                i_row = {h: rows_t[h:h + 1, :] for h in grp}
                b_row = {h: rows_t[N_HEADS + h:N_HEADS + h + 1, :] for h in grp}
                i_col = {h: cols_t[:, h:h + 1] for h in grp}
                b_col = {h: cols_t[:, N_HEADS + h:N_HEADS + h + 1] for h in grp}
                a_col = {h: b_col[h] + m_old[h] for h in grp}
                dm = per_head(grp, lambda h: (b_col[h] + (i_row[h] - b_row[h])) + causal_add)
                mt = {h: jnp.maximum(a_col[h], jnp.max(dm[h], axis=-1, keepdims=True)) for h in grp}
                w_inter = {h: jnp.exp(a_col[h] - mt[h]) for h in grp}
                res = per_head(grp, lambda h: _dot(
                    jnp.concatenate([(qk[h] * jnp.exp(dm[h] - mt[h])).astype(BF16),
                                     (q[h] * w_inter[h]).astype(BF16)], axis=1),
                    jnp.concatenate([vaug_bf[h], caug[h].astype(BF16)], axis=0)))
                b_last = {h: b_col[h][CHUNK - 1:CHUNK, :] for h in grp}
                g_col = {h: b_last[h] - b_col[h] + i_col[h] for h in grp}
                m_new = {h: jnp.maximum(b_last[h] + m_old[h], jnp.max(g_col[h], axis=0, keepdims=True))
                         for h in grp}
                wk = {h: jnp.exp(g_col[h] - m_new[h]) for h in grp}
                wc = {h: jnp.exp(b_last[h] + m_old[h] - m_new[h]) for h in grp}

                def mlstm_update(h):
                    caug_scr[h] = caug[h] * wc[h] + _dot_tn((k[h] * wk[h]).astype(BF16), vaug_bf[h])
                    m_scr[h] = jnp.broadcast_to(m_new[h], (SUBLANES, LANES))

                per_head(grp, mlstm_update)

                def mlstm_out(h):
                    den = jnp.maximum(jnp.abs(res[h][:, HEAD_DIM:HEAD_DIM + 1]), jnp.exp(-mt[h]))
                    hn = _head_norm(res[h][:, :HEAD_DIM] / den, g_m_ref[lrow, hcols(0, h)])
                    mix_scr[rows, hcols(GROUP_W, h)] = (hn * gz_c[rows, hcols(GZ_M, h)]).astype(BF16)

                per_head(grp, mlstm_out)

            emit_proj_pieces()
            mix = _dot(mix_scr[rows, :], w_out_ref[...])
            y_ref[0, rows, :] = _layer_norm(ALPHA * x_ref[0, rows, :] + mix, ln_g_ref[lrow, :], ln_b_ref[lrow, :])

        assert slots_done[0] == n_slots and not pieces

    set_a = (rqk_a, vv_a, gz_a, qkm_a, gt_a)
    set_b = (rqk_b, vv_b, gz_b, qkm_b, gt_b)
    parity = lax.rem(g, 2)

    @pl.when(parity == 0)
    def _even_step():
        step_body(set_a, set_b)

    @pl.when(parity == 1)
    def _odd_step():
        step_body(set_b, set_a)

    @pl.when(jnp.logical_and(tn == nt - 1, g < n_blocks))
    def _write_conv_state():
        if layer > 0:
            conv_out_ref[0:layer] = prev_refs[-1][...]
        conv_out_ref[layer, 0] = u_scr[SUBLANES + block_t - carry_rows:SUBLANES + block_t, :]

    @pl.when(jnp.logical_and(t == nt - 1, g > 0))
    def _write_state():
        if layer > 0:
            for out_ref, prev_ref in zip((s_out_ref, c_out_ref, n_out_ref, m_out_ref), prev_refs):
                out_ref[0:layer] = prev_ref[...]
        s_out_ref[layer, 0] = s_scr[...]
        m_out_ref[layer, 0] = jnp.zeros((SUBLANES, LANES), F32)
        for h in heads:
            caug = caug_scr[h]
            c_out_ref[layer, 0, h] = caug[:, :HEAD_DIM]
            n_out_ref[layer, 0, h:h + 1, :] = caug[:, HEAD_DIM:].T[0:1, :]
            m_out_ref[layer, 0, h:h + 1, :] = m_scr[h, 0:1, :]

    u_scr[0:SUBLANES, :] = u_scr[block_t:block_t + SUBLANES, :]


def _const_spec(shape):
    nd = len(shape)
    return pl.BlockSpec(shape, lambda g: (0,) * nd)


def _layer_spec(shape, layer):
    return pl.BlockSpec((None,) + tuple(shape[1:]), lambda g: (layer,) + (0,) * (len(shape) - 1))


def _prompt_layer(layer, x, w_in_bf, wg_bf, conv_w, conv_b, gbias, g_ret, g_m, w_out_bf, ln_g, ln_b, tabs,
                  prev_states):
    B, T, _ = x.shape
    bt = PROMPT_BLOCK_T
    assert T % bt == 0 and bt % CHUNK == 0 and GROUP_W % PROJ_PIECE_COLS == 0
    nt = T // bt
    n_blocks = B * nt
    n_chunks = bt // CHUNK
    cos_t, sin_t, kcos_t, ksin_t, decay, qd, kd, cdec, triu, mask, e0 = tabs

    def nxt(g):
        return jnp.minimum(g, n_blocks - 1)

    def cur(g):
        return jnp.maximum(g - 1, 0)

    row_spec = pl.BlockSpec((bt, LANES), lambda g: (nxt(g) % nt, 0))
    in_specs = [
        pl.BlockSpec((1, bt, D_MODEL), lambda g: (nxt(g) // nt, nxt(g) % nt, 0)),
        pl.BlockSpec((1, bt, D_MODEL), lambda g: (cur(g) // nt, cur(g) % nt, 0)),
        _layer_spec(w_in_bf.shape, layer), _layer_spec(wg_bf.shape, layer),
        _layer_spec(conv_w.shape, layer), _const_spec(conv_b.shape),
        _layer_spec(gbias.shape, layer), _const_spec(g_ret.shape), _const_spec(g_m.shape),
        _layer_spec(w_out_bf.shape, layer), _const_spec(ln_g.shape), _const_spec(ln_b.shape),
        row_spec, row_spec, row_spec, row_spec,
        _const_spec(decay.shape), _const_spec(qd.shape), _const_spec(kd.shape),
        pl.BlockSpec(memory_space=pltpu.SMEM),
        _const_spec(triu.shape), _const_spec(mask.shape), _const_spec(e0.shape),
    ]
    state_tails = ((N_HEADS, HEAD_DIM, HEAD_DIM), (N_HEADS, HEAD_DIM, HEAD_DIM), (N_HEADS, HEAD_DIM),
                   (SUBLANES, LANES), (CONV_W - 1, 2 * GROUP_W))
    assert len(state_tails) == N_STATE_OUTPUTS and len(prev_states) in (0, N_STATE_OUTPUTS)

    def state_spec(depth, tail, batch_of):
        zeros = (0,) * len(tail)
        return pl.BlockSpec((depth, 1) + tail, lambda g: (0, batch_of(g)) + zeros)

    batch_ofs = [lambda g: cur(g) // nt] * (N_STATE_OUTPUTS - 1) + [lambda g: nxt(g) // nt]
    out_shape = (jax.ShapeDtypeStruct((B, T, D_MODEL), F32),) + tuple(
        jax.ShapeDtypeStruct((layer + 1, B) + tail, F32) for tail in state_tails)
    out_specs = (pl.BlockSpec((1, bt, D_MODEL), lambda g: (cur(g) // nt, cur(g) % nt, 0)),) + tuple(
        state_spec(layer + 1, tail, bo) for tail, bo in zip(state_tails, batch_ofs))
    if prev_states:
        in_specs = in_specs + [state_spec(layer, tail, bo) for tail, bo in zip(state_tails, batch_ofs)]
    operand_set = [
        pltpu.VMEM((bt, 4 * GROUP_W), BF16),
        pltpu.VMEM((bt, 2 * GROUP_W), BF16),
        pltpu.VMEM((bt, 2 * GROUP_W), F32),
        pltpu.VMEM((bt, 2 * GROUP_W), F32),
        pltpu.VMEM((n_chunks * GT_PER_CHUNK, CHUNK, LANES), F32),
    ]
    scratch = operand_set + operand_set + [
        pltpu.VMEM((bt, D_MODEL), BF16),
        pltpu.VMEM((SUBLANES + bt, 2 * GROUP_W), F32),
        pltpu.VMEM((bt, 2 * GROUP_W), BF16),
        pltpu.VMEM((N_HEADS, HEAD_DIM, HEAD_DIM), F32),
        pltpu.VMEM((N_HEADS, HEAD_DIM, 2 * HEAD_DIM), F32),
        pltpu.VMEM((N_HEADS, SUBLANES, LANES), F32),
    ]
    y, *states = pl.pallas_call(
        functools.partial(_prompt_kernel, nt, n_blocks, layer),
        grid=(n_blocks + 1,),
        in_specs=in_specs,
        out_specs=out_specs,
        out_shape=out_shape,
        scratch_shapes=scratch,
        compiler_params=pltpu.CompilerParams(
            dimension_semantics=("arbitrary",),
            vmem_limit_bytes=VMEM_LIMIT_BYTES),
        name="prompt_layer",
    )(x, x, w_in_bf, wg_bf, conv_w, conv_b, gbias, g_ret, g_m, w_out_bf, ln_g, ln_b,
      cos_t, sin_t, kcos_t, ksin_t, decay, qd, kd, cdec, triu, mask, e0, *prev_states)
    return y, states


def _sample_kernel(x_ref, w_in_ref, wg_ref, conv_w_ref, conv_b_ref, gbias_ref, g_ret_ref, g_m_ref,
                   w_out_ref, ln_g_ref, ln_b_ref, rot_ref, gam_ref, esel_ref,
                   s_hbm, c_hbm, n_ref, m_ref, cv_ref,
                   y_ref, s_out_ref, c_out_ref, n_out_ref, m_out_ref, cv_out_ref,
                   proj_scr, xcur_scr, mix_scr, s_ring, c_ring, ring_sem):
    layer = pl.program_id(0)
    j = pl.program_id(1)
    nb = pl.num_programs(1)
    bb = s_out_ref.shape[1]

    n_steps = DEPTH * nb
    step = layer * nb + j
    depth = s_ring.shape[0]
    lookahead = depth - 1

    def ring_copies(target):
        slot = lax.rem(target, depth)
        t_layer = target // nb
        row0 = pl.multiple_of(lax.rem(target, nb) * bb, bb)
        return [pltpu.make_async_copy(hbm.at[t_layer, pl.ds(row0, bb)], ring.at[slot], ring_sem.at[slot, i])
                for i, (hbm, ring) in enumerate(((s_hbm, s_ring), (c_hbm, c_ring)))]

    @pl.when(step == 0)
    def _prime_ring():
        for ahead in range(lookahead):
            for copy in ring_copies(ahead):
                copy.start()

    @pl.when(step + lookahead < n_steps)
    def _prefetch():
        for copy in ring_copies(step + lookahead):
            copy.start()

    for copy in ring_copies(step):
        copy.wait()
    slot = lax.rem(step, depth)

    def layer_row(ref, cols=slice(None)):
        row = ref[0:1, cols]
        for l in range(1, DEPTH):
            row = jnp.where(layer == l, ref[l:l + 1, cols], row)
        return row

    @pl.when(jnp.logical_and(layer == 0, j == 0))
    def _load_x():
        xcur_scr[...] = x_ref[...]

    @pl.when(j == 0)
    def _project():
        xb = xcur_scr[...].astype(BF16)
        for lo in range(0, COL_GATE, 512):
            proj_scr[:, lo:lo + 512] = _dot(xb, w_in_ref[0, :, lo:lo + 512])
        proj_scr[:, COL_GATE:N_PAD] = jnp.zeros((proj_scr.shape[0], N_PAD - COL_GATE), F32)
        proj_scr[:, COL_GATE:COL_GATE + GATE_ROWS] = _dot_nt(xb, wg_ref[0])

    r0 = pl.multiple_of(j * bb, bb)
    rows = pl.ds(r0, bb)
    cos_t = rot_ref[0:1, :]
    sin_t = rot_ref[1:2, :]
    kcos_t = rot_ref[2:3, :]
    ksin_t = rot_ref[3:4, :]
    esel = esel_ref[...]
    row_id = lax.broadcasted_iota(jnp.int32, (bb, bb * HEAD_DIM), 0)
    blk_id = jnp.right_shift(lax.broadcasted_iota(jnp.int32, (bb, bb * HEAD_DIM), 1), 7)
    pad_rows = jnp.zeros((LANES - bb, HEAD_DIM), F32)
    pad_wide = jnp.zeros((LANES - bb, bb * HEAD_DIM), BF16)

    def col_form(x8):
        return jnp.concatenate([x8, pad_rows], axis=0).T.astype(BF16)

    def outer_all(k8, v8):
        vt = jnp.concatenate([v8] * bb, axis=1)
        vsel = jnp.where(row_id == blk_id, vt, 0.0).astype(BF16)
        return _dot(col_form(k8), jnp.concatenate([vsel, pad_wide], axis=0))

    def col_bcast_all(q8):
        return _dot(col_form(q8), esel)

    for h in range(N_HEADS):
        hc = h * HEAD_DIM
        q8 = _rotary(proj_scr[rows, COL_RQ + hc:COL_RQ + hc + HEAD_DIM], cos_t, sin_t)
        k8 = _rotary(proj_scr[rows, COL_RK + hc:COL_RK + hc + HEAD_DIM], kcos_t, ksin_t)
        v8 = proj_scr[rows, COL_RV + hc:COL_RV + hc + HEAD_DIM]
        kv_all = outer_all(k8, v8)
        qc_all = col_bcast_all(q8)
        gamma = gam_ref[h]
        o_rows = []
        for r in range(bb):
            blk = slice(r * HEAD_DIM, (r + 1) * HEAD_DIM)
            s_new = s_ring[slot, r, h] * gamma + kv_all[:, blk]
            s_out_ref[0, r, h] = s_new
            o_rows.append(jnp.sum(qc_all[:, blk] * s_new, axis=0, keepdims=True))
        o8 = _head_norm(jnp.concatenate(o_rows, axis=0), layer_row(g_ret_ref, slice(hc, hc + HEAD_DIM)))
        z8 = proj_scr[rows, COL_RZ + hc:COL_RZ + hc + HEAD_DIM]
        mix_scr[rows, hc:hc + HEAD_DIM] = o8 * _silu(z8)

    u8 = proj_scr[rows, COL_MQK:COL_MQK + 2 * GROUP_W]
    acc = layer_row(conv_b_ref) + u8 * conv_w_ref[0, CONV_W - 1:CONV_W, :]
    for jj in range(CONV_W - 1):
        acc = acc + cv_ref[0, jj] * conv_w_ref[0, jj:jj + 1, :]
    for jj in range(1, CONV_W - 1):
        cv_out_ref[0, jj - 1] = cv_ref[0, jj]
    cv_out_ref[0, CONV_W - 2] = u8
    qk8 = _silu(acc)

    gates = proj_scr[rows, COL_GATE:COL_GATE + LANES] + gbias_ref[0]
    i_al = pltpu.roll(gates, N_HEADS, axis=1)
    bm = _log_sigmoid(gates) + m_ref[0]
    m_new = jnp.maximum(bm, i_al)
    wk = jnp.exp(i_al - m_new)
    wc = jnp.exp(bm - m_new)
    einv = jnp.exp(-m_new)
    m_out_ref[0] = m_new
    for h in range(N_HEADS):
        hc = h * HEAD_DIM
        gl = N_HEADS + h
        q8 = qk8[:, hc:hc + HEAD_DIM]
        k8 = qk8[:, GROUP_W + hc:GROUP_W + hc + HEAD_DIM] * QK_SCALE
        v8 = proj_scr[rows, COL_MV + hc:COL_MV + hc + HEAD_DIM]
        wk_h = wk[:, gl:gl + 1]
        wc_h = jnp.broadcast_to(wc[:, gl:gl + 1], (bb, HEAD_DIM))
        kw8 = k8 * wk_h
        kv_all = outer_all(kw8, v8)
        qc_all = col_bcast_all(q8)
        n_new = n_ref[0, :, h, :] * wc_h + kw8
        n_out_ref[0, :, h, :] = n_new
        num_rows = []
        for r in range(bb):
            blk = slice(r * HEAD_DIM, (r + 1) * HEAD_DIM)
            c_new = c_ring[slot, r, h] * wc_h[r:r + 1, :] + kv_all[:, blk]
            c_out_ref[0, r, h] = c_new
            num_rows.append(jnp.sum(qc_all[:, blk] * c_new, axis=0, keepdims=True))
        num = jnp.concatenate(num_rows, axis=0)
        q_bf = q8.astype(BF16).astype(F32)
        den = jnp.sum(q_bf * n_new, axis=-1, keepdims=True)
        hout = num / jnp.maximum(jnp.abs(den), einv[:, gl:gl + 1])
        hn = _head_norm(hout, layer_row(g_m_ref, slice(hc, hc + HEAD_DIM)))
        og = _sigmoid(proj_scr[rows, COL_MO + hc:COL_MO + hc + HEAD_DIM])
        zg = _silu(proj_scr[rows, COL_MZ + hc:COL_MZ + hc + HEAD_DIM])
        mix_scr[rows, GROUP_W + hc:GROUP_W + hc + HEAD_DIM] = hn * og * zg

    @pl.when(j == pl.num_programs(1) - 1)
    def _finish_layer():
        mix = _dot(mix_scr[...].astype(BF16), w_out_ref[0])
        y = _layer_norm(ALPHA * xcur_scr[...] + mix, layer_row(ln_g_ref), layer_row(ln_b_ref))
        xcur_scr[...] = y
        y_ref[...] = y


def _sample_layers(x, w_in_bf, wg_bf, conv_w, conv_b, gbias, g_ret, g_m, w_out_bf, ln_g, ln_b,
                   rot, gam, esel, state_ret, state_c, state_n, m_pad, state_conv):
    bs = x.shape[0]
    bb = SAMPLE_BLOCK_B
    assert bs % bb == 0
    nb = bs // bb

    def lspec(shape, buffers=None):
        nd = len(shape)
        mode = None if buffers is None else pl.Buffered(buffers)
        return pl.BlockSpec((1,) + tuple(shape[1:]), lambda l, j: (l,) + (0,) * (nd - 1),
                            pipeline_mode=mode)

    def cspec(shape):
        nd = len(shape)
        return pl.BlockSpec(tuple(shape), lambda l, j: (0,) * nd)

    mat_spec = pl.BlockSpec((1, bb, N_HEADS, HEAD_DIM, HEAD_DIM), lambda l, j: (l, j, 0, 0, 0))
    n_spec = pl.BlockSpec((1, bb, N_HEADS, HEAD_DIM), lambda l, j: (l, j, 0, 0))
    m_spec = pl.BlockSpec((1, bb, LANES), lambda l, j: (l, j, 0))
    cv_spec = pl.BlockSpec((1, CONV_W - 1, bb, 2 * GROUP_W), lambda l, j: (l, 0, j, 0))
    in_specs = [
        cspec(x.shape), lspec(w_in_bf.shape), lspec(wg_bf.shape),
        lspec(conv_w.shape), cspec(conv_b.shape),
        lspec(gbias.shape), cspec(g_ret.shape), cspec(g_m.shape), lspec(w_out_bf.shape, 1),
        cspec(ln_g.shape), cspec(ln_b.shape), cspec(rot.shape),
        pl.BlockSpec(memory_space=pltpu.SMEM), cspec(esel.shape),
        pl.BlockSpec(memory_space=pl.ANY), pl.BlockSpec(memory_space=pl.ANY), n_spec, m_spec, cv_spec,
    ]
    out_shape = (
        jax.ShapeDtypeStruct(x.shape, F32),
        jax.ShapeDtypeStruct(state_ret.shape, F32),
        jax.ShapeDtypeStruct(state_c.shape, F32),
        jax.ShapeDtypeStruct(state_n.shape, F32),
        jax.ShapeDtypeStruct(m_pad.shape, F32),
        jax.ShapeDtypeStruct(state_conv.shape, F32),
    )
    out_specs = (cspec(x.shape), mat_spec, mat_spec, n_spec, m_spec, cv_spec)
    scratch = [
        pltpu.VMEM((bs, N_PAD), F32),
        pltpu.VMEM((bs, D_MODEL), F32),
        pltpu.VMEM((bs, 2 * GROUP_W), F32),
        pltpu.VMEM((SAMPLE_RING_DEPTH, bb, N_HEADS, HEAD_DIM, HEAD_DIM), F32),
        pltpu.VMEM((SAMPLE_RING_DEPTH, bb, N_HEADS, HEAD_DIM, HEAD_DIM), F32),
        pltpu.SemaphoreType.DMA((SAMPLE_RING_DEPTH, 2)),
    ]
    assert DEPTH * nb >= SAMPLE_RING_DEPTH - 1
    return pl.pallas_call(
        _sample_kernel,
        grid=(DEPTH, nb),
        in_specs=in_specs,
        out_specs=out_specs,
        out_shape=out_shape,
        scratch_shapes=scratch,
        compiler_params=pltpu.CompilerParams(
            dimension_semantics=("arbitrary", "arbitrary"),
            vmem_limit_bytes=VMEM_LIMIT_BYTES),
        name="sample_layers",
    )(x, w_in_bf, wg_bf, conv_w, conv_b, gbias, g_ret, g_m, w_out_bf, ln_g, ln_b,
      rot, gam, esel, state_ret, state_c, state_n, m_pad, state_conv)


def _cast_weight_kernel(w_ref, o_ref):
    o_ref[0] = w_ref[0].astype(BF16)


def _cast_weight(w):
    depth, k, n = w.shape
    rows = WEIGHT_CAST_ROWS
    assert k % rows == 0
    return pl.pallas_call(
        _cast_weight_kernel,
        grid=(depth, k // rows),
        in_specs=[pl.BlockSpec((1, rows, n), lambda l, r: (l, r, 0))],
        out_specs=pl.BlockSpec((1, rows, n), lambda l, r: (l, r, 0)),
        out_shape=jax.ShapeDtypeStruct((depth, k, n), BF16),
        compiler_params=pltpu.CompilerParams(dimension_semantics=("parallel", "parallel")),
        name="cast_weight",
    )(w)


def _cast_transposed_weight_kernel(n_valid, wt_ref, o_ref):
    rows = wt_ref.shape[1]
    row_id = pl.program_id(1) * rows + lax.broadcasted_iota(jnp.int32, wt_ref.shape[1:], 0)
    wt = jnp.where(row_id < n_valid, wt_ref[0], 0.0)
    o_ref[0] = wt.astype(BF16).T


def _cast_transposed_weight(wt, n_out):
    depth, n, k = wt.shape
    rows = WEIGHT_CAST_ROWS
    assert n_out % LANES == 0 and n_out - n < LANES
    return pl.pallas_call(
        functools.partial(_cast_transposed_weight_kernel, min(n, n_out)),
        grid=(depth, pl.cdiv(n_out, rows)),
        in_specs=[pl.BlockSpec((1, rows, k), lambda l, r: (l, r, 0))],
        out_specs=pl.BlockSpec((1, k, rows), lambda l, r: (l, 0, r)),
        out_shape=jax.ShapeDtypeStruct((depth, k, n_out), BF16),
        compiler_params=pltpu.CompilerParams(dimension_semantics=("parallel", "parallel")),
        name="cast_weight_t",
    )(wt)


def _rotary_tables(pos):
    half = HEAD_DIM // 2
    inv = np.float64(ROPE_BASE) ** (-np.arange(half, dtype=np.float64) / half)
    ang = pos.astype(np.float64)[:, None] * inv[None, :]
    cos = np.cos(ang)
    sin = np.sin(ang)
    cos_t = np.concatenate([cos, cos], axis=-1)
    sin_t = np.concatenate([-sin, sin], axis=-1)
    tabs = (cos_t, sin_t, cos_t * QK_SCALE, sin_t * QK_SCALE)
    return tuple(t.astype(np.float32) for t in tabs)


def _retention_tables():
    L = CHUNK
    f32 = np.float32
    log_gamma = np.log(1.0 - 2.0 ** (-5.0 - np.arange(N_HEADS, dtype=np.float64)))
    idx = np.arange(L, dtype=np.float64)
    diff = idx[:, None] - idx[None, :]
    decay = (np.exp(log_gamma[:, None, None] * np.maximum(diff, 0.0)) * (diff >= 0)).astype(f32)
    q_decay = np.exp(log_gamma[:, None] * (idx + 1.0)).astype(f32)
    k_decay = np.exp(log_gamma[:, None] * (L - 1.0 - idx)).astype(f32)
    c_decay = np.exp(log_gamma * L).astype(f32)
    qd = np.ascontiguousarray(np.broadcast_to(q_decay[:, :, None], (N_HEADS, L, HEAD_DIM)))
    kd = np.ascontiguousarray(np.broadcast_to(k_decay[:, :, None], (N_HEADS, L, HEAD_DIM)))
    gamma1 = np.exp(log_gamma).astype(f32)
    return decay, qd, kd, c_decay, gamma1


def kernel(x_prompt, x_sample, state_ret, state_mlstm_C, state_mlstm_n, state_mlstm_m, state_conv,
           w_in, conv_w, conv_b, b_i, b_f, g_ret, g_m, w_out, ln_g, ln_b):
    B, T, _ = x_prompt.shape
    Bs, Ts, _ = x_sample.shape
    assert Ts == 1

    w_in_t = jnp.swapaxes(w_in, 1, 2)
    w_in_bf = _cast_transposed_weight(w_in_t, COL_GATE)
    wg_bf = jnp.pad(w_in_t[:, COL_GATE:N_IN, :], ((0, 0), (0, GATE_ROWS - 2 * N_HEADS), (0, 0))).astype(BF16)
    w_out_bf = _cast_weight(w_out)
    gate_b = jnp.concatenate([b_i, b_f], axis=-1)
    gbias = jnp.pad(gate_b, ((0, 0), (0, LANES - 2 * N_HEADS))).reshape(DEPTH, 1, LANES)
    gbias_rows = jnp.broadcast_to(gate_b[:, :, None], (DEPTH, 2 * N_HEADS, LANES))

    decay, qd, kd, c_decay, gamma1 = _retention_tables()
    idx = np.arange(CHUNK)
    causal = idx[:, None] >= idx[None, :]
    triu = jnp.asarray(causal.T, BF16)
    mask_add = np.where(causal, 0.0, -np.inf).astype(np.float32)
    e0 = jnp.asarray(np.broadcast_to(np.arange(LANES)[None, :] == 0, (CHUNK, LANES)), BF16)
    tabs_p = _rotary_tables(np.arange(T)) + (decay, qd, kd, c_decay, triu, mask_add, e0)
    tabs_p = tuple(jnp.asarray(a) for a in tabs_p)

    xp = x_prompt
    prompt_states = []
    for l in range(DEPTH):
        xp, prompt_states = _prompt_layer(
            l, xp, w_in_bf, wg_bf, conv_w, conv_b, gbias_rows, g_ret, g_m, w_out_bf, ln_g, ln_b, tabs_p,
            prompt_states)
    rp, cp, np_, mp_pad, vp = prompt_states
    mp = mp_pad[:, :, :N_HEADS, 0]

    rot = jnp.asarray(np.concatenate(_rotary_tables(PAST_LEN + np.arange(Ts)), axis=0))
    bb = SAMPLE_BLOCK_B
    esel = jnp.asarray(np.arange(LANES)[:, None] == (np.arange(bb * HEAD_DIM)[None, :] // HEAD_DIM), BF16)
    gamma1 = jnp.asarray(gamma1)
    m_pad = jnp.pad(state_mlstm_m, ((0, 0), (0, 0), (N_HEADS, LANES - 2 * N_HEADS)))
    ys, rs, cs, ns, ms_pad, vs = _sample_layers(
        x_sample.reshape(Bs, D_MODEL), w_in_bf, wg_bf, conv_w, conv_b, gbias, g_ret, g_m, w_out_bf,
        ln_g, ln_b, rot, gamma1, esel, state_ret, state_mlstm_C, state_mlstm_n, m_pad,
        jnp.swapaxes(state_conv, 1, 2))
    vs = jnp.swapaxes(vs, 1, 2)
    ms = ms_pad[:, :, N_HEADS:2 * N_HEADS]

    return (xp, ys.reshape(Bs, Ts, D_MODEL),
            rp, cp, np_, mp, vp, rs, cs, ns, ms, vs)
```

```python
import functools

import jax
import jax.numpy as jnp
import numpy as np
from jax import lax
from jax.experimental import pallas as pl
from jax.experimental.pallas import tpu as pltpu

F32 = jnp.float32
BF16 = jnp.bfloat16

D_MODEL = 1024
DEPTH = 2
PAST_LEN = 16384
N_HEADS = 4
HEAD_DIM = 128
GROUP_W = N_HEADS * HEAD_DIM
CONV_W = 4
CHUNK = 128
ROPE_BASE = 10000.0
LN_EPS = 1e-5
GN_EPS = 1e-5
ALPHA = (2 * DEPTH) ** 0.25
QK_SCALE = HEAD_DIM ** -0.5

LANES = 128
SUBLANES = 8

COL_RQ = 0
COL_RK = COL_RQ + GROUP_W
COL_RV = COL_RK + GROUP_W
COL_RZ = COL_RV + GROUP_W
COL_MQK = COL_RZ + GROUP_W
COL_MV = COL_MQK + 2 * GROUP_W
COL_MO = COL_MV + GROUP_W
COL_MZ = COL_MO + GROUP_W
COL_GATE = COL_MZ + GROUP_W
N_IN = COL_GATE + 2 * N_HEADS
N_PAD = COL_GATE + LANES
GATE_ROWS = 16

PROMPT_BLOCK_T = 256
PROJ_PIECE_COLS = 256
PROJ_PIECE_ROWS = 256
SAMPLE_BLOCK_B = 16
WEIGHT_CAST_ROWS = 1024
PLAIN_CAST_ROWS = 256
VMEM_LIMIT_BYTES = 56 * 1024 * 1024


def _sigmoid(x):
    return 1.0 / (1.0 + jnp.exp(-x))


def _silu(x):
    return x * _sigmoid(x)


def _log_sigmoid(x):
    return jnp.minimum(x, 0.0) - jnp.log1p(jnp.exp(-jnp.abs(x)))


def _dot(a, b):
    return jnp.dot(a, b, preferred_element_type=F32)


def _dot_nt(a, b):
    return lax.dot_general(a, b, (((1,), (1,)), ((), ())), preferred_element_type=F32)


def _dot_tn(a, b):
    return lax.dot_general(a, b, (((0,), (0,)), ((), ())), preferred_element_type=F32)


def _rotary(x, cos_t, sin_t):
    return x * cos_t + pltpu.roll(x, HEAD_DIM // 2, axis=1) * sin_t


def _head_norm(h, g):
    mu = jnp.mean(h, axis=-1, keepdims=True)
    d = h - mu
    var = jnp.mean(d * d, axis=-1, keepdims=True)
    return d * lax.rsqrt(var + GN_EPS) * g


def _layer_norm(x, g, b):
    mu = jnp.mean(x, axis=-1, keepdims=True)
    d = x - mu
    var = jnp.mean(d * d, axis=-1, keepdims=True)
    return d * lax.rsqrt(var + LN_EPS) * g + b


def _cumsum_lanes(triu_bf, x):
    hi = x.astype(BF16)
    r1 = x - hi.astype(F32)
    mid = r1.astype(BF16)
    lo = (r1 - mid.astype(F32)).astype(BF16)
    return _dot(hi, triu_bf) + _dot(mid, triu_bf) + _dot(lo, triu_bf)


RQK_Q, RQK_QDEC, RQK_K, RQK_KDEC = 0, GROUP_W, 2 * GROUP_W, 3 * GROUP_W
VV_RET, VV_M = 0, GROUP_W
GZ_RET, GZ_M = 0, GROUP_W
GT_COLS, GT_ROWS, GT_PER_CHUNK = 0, 1, 2
GT_ROWS_USED = SUBLANES
EMITS_PER_CHUNK = 9 * N_HEADS + 1
N_STATE_OUTPUTS = 5
HEADS_PER_STAGE = 2


def _prompt_kernel(nt, n_blocks, layer, xn_ref, x_ref, w_in_ref, wg_ref, conv_w_ref, conv_b_ref, gbias_ref,
                   g_ret_ref, g_m_ref, w_out_ref, ln_g_ref, ln_b_ref,
                   cos_ref, sin_ref, kcos_ref, ksin_ref,
                   decay_ref, qd_ref, kd_ref, cdec_ref, triu_ref, mask_ref, e0_ref, *refs):
    prev_refs = refs[:N_STATE_OUTPUTS] if layer > 0 else ()
    refs = refs[len(prev_refs):]
    y_ref, s_out_ref, c_out_ref, n_out_ref, m_out_ref, conv_out_ref = refs[:1 + N_STATE_OUTPUTS]
    (rqk_a, vv_a, gz_a, qkm_a, gt_a, rqk_b, vv_b, gz_b, qkm_b, gt_b,
     xb_scr, u_scr, mix_scr, s_scr, caug_scr, m_scr) = refs[1 + N_STATE_OUTPUTS:]
    g = pl.program_id(0)
    t = lax.rem(jnp.maximum(g - 1, 0), nt)
    tn = lax.rem(jnp.minimum(g, n_blocks - 1), nt)
    block_t = x_ref.shape[1]
    n_chunks = block_t // CHUNK
    carry_rows = CONV_W - 1
    heads = range(N_HEADS)
    lrow = slice(layer, layer + 1)

    @pl.when(g == 0)
    def _init_pipeline():
        for ref in (rqk_b, vv_b, gz_b, qkm_b, gt_b):
            ref[...] = jnp.zeros_like(ref)

    @pl.when(t == 0)
    def _init_state():
        s_scr[...] = jnp.zeros_like(s_scr)
        caug_scr[...] = jnp.zeros_like(caug_scr)
        m_scr[...] = jnp.zeros_like(m_scr)

    @pl.when(tn == 0)
    def _init_conv_carry():
        u_scr[0:SUBLANES, :] = jnp.zeros((SUBLANES, 2 * GROUP_W), F32)

    def hcols(base, h):
        return slice(base + h * HEAD_DIM, base + (h + 1) * HEAD_DIM)

    def step_body(set_in, set_cur):
        rqk_i, vv_i, gz_i, qkm_i, gt_i = set_in
        rqk_c, vv_c, gz_c, qkm_c, gt_c = set_cur
        triu_bf = triu_ref[...]
        causal_add = mask_ref[...]
        ones_col = e0_ref[...]
        sub_id = lax.broadcasted_iota(jnp.int32, (SUBLANES, LANES), 0)
        pad_rows = jnp.zeros((CHUNK - GT_ROWS_USED, LANES), F32)
        xb_scr[...] = xn_ref[0].astype(BF16)
        pw = PROJ_PIECE_COLS
        heads_per_piece = pw // HEAD_DIM

        pr = PROJ_PIECE_ROWS
        chunks_per_piece = pr // CHUNK

        def group_rows(p):
            return slice(p * pr, (p + 1) * pr)

        def proj(p, col, width=pw):
            return _dot(xb_scr[group_rows(p), :], w_in_ref[:, col:col + width])

        def piece_rot(p, i, col_base, cos_r, sin_r, dec_ref, dst, dst_dec):
            res = proj(p, col_base + i * pw)
            for hh in range(heads_per_piece):
                h = i * heads_per_piece + hh
                for cc in range(chunks_per_piece):
                    rows = slice(p * pr + cc * CHUNK, p * pr + (cc + 1) * CHUNK)
                    r = _rotary(res[cc * CHUNK:(cc + 1) * CHUNK, hh * HEAD_DIM:(hh + 1) * HEAD_DIM],
                                cos_r[rows, :], sin_r[rows, :])
                    rqk_i[rows, hcols(dst, h)] = r.astype(BF16)
                    rqk_i[rows, hcols(dst_dec, h)] = (r * dec_ref[h]).astype(BF16)

        def piece_cast(p, i, col_base, dst):
            vv_i[group_rows(p), dst + i * pw:dst + (i + 1) * pw] = proj(p, col_base + i * pw).astype(BF16)

        def piece_rz(p, i):
            gz_i[group_rows(p), GZ_RET + i * pw:GZ_RET + (i + 1) * pw] = _silu(proj(p, COL_RZ + i * pw))

        def piece_moz(p, i):
            og = _sigmoid(proj(p, COL_MO + i * pw))
            gz_i[group_rows(p), GZ_M + i * pw:GZ_M + (i + 1) * pw] = og * _silu(proj(p, COL_MZ + i * pw))

        def piece_conv(p, i):
            base = SUBLANES + p * pr
            u_scr[base:base + pr, i * pw:(i + 1) * pw] = proj(p, COL_MQK + i * pw)
            for cs in range(i * pw, (i + 1) * pw, LANES):
                cols = slice(cs, cs + LANES)
                acc = conv_b_ref[lrow, cols]
                for j in range(CONV_W):
                    r0 = base - carry_rows + j
                    acc = acc + u_scr[r0:r0 + pr, cols] * conv_w_ref[j:j + 1, cols]
                act = _silu(acc)
                if cs >= GROUP_W:
                    act = act * QK_SCALE
                qkm_i[group_rows(p), cols] = act

        def piece_gate(p):
            gates_t = _dot_nt(wg_ref[...], xb_scr[group_rows(p), :])
            for cc in range(chunks_per_piece):
                c = p * chunks_per_piece + cc
                g8 = gates_t[0:SUBLANES, cc * CHUNK:(cc + 1) * CHUNK] + gbias_ref[...]
                rows8 = jnp.where(sub_id < N_HEADS, g8, _cumsum_lanes(triu_bf, _log_sigmoid(g8)))
                gt_i[c * GT_PER_CHUNK + GT_ROWS, 0:GT_ROWS_USED, :] = rows8
                gt_i[c * GT_PER_CHUNK + GT_COLS] = jnp.concatenate([rows8, pad_rows], axis=0).T

        n_col_pieces = GROUP_W // pw
        pieces = []
        for p in range(block_t // pr):
            pieces.append(functools.partial(piece_gate, p))
            for i in range(n_col_pieces):
                pieces.append(functools.partial(piece_conv, p, 2 * i))
                pieces.append(functools.partial(piece_cast, p, i, COL_RV, VV_RET))
                pieces.append(functools.partial(piece_rot, p, i, COL_RQ, cos_ref, sin_ref, qd_ref,
                                                RQK_Q, RQK_QDEC))
                pieces.append(functools.partial(piece_rz, p, i))
                pieces.append(functools.partial(piece_conv, p, 2 * i + 1))
                pieces.append(functools.partial(piece_cast, p, i, COL_MV, VV_M))
                pieces.append(functools.partial(piece_rot, p, i, COL_RK, kcos_ref, ksin_ref, kd_ref,
                                                RQK_K, RQK_KDEC))
                pieces.append(functools.partial(piece_moz, p, i))
        n_pieces = len(pieces)
        n_slots = EMITS_PER_CHUNK * n_chunks
        slots_done = [0]

        def emit_proj_pieces():
            slots_done[0] += 1
            target = -(-(slots_done[0] * n_pieces) // n_slots)
            while n_pieces - len(pieces) < target:
                pieces.pop(0)()

        def per_head(group, fn):
            out = {}
            for h in group:
                emit_proj_pieces()
                out[h] = fn(h)
            return out

        head_groups = [range(h0, h0 + HEADS_PER_STAGE) for h0 in range(0, N_HEADS, HEADS_PER_STAGE)]

        for c in range(n_chunks):
            rows = slice(c * CHUNK, (c + 1) * CHUNK)

            for grp in head_groups:
                q_bf = {h: rqk_c[rows, hcols(RQK_Q, h)] for h in grp}
                k_bf = {h: rqk_c[rows, hcols(RQK_K, h)] for h in grp}
                v_bf = {h: vv_c[rows, hcols(VV_RET, h)] for h in grp}
                sc = per_head(grp, lambda h: _dot_nt(q_bf[h], k_bf[h]))
                state = {h: s_scr[h] for h in grp}
                upd = per_head(grp, lambda h: _dot_tn(rqk_c[rows, hcols(RQK_KDEC, h)], v_bf[h]))
                for h in grp:
                    s_scr[h] = state[h] * cdec_ref[h] + upd[h]
                o = per_head(grp, lambda h: _dot(
                    jnp.concatenate([(sc[h] * decay_ref[h]).astype(BF16),
                                     rqk_c[rows, hcols(RQK_QDEC, h)]], axis=1),
                    jnp.concatenate([v_bf[h], state[h].astype(BF16)], axis=0)))

                def ret_out(h):
                    hn = _head_norm(o[h], g_ret_ref[lrow, hcols(0, h)])
                    mix_scr[rows, hcols(0, h)] = (hn * gz_c[rows, hcols(GZ_RET, h)]).astype(BF16)

                per_head(grp, ret_out)

            cols_t = gt_c[c * GT_PER_CHUNK + GT_COLS]
            rows_t = gt_c[c * GT_PER_CHUNK + GT_ROWS, 0:GT_ROWS_USED, :]
            for grp in head_groups:
                q = {h: qkm_c[rows, hcols(0, h)] for h in grp}
                k = {h: qkm_c[rows, hcols(GROUP_W, h)] for h in grp}
                vaug_bf = {h: jnp.concatenate([vv_c[rows, hcols(VV_M, h)], ones_col], axis=1)
                           for h in grp}
                qk = per_head(grp, lambda h: _dot_nt(q[h].astype(BF16), k[h].astype(BF16)))
                caug = {h: caug_scr[h] for h in grp}
                m_old = {h: m_scr[h, 0:1, 0:1] for h in grp}
                i_row = {h: rows_t[h:h + 1, :] for h in grp}
                b_row = {h: rows_t[N_HEADS + h:N_HEADS + h + 1, :] for h in grp}
                i_col = {h: cols_t[:, h:h + 1] for h in grp}
                b_col = {h: cols_t[:, N_HEADS + h:N_HEADS + h + 1] for h in grp}
                a_col = {h: b_col[h] + m_old[h] for h in grp}
                dm = per_head(grp, lambda h: (b_col[h] + (i_row[h] - b_row[h])) + causal_add)
                mt = {h: jnp.maximum(a_col[h], jnp.max(dm[h], axis=-1, keepdims=True)) for h in grp}
                w_inter = {h: jnp.exp(a_col[h] - mt[h]) for h in grp}
                res = per_head(grp, lambda h: _dot(
                    jnp.concatenate([(qk[h] * jnp.exp(dm[h] - mt[h])).astype(BF16),
                                     (q[h] * w_inter[h]).astype(BF16)], axis=1),
                    jnp.concatenate([vaug_bf[h], caug[h].astype(BF16)], axis=0)))
                b_last = {h: b_col[h][CHUNK - 1:CHUNK, :] for h in grp}
                g_col = {h: b_last[h] - b_col[h] + i_col[h] for h in grp}
                m_new = {h: jnp.maximum(b_last[h] + m_old[h], jnp.max(g_col[h], axis=0, keepdims=True))
                         for h in grp}
                wk = {h: jnp.exp(g_col[h] - m_new[h]) for h in grp}
                wc = {h: jnp.exp(b_last[h] + m_old[h] - m_new[h]) for h in grp}

                def mlstm_update(h):
                    caug_scr[h] = caug[h] * wc[h] + _dot_tn((k[h] * wk[h]).astype(BF16), vaug_bf[h])
                    m_scr[h] = jnp.broadcast_to(m_new[h], (SUBLANES, LANES))

                per_head(grp, mlstm_update)

                def mlstm_out(h):
                    den = jnp.maximum(jnp.abs(res[h][:, HEAD_DIM:HEAD_DIM + 1]), jnp.exp(-mt[h]))
                    hn = _head_norm(res[h][:, :HEAD_DIM] / den, g_m_ref[lrow, hcols(0, h)])
                    mix_scr[rows, hcols(GROUP_W, h)] = (hn * gz_c[rows, hcols(GZ_M, h)]).astype(BF16)

                per_head(grp, mlstm_out)

            emit_proj_pieces()
            mix = _dot(mix_scr[rows, :], w_out_ref[...])
            y_ref[0, rows, :] = _layer_norm(ALPHA * x_ref[0, rows, :] + mix, ln_g_ref[lrow, :], ln_b_ref[lrow, :])

        assert slots_done[0] == n_slots and not pieces

    set_a = (rqk_a, vv_a, gz_a, qkm_a, gt_a)
    set_b = (rqk_b, vv_b, gz_b, qkm_b, gt_b)
    parity = lax.rem(g, 2)

    @pl.when(parity == 0)
    def _even_step():
        step_body(set_a, set_b)

    @pl.when(parity == 1)
    def _odd_step():
        step_body(set_b, set_a)

    @pl.when(jnp.logical_and(tn == nt - 1, g < n_blocks))
    def _write_conv_state():
        if layer > 0:
            conv_out_ref[0:layer] = prev_refs[-1][...]
        conv_out_ref[layer, 0] = u_scr[SUBLANES + block_t - carry_rows:SUBLANES + block_t, :]

    @pl.when(jnp.logical_and(t == nt - 1, g > 0))
    def _write_state():
        if layer > 0:
            for out_ref, prev_ref in zip((s_out_ref, c_out_ref, n_out_ref, m_out_ref), prev_refs):
                out_ref[0:layer] = prev_ref[...]
        s_out_ref[layer, 0] = s_scr[...]
        m_out_ref[layer, 0] = jnp.zeros((SUBLANES, LANES), F32)
        for h in heads:
            caug = caug_scr[h]
            c_out_ref[layer, 0, h] = caug[:, :HEAD_DIM]
            n_out_ref[layer, 0, h:h + 1, :] = caug[:, HEAD_DIM:].T[0:1, :]
            m_out_ref[layer, 0, h:h + 1, :] = m_scr[h, 0:1, :]

    u_scr[0:SUBLANES, :] = u_scr[block_t:block_t + SUBLANES, :]


def _const_spec(shape):
    nd = len(shape)
    return pl.BlockSpec(shape, lambda g: (0,) * nd)


def _layer_spec(shape, layer):
    return pl.BlockSpec((None,) + tuple(shape[1:]), lambda g: (layer,) + (0,) * (len(shape) - 1))


def _prompt_layer(layer, x, w_in_bf, wg_bf, conv_w, conv_b, gbias, g_ret, g_m, w_out_bf, ln_g, ln_b, tabs,
                  prev_states):
    B, T, _ = x.shape
    bt = PROMPT_BLOCK_T
    assert T % bt == 0 and bt % CHUNK == 0 and GROUP_W % PROJ_PIECE_COLS == 0
    nt = T // bt
    n_blocks = B * nt
    n_chunks = bt // CHUNK
    cos_t, sin_t, kcos_t, ksin_t, decay, qd, kd, cdec, triu, mask, e0 = tabs

    def nxt(g):
        return jnp.minimum(g, n_blocks - 1)

    def cur(g):
        return jnp.maximum(g - 1, 0)

    row_spec = pl.BlockSpec((bt, LANES), lambda g: (nxt(g) % nt, 0))
    in_specs = [
        pl.BlockSpec((1, bt, D_MODEL), lambda g: (nxt(g) // nt, nxt(g) % nt, 0)),
        pl.BlockSpec((1, bt, D_MODEL), lambda g: (cur(g) // nt, cur(g) % nt, 0)),
        _layer_spec(w_in_bf.shape, layer), _layer_spec(wg_bf.shape, layer),
        _layer_spec(conv_w.shape, layer), _const_spec(conv_b.shape),
        _layer_spec(gbias.shape, layer), _const_spec(g_ret.shape), _const_spec(g_m.shape),
        _layer_spec(w_out_bf.shape, layer), _const_spec(ln_g.shape), _const_spec(ln_b.shape),
        row_spec, row_spec, row_spec, row_spec,
        _const_spec(decay.shape), _const_spec(qd.shape), _const_spec(kd.shape),
        pl.BlockSpec(memory_space=pltpu.SMEM),
        _const_spec(triu.shape), _const_spec(mask.shape), _const_spec(e0.shape),
    ]
    state_tails = ((N_HEADS, HEAD_DIM, HEAD_DIM), (N_HEADS, HEAD_DIM, HEAD_DIM), (N_HEADS, HEAD_DIM),
                   (SUBLANES, LANES), (CONV_W - 1, 2 * GROUP_W))
    assert len(state_tails) == N_STATE_OUTPUTS and len(prev_states) in (0, N_STATE_OUTPUTS)

    def state_spec(depth, tail, batch_of):
        zeros = (0,) * len(tail)
        return pl.BlockSpec((depth, 1) + tail, lambda g: (0, batch_of(g)) + zeros)

    batch_ofs = [lambda g: cur(g) // nt] * (N_STATE_OUTPUTS - 1) + [lambda g: nxt(g) // nt]
    out_shape = (jax.ShapeDtypeStruct((B, T, D_MODEL), F32),) + tuple(
        jax.ShapeDtypeStruct((layer + 1, B) + tail, F32) for tail in state_tails)
    out_specs = (pl.BlockSpec((1, bt, D_MODEL), lambda g: (cur(g) // nt, cur(g) % nt, 0)),) + tuple(
        state_spec(layer + 1, tail, bo) for tail, bo in zip(state_tails, batch_ofs))
    if prev_states:
        in_specs = in_specs + [state_spec(layer, tail, bo) for tail, bo in zip(state_tails, batch_ofs)]
    operand_set = [
        pltpu.VMEM((bt, 4 * GROUP_W), BF16),
        pltpu.VMEM((bt, 2 * GROUP_W), BF16),
        pltpu.VMEM((bt, 2 * GROUP_W), F32),
        pltpu.VMEM((bt, 2 * GROUP_W), F32),
        pltpu.VMEM((n_chunks * GT_PER_CHUNK, CHUNK, LANES), F32),
    ]
    scratch = operand_set + operand_set + [
        pltpu.VMEM((bt, D_MODEL), BF16),
        pltpu.VMEM((SUBLANES + bt, 2 * GROUP_W), F32),
        pltpu.VMEM((bt, 2 * GROUP_W), BF16),
        pltpu.VMEM((N_HEADS, HEAD_DIM, HEAD_DIM), F32),
        pltpu.VMEM((N_HEADS, HEAD_DIM, 2 * HEAD_DIM), F32),
        pltpu.VMEM((N_HEADS, SUBLANES, LANES), F32),
    ]
    y, *states = pl.pallas_call(
        functools.partial(_prompt_kernel, nt, n_blocks, layer),
        grid=(n_blocks + 1,),
        in_specs=in_specs,
        out_specs=out_specs,
        out_shape=out_shape,
        scratch_shapes=scratch,
        compiler_params=pltpu.CompilerParams(
            dimension_semantics=("arbitrary",),
            vmem_limit_bytes=VMEM_LIMIT_BYTES),
        name="prompt_layer",
    )(x, x, w_in_bf, wg_bf, conv_w, conv_b, gbias, g_ret, g_m, w_out_bf, ln_g, ln_b,
      cos_t, sin_t, kcos_t, ksin_t, decay, qd, kd, cdec, triu, mask, e0, *prev_states)
    return y, states


def _sample_kernel(x_ref, w_in_ref, wg_ref, conv_w_ref, conv_b_ref, gbias_ref, g_ret_ref, g_m_ref,
                   w_out_ref, ln_g_ref, ln_b_ref, rot_ref, gam_ref, esel_ref,
                   s_ref, c_ref, n_ref, m_ref, cv_ref,
                   y_ref, s_out_ref, c_out_ref, n_out_ref, m_out_ref, cv_out_ref,
                   proj_scr, xcur_scr, mix_scr):
    layer = pl.program_id(0)
    j = pl.program_id(1)
    bb = s_ref.shape[1]

    def layer_row(ref, cols=slice(None)):
        row = ref[0:1, cols]
        for l in range(1, DEPTH):
            row = jnp.where(layer == l, ref[l:l + 1, cols], row)
        return row

    @pl.when(jnp.logical_and(layer == 0, j == 0))
    def _load_x():
        xcur_scr[...] = x_ref[...]

    @pl.when(j == 0)
    def _project():
        xb = xcur_scr[...].astype(BF16)
        for lo in range(0, COL_GATE, 512):
            proj_scr[:, lo:lo + 512] = _dot(xb, w_in_ref[0, :, lo:lo + 512])
        proj_scr[:, COL_GATE:N_PAD] = jnp.zeros((proj_scr.shape[0], N_PAD - COL_GATE), F32)
        proj_scr[:, COL_GATE:COL_GATE + GATE_ROWS] = _dot_nt(xb, wg_ref[0])

    r0 = pl.multiple_of(j * bb, bb)
    rows = pl.ds(r0, bb)
    cos_t = rot_ref[0:1, :]
    sin_t = rot_ref[1:2, :]
    kcos_t = rot_ref[2:3, :]
    ksin_t = rot_ref[3:4, :]
    esel = esel_ref[...]
    row_id = lax.broadcasted_iota(jnp.int32, (bb, bb * HEAD_DIM), 0)
    blk_id = jnp.right_shift(lax.broadcasted_iota(jnp.int32, (bb, bb * HEAD_DIM), 1), 7)
    pad_rows = jnp.zeros((LANES - bb, HEAD_DIM), F32)
    pad_wide = jnp.zeros((LANES - bb, bb * HEAD_DIM), BF16)

    def col_form(x8):
        return jnp.concatenate([x8, pad_rows], axis=0).T.astype(BF16)

    def outer_all(k8, v8):
        vt = jnp.concatenate([v8] * bb, axis=1)
        vsel = jnp.where(row_id == blk_id, vt, 0.0).astype(BF16)
        return _dot(col_form(k8), jnp.concatenate([vsel, pad_wide], axis=0))

    def col_bcast_all(q8):
        return _dot(col_form(q8), esel)

    for h in range(N_HEADS):
        hc = h * HEAD_DIM
        q8 = _rotary(proj_scr[rows, COL_RQ + hc:COL_RQ + hc + HEAD_DIM], cos_t, sin_t)
        k8 = _rotary(proj_scr[rows, COL_RK + hc:COL_RK + hc + HEAD_DIM], kcos_t, ksin_t)
        v8 = proj_scr[rows, COL_RV + hc:COL_RV + hc + HEAD_DIM]
        kv_all = outer_all(k8, v8)
        qc_all = col_bcast_all(q8)
        gamma = gam_ref[h]
        o_rows = []
        for r in range(bb):
            blk = slice(r * HEAD_DIM, (r + 1) * HEAD_DIM)
            s_new = s_ref[0, r, h] * gamma + kv_all[:, blk]
            s_out_ref[0, r, h] = s_new
            o_rows.append(jnp.sum(qc_all[:, blk] * s_new, axis=0, keepdims=True))
        o8 = _head_norm(jnp.concatenate(o_rows, axis=0), layer_row(g_ret_ref, slice(hc, hc + HEAD_DIM)))
        z8 = proj_scr[rows, COL_RZ + hc:COL_RZ + hc + HEAD_DIM]
        mix_scr[rows, hc:hc + HEAD_DIM] = o8 * _silu(z8)

    u8 = proj_scr[rows, COL_MQK:COL_MQK + 2 * GROUP_W]
    acc = layer_row(conv_b_ref) + u8 * conv_w_ref[0, CONV_W - 1:CONV_W, :]
    for jj in range(CONV_W - 1):
        acc = acc + cv_ref[0, jj] * conv_w_ref[0, jj:jj + 1, :]
    for jj in range(1, CONV_W - 1):
        cv_out_ref[0, jj - 1] = cv_ref[0, jj]
    cv_out_ref[0, CONV_W - 2] = u8
    qk8 = _silu(acc)

    gates = proj_scr[rows, COL_GATE:COL_GATE + LANES] + gbias_ref[0]
    i_al = pltpu.roll(gates, N_HEADS, axis=1)
    bm = _log_sigmoid(gates) + m_ref[0]
    m_new = jnp.maximum(bm, i_al)
    wk = jnp.exp(i_al - m_new)
    wc = jnp.exp(bm - m_new)
    einv = jnp.exp(-m_new)
    m_out_ref[0] = m_new
    for h in range(N_HEADS):
        hc = h * HEAD_DIM
        gl = N_HEADS + h
        q8 = qk8[:, hc:hc + HEAD_DIM]
        k8 = qk8[:, GROUP_W + hc:GROUP_W + hc + HEAD_DIM] * QK_SCALE
        v8 = proj_scr[rows, COL_MV + hc:COL_MV + hc + HEAD_DIM]
        wk_h = wk[:, gl:gl + 1]
        wc_h = jnp.broadcast_to(wc[:, gl:gl + 1], (bb, HEAD_DIM))
        kw8 = k8 * wk_h
        kv_all = outer_all(kw8, v8)
        qc_all = col_bcast_all(q8)
        n_new = n_ref[0, :, h, :] * wc_h + kw8
        n_out_ref[0, :, h, :] = n_new
        num_rows = []
        for r in range(bb):
            blk = slice(r * HEAD_DIM, (r + 1) * HEAD_DIM)
            c_new = c_ref[0, r, h] * wc_h[r:r + 1, :] + kv_all[:, blk]
            c_out_ref[0, r, h] = c_new
            num_rows.append(jnp.sum(qc_all[:, blk] * c_new, axis=0, keepdims=True))
        num = jnp.concatenate(num_rows, axis=0)
        q_bf = q8.astype(BF16).astype(F32)
        den = jnp.sum(q_bf * n_new, axis=-1, keepdims=True)
        hout = num / jnp.maximum(jnp.abs(den), einv[:, gl:gl + 1])
        hn = _head_norm(hout, layer_row(g_m_ref, slice(hc, hc + HEAD_DIM)))
        og = _sigmoid(proj_scr[rows, COL_MO + hc:COL_MO + hc + HEAD_DIM])
        zg = _silu(proj_scr[rows, COL_MZ + hc:COL_MZ + hc + HEAD_DIM])
        mix_scr[rows, GROUP_W + hc:GROUP_W + hc + HEAD_DIM] = hn * og * zg

    @pl.when(j == pl.num_programs(1) - 1)
    def _finish_layer():
        mix = _dot(mix_scr[...].astype(BF16), w_out_ref[0])
        y = _layer_norm(ALPHA * xcur_scr[...] + mix, layer_row(ln_g_ref), layer_row(ln_b_ref))
        xcur_scr[...] = y
        y_ref[...] = y


def _sample_layers(x, w_in_bf, wg_bf, conv_w, conv_b, gbias, g_ret, g_m, w_out_bf, ln_g, ln_b,
                   rot, gam, esel, state_ret, state_c, state_n, m_pad, state_conv):
    bs = x.shape[0]
    bb = SAMPLE_BLOCK_B
    assert bs % bb == 0
    nb = bs // bb

    def lspec(shape, buffers=None):
        nd = len(shape)
        mode = None if buffers is None else pl.Buffered(buffers)
        return pl.BlockSpec((1,) + tuple(shape[1:]), lambda l, j: (l,) + (0,) * (nd - 1),
                            pipeline_mode=mode)

    def cspec(shape):
        nd = len(shape)
        return pl.BlockSpec(tuple(shape), lambda l, j: (0,) * nd)

    mat_spec = pl.BlockSpec((1, bb, N_HEADS, HEAD_DIM, HEAD_DIM), lambda l, j: (l, j, 0, 0, 0))
    n_spec = pl.BlockSpec((1, bb, N_HEADS, HEAD_DIM), lambda l, j: (l, j, 0, 0))
    m_spec = pl.BlockSpec((1, bb, LANES), lambda l, j: (l, j, 0))
    cv_spec = pl.BlockSpec((1, CONV_W - 1, bb, 2 * GROUP_W), lambda l, j: (l, 0, j, 0))
    in_specs = [
        cspec(x.shape), lspec(w_in_bf.shape, 1), lspec(wg_bf.shape),
        lspec(conv_w.shape), cspec(conv_b.shape),
        lspec(gbias.shape), cspec(g_ret.shape), cspec(g_m.shape), lspec(w_out_bf.shape, 1),
        cspec(ln_g.shape), cspec(ln_b.shape), cspec(rot.shape),
        pl.BlockSpec(memory_space=pltpu.SMEM), cspec(esel.shape),
        mat_spec, mat_spec, n_spec, m_spec, cv_spec,
    ]
    out_shape = (
        jax.ShapeDtypeStruct(x.shape, F32),
        jax.ShapeDtypeStruct(state_ret.shape, F32),
        jax.ShapeDtypeStruct(state_c.shape, F32),
        jax.ShapeDtypeStruct(state_n.shape, F32),
        jax.ShapeDtypeStruct(m_pad.shape, F32),
        jax.ShapeDtypeStruct(state_conv.shape, F32),
    )
    out_specs = (cspec(x.shape), mat_spec, mat_spec, n_spec, m_spec, cv_spec)
    scratch = [
        pltpu.VMEM((bs, N_PAD), F32),
        pltpu.VMEM((bs, D_MODEL), F32),
        pltpu.VMEM((bs, 2 * GROUP_W), F32),
    ]
    return pl.pallas_call(
        _sample_kernel,
        grid=(DEPTH, nb),
        in_specs=in_specs,
        out_specs=out_specs,
        out_shape=out_shape,
        scratch_shapes=scratch,
        compiler_params=pltpu.CompilerParams(
            dimension_semantics=("arbitrary", "arbitrary"),
            vmem_limit_bytes=VMEM_LIMIT_BYTES),
        name="sample_layers",
    )(x, w_in_bf, wg_bf, conv_w, conv_b, gbias, g_ret, g_m, w_out_bf, ln_g, ln_b,
      rot, gam, esel, state_ret, state_c, state_n, m_pad, state_conv)


def _cast_transposed_weight_kernel(n_valid, n_plain_blocks, wt_ref, w2_ref, o_ref, o2_ref):
    rows = wt_ref.shape[1]
    row_id = pl.program_id(1) * rows + lax.broadcasted_iota(jnp.int32, wt_ref.shape[1:], 0)
    wt = jnp.where(row_id < n_valid, wt_ref[0], 0.0)
    o_ref[0] = wt.astype(BF16).T

    @pl.when(pl.program_id(1) < n_plain_blocks)
    def _cast_plain():
        o2_ref[0] = w2_ref[0].astype(BF16)


def _cast_transposed_weight(wt, n_out, w2):
    depth, n, k = wt.shape
    rows = WEIGHT_CAST_ROWS
    assert n_out % LANES == 0 and n_out - n < LANES
    n_steps = pl.cdiv(n_out, rows)
    _, k2, n2 = w2.shape
    plain_rows = PLAIN_CAST_ROWS
    n_plain = k2 // plain_rows
    assert k2 % plain_rows == 0 and n_plain <= n_steps

    def plain_block(l, r):
        return (l, jnp.minimum(r, n_plain - 1), 0)

    return pl.pallas_call(
        functools.partial(_cast_transposed_weight_kernel, min(n, n_out), n_plain),
        grid=(depth, n_steps),
        in_specs=[pl.BlockSpec((1, rows, k), lambda l, r: (l, r, 0)),
                  pl.BlockSpec((1, plain_rows, n2), plain_block)],
        out_specs=(pl.BlockSpec((1, k, rows), lambda l, r: (l, 0, r)),
                   pl.BlockSpec((1, plain_rows, n2), plain_block)),
        out_shape=(jax.ShapeDtypeStruct((depth, k, n_out), BF16),
                   jax.ShapeDtypeStruct((depth, k2, n2), BF16)),
        compiler_params=pltpu.CompilerParams(dimension_semantics=("parallel", "arbitrary")),
        name="cast_weight_t",
    )(wt, w2)


def _rotary_tables(pos):
    half = HEAD_DIM // 2
    inv = np.float64(ROPE_BASE) ** (-np.arange(half, dtype=np.float64) / half)
    ang = pos.astype(np.float64)[:, None] * inv[None, :]
    cos = np.cos(ang)
    sin = np.sin(ang)
    cos_t = np.concatenate([cos, cos], axis=-1)
    sin_t = np.concatenate([-sin, sin], axis=-1)
    tabs = (cos_t, sin_t, cos_t * QK_SCALE, sin_t * QK_SCALE)
    return tuple(t.astype(np.float32) for t in tabs)


def _retention_tables():
    L = CHUNK
    f32 = np.float32
    log_gamma = np.log(1.0 - 2.0 ** (-5.0 - np.arange(N_HEADS, dtype=np.float64)))
    idx = np.arange(L, dtype=np.float64)
    diff = idx[:, None] - idx[None, :]
    decay = (np.exp(log_gamma[:, None, None] * np.maximum(diff, 0.0)) * (diff >= 0)).astype(f32)
    q_decay = np.exp(log_gamma[:, None] * (idx + 1.0)).astype(f32)
    k_decay = np.exp(log_gamma[:, None] * (L - 1.0 - idx)).astype(f32)
    c_decay = np.exp(log_gamma * L).astype(f32)
    qd = np.ascontiguousarray(np.broadcast_to(q_decay[:, :, None], (N_HEADS, L, HEAD_DIM)))
    kd = np.ascontiguousarray(np.broadcast_to(k_decay[:, :, None], (N_HEADS, L, HEAD_DIM)))
    gamma1 = np.exp(log_gamma).astype(f32)
    return decay, qd, kd, c_decay, gamma1


def kernel(x_prompt, x_sample, state_ret, state_mlstm_C, state_mlstm_n, state_mlstm_m, state_conv,
           w_in, conv_w, conv_b, b_i, b_f, g_ret, g_m, w_out, ln_g, ln_b):
    B, T, _ = x_prompt.shape
    Bs, Ts, _ = x_sample.shape
    assert Ts == 1

    w_in_t = jnp.swapaxes(w_in, 1, 2)
    w_in_bf, w_out_bf = _cast_transposed_weight(w_in_t, COL_GATE, w_out)
    wg_bf = jnp.pad(w_in_t[:, COL_GATE:N_IN, :], ((0, 0), (0, GATE_ROWS - 2 * N_HEADS), (0, 0))).astype(BF16)
    gate_b = jnp.concatenate([b_i, b_f], axis=-1)
    gbias = jnp.pad(gate_b, ((0, 0), (0, LANES - 2 * N_HEADS))).reshape(DEPTH, 1, LANES)
    gbias_rows = jnp.broadcast_to(gate_b[:, :, None], (DEPTH, 2 * N_HEADS, LANES))

    decay, qd, kd, c_decay, gamma1 = _retention_tables()
    idx = np.arange(CHUNK)
    causal = idx[:, None] >= idx[None, :]
    triu = jnp.asarray(causal.T, BF16)
    mask_add = np.where(causal, 0.0, -np.inf).astype(np.float32)
    e0 = jnp.asarray(np.broadcast_to(np.arange(LANES)[None, :] == 0, (CHUNK, LANES)), BF16)
    tabs_p = _rotary_tables(np.arange(T)) + (decay, qd, kd, c_decay, triu, mask_add, e0)
    tabs_p = tuple(jnp.asarray(a) for a in tabs_p)

    xp = x_prompt
    prompt_states = []
    for l in range(DEPTH):
        xp, prompt_states = _prompt_layer(
            l, xp, w_in_bf, wg_bf, conv_w, conv_b, gbias_rows, g_ret, g_m, w_out_bf, ln_g, ln_b, tabs_p,
            prompt_states)
    rp, cp, np_, mp_pad, vp = prompt_states
    mp = mp_pad[:, :, :N_HEADS, 0]

    rot = jnp.asarray(np.concatenate(_rotary_tables(PAST_LEN + np.arange(Ts)), axis=0))
    bb = SAMPLE_BLOCK_B
    esel = jnp.asarray(np.arange(LANES)[:, None] == (np.arange(bb * HEAD_DIM)[None, :] // HEAD_DIM), BF16)
    gamma1 = jnp.asarray(gamma1)
    m_pad = jnp.pad(state_mlstm_m, ((0, 0), (0, 0), (N_HEADS, LANES - 2 * N_HEADS)))
    ys, rs, cs, ns, ms_pad, vs = _sample_layers(
        x_sample.reshape(Bs, D_MODEL), w_in_bf, wg_bf, conv_w, conv_b, gbias, g_ret, g_m, w_out_bf,
        ln_g, ln_b, rot, gamma1, esel, state_ret, state_mlstm_C, state_mlstm_n, m_pad,
        jnp.swapaxes(state_conv, 1, 2))
    vs = jnp.swapaxes(vs, 1, 2)
    ms = ms_pad[:, :, N_HEADS:2 * N_HEADS]

    return (xp, ys.reshape(Bs, Ts, D_MODEL),
            rp, cp, np_, mp, vp, rs, cs, ns, ms, vs)
```
